```python
import jax, jax.numpy as jnp
from jax import lax
import numpy as np

D_MODEL = 1024
BATCH = 4
SEQ = 8192
DEPTH = 1

N_HEADS_A = 8
N_KV_GROUPS = 2
HEADS_PER_GROUP = N_HEADS_A // N_KV_GROUPS
HEAD_DIM = 64
L_CMP = 32
D_CMP = 16
CMP_HIDDEN = 256
L_SLC = 64
N_SELECT = 16
WINDOW = 512
Q_BLOCK = 128
ROPE_THETA = 10000.0
FORCE_SCORE = 1e4
NEG_INF = -1e30
N_HEADS_B = 8
HEAD_DIM_B = 64
W_LORA = 64
A_LORA = 64
G_LORA = 128
GN_EPS = 64e-5
N_GROUPS = 4
EXPERTS_PER_GROUP = 8
N_EXPERTS = N_GROUPS * EXPERTS_PER_GROUP
TOP_K = 2
D_EXPERT = 256
MOE_BLOCK = 256
RMS_EPS = 1e-6
WIDTH_A = N_HEADS_A * HEAD_DIM
KV_WIDTH = N_KV_GROUPS * HEAD_DIM
WIDTH_B = N_HEADS_B * HEAD_DIM_B
C_RWKV = 3 * WIDTH_B + W_LORA + A_LORA + G_LORA
C_IN = WIDTH_A + 6 * KV_WIDTH + 3 * N_HEADS_A + C_RWKV + 2 * D_MODEL

kernel_name = "hybrid_nsa_rwkv7_hiermoe"


def rms_norm(x, g):
    xf = x.astype(jnp.float32)
    y = xf * lax.rsqrt(jnp.mean(xf * xf, axis=-1, keepdims=True) + RMS_EPS)
    return (y * g.astype(jnp.float32)).astype(x.dtype)


def rope(x, pos):
    half = x.shape[-1] // 2
    freqs = jnp.power(ROPE_THETA, -jnp.arange(half, dtype=jnp.float32) / half)
    ang = pos.astype(jnp.float32)[..., None] * freqs
    cos, sin = jnp.cos(ang)[:, :, None, :], jnp.sin(ang)[:, :, None, :]
    xf = x.astype(jnp.float32)
    x1, x2 = xf[..., :half], xf[..., half:]
    return jnp.concatenate([x1 * cos - x2 * sin, x1 * sin + x2 * cos], axis=-1).astype(x.dtype)


def masked_softmax(s, valid):
    p = jax.nn.softmax(jnp.where(valid, s, NEG_INF), axis=-1)
    return jnp.where(valid, p, 0.0)


def compress(kv, pe, w1, b1, w2, b2):
    bsz, s = kv.shape[:2]
    n_cmp = (s - L_CMP) // D_CMP + 1
    idx = np.arange(n_cmp)[:, None] * D_CMP + np.arange(L_CMP)[None, :]
    blk = kv[:, idx] + pe[:, None, :]
    blk = blk.transpose(0, 1, 3, 2, 4).reshape(bsz, n_cmp, N_KV_GROUPS, L_CMP * HEAD_DIM)
    return jax.nn.gelu(blk @ w1 + b1) @ w2 + b2


def nsa_mixer(q, k_c, v_c, k_slc, v_slc, k_win, v_win, gates):
    bsz, s = q.shape[:2]
    n_cmp = k_c.shape[1]
    n_slc = s // L_SLC
    n_top = min(N_SELECT, n_slc)
    scale = HEAD_DIM ** -0.5
    f32 = jnp.float32
    cmp_end = jnp.arange(n_cmp) * D_CMP + (L_CMP - 1)
    ii = np.arange(n_cmp)[:, None]
    jj = np.arange(n_slc)[None, :]
    overlap = jnp.asarray((ii * D_CMP < (jj + 1) * L_SLC) & (ii * D_CMP + L_CMP > jj * L_SLC), f32)
    blk_ids = jnp.arange(n_slc)
    ks_blk = k_slc.reshape(bsz, n_slc, L_SLC, N_KV_GROUPS, HEAD_DIM).transpose(0, 3, 1, 2, 4)
    vs_blk = v_slc.reshape(bsz, n_slc, L_SLC, N_KV_GROUPS, HEAD_DIM).transpose(0, 3, 1, 2, 4)
    pad = ((0, 0), (WINDOW, 0), (0, 0), (0, 0))
    kw_pad, vw_pad = jnp.pad(k_win, pad), jnp.pad(v_win, pad)
    bi = jnp.arange(bsz)[:, None, None, None]
    gi = jnp.arange(N_KV_GROUPS)[None, :, None, None]
    v_c32 = v_c.astype(f32)

    def block(qb_i):
        s0 = qb_i * Q_BLOCK
        t = s0 + jnp.arange(Q_BLOCK)
        qb = lax.dynamic_slice_in_dim(q, s0, Q_BLOCK, axis=1).reshape(
            bsz, Q_BLOCK, N_KV_GROUPS, HEADS_PER_GROUP, HEAD_DIM)
        gb = jax.nn.sigmoid(lax.dynamic_slice_in_dim(gates, s0, Q_BLOCK, axis=1).astype(f32)).reshape(
            bsz, Q_BLOCK, N_KV_GROUPS, HEADS_PER_GROUP, 3)
        s_c = jnp.einsum('bqghd,bngd->bghqn', qb, k_c, preferred_element_type=f32) * scale
        p_c = masked_softmax(s_c, cmp_end[None, :] <= t[:, None])
        o_c = jnp.einsum('bghqn,bngd->bqghd', p_c, v_c32)
        imp = jnp.einsum('bghqn,nj->bgqj', p_c, overlap)
        cur = t // L_SLC
        valid_s = blk_ids[None, :] <= cur[:, None]
        forced = (blk_ids[None, :] == 0) | (blk_ids[None, :] == cur[:, None]) | (blk_ids[None, :] == cur[:, None] - 1)
        score = jnp.where(forced, FORCE_SCORE, jnp.where(valid_s, imp, -1.0))
        _, sel = lax.top_k(score, n_top)
        k_sel = ks_blk[bi, gi, sel].reshape(bsz, N_KV_GROUPS, Q_BLOCK, n_top * L_SLC, HEAD_DIM)
        v_sel = vs_blk[bi, gi, sel].reshape(bsz, N_KV_GROUPS, Q_BLOCK, n_top * L_SLC, HEAD_DIM)
        pos_sel = (sel[..., None] * L_SLC + jnp.arange(L_SLC)).reshape(bsz, N_KV_GROUPS, Q_BLOCK, n_top * L_SLC)
        s_s = jnp.einsum('bqghd,bgqkd->bghqk', qb, k_sel, preferred_element_type=f32) * scale
        p_s = masked_softmax(s_s, (pos_sel <= t[:, None])[:, :, None])
        o_s = jnp.einsum('bghqk,bgqkd->bqghd', p_s, v_sel.astype(f32))
        kw = lax.dynamic_slice_in_dim(kw_pad, s0, WINDOW + Q_BLOCK, axis=1)
        vw = lax.dynamic_slice_in_dim(vw_pad, s0, WINDOW + Q_BLOCK, axis=1)
        pos_w = s0 - WINDOW + jnp.arange(WINDOW + Q_BLOCK)
        valid_w = (pos_w[None, :] <= t[:, None]) & (pos_w[None, :] > t[:, None] - WINDOW) & (pos_w[None, :] >= 0)
        s_w = jnp.einsum('bqghd,bkgd->bghqk', qb, kw, preferred_element_type=f32) * scale
        p_w = masked_softmax(s_w, valid_w)
        o_w = jnp.einsum('bghqk,bkgd->bqghd', p_w, vw.astype(f32))
        o = gb[..., 0:1] * o_c + gb[..., 1:2] * o_s + gb[..., 2:3] * o_w
        return o.reshape(bsz, Q_BLOCK, WIDTH_A).astype(q.dtype)

    out = lax.map(block, jnp.arange(s // Q_BLOCK))
    return out.transpose(1, 0, 2, 3).reshape(bsz, s, WIDTH_A)


def rwkv7_mixer(p, mu, w0, w2, a0, a2, g2, k_k, k_a, r_k, ln_w, ln_b):
    bsz, s = p.shape[:2]
    f32 = jnp.float32
    p_prev = jnp.pad(p[:, :-1], ((0, 0), (1, 0), (0, 0)))
    xs = p + (p_prev - p) * mu
    r, k, v, xw, xa, xg = jnp.split(xs, np.cumsum([WIDTH_B, WIDTH_B, WIDTH_B, W_LORA, A_LORA]), axis=-1)
    w_log = -jax.nn.softplus(-(w0 + jnp.tanh(xw) @ w2)) - 0.5
    decay = jnp.exp(-jnp.exp(w_log.astype(f32)))
    a = jax.nn.sigmoid(a0 + xa @ a2)
    g = jax.nn.sigmoid(xg) @ g2
    heads = lambda z: z.reshape(bsz, s, N_HEADS_B, HEAD_DIM_B).astype(f32)
    kk = heads(k * k_k)
    kk = kk / jnp.maximum(jnp.linalg.norm(kk, axis=-1, keepdims=True), 1e-12)
    k = k * (1 + (a - 1) * k_a)
    rh, kh, vh, ah, wh = heads(r), heads(k), heads(v), heads(a), heads(decay)

    def step(state, inp):
        r_t, w_t, k_t, v_t, kk_t, a_t = inp
        sa = jnp.einsum('bhvk,bhk->bhv', state, -kk_t)
        state = state * w_t[:, :, None, :] + sa[..., None] * (kk_t * a_t)[:, :, None, :] \
            + v_t[..., None] * k_t[:, :, None, :]
        return state, jnp.einsum('bhvk,bhk->bhv', state, r_t)

    tm = lambda z: jnp.swapaxes(z, 0, 1)
    state0 = jnp.zeros((bsz, N_HEADS_B, HEAD_DIM_B, HEAD_DIM_B), f32)
    _, y = lax.scan(step, state0, (tm(rh), tm(wh), tm(kh), tm(vh), tm(kk), tm(ah)))
    y = tm(y)
    mean = jnp.mean(y, axis=-1, keepdims=True)
    var = jnp.mean(jnp.square(y - mean), axis=-1, keepdims=True)
    y = ((y - mean) * lax.rsqrt(var + GN_EPS)).reshape(bsz, s, WIDTH_B) * ln_w + ln_b
    bonus = jnp.sum(rh * kh * r_k, axis=-1, keepdims=True) * vh
    y = y + bonus.reshape(bsz, s, WIDTH_B)
    return (y * g).astype(p.dtype)


def hier_moe(h, w_grp, b_grp, w_exp, b_exp, e_w1, e_w3, e_w2):
    bsz, s, d = h.shape
    n_tok = bsz * s
    f32 = jnp.float32
    t = h.reshape(n_tok, d)
    grp_logits = (t @ w_grp + b_grp).astype(f32)
    p_grp = jax.nn.softmax(grp_logits, axis=-1)
    g_star = jnp.argmax(grp_logits, axis=-1)
    p_g = jnp.take_along_axis(p_grp, g_star[:, None], axis=1)[:, 0]
    exp_logits = (t @ w_exp + b_exp).astype(f32).reshape(n_tok, N_GROUPS, EXPERTS_PER_GROUP)
    in_grp = jnp.take_along_axis(exp_logits, g_star[:, None, None], axis=1)[:, 0]
    top_p, top_j = lax.top_k(jax.nn.softmax(in_grp, axis=-1), TOP_K)
    gate = p_g[:, None] * top_p / jnp.sum(top_p, axis=-1, keepdims=True)
    expert = g_star[:, None] * EXPERTS_PER_GROUP + top_j
    n_asg = n_tok * TOP_K
    e_flat = expert.reshape(n_asg)
    tok_flat = jnp.repeat(jnp.arange(n_tok), TOP_K)
    gate_flat = gate.reshape(n_asg)
    order = jnp.argsort(e_flat)
    e_sorted = e_flat[order]
    counts = jnp.bincount(e_flat, length=N_EXPERTS)
    padded = (counts + MOE_BLOCK - 1) // MOE_BLOCK * MOE_BLOCK
    pad_end = jnp.cumsum(padded)
    pad_start = pad_end - padded
    start = jnp.cumsum(counts) - counts
    dest = pad_start[e_sorted] + (jnp.arange(n_asg) - start[e_sorted])
    n_pad = n_asg + N_EXPERTS * MOE_BLOCK
    buf_tok = jnp.zeros((n_pad,), jnp.int32).at[dest].set(tok_flat[order].astype(jnp.int32))
    buf_gate = jnp.zeros((n_pad,), f32).at[dest].set(gate_flat[order])
    n_blk = n_pad // MOE_BLOCK
    blk_expert = jnp.minimum(jnp.searchsorted(pad_end, jnp.arange(n_blk) * MOE_BLOCK, side='right'), N_EXPERTS - 1)

    def run_block(args):
        tok_idx, g_b, e = args
        xb = t[tok_idx]
        hid = jax.nn.silu(xb @ e_w1[e]) * (xb @ e_w3[e])
        return ((hid @ e_w2[e]) * g_b[:, None]).astype(t.dtype)

    y = lax.map(run_block, (buf_tok.reshape(n_blk, MOE_BLOCK), buf_gate.reshape(n_blk, MOE_BLOCK), blk_expert))
    out = jax.ops.segment_sum(y.reshape(n_pad, d), buf_tok, num_segments=n_tok)
    return out.reshape(bsz, s, d).astype(h.dtype)


def setup_inputs(seed: int = 0) -> dict:
    key = jax.random.key(seed)
    ks = iter(jax.random.split(key, 48))
    f32 = jnp.float32
    nrm = lambda shape, sc: jax.random.normal(next(ks), shape, f32) * sc
    gain = lambda shape: 1.0 + nrm(shape, 0.02)
    L = DEPTH
    inp = {}
    inp['x'] = nrm((BATCH, SEQ, D_MODEL), 1.0)
    inp['positions'] = jnp.broadcast_to(jnp.arange(SEQ, dtype=jnp.int32), (BATCH, SEQ))
    inp['g_mix'] = gain((L, D_MODEL))
    inp['w_in'] = nrm((L, D_MODEL, C_IN), D_MODEL ** -0.5)
    for nm in ('k', 'v'):
        inp['cmp_pe_' + nm] = nrm((L, L_CMP, HEAD_DIM), 0.1)
        inp['cmp_w1_' + nm] = nrm((L, L_CMP * HEAD_DIM, CMP_HIDDEN), (L_CMP * HEAD_DIM) ** -0.5)
        inp['cmp_b1_' + nm] = nrm((L, CMP_HIDDEN), 0.02)
        inp['cmp_w2_' + nm] = nrm((L, CMP_HIDDEN, HEAD_DIM), CMP_HIDDEN ** -0.5)
        inp['cmp_b2_' + nm] = nrm((L, HEAD_DIM), 0.02)
    inp['rwkv_mu'] = jax.random.uniform(next(ks), (L, C_RWKV), f32)
    inp['rwkv_w0'] = jax.random.uniform(next(ks), (L, WIDTH_B), f32, -5.0, 1.0)
    inp['rwkv_w2'] = nrm((L, W_LORA, WIDTH_B), W_LORA ** -0.5)
    inp['rwkv_a0'] = nrm((L, WIDTH_B), 0.5)
    inp['rwkv_a2'] = nrm((L, A_LORA, WIDTH_B), A_LORA ** -0.5)
    inp['rwkv_g2'] = nrm((L, G_LORA, WIDTH_B), G_LORA ** -0.5)
    inp['rwkv_k_k'] = 0.85 + nrm((L, WIDTH_B), 0.05)
    inp['rwkv_k_a'] = 1.0 + nrm((L, WIDTH_B), 0.05)
    inp['rwkv_r_k'] = nrm((L, N_HEADS_B, HEAD_DIM_B), 0.1)
    inp['rwkv_ln_w'] = gain((L, WIDTH_B))
    inp['rwkv_ln_b'] = nrm((L, WIDTH_B), 0.02)
    inp['w_proj_a'] = nrm((L, WIDTH_A, D_MODEL), WIDTH_A ** -0.5)
    inp['w_proj_b'] = nrm((L, WIDTH_B, D_MODEL), WIDTH_B ** -0.5)
    inp['w_out'] = nrm((L, D_MODEL, D_MODEL), D_MODEL ** -0.5)
    inp['g_ffn'] = gain((L, D_MODEL))
    inp['w_grp'] = nrm((L, D_MODEL, N_GROUPS), D_MODEL ** -0.5)
    inp['b_grp'] = nrm((L, N_GROUPS), 0.01)
    inp['w_exp'] = nrm((L, D_MODEL, N_EXPERTS), D_MODEL ** -0.5)
    inp['b_exp'] = nrm((L, N_EXPERTS), 0.01)
    inp['e_w1'] = nrm((L, N_EXPERTS, D_MODEL, D_EXPERT), D_MODEL ** -0.5)
    inp['e_w3'] = nrm((L, N_EXPERTS, D_MODEL, D_EXPERT), D_MODEL ** -0.5)
    inp['e_w2'] = nrm((L, N_EXPERTS, D_EXPERT, D_MODEL), D_EXPERT ** -0.5)
    inp['g_final'] = gain((D_MODEL,))
    return inp


def reference(x, positions, g_mix, w_in,
              cmp_pe_k, cmp_w1_k, cmp_b1_k, cmp_w2_k, cmp_b2_k,
              cmp_pe_v, cmp_w1_v, cmp_b1_v, cmp_w2_v, cmp_b2_v,
              rwkv_mu, rwkv_w0, rwkv_w2, rwkv_a0, rwkv_a2, rwkv_g2,
              rwkv_k_k, rwkv_k_a, rwkv_r_k, rwkv_ln_w, rwkv_ln_b,
              w_proj_a, w_proj_b, w_out, g_ffn,
              w_grp, b_grp, w_exp, b_exp, e_w1, e_w3, e_w2, g_final):
    bsz, s, _ = x.shape
    splits = np.cumsum([WIDTH_A] + [KV_WIDTH] * 6 + [3 * N_HEADS_A, C_RWKV, D_MODEL])
    for l in range(DEPTH):
        h = rms_norm(x, g_mix[l])
        p = h @ w_in[l]
        q, kc, vc, ksl, vsl, kwn, vwn, nsa_g, p_rwkv, gate_a, gate_b = jnp.split(p, splits, axis=-1)
        hd = lambda z, n: z.reshape(bsz, s, n, HEAD_DIM)
        q = rope(hd(q, N_HEADS_A), positions)
        kc = rope(hd(kc, N_KV_GROUPS), positions)
        ksl = rope(hd(ksl, N_KV_GROUPS), positions)
        kwn = rope(hd(kwn, N_KV_GROUPS), positions)
        vc, vsl, vwn = hd(vc, N_KV_GROUPS), hd(vsl, N_KV_GROUPS), hd(vwn, N_KV_GROUPS)
        k_c = compress(kc, cmp_pe_k[l], cmp_w1_k[l], cmp_b1_k[l], cmp_w2_k[l], cmp_b2_k[l])
        v_c = compress(vc, cmp_pe_v[l], cmp_w1_v[l], cmp_b1_v[l], cmp_w2_v[l], cmp_b2_v[l])
        o_a = nsa_mixer(q, k_c, v_c, ksl, vsl, kwn, vwn, nsa_g.reshape(bsz, s, N_HEADS_A, 3))
        o_b = rwkv7_mixer(p_rwkv, rwkv_mu[l], rwkv_w0[l], rwkv_w2[l], rwkv_a0[l], rwkv_a2[l], rwkv_g2[l],
                          rwkv_k_k[l], rwkv_k_a[l], rwkv_r_k[l], rwkv_ln_w[l], rwkv_ln_b[l])
        merged = jax.nn.sigmoid(gate_a) * (o_a @ w_proj_a[l]) + jax.nn.sigmoid(gate_b) * (o_b @ w_proj_b[l])
        x = x + merged @ w_out[l]
        h2 = rms_norm(x, g_ffn[l])
        x = x + hier_moe(h2, w_grp[l], b_grp[l], w_exp[l], b_exp[l], e_w1[l], e_w3[l], e_w2[l])
    return rms_norm(x, g_final)
```

```python
import functools

import numpy as np
import jax
import jax.numpy as jnp
from jax import lax
from jax.experimental import pallas as pl
from jax.experimental.pallas import tpu as pltpu

F32 = jnp.float32
BF16 = jnp.bfloat16
I32 = jnp.int32

D_MODEL = 1024
N_HEADS_A = 8
N_KV_GROUPS = 2
HEADS_PER_GROUP = N_HEADS_A // N_KV_GROUPS
HEAD_DIM = 64
L_CMP = 32
D_CMP = 16
CMP_HIDDEN = 256
L_SLC = 64
N_SELECT = 16
WINDOW = 512
Q_BLOCK = 128
ROPE_THETA = 10000.0
FORCE_SCORE = 1e4
NEG_INF = -1e30
N_HEADS_B = 8
HEAD_DIM_B = 64
W_LORA = 64
A_LORA = 64
G_LORA = 128
GN_EPS = 64e-5
N_GROUPS = 4
EXPERTS_PER_GROUP = 8
N_EXPERTS = N_GROUPS * EXPERTS_PER_GROUP
TOP_K = 2
D_EXPERT = 256
MOE_BLOCK = 256
RMS_EPS = 1e-6
WIDTH_A = N_HEADS_A * HEAD_DIM
KV_WIDTH = N_KV_GROUPS * HEAD_DIM
WIDTH_B = N_HEADS_B * HEAD_DIM_B
C_RWKV = 3 * WIDTH_B + W_LORA + A_LORA + G_LORA

LANES = 128
CHUNK = 64
SUB = 16
KEY_TILE = 512
VMEM_LIMIT = 56 * 1024 * 1024

HI = lax.Precision.HIGHEST


def _mm(a, b):
    return jnp.dot(a.astype(BF16), b.astype(BF16), preferred_element_type=F32)


def _mm_nt(a, b):
    return lax.dot_general(a.astype(BF16), b.astype(BF16), (((1,), (1,)), ((), ())), preferred_element_type=F32)


def _mm_tn(a, b):
    return lax.dot_general(a.astype(BF16), b.astype(BF16), (((0,), (0,)), ((), ())), preferred_element_type=F32)


def _mm_hi(a, b):
    return jnp.dot(a, b, preferred_element_type=F32, precision=HI)


def _iota(shape, dim):
    return lax.broadcasted_iota(I32, shape, dim)


def _params(sem):
    return pltpu.CompilerParams(dimension_semantics=sem, vmem_limit_bytes=VMEM_LIMIT)


def _full(shape):
    nd = len(shape)
    return pl.BlockSpec(shape, lambda *_: (0,) * nd)


def _inproj_kernel(x_ref, g_ref, cs_ref, sn_ref, wr_ref, ws_ref, wv_ref, wn_ref, ww_ref, wg_ref,
                   q_ref, kc_ref, ksl_ref, kwn_ref, vc_ref, vsl_ref, vwn_ref, ng_ref, prw_ref, ga_ref, gb_ref):
    x = x_ref[...]
    h = (x * lax.rsqrt(jnp.mean(x * x, axis=-1, keepdims=True) + RMS_EPS) * g_ref[...]).astype(BF16)
    n_rope = wr_ref.shape[1] // LANES
    cs = jnp.concatenate([cs_ref[...]] * n_rope, axis=1)
    sn = jnp.concatenate([sn_ref[...]] * n_rope, axis=1)
    ro = (jnp.dot(h, wr_ref[...], preferred_element_type=F32) * cs
          + jnp.dot(h, ws_ref[...], preferred_element_type=F32) * sn)
    q_ref[...] = (ro[:, 0:WIDTH_A] * (HEAD_DIM ** -0.5)).astype(BF16)
    o = WIDTH_A
    kc_ref[...] = ro[:, o:o + KV_WIDTH]
    ksl_ref[...] = ro[:, o + KV_WIDTH:o + 2 * KV_WIDTH].astype(BF16)
    kwn_ref[...] = ro[:, o + 2 * KV_WIDTH:o + 3 * KV_WIDTH].astype(BF16)
    v = jnp.dot(h, wv_ref[...], preferred_element_type=F32)
    vc_ref[...] = v[:, 0:KV_WIDTH]
    vsl_ref[...] = v[:, KV_WIDTH:2 * KV_WIDTH].astype(BF16)
    vwn_ref[...] = v[:, 2 * KV_WIDTH:3 * KV_WIDTH].astype(BF16)
    ng_ref[...] = jnp.dot(h, wn_ref[...], preferred_element_type=F32)
    prw_ref[...] = jnp.dot(h, ww_ref[...], preferred_element_type=F32)
    gg = jnp.dot(h, wg_ref[...], preferred_element_type=F32)
    ga_ref[...] = gg[:, 0:D_MODEL]
    gb_ref[...] = gg[:, D_MODEL:2 * D_MODEL]


def _inproj(x2, g_mix, cs, sn, w_in, tm=256):
    n = x2.shape[0]
    o = 0
    cols = {}
    for name, wd in (("q", WIDTH_A), ("kc", KV_WIDTH), ("vc", KV_WIDTH), ("ksl", KV_WIDTH), ("vsl", KV_WIDTH),
                     ("kwn", KV_WIDTH), ("vwn", KV_WIDTH), ("ng", 3 * N_HEADS_A), ("rw", C_RWKV),
                     ("ga", D_MODEL), ("gb", D_MODEL)):
        cols[name] = w_in[:, o:o + wd]
        o += wd
    w_rope = jnp.concatenate([cols["q"], cols["kc"], cols["ksl"], cols["kwn"]], axis=1)
    n_rope = w_rope.shape[1]
    w_swap = w_rope.reshape(D_MODEL, n_rope // HEAD_DIM, 2, HEAD_DIM // 2)[:, :, ::-1, :].reshape(D_MODEL, n_rope)
    w_v = jnp.concatenate([cols["vc"], cols["vsl"], cols["vwn"]], axis=1)
    w_ng = jnp.pad(cols["ng"], ((0, 0), (0, LANES - 3 * N_HEADS_A)))
    w_gate = jnp.concatenate([cols["ga"], cols["gb"]], axis=1)
    ws = [w.astype(BF16) for w in (w_rope, w_swap, w_v, w_ng, cols["rw"], w_gate)]
    row = lambda wd: pl.BlockSpec((tm, wd), lambda i: (i, 0))
    outs = [(WIDTH_A, BF16), (KV_WIDTH, F32), (KV_WIDTH, BF16), (KV_WIDTH, BF16), (KV_WIDTH, F32), (KV_WIDTH, BF16),
            (KV_WIDTH, BF16), (LANES, F32), (C_RWKV, F32), (D_MODEL, F32), (D_MODEL, F32)]
    return pl.pallas_call(
        _inproj_kernel,
        grid=(n // tm,),
        in_specs=[row(D_MODEL), _full((1, D_MODEL)), row(LANES), row(LANES)] + [_full(w.shape) for w in ws],
        out_specs=[row(wd) for wd, _ in outs],
        out_shape=[jax.ShapeDtypeStruct((n, wd), dt) for wd, dt in outs],
        compiler_params=_params(("parallel",)),
        name="inproj",
    )(x2, g_mix.reshape(1, D_MODEL), cs, sn, *ws)


def _compress_kernel(c_ref, pe_ref, w1_ref, b1_ref, w2_ref, b2_ref, o_ref):
    half = D_CMP * HEAD_DIM
    c = c_ref[0, 0]
    w1 = w1_ref[...]
    z1 = _mm(c, w1[0:half])
    z2 = _mm(c, w1[half:2 * half])
    z2 = jnp.concatenate([z2[1:], z2[:1]], axis=0)
    pb = _mm(pe_ref[...], w1)[0:1] + b1_ref[...]
    hid = jax.nn.gelu(z1 + z2 + pb)
    o_ref[0, 0] = _mm(hid, w2_ref[...]) + b2_ref[...]


def _compress(kv, pe, w1, b1, w2, b2, bsz, s):
    nch = s // D_CMP
    c = kv.reshape(bsz, nch, D_CMP, N_KV_GROUPS, HEAD_DIM).transpose(0, 3, 1, 2, 4).reshape(
        bsz, N_KV_GROUPS, nch, D_CMP * HEAD_DIM)
    pe8 = jnp.broadcast_to(pe.reshape(1, L_CMP * HEAD_DIM), (8, L_CMP * HEAD_DIM))
    return pl.pallas_call(
        _compress_kernel,
        grid=(bsz, N_KV_GROUPS),
        in_specs=[pl.BlockSpec((1, 1, nch, D_CMP * HEAD_DIM), lambda b, g: (b, g, 0, 0)),
                  _full(pe8.shape), _full(w1.shape), _full((1, CMP_HIDDEN)), _full(w2.shape), _full((1, HEAD_DIM))],
        out_specs=pl.BlockSpec((1, 1, nch, HEAD_DIM), lambda b, g: (b, g, 0, 0)),
        out_shape=jax.ShapeDtypeStruct((bsz, N_KV_GROUPS, nch, HEAD_DIM), F32),
        compiler_params=_params(("parallel", "parallel")),
        name="compress",
    )(c, pe8, w1, b1.reshape(1, CMP_HIDDEN), w2, b2.reshape(1, HEAD_DIM))


def _softmax_rows(s, mask):
    s = jnp.where(mask, s, NEG_INF)
    m = jnp.max(s, axis=-1, keepdims=True)
    e = jnp.where(mask, jnp.exp(s - m), 0.0)
    l = jnp.sum(e, axis=-1, keepdims=True)
    return e * (1.0 / jnp.where(l > 0.0, l, 1.0))


def _nsa_kernel(q_ref, kc_ref, vc_ref, ksl_ref, vsl_ref, kwn_ref, vwn_ref, ng_ref, ov_ref, o_ref, *, n_top):
    qb = q_ref[0]
    n_cmp = kc_ref.shape[2]
    n_slc = ov_ref.shape[1]
    s0 = pl.program_id(1) * Q_BLOCK
    hq = HEADS_PER_GROUP * Q_BLOCK
    t_row = s0 + (_iota((hq, 1), 0) & (Q_BLOCK - 1))
    t_q = s0 + _iota((Q_BLOCK, 1), 0)
    sig = jax.nn.sigmoid(ng_ref[0])
    win_start = pl.multiple_of(jnp.maximum(s0 - WINDOW, 0), LANES)
    n_tiles = s0 // KEY_TILE + 1
    blocks_per_tile = KEY_TILE // L_SLC
    heads_out = []
    for g in range(N_KV_GROUPS):
        lo = g * HEAD_DIM
        qg = jnp.concatenate(
            [qb[:, (g * HEADS_PER_GROUP + h) * HEAD_DIM:(g * HEADS_PER_GROUP + h + 1) * HEAD_DIM]
             for h in range(HEADS_PER_GROUP)], axis=0)

        s_c = _mm_nt(qg, kc_ref[0, g])
        valid_c = (_iota((1, n_cmp), 1) * D_CMP + (L_CMP - 1)) <= t_row
        p_c = _softmax_rows(s_c, valid_c)
        o_c = _mm(p_c, vc_ref[0, g])

        p_sum = p_c[0:Q_BLOCK]
        for h in range(1, HEADS_PER_GROUP):
            p_sum = p_sum + p_c[h * Q_BLOCK:(h + 1) * Q_BLOCK]
        imp_t = _mm_hi(p_sum, ov_ref[...]).T
        j = _iota((n_slc, Q_BLOCK), 0)
        cur = (s0 + _iota((n_slc, Q_BLOCK), 1)) >> 6
        forced = (j == 0) | (j == cur) | (j == cur - 1)
        score = jnp.where(forced, FORCE_SCORE, jnp.where(j <= cur, imp_t, -1.0))
        sel_t = jnp.zeros((n_slc, Q_BLOCK), F32)
        for _ in range(n_top):
            m = jnp.max(score, axis=0, keepdims=True)
            first = jnp.min(jnp.where(score == m, j, n_slc), axis=0, keepdims=True)
            hit = j == first
            sel_t = jnp.where(hit, 1.0, sel_t)
            score = jnp.where(hit, -3e38, score)
        sel = sel_t.T.astype(BF16)

        def sel_step(kt, carry):
            m_i, l_i, acc = carry
            k0 = pl.multiple_of(kt * KEY_TILE, KEY_TILE)
            kk = ksl_ref[0, pl.ds(k0, KEY_TILE), lo:lo + HEAD_DIM]
            vv = vsl_ref[0, pl.ds(k0, KEY_TILE), lo:lo + HEAD_DIM]
            s = _mm_nt(qg, kk)
            expand = (_iota((n_slc, KEY_TILE), 0)
                      == kt * blocks_per_tile + (_iota((n_slc, KEY_TILE), 1) >> 6)).astype(BF16)
            chosen = jnp.dot(sel, expand, preferred_element_type=F32)
            mask = (chosen > 0.5) & ((k0 + _iota((Q_BLOCK, KEY_TILE), 1)) <= t_q)
            mask = jnp.concatenate([mask] * HEADS_PER_GROUP, axis=0)
            s = jnp.where(mask, s, NEG_INF)
            m_new = jnp.maximum(m_i, jnp.max(s, axis=-1, keepdims=True))
            alpha = jnp.exp(m_i - m_new)
            p = jnp.where(mask, jnp.exp(s - m_new), 0.0)
            l_new = alpha * l_i + jnp.sum(p, axis=-1, keepdims=True)
            acc = alpha * acc + _mm(p, vv)
            return m_new, l_new, acc

        init = (jnp.full((hq, 1), NEG_INF, F32), jnp.zeros((hq, 1), F32), jnp.zeros((hq, HEAD_DIM), F32))
        _, l_s, acc_s = lax.fori_loop(0, n_tiles, sel_step, init)
        o_s = acc_s * (1.0 / l_s)

        kw = kwn_ref[0, pl.ds(win_start, WINDOW + Q_BLOCK), lo:lo + HEAD_DIM]
        vw = vwn_ref[0, pl.ds(win_start, WINDOW + Q_BLOCK), lo:lo + HEAD_DIM]
        pos_w = win_start + _iota((1, WINDOW + Q_BLOCK), 1)
        valid_w = (pos_w <= t_row) & (pos_w > t_row - WINDOW)
        o_w = _mm(_softmax_rows(_mm_nt(qg, kw), valid_w), vw)

        for h in range(HEADS_PER_GROUP):
            hh = g * HEADS_PER_GROUP + h
            r = slice(h * Q_BLOCK, (h + 1) * Q_BLOCK)
            heads_out.append(sig[:, 3 * hh:3 * hh + 1] * o_c[r] + sig[:, 3 * hh + 1:3 * hh + 2] * o_s[r]
                             + sig[:, 3 * hh + 2:3 * hh + 3] * o_w[r])
    o_ref[0] = jnp.concatenate(heads_out, axis=1).astype(o_ref.dtype)


def _nsa(q, kc, vc, ksl, vsl, kwn, vwn, ng, bsz, s):
    n_cmp = s // D_CMP
    n_slc = s // L_SLC
    ii = np.arange(n_cmp)[:, None]
    jj = np.arange(n_slc)[None, :]
    overlap = jnp.asarray((ii * D_CMP < (jj + 1) * L_SLC) & (ii * D_CMP + L_CMP > jj * L_SLC), F32)
    seq = lambda wd: pl.BlockSpec((1, s, wd), lambda b, i: (b, 0, 0))
    blk = lambda wd: pl.BlockSpec((1, Q_BLOCK, wd), lambda b, i: (b, i, 0))
    cmp_spec = pl.BlockSpec((1, N_KV_GROUPS, n_cmp, HEAD_DIM), lambda b, i: (b, 0, 0, 0))
    r3 = lambda a: a.reshape(bsz, s, a.shape[-1])
    return pl.pallas_call(
        functools.partial(_nsa_kernel, n_top=min(N_SELECT, n_slc)),
        grid=(bsz, s // Q_BLOCK),
        in_specs=[blk(WIDTH_A), cmp_spec, cmp_spec, seq(KV_WIDTH), seq(KV_WIDTH), seq(KV_WIDTH), seq(KV_WIDTH),
                  blk(LANES), _full(overlap.shape)],
        out_specs=blk(WIDTH_A),
        out_shape=jax.ShapeDtypeStruct((bsz, s, WIDTH_A), BF16),
        compiler_params=_params(("parallel", "arbitrary")),
        name="nsa",
    )(r3(q), kc, vc, r3(ksl), r3(vsl), r3(kwn), r3(vwn), r3(ng), overlap)


def _rwkv_prep_kernel(p_ref, pv_ref, mu_ref, w0_ref, w2_ref, a0_ref, a2_ref, g2_ref, kk_ref, ka_ref, bd_ref,
                      r_ref, lw_ref, k_ref, v_ref, kn_ref, kb_ref, g_ref, *, tiles_per_seq):
    p = p_ref[...]
    first = (pl.program_id(0) % tiles_per_seq) == 0
    prev_last = jnp.where(first, 0.0, pv_ref[7:8, :])
    p_prev = jnp.concatenate([prev_last, p[:-1]], axis=0)
    xs = p + (p_prev - p) * mu_ref[...]
    wb = WIDTH_B
    r = xs[:, 0:wb]
    k = xs[:, wb:2 * wb]
    v = xs[:, 2 * wb:3 * wb]
    o = 3 * wb
    xw = xs[:, o:o + W_LORA]
    xa = xs[:, o + W_LORA:o + W_LORA + A_LORA]
    xg = xs[:, o + W_LORA + A_LORA:o + W_LORA + A_LORA + G_LORA]
    z = w0_ref[...] + _mm_hi(jnp.tanh(xw), w2_ref[...])
    w_log = -(jnp.maximum(-z, 0.0) + jnp.log(1.0 + jnp.exp(-jnp.abs(z)))) - 0.5
    a = jax.nn.sigmoid(a0_ref[...] + _mm_hi(xa, a2_ref[...]))
    kk = k * kk_ref[...]
    ss = _mm_hi(kk * kk, bd_ref[...])
    kn = kk * (1.0 / jnp.maximum(jnp.sqrt(ss), 1e-12))
    r_ref[...] = r
    lw_ref[...] = -jnp.exp(w_log)
    k_ref[...] = k * (1.0 + (a - 1.0) * ka_ref[...])
    v_ref[...] = v
    kn_ref[...] = kn
    kb_ref[...] = kn * a
    g_ref[...] = _mm(jax.nn.sigmoid(xg), g2_ref[...])


def _rwkv_prep(prw, mu, w0, w2, a0, a2, g2, k_k, k_a, s, tm=256):
    n = prw.shape[0]
    hid = np.arange(WIDTH_B) // HEAD_DIM_B
    bd = jnp.asarray(hid[:, None] == hid[None, :], F32)
    row = lambda wd: pl.BlockSpec((tm, wd), lambda i: (i, 0))
    vec = lambda a: a.reshape(1, -1)
    return pl.pallas_call(
        functools.partial(_rwkv_prep_kernel, tiles_per_seq=s // tm),
        grid=(n // tm,),
        in_specs=[row(C_RWKV), pl.BlockSpec((8, C_RWKV), lambda i: (jnp.maximum(i * (tm // 8) - 1, 0), 0)),
                  _full((1, C_RWKV)), _full((1, WIDTH_B)), _full(w2.shape), _full((1, WIDTH_B)), _full(a2.shape),
                  _full(g2.shape), _full((1, WIDTH_B)), _full((1, WIDTH_B)), _full(bd.shape)],
        out_specs=[row(WIDTH_B)] * 7,
        out_shape=[jax.ShapeDtypeStruct((n, WIDTH_B), F32)] * 7,
        compiler_params=_params(("parallel",)),
        name="rwkv_prep",
    )(prw, prw, vec(mu), vec(w0), w2, vec(a0), a2, g2, vec(k_k), vec(k_a), bd)


def _rwkv_chunk_kernel(r_ref, lw_ref, k_ref, v_ref, kn_ref, kb_ref, rk_ref,
                       qh_ref, y0_ref, bonus_ref, g_ref, h_ref):
    L = CHUNK
    hd = HEAD_DIM_B
    lw = lw_ref[0]
    ti = _iota((L, L), 0)
    si = _iota((L, L), 1)
    cs = _mm_hi((si <= ti).astype(F32), lw)
    p_in = jnp.exp(cs)
    p_inv = jnp.exp(-cs)
    p_ex = jnp.exp(cs - lw)
    p_end = jnp.exp(cs[L - 1:L, :] - cs)
    p_tot = jnp.exp(cs[L - 1:L, :])
    r = r_ref[0]
    k = k_ref[0]
    v = v_ref[0]
    at = -kn_ref[0] * p_ex
    bt = kb_ref[0] * p_inv
    kt = k * p_inv
    rt = r * p_in
    bl = kb_ref[0] * p_end
    kl = k * p_end
    rkr = r * k * rk_ref[...]
    strict = si < ti
    incl = si <= ti
    same_sub = (ti // SUB) == (si // SUB)
    eye = (ti == si).astype(F32)
    qh, y0, bonus, gs, hs = [], [], [], [], []
    for h in range(N_HEADS_B):
        c = slice(h * hd, (h + 1) * hd)
        x1 = jnp.concatenate([at[:, c], rt[:, c]], axis=0)
        x2 = jnp.concatenate([bt[:, c], kt[:, c]], axis=0)
        aa = _mm_nt(x1, x2)
        a_ab = jnp.where(strict, aa[0:L, 0:L], 0.0)
        a_ak = jnp.where(strict, aa[0:L, L:2 * L], 0.0)
        a_rb = jnp.where(incl, aa[L:2 * L, 0:L], 0.0)
        a_rk = jnp.where(incl, aa[L:2 * L, L:2 * L], 0.0)
        ad = jnp.where(same_sub, a_ab, 0.0)
        t = eye + ad
        pw = ad
        for _ in range(3):
            pw = _mm(pw, pw)
            t = t + _mm(t, pw)
        width = SUB
        while width < L:
            off = jnp.where(((ti // width) == (si // width) + 1) & ((ti // (2 * width)) == (si // (2 * width))),
                            a_ab, 0.0)
            t = t + _mm(t, _mm(off, t))
            width *= 2
        av = _mm(jnp.concatenate([a_ak, a_rk], axis=0), v[:, c])
        wu = _mm(t, jnp.concatenate([at[:, c], av[0:L]], axis=1))
        qy = jnp.concatenate([rt[:, c], av[L:2 * L]], axis=1) + _mm(a_rb, wu)
        gh = _mm_tn(bl[:, c], wu) + jnp.concatenate([eye * p_tot[:, c], _mm_tn(kl[:, c], v[:, c])], axis=1)
        qh.append(qy[:, 0:hd])
        y0.append(qy[:, hd:2 * hd])
        gs.append(gh[:, 0:hd])
        hs.append(gh[:, hd:2 * hd])
        bonus.append(jnp.sum(rkr[:, c], axis=-1, keepdims=True) * v[:, c])
    qh_ref[0] = jnp.concatenate(qh, axis=1)
    y0_ref[0] = jnp.concatenate(y0, axis=1)
    bonus_ref[0] = jnp.concatenate(bonus, axis=1)
    g_ref[0, 0] = jnp.concatenate(gs, axis=1)
    h_ref[0, 0] = jnp.concatenate(hs, axis=1)


def _rwkv_chunk(r, lw, k, v, kn, kb, r_k, bsz, s):
    nch = s // CHUNK
    blk = pl.BlockSpec((1, CHUNK, WIDTH_B), lambda b, c: (b, c, 0))
    mat = pl.BlockSpec((1, 1, HEAD_DIM_B, WIDTH_B), lambda b, c: (b, c, 0, 0))
    r3 = lambda a: a.reshape(bsz, s, WIDTH_B)
    seq_sds = jax.ShapeDtypeStruct((bsz, s, WIDTH_B), F32)
    mat_sds = jax.ShapeDtypeStruct((bsz, nch, HEAD_DIM_B, WIDTH_B), F32)
    return pl.pallas_call(
        _rwkv_chunk_kernel,
        grid=(bsz, nch),
        in_specs=[blk] * 6 + [_full((1, WIDTH_B))],
        out_specs=[blk, blk, blk, mat, mat],
        out_shape=[seq_sds, seq_sds, seq_sds, mat_sds, mat_sds],
        compiler_params=_params(("parallel", "parallel")),
        name="rwkv_chunk",
    )(r3(r), r3(lw), r3(k), r3(v), r3(kn), r3(kb), r_k.reshape(1, WIDTH_B))


def _rwkv_scan_kernel(qh_ref, y0_ref, g_ref, h_ref, yn_ref, st_ref):
    hd = HEAD_DIM_B

    @pl.when(pl.program_id(0) == 0)
    def _():
        st_ref[...] = jnp.zeros_like(st_ref)

    for b in range(qh_ref.shape[0]):
        ys = []
        for h in range(N_HEADS_B):
            c = slice(h * hd, (h + 1) * hd)
            st = st_ref[b * N_HEADS_B + h]
            y = _mm_hi(qh_ref[b, :, c], st) + y0_ref[b, :, c]
            st_ref[b * N_HEADS_B + h] = _mm_hi(g_ref[b, 0, :, c], st) + h_ref[b, 0, :, c]
            mean = jnp.mean(y, axis=-1, keepdims=True)
            d = y - mean
            var = jnp.mean(d * d, axis=-1, keepdims=True)
            ys.append(d * lax.rsqrt(var + GN_EPS))
        yn_ref[b] = jnp.concatenate(ys, axis=1)


def _rwkv_scan(qh, y0, g, h, bsz, s):
    nch = s // CHUNK
    blk = pl.BlockSpec((bsz, CHUNK, WIDTH_B), lambda c: (0, c, 0))
    mat = pl.BlockSpec((bsz, 1, HEAD_DIM_B, WIDTH_B), lambda c: (0, c, 0, 0))
    return pl.pallas_call(
        _rwkv_scan_kernel,
        grid=(nch,),
        in_specs=[blk, blk, mat, mat],
        out_specs=blk,
        out_shape=jax.ShapeDtypeStruct((bsz, s, WIDTH_B), F32),
        scratch_shapes=[pltpu.VMEM((bsz * N_HEADS_B, HEAD_DIM_B, HEAD_DIM_B), F32)],
        compiler_params=_params(("arbitrary",)),
        name="rwkv_scan",
    )(qh, y0, g, h)


def _merge_kernel(x_ref, oa_ref, yn_ref, bonus_ref, g_ref, ga_ref, gb_ref, lnw_ref, lnb_ref, wa_ref, wb_ref,
                  wo_ref, gf_ref, wr_ref, br_ref, x1_ref, h2_ref, route_ref):
    ob = (yn_ref[...] * lnw_ref[...] + lnb_ref[...] + bonus_ref[...]) * g_ref[...]
    merged = (jax.nn.sigmoid(ga_ref[...]) * jnp.dot(oa_ref[...], wa_ref[...], preferred_element_type=F32)
              + jax.nn.sigmoid(gb_ref[...]) * _mm(ob, wb_ref[...]))
    x1 = x_ref[...] + _mm(merged, wo_ref[...])
    x1_ref[...] = x1
    h2 = x1 * lax.rsqrt(jnp.mean(x1 * x1, axis=-1, keepdims=True) + RMS_EPS) * gf_ref[...]
    h2_ref[...] = h2
    logits = _mm_hi(h2, wr_ref[...]) + br_ref[...]
    lane = _iota(logits.shape, 1)
    big = 4 * LANES
    is_grp = lane < N_GROUPS
    lg = jnp.where(is_grp, logits, NEG_INF)
    mg = jnp.max(lg, axis=-1, keepdims=True)
    gidx = jnp.min(jnp.where(lg == mg, lane, big), axis=-1, keepdims=True)
    pg = 1.0 / jnp.sum(jnp.where(is_grp, jnp.exp(lg - mg), 0.0), axis=-1, keepdims=True)
    in_grp = (lane >= N_GROUPS) & (((lane - N_GROUPS) >> 3) == gidx) & (lane < N_GROUPS + N_EXPERTS)
    le = jnp.where(in_grp, logits, NEG_INF)
    m1 = jnp.max(le, axis=-1, keepdims=True)
    i1 = jnp.min(jnp.where(le == m1, lane, big), axis=-1, keepdims=True)
    le2 = jnp.where(lane == i1, NEG_INF, le)
    m2 = jnp.max(le2, axis=-1, keepdims=True)
    i2 = jnp.min(jnp.where(le2 == m2, lane, big), axis=-1, keepdims=True)
    e2 = jnp.exp(m2 - m1)
    inv = pg / (1.0 + e2)
    route_ref[...] = jnp.where(lane == 0, (i1 - N_GROUPS).astype(F32),
                               jnp.where(lane == 1, (i2 - N_GROUPS).astype(F32),
                                         jnp.where(lane == 2, inv, jnp.where(lane == 3, inv * e2, 0.0))))


def _merge(x2, oa, yn, bonus, g, ga, gb, ln_w, ln_b, w_a, w_b, w_o, g_ffn, w_grp, b_grp, w_exp, b_exp, tm=256):
    n = x2.shape[0]
    w_r = jnp.pad(jnp.concatenate([w_grp, w_exp], axis=1), ((0, 0), (0, LANES - N_GROUPS - N_EXPERTS)))
    b_r = jnp.pad(jnp.concatenate([b_grp, b_exp]), (0, LANES - N_GROUPS - N_EXPERTS)).reshape(1, LANES)
    row = lambda wd: pl.BlockSpec((tm, wd), lambda i: (i, 0))
    vec = lambda a: a.reshape(1, -1)
    wa, wb, wo = w_a.astype(BF16), w_b.astype(BF16), w_o.astype(BF16)
    return pl.pallas_call(
        _merge_kernel,
        grid=(n // tm,),
        in_specs=[row(D_MODEL), row(WIDTH_A), row(WIDTH_B), row(WIDTH_B), row(WIDTH_B), row(D_MODEL), row(D_MODEL),
                  _full((1, WIDTH_B)), _full((1, WIDTH_B)), _full(wa.shape), _full(wb.shape), _full(wo.shape),
                  _full((1, D_MODEL)), _full(w_r.shape), _full((1, LANES))],
        out_specs=[row(D_MODEL), row(D_MODEL), row(LANES)],
        out_shape=[jax.ShapeDtypeStruct((n, D_MODEL), F32), jax.ShapeDtypeStruct((n, D_MODEL), F32),
                   jax.ShapeDtypeStruct((n, LANES), F32)],
        compiler_params=_params(("parallel",)),
        name="merge",
    )(x2, oa, yn, bonus, g, ga, gb, vec(ln_w), vec(ln_b), wa, wb, wo, vec(g_ffn), w_r, b_r)


def _gather_start(idx_ref, src_hbm, buf, sem, slot, rows):
    def body(r, carry):
        pltpu.make_async_copy(src_hbm.at[pl.ds(idx_ref[0, 0, r], 1)], buf.at[slot, pl.ds(r, 1)], sem.at[slot]).start()
        return carry
    lax.fori_loop(0, rows, body, 0, unroll=8)


def _gather_wait(src_hbm, buf, sem, slot, rows):
    pltpu.make_async_copy(src_hbm.at[pl.ds(0, rows)], buf.at[slot], sem.at[slot]).wait()


def _gather_pipeline(idx_ref, idx_next_ref, src_hbm, buf, sem, rows):
    i = pl.program_id(0)
    slot = i % 2

    @pl.when(i == 0)
    def _():
        _gather_start(idx_ref, src_hbm, buf, sem, 0, rows)

    @pl.when(i + 1 < pl.num_programs(0))
    def _():
        _gather_start(idx_next_ref, src_hbm, buf, sem, 1 - slot, rows)

    _gather_wait(src_hbm, buf, sem, slot, rows)
    return slot


def _moe_kernel(be_ref, tok_ref, tok_next_ref, gate_ref, h2_hbm, w13_ref, w2_ref, y_ref, buf, sem):
    slot = _gather_pipeline(tok_ref, tok_next_ref, h2_hbm, buf, sem, MOE_BLOCK)
    xb = buf[slot]
    a = _mm(xb, w13_ref[0])
    hid = jax.nn.silu(a[:, 0:D_EXPERT]) * a[:, D_EXPERT:2 * D_EXPERT]
    y_ref[...] = _mm(hid, w2_ref[0]) * gate_ref[...]


def _moe(h2, blk_expert, buf_tok, buf_gate, w13, w2):
    n_pad = buf_tok.shape[0]
    n_blk = n_pad // MOE_BLOCK
    tok3 = buf_tok.reshape(n_blk, 1, MOE_BLOCK)
    smem_blk = lambda f: pl.BlockSpec((1, 1, MOE_BLOCK), f, memory_space=pltpu.SMEM)
    grid_spec = pltpu.PrefetchScalarGridSpec(
        num_scalar_prefetch=1,
        grid=(n_blk,),
        in_specs=[smem_blk(lambda i, be: (i, 0, 0)),
                  smem_blk(lambda i, be: (jnp.minimum(i + 1, n_blk - 1), 0, 0)),
                  pl.BlockSpec((MOE_BLOCK, 1), lambda i, be: (i, 0)),
                  pl.BlockSpec(memory_space=pl.ANY),
                  pl.BlockSpec((1, D_MODEL, 2 * D_EXPERT), lambda i, be: (be[i], 0, 0)),
                  pl.BlockSpec((1, D_EXPERT, D_MODEL), lambda i, be: (be[i], 0, 0))],
        out_specs=pl.BlockSpec((MOE_BLOCK, D_MODEL), lambda i, be: (i, 0)),
        scratch_shapes=[pltpu.VMEM((2, MOE_BLOCK, D_MODEL), F32), pltpu.SemaphoreType.DMA((2,))],
    )
    return pl.pallas_call(
        _moe_kernel,
        grid_spec=grid_spec,
        out_shape=jax.ShapeDtypeStruct((n_pad, D_MODEL), F32),
        compiler_params=_params(("arbitrary",)),
        name="moe",
    )(blk_expert, tok3, tok3, buf_gate.reshape(n_pad, 1), h2, w13, w2)


def _final_kernel(pos_ref, pos_next_ref, x1_ref, g_ref, y_hbm, o_ref, buf, sem, *, tm):
    slot = _gather_pipeline(pos_ref, pos_next_ref, y_hbm, buf, sem, 2 * tm)
    x = x1_ref[...] + buf[slot, 0:tm] + buf[slot, tm:2 * tm]
    o_ref[...] = x * lax.rsqrt(jnp.mean(x * x, axis=-1, keepdims=True) + RMS_EPS) * g_ref[...]


def _final(x1, y_pad, pos, g_final, tm=256):
    n = x1.shape[0]
    nt = n // tm
    pos3 = pos.reshape(nt, tm, TOP_K).transpose(0, 2, 1).reshape(nt, 1, TOP_K * tm)
    smem_blk = lambda f: pl.BlockSpec((1, 1, TOP_K * tm), f, memory_space=pltpu.SMEM)
    return pl.pallas_call(
        functools.partial(_final_kernel, tm=tm),
        grid=(nt,),
        in_specs=[smem_blk(lambda i: (i, 0, 0)), smem_blk(lambda i: (jnp.minimum(i + 1, nt - 1), 0, 0)),
                  pl.BlockSpec((tm, D_MODEL), lambda i: (i, 0)), _full((1, D_MODEL)),
                  pl.BlockSpec(memory_space=pl.ANY)],
        out_specs=pl.BlockSpec((tm, D_MODEL), lambda i: (i, 0)),
        out_shape=jax.ShapeDtypeStruct((n, D_MODEL), F32),
        scratch_shapes=[pltpu.VMEM((2, TOP_K * tm, D_MODEL), F32), pltpu.SemaphoreType.DMA((2,))],
        compiler_params=_params(("arbitrary",)),
        name="final",
    )(pos3, pos3, x1, g_final.reshape(1, D_MODEL), y_pad)


def _dispatch_plan(route, n_tok):
    e_flat = route[:, 0:TOP_K].astype(I32).reshape(n_tok * TOP_K)
    gate_flat = route[:, TOP_K:2 * TOP_K].reshape(n_tok * TOP_K)
    onehot = (e_flat[:, None] == jnp.arange(N_EXPERTS, dtype=I32)[None, :]).astype(I32)
    csum = jnp.cumsum(onehot, axis=0)
    counts = csum[-1]
    rank = jnp.sum(jnp.where(onehot > 0, csum - 1, 0), axis=1)
    padded = (counts + MOE_BLOCK - 1) // MOE_BLOCK * MOE_BLOCK
    pad_end = jnp.cumsum(padded)
    pad_start = pad_end - padded
    dest = pad_start[e_flat] + rank
    n_pad = n_tok * TOP_K + N_EXPERTS * MOE_BLOCK
    tok_flat = jnp.repeat(jnp.arange(n_tok, dtype=I32), TOP_K)
    buf_tok = jnp.zeros((n_pad,), I32).at[dest].set(tok_flat)
    buf_gate = jnp.zeros((n_pad,), F32).at[dest].set(gate_flat)
    n_blk = n_pad // MOE_BLOCK
    blk_expert = jnp.minimum(jnp.searchsorted(pad_end, jnp.arange(n_blk, dtype=I32) * MOE_BLOCK, side="right"),
                             N_EXPERTS - 1).astype(I32)
    return buf_tok, buf_gate, blk_expert, dest.reshape(n_tok, TOP_K)


def kernel(x, positions, g_mix, w_in, cmp_pe_k, cmp_w1_k, cmp_b1_k, cmp_w2_k, cmp_b2_k, cmp_pe_v, cmp_w1_v, cmp_b1_v, cmp_w2_v, cmp_b2_v, rwkv_mu, rwkv_w0, rwkv_w2, rwkv_a0, rwkv_a2, rwkv_g2, rwkv_k_k, rwkv_k_a, rwkv_r_k, rwkv_ln_w, rwkv_ln_b, w_proj_a, w_proj_b, w_out, g_ffn, w_grp, b_grp, w_exp, b_exp, e_w1, e_w3, e_w2, g_final):
    bsz, s, _ = x.shape
    assert w_in.shape[0] == 1, "single-layer stack"
    n_tok = bsz * s
    x2 = x.reshape(n_tok, D_MODEL)
    half = HEAD_DIM // 2
    freqs = jnp.power(ROPE_THETA, -jnp.arange(half, dtype=F32) / half)
    ang = positions.astype(F32).reshape(n_tok, 1) * freqs
    cs = jnp.tile(jnp.cos(ang), (1, 4))
    sn = jnp.tile(jnp.concatenate([-jnp.sin(ang), jnp.sin(ang)], axis=1), (1, 2))
    q, kc, ksl, kwn, vc, vsl, vwn, ng, prw, ga, gb = _inproj(x2, g_mix[0], cs, sn, w_in[0])
    k_c = _compress(kc, cmp_pe_k[0], cmp_w1_k[0], cmp_b1_k[0], cmp_w2_k[0], cmp_b2_k[0], bsz, s)
    v_c = _compress(vc, cmp_pe_v[0], cmp_w1_v[0], cmp_b1_v[0], cmp_w2_v[0], cmp_b2_v[0], bsz, s)
    o_a = _nsa(q, k_c, v_c, ksl, vsl, kwn, vwn, ng, bsz, s).reshape(n_tok, WIDTH_A)
    r, lw, k, v, kn, kb, g = _rwkv_prep(prw, rwkv_mu[0], rwkv_w0[0], rwkv_w2[0], rwkv_a0[0], rwkv_a2[0],
                                        rwkv_g2[0], rwkv_k_k[0], rwkv_k_a[0], s)
    qh, y0, bonus, gm, hm = _rwkv_chunk(r, lw, k, v, kn, kb, rwkv_r_k[0], bsz, s)
    yn = _rwkv_scan(qh, y0, gm, hm, bsz, s)
    x1, h2, route = _merge(x2, o_a, yn.reshape(n_tok, WIDTH_B), bonus.reshape(n_tok, WIDTH_B), g, ga, gb,
                           rwkv_ln_w[0], rwkv_ln_b[0], w_proj_a[0], w_proj_b[0], w_out[0], g_ffn[0],
                           w_grp[0], b_grp[0], w_exp[0], b_exp[0])
    buf_tok, buf_gate, blk_expert, pos = _dispatch_plan(route, n_tok)
    w13 = jnp.concatenate([e_w1[0], e_w3[0]], axis=-1).astype(BF16)
    y_pad = _moe(h2, blk_expert, buf_tok, buf_gate, w13, e_w2[0].astype(BF16))
    out = _final(x1, y_pad, pos, g_final)
    return out.reshape(bsz, s, D_MODEL)
```

```python
import functools

import numpy as np
import jax
import jax.numpy as jnp
from jax import lax
from jax.experimental import pallas as pl
from jax.experimental.pallas import tpu as pltpu

F32 = jnp.float32
BF16 = jnp.bfloat16
I32 = jnp.int32

D_MODEL = 1024
N_HEADS_A = 8
N_KV_GROUPS = 2
HEADS_PER_GROUP = N_HEADS_A // N_KV_GROUPS
HEAD_DIM = 64
L_CMP = 32
D_CMP = 16
CMP_HIDDEN = 256
L_SLC = 64
N_SELECT = 16
WINDOW = 512
Q_BLOCK = 128
ROPE_THETA = 10000.0
FORCE_SCORE = 1e4
NEG_INF = -1e30
N_HEADS_B = 8
HEAD_DIM_B = 64
W_LORA = 64
A_LORA = 64
G_LORA = 128
GN_EPS = 64e-5
N_GROUPS = 4
EXPERTS_PER_GROUP = 8
N_EXPERTS = N_GROUPS * EXPERTS_PER_GROUP
TOP_K = 2
D_EXPERT = 256
MOE_BLOCK = 256
RMS_EPS = 1e-6
WIDTH_A = N_HEADS_A * HEAD_DIM
KV_WIDTH = N_KV_GROUPS * HEAD_DIM
WIDTH_B = N_HEADS_B * HEAD_DIM_B
C_RWKV = 3 * WIDTH_B + W_LORA + A_LORA + G_LORA

LANES = 128
CHUNK = 64
SUB = 16
KEY_TILE = 1024
BLOCKS_PER_TILE = KEY_TILE // L_SLC
VMEM_LIMIT = 56 * 1024 * 1024

HI = lax.Precision.HIGHEST


def _mm(a, b):
    return jnp.dot(a.astype(BF16), b.astype(BF16), preferred_element_type=F32)


def _mm_nt(a, b):
    return lax.dot_general(a.astype(BF16), b.astype(BF16), (((1,), (1,)), ((), ())), preferred_element_type=F32)


def _mm_tn(a, b):
    return lax.dot_general(a.astype(BF16), b.astype(BF16), (((0,), (0,)), ((), ())), preferred_element_type=F32)


def _mm_hi(a, b):
    return jnp.dot(a, b, preferred_element_type=F32, precision=HI)


def _iota(shape, dim):
    return lax.broadcasted_iota(I32, shape, dim)


def _params(sem):
    return pltpu.CompilerParams(dimension_semantics=sem, vmem_limit_bytes=VMEM_LIMIT)


def _full(shape):
    nd = len(shape)
    return pl.BlockSpec(shape, lambda *_: (0,) * nd)


Q_PAD = N_HEADS_A * LANES
KV_PAD = N_KV_GROUPS * LANES


def _inproj_kernel(x_ref, g_ref, cs_ref, sn_ref, wr_ref, ws_ref, wv_ref, wn_ref, ww_ref, wg_ref,
                   q_ref, kc_ref, ksl_ref, kwn_ref, vc_ref, vsl_ref, vwn_ref, ng_ref, prw_ref, ga_ref, gb_ref):
    x = x_ref[...]
    h = (x * lax.rsqrt(jnp.mean(x * x, axis=-1, keepdims=True) + RMS_EPS) * g_ref[...]).astype(BF16)
    n_rope = wr_ref.shape[1] // LANES
    cs = jnp.concatenate([cs_ref[...]] * n_rope, axis=1)
    sn = jnp.concatenate([sn_ref[...]] * n_rope, axis=1)
    ro = (jnp.dot(h, wr_ref[...], preferred_element_type=F32) * cs
          + jnp.dot(h, ws_ref[...], preferred_element_type=F32) * sn)
    q_ref[...] = (ro[:, 0:Q_PAD] * (HEAD_DIM ** -0.5)).astype(BF16)
    o = Q_PAD
    kc_ref[...] = ro[:, o:o + KV_WIDTH]
    tm = x_ref.shape[0]
    blk = ((pl.program_id(0) * tm + _iota((tm, 1), 0)) >> 6) & (BLOCKS_PER_TILE - 1)
    code = jnp.where((_iota((1, KV_PAD), 1) & (LANES - 1)) == HEAD_DIM + blk, NEG_INF, 0.0)
    ksl_ref[...] = (ro[:, o + KV_WIDTH:o + KV_WIDTH + KV_PAD] + code).astype(BF16)
    kwn_ref[...] = ro[:, o + KV_WIDTH + KV_PAD:o + KV_WIDTH + 2 * KV_PAD].astype(BF16)
    v = jnp.dot(h, wv_ref[...], preferred_element_type=F32)
    vc_ref[...] = v[:, 0:KV_WIDTH]
    ones = ((_iota((1, 2 * KV_PAD), 1) & (LANES - 1)) >= HEAD_DIM).astype(F32)
    vp = v[:, KV_WIDTH:KV_WIDTH + 2 * KV_PAD] + ones
    vsl_ref[...] = vp[:, 0:KV_PAD].astype(BF16)
    vwn_ref[...] = vp[:, KV_PAD:2 * KV_PAD].astype(BF16)
    ng_ref[...] = jnp.dot(h, wn_ref[...], preferred_element_type=F32)
    prw_ref[...] = jnp.dot(h, ww_ref[...], preferred_element_type=F32)
    gg = jnp.dot(h, wg_ref[...], preferred_element_type=F32)
    ga_ref[...] = gg[:, 0:D_MODEL]
    gb_ref[...] = gg[:, D_MODEL:2 * D_MODEL]


def _pad_heads(w):
    d, wd = w.shape
    return jnp.pad(w.reshape(d, wd // HEAD_DIM, HEAD_DIM), ((0, 0), (0, 0), (0, LANES - HEAD_DIM))).reshape(d, -1)


def _swap_halves(w):
    d, wd = w.shape
    return w.reshape(d, wd // HEAD_DIM, 2, HEAD_DIM // 2)[:, :, ::-1, :].reshape(d, wd)


def _inproj(x2, g_mix, cs, sn, w_in, tm=256):
    n = x2.shape[0]
    o = 0
    cols = {}
    for name, wd in (("q", WIDTH_A), ("kc", KV_WIDTH), ("vc", KV_WIDTH), ("ksl", KV_WIDTH), ("vsl", KV_WIDTH),
                     ("kwn", KV_WIDTH), ("vwn", KV_WIDTH), ("ng", 3 * N_HEADS_A), ("rw", C_RWKV),
                     ("ga", D_MODEL), ("gb", D_MODEL)):
        cols[name] = w_in[:, o:o + wd]
        o += wd
    rope_cols = lambda f: jnp.concatenate(
        [_pad_heads(f(cols["q"])), f(cols["kc"]), _pad_heads(f(cols["ksl"])), _pad_heads(f(cols["kwn"]))], axis=1)
    w_rope = rope_cols(lambda w: w)
    w_swap = rope_cols(_swap_halves)
    w_v = jnp.concatenate([cols["vc"], _pad_heads(cols["vsl"]), _pad_heads(cols["vwn"])], axis=1)
    w_ng = jnp.pad(cols["ng"], ((0, 0), (0, LANES - 3 * N_HEADS_A)))
    w_gate = jnp.concatenate([cols["ga"], cols["gb"]], axis=1)
    ws = [w.astype(BF16) for w in (w_rope, w_swap, w_v, w_ng, cols["rw"], w_gate)]
    row = lambda wd: pl.BlockSpec((tm, wd), lambda i: (i, 0))
    outs = [(Q_PAD, BF16), (KV_WIDTH, F32), (KV_PAD, BF16), (KV_PAD, BF16), (KV_WIDTH, F32), (KV_PAD, BF16),
            (KV_PAD, BF16), (LANES, F32), (C_RWKV, F32), (D_MODEL, F32), (D_MODEL, F32)]
    return pl.pallas_call(
        _inproj_kernel,
        grid=(n // tm,),
        in_specs=[row(D_MODEL), _full((1, D_MODEL)), row(LANES), row(LANES)] + [_full(w.shape) for w in ws],
        out_specs=[row(wd) for wd, _ in outs],
        out_shape=[jax.ShapeDtypeStruct((n, wd), dt) for wd, dt in outs],
        compiler_params=_params(("parallel",)),
        name="inproj",
    )(x2, g_mix.reshape(1, D_MODEL), cs, sn, *ws)


def _compress_kernel(c_ref, pe_ref, w1_ref, b1_ref, w2_ref, b2_ref, o_ref, *, pad_value):
    half = D_CMP * HEAD_DIM
    c = c_ref[0, 0]
    w1 = w1_ref[...]
    z1 = _mm(c, w1[0:half])
    z2 = _mm(c, w1[half:2 * half])
    z2 = jnp.concatenate([z2[1:], z2[:1]], axis=0)
    pb = _mm(pe_ref[...], w1)[0:1] + b1_ref[...]
    hid = jax.nn.gelu(z1 + z2 + pb)
    out = _mm(hid, w2_ref[...]) + b2_ref[...]
    o_ref[0, 0] = jnp.concatenate([out, jnp.full(out.shape, pad_value, F32)], axis=1).astype(o_ref.dtype)


def _compress(kv, pe, w1, b1, w2, b2, bsz, s, pad_value):
    nch = s // D_CMP
    c = kv.reshape(bsz, nch, D_CMP, N_KV_GROUPS, HEAD_DIM).transpose(0, 3, 1, 2, 4).reshape(
        bsz, N_KV_GROUPS, nch, D_CMP * HEAD_DIM)
    pe8 = jnp.broadcast_to(pe.reshape(1, L_CMP * HEAD_DIM), (8, L_CMP * HEAD_DIM))
    return pl.pallas_call(
        functools.partial(_compress_kernel, pad_value=pad_value),
        grid=(bsz, N_KV_GROUPS),
        in_specs=[pl.BlockSpec((1, 1, nch, D_CMP * HEAD_DIM), lambda b, g: (b, g, 0, 0)),
                  _full(pe8.shape), _full(w1.shape), _full((1, CMP_HIDDEN)), _full(w2.shape), _full((1, HEAD_DIM))],
        out_specs=pl.BlockSpec((1, 1, nch, LANES), lambda b, g: (b, g, 0, 0)),
        out_shape=jax.ShapeDtypeStruct((bsz, N_KV_GROUPS, nch, LANES), BF16),
        compiler_params=_params(("parallel", "parallel")),
        name="compress",
    )(c, pe8, w1, b1.reshape(1, CMP_HIDDEN), w2, b2.reshape(1, HEAD_DIM))


def _nsa_kernel(q_ref, kc_ref, vc_ref, ksl_ref, vsl_ref, kwn_ref, vwn_ref, ng_ref, ov_ref, o_ref, *, n_top):
    n_cmp = kc_ref.shape[2]
    n_slc = ov_ref.shape[1]
    s0 = pl.program_id(1) * Q_BLOCK
    hq = HEADS_PER_GROUP * Q_BLOCK
    t_row = s0 + (_iota((hq, 1), 0) & (Q_BLOCK - 1))
    t_q = s0 + _iota((Q_BLOCK, 1), 0)
    sig = jax.nn.sigmoid(ng_ref[0])
    win_start = pl.multiple_of(jnp.maximum(s0 - WINDOW, 0), LANES)
    n_tiles = s0 // KEY_TILE + 1
    groups = range(N_KV_GROUPS)
    lanes = lambda g: slice(g * LANES, (g + 1) * LANES)
    qg = [jnp.concatenate([q_ref[0, :, lanes(g * HEADS_PER_GROUP + h)] for h in range(HEADS_PER_GROUP)], axis=0)
          for g in groups]

    o_c, sel = [], []
    for g in groups:
        s_c = _mm_nt(qg[g], kc_ref[0, g])
        valid_c = (_iota((1, n_cmp), 1) * D_CMP + (L_CMP - 1)) <= t_row
        s_c = jnp.where(valid_c, s_c, NEG_INF)
        e = jnp.where(valid_c, jnp.exp(s_c - jnp.max(s_c, axis=-1, keepdims=True)), 0.0)
        l = jnp.sum(e, axis=-1, keepdims=True)
        p_c = e * (1.0 / jnp.where(l > 0.0, l, 1.0))
        o_c.append(_mm(p_c, vc_ref[0, g]))

        p_sum = p_c[0:Q_BLOCK]
        for h in range(1, HEADS_PER_GROUP):
            p_sum = p_sum + p_c[h * Q_BLOCK:(h + 1) * Q_BLOCK]
        imp_t = _mm_hi(p_sum, ov_ref[...]).T
        j = _iota((n_slc, Q_BLOCK), 0)
        cur = (s0 + _iota((n_slc, Q_BLOCK), 1)) >> 6
        forced = (j == 0) | (j == cur) | (j == cur - 1)
        score = jnp.where(forced, FORCE_SCORE, jnp.where(j <= cur, imp_t, -1.0))
        sel_t = jnp.zeros((n_slc, Q_BLOCK), F32)
        for _ in range(n_top):
            m = jnp.max(score, axis=0, keepdims=True)
            first = jnp.min(jnp.where(score == m, j, n_slc), axis=0, keepdims=True)
            hit = j == first
            sel_t = jnp.where(hit, 1.0, sel_t)
            score = jnp.where(hit, -3e38, score)
        sel.append(1.0 - sel_t.T)

    lane_q = _iota((Q_BLOCK, LANES), 1)
    in_code = (lane_q >= HEAD_DIM) & (lane_q < HEAD_DIM + BLOCKS_PER_TILE)

    def sel_step(kt, carry, diagonal):
        k0 = pl.multiple_of(kt * KEY_TILE, KEY_TILE)
        shift = (HEAD_DIM - BLOCKS_PER_TILE * kt) & (LANES - 1)
        out = []
        for g in groups:
            m_i, acc = carry[2 * g], carry[2 * g + 1]
            code = jnp.where(in_code, pltpu.roll(sel[g], shift, 1), 0.0).astype(BF16)
            qa = qg[g] + jnp.concatenate([code] * HEADS_PER_GROUP, axis=0)
            s = _mm_nt(qa, ksl_ref[0, pl.ds(k0, KEY_TILE), lanes(g)])
            if diagonal:
                bias = jnp.where((k0 + _iota((Q_BLOCK, KEY_TILE), 1)) <= t_q, 0.0, NEG_INF)
                s = (s.reshape(HEADS_PER_GROUP, Q_BLOCK, KEY_TILE) + bias[None]).reshape(hq, KEY_TILE)
            m_new = jnp.maximum(m_i, jnp.max(s, axis=-1, keepdims=True))
            p = jnp.exp(s - m_new)
            acc = jnp.exp(m_i - m_new) * acc + _mm(p, vsl_ref[0, pl.ds(k0, KEY_TILE), lanes(g)])
            out += [m_new, acc]
        return tuple(out)

    init = (jnp.full((hq, 1), NEG_INF, F32), jnp.zeros((hq, LANES), F32)) * N_KV_GROUPS
    fin = lax.fori_loop(0, n_tiles - 1, lambda kt, c: sel_step(kt, c, False), init)
    fin = sel_step(n_tiles - 1, fin, True)

    heads_out = []
    for g in groups:
        acc_s = fin[2 * g + 1]
        o_s = acc_s * (1.0 / acc_s[:, HEAD_DIM:HEAD_DIM + 1])
        kw = kwn_ref[0, pl.ds(win_start, WINDOW + Q_BLOCK), lanes(g)]
        vw = vwn_ref[0, pl.ds(win_start, WINDOW + Q_BLOCK), lanes(g)]
        pos_w = win_start + _iota((1, WINDOW + Q_BLOCK), 1)
        s_w = jnp.where((pos_w <= t_row) & (pos_w > t_row - WINDOW), _mm_nt(qg[g], kw), NEG_INF)
        acc_w = _mm(jnp.exp(s_w - jnp.max(s_w, axis=-1, keepdims=True)), vw)
        o_w = acc_w * (1.0 / acc_w[:, HEAD_DIM:HEAD_DIM + 1])
        for h in range(HEADS_PER_GROUP):
            hh = g * HEADS_PER_GROUP + h
            r = slice(h * Q_BLOCK, (h + 1) * Q_BLOCK)
            mix = (sig[:, 3 * hh:3 * hh + 1] * o_c[g][r] + sig[:, 3 * hh + 1:3 * hh + 2] * o_s[r]
                   + sig[:, 3 * hh + 2:3 * hh + 3] * o_w[r])
            heads_out.append(mix[:, 0:HEAD_DIM])
    o_ref[0] = jnp.concatenate(heads_out, axis=1).astype(o_ref.dtype)


def _nsa(q, kc, vc, ksl, vsl, kwn, vwn, ng, bsz, s):
    n_cmp = s // D_CMP
    n_slc = s // L_SLC
    ii = np.arange(n_cmp)[:, None]
    jj = np.arange(n_slc)[None, :]
    assert n_slc <= LANES and s % KEY_TILE == 0
    overlap = (ii * D_CMP < (jj + 1) * L_SLC) & (ii * D_CMP + L_CMP > jj * L_SLC)
    overlap = jnp.asarray(np.pad(overlap, ((0, 0), (0, LANES - n_slc))), F32)
    seq = lambda wd: pl.BlockSpec((1, s, wd), lambda b, i: (b, 0, 0))
    blk = lambda wd: pl.BlockSpec((1, Q_BLOCK, wd), lambda b, i: (b, i, 0))
    cmp_spec = pl.BlockSpec((1, N_KV_GROUPS, n_cmp, LANES), lambda b, i: (b, 0, 0, 0))
    r3 = lambda a: a.reshape(bsz, s, a.shape[-1])
    return pl.pallas_call(
        functools.partial(_nsa_kernel, n_top=min(N_SELECT, n_slc)),
        grid=(bsz, s // Q_BLOCK),
        in_specs=[blk(Q_PAD), cmp_spec, cmp_spec, seq(KV_PAD), seq(KV_PAD), seq(KV_PAD), seq(KV_PAD),
                  blk(LANES), _full(overlap.shape)],
        out_specs=blk(WIDTH_A),
        out_shape=jax.ShapeDtypeStruct((bsz, s, WIDTH_A), BF16),
        compiler_params=_params(("parallel", "arbitrary")),
        name="nsa",
    )(r3(q), kc, vc, r3(ksl), r3(vsl), r3(kwn), r3(vwn), r3(ng), overlap)


def _rwkv_prep_kernel(p_ref, pv_ref, mu_ref, w0_ref, w2_ref, a0_ref, a2_ref, g2_ref, kk_ref, ka_ref, bd_ref,
                      r_ref, lw_ref, k_ref, v_ref, kn_ref, kb_ref, g_ref, *, tiles_per_seq):
    p = p_ref[...]
    first = (pl.program_id(0) % tiles_per_seq) == 0
    prev_last = jnp.where(first, 0.0, pv_ref[7:8, :])
    p_prev = jnp.concatenate([prev_last, p[:-1]], axis=0)
    xs = p + (p_prev - p) * mu_ref[...]
    wb = WIDTH_B
    r = xs[:, 0:wb]
    k = xs[:, wb:2 * wb]
    v = xs[:, 2 * wb:3 * wb]
    o = 3 * wb
    xw = xs[:, o:o + W_LORA]
    xa = xs[:, o + W_LORA:o + W_LORA + A_LORA]
    xg = xs[:, o + W_LORA + A_LORA:o + W_LORA + A_LORA + G_LORA]
    z = w0_ref[...] + _mm_hi(jnp.tanh(xw), w2_ref[...])
    w_log = -(jnp.maximum(-z, 0.0) + jnp.log(1.0 + jnp.exp(-jnp.abs(z)))) - 0.5
    a = jax.nn.sigmoid(a0_ref[...] + _mm_hi(xa, a2_ref[...]))
    kk = k * kk_ref[...]
    ss = _mm_hi(kk * kk, bd_ref[...])
    kn = kk * (1.0 / jnp.maximum(jnp.sqrt(ss), 1e-12))
    r_ref[...] = r
    lw_ref[...] = -jnp.exp(w_log)
    k_ref[...] = k * (1.0 + (a - 1.0) * ka_ref[...])
    v_ref[...] = v
    kn_ref[...] = kn
    kb_ref[...] = kn * a
    g_ref[...] = _mm(jax.nn.sigmoid(xg), g2_ref[...])


def _rwkv_prep(prw, mu, w0, w2, a0, a2, g2, k_k, k_a, s, tm=256):
    n = prw.shape[0]
    hid = np.arange(WIDTH_B) // HEAD_DIM_B
    bd = jnp.asarray(hid[:, None] == hid[None, :], F32)
    row = lambda wd: pl.BlockSpec((tm, wd), lambda i: (i, 0))
    vec = lambda a: a.reshape(1, -1)
    return pl.pallas_call(
        functools.partial(_rwkv_prep_kernel, tiles_per_seq=s // tm),
        grid=(n // tm,),
        in_specs=[row(C_RWKV), pl.BlockSpec((8, C_RWKV), lambda i: (jnp.maximum(i * (tm // 8) - 1, 0), 0)),
                  _full((1, C_RWKV)), _full((1, WIDTH_B)), _full(w2.shape), _full((1, WIDTH_B)), _full(a2.shape),
                  _full(g2.shape), _full((1, WIDTH_B)), _full((1, WIDTH_B)), _full(bd.shape)],
        out_specs=[row(WIDTH_B)] * 7,
        out_shape=[jax.ShapeDtypeStruct((n, WIDTH_B), F32)] * 7,
        compiler_params=_params(("parallel",)),
        name="rwkv_prep",
    )(prw, prw, vec(mu), vec(w0), w2, vec(a0), a2, g2, vec(k_k), vec(k_a), bd)


def _rwkv_chunk_kernel(r_ref, lw_ref, k_ref, v_ref, kn_ref, kb_ref, rk_ref,
                       qh_ref, y0_ref, bonus_ref, g_ref, h_ref, *, n_sub):
    L = CHUNK
    hd = HEAD_DIM_B
    ti = _iota((L, L), 0)
    si = _iota((L, L), 1)
    strict = si < ti
    incl = si <= ti
    same_sub = (ti // SUB) == (si // SUB)
    eye = (ti == si).astype(F32)
    tri = incl.astype(F32)
    at, bt, kt, rt, bl, kl, vv, rkr, p_tot = [], [], [], [], [], [], [], [], []
    for j in range(n_sub):
        rows = slice(j * L, (j + 1) * L)
        lw = lw_ref[0, rows, :]
        cs = _mm_hi(tri, lw)
        p_inv = jnp.exp(-cs)
        p_end = jnp.exp(cs[L - 1:L, :] - cs)
        r = r_ref[0, rows, :]
        k = k_ref[0, rows, :]
        kb = kb_ref[0, rows, :]
        at.append(-kn_ref[0, rows, :] * jnp.exp(cs - lw))
        bt.append(kb * p_inv)
        kt.append(k * p_inv)
        rt.append(r * jnp.exp(cs))
        bl.append(kb * p_end)
        kl.append(k * p_end)
        vv.append(v_ref[0, rows, :])
        rkr.append(r * k * rk_ref[...])
        p_tot.append(jnp.exp(cs[L - 1:L, :]))
    units = [(j, h) for j in range(n_sub) for h in range(N_HEADS_B)]
    sl = lambda arr, u: arr[u[0]][:, u[1] * hd:(u[1] + 1) * hd]
    aa = [_mm_nt(jnp.concatenate([sl(at, u), sl(rt, u)], axis=0),
                 jnp.concatenate([sl(bt, u), sl(kt, u)], axis=0)) for u in units]
    a_ab = [jnp.where(strict, x[0:L, 0:L], 0.0) for x in aa]
    a_low = [jnp.concatenate([jnp.where(strict, x[0:L, L:2 * L], 0.0),
                              jnp.where(incl, x[L:2 * L, L:2 * L], 0.0)], axis=0) for x in aa]
    a_rb = [jnp.where(incl, x[L:2 * L, 0:L], 0.0) for x in aa]
    av = [_mm(x, sl(vv, u)) for x, u in zip(a_low, units)]
    kv = [_mm_tn(sl(kl, u), sl(vv, u)) for u in units]
    pw = [jnp.where(same_sub, x, 0.0) for x in a_ab]
    t = [eye + x for x in pw]
    for _ in range(3):
        pw = [_mm(x, x) for x in pw]
        t = [x + _mm(x, y) for x, y in zip(t, pw)]
    width = SUB
    while width < L:
        m = ((ti // width) == (si // width) + 1) & ((ti // (2 * width)) == (si // (2 * width)))
        ot = [_mm(jnp.where(m, x, 0.0), y) for x, y in zip(a_ab, t)]
        t = [x + _mm(x, y) for x, y in zip(t, ot)]
        width *= 2
    wu = [_mm(x, jnp.concatenate([sl(at, u), y[0:L]], axis=1)) for x, y, u in zip(t, av, units)]
    qy = [jnp.concatenate([sl(rt, u), y[L:2 * L]], axis=1) + _mm(x, w)
          for x, y, w, u in zip(a_rb, av, wu, units)]
    gh = [_mm_tn(sl(bl, u), w) + jnp.concatenate([eye * sl(p_tot, u), y], axis=1)
          for w, y, u in zip(wu, kv, units)]
    bonus = [jnp.sum(sl(rkr, u), axis=-1, keepdims=True) * sl(vv, u) for u in units]
    heads = lambda xs, j, c: jnp.concatenate([x[:, c] for x in xs[j * N_HEADS_B:(j + 1) * N_HEADS_B]], axis=1)
    lo, hi = slice(0, hd), slice(hd, 2 * hd)
    for j in range(n_sub):
        rows = slice(j * L, (j + 1) * L)
        qh_ref[0, rows, :] = heads(qy, j, lo)
        y0_ref[0, rows, :] = heads(qy, j, hi)
        bonus_ref[0, rows, :] = heads(bonus, j, slice(None))
        g_ref[0, j] = heads(gh, j, lo)
        h_ref[0, j] = heads(gh, j, hi)


def _rwkv_chunk(r, lw, k, v, kn, kb, r_k, bsz, s, n_sub=2):
    nch = s // CHUNK
    blk = pl.BlockSpec((1, n_sub * CHUNK, WIDTH_B), lambda b, c: (b, c, 0))
    mat = pl.BlockSpec((1, n_sub, HEAD_DIM_B, WIDTH_B), lambda b, c: (b, c, 0, 0))
    r3 = lambda a: a.reshape(bsz, s, WIDTH_B)
    seq_sds = jax.ShapeDtypeStruct((bsz, s, WIDTH_B), F32)
    mat_sds = jax.ShapeDtypeStruct((bsz, nch, HEAD_DIM_B, WIDTH_B), F32)
    return pl.pallas_call(
        functools.partial(_rwkv_chunk_kernel, n_sub=n_sub),
        grid=(bsz, nch // n_sub),
        in_specs=[blk] * 6 + [_full((1, WIDTH_B))],
        out_specs=[blk, blk, blk, mat, mat],
        out_shape=[seq_sds, seq_sds, seq_sds, mat_sds, mat_sds],
        compiler_params=_params(("parallel", "parallel")),
        name="rwkv_chunk",
    )(r3(r), r3(lw), r3(k), r3(v), r3(kn), r3(kb), r_k.reshape(1, WIDTH_B))


def _rwkv_scan_kernel(qh_ref, y0_ref, g_ref, h_ref, yn_ref, st_ref):
    hd = HEAD_DIM_B

    @pl.when(pl.program_id(0) == 0)
    def _():
        st_ref[...] = jnp.zeros_like(st_ref)

    for b in range(qh_ref.shape[0]):
        ys = []
        for h in range(N_HEADS_B):
            c = slice(h * hd, (h + 1) * hd)
            st = st_ref[b * N_HEADS_B + h]
            y = _mm_hi(qh_ref[b, :, c], st) + y0_ref[b, :, c]
            st_ref[b * N_HEADS_B + h] = _mm_hi(g_ref[b, 0, :, c], st) + h_ref[b, 0, :, c]
            mean = jnp.mean(y, axis=-1, keepdims=True)
            d = y - mean
            var = jnp.mean(d * d, axis=-1, keepdims=True)
            ys.append(d * lax.rsqrt(var + GN_EPS))
        yn_ref[b] = jnp.concatenate(ys, axis=1)


def _rwkv_scan(qh, y0, g, h, bsz, s):
    nch = s // CHUNK
    blk = pl.BlockSpec((bsz, CHUNK, WIDTH_B), lambda c: (0, c, 0))
    mat = pl.BlockSpec((bsz, 1, HEAD_DIM_B, WIDTH_B), lambda c: (0, c, 0, 0))
    return pl.pallas_call(
        _rwkv_scan_kernel,
        grid=(nch,),
        in_specs=[blk, blk, mat, mat],
        out_specs=blk,
        out_shape=jax.ShapeDtypeStruct((bsz, s, WIDTH_B), F32),
        scratch_shapes=[pltpu.VMEM((bsz * N_HEADS_B, HEAD_DIM_B, HEAD_DIM_B), F32)],
        compiler_params=_params(("arbitrary",)),
        name="rwkv_scan",
    )(qh, y0, g, h)


def _merge_kernel(x_ref, oa_ref, yn_ref, bonus_ref, g_ref, ga_ref, gb_ref, lnw_ref, lnb_ref, wa_ref, wb_ref,
                  wo_ref, gf_ref, wr_ref, br_ref, x1_ref, h2_ref, route_ref):
    ob = (yn_ref[...] * lnw_ref[...] + lnb_ref[...] + bonus_ref[...]) * g_ref[...]
    merged = (jax.nn.sigmoid(ga_ref[...]) * jnp.dot(oa_ref[...], wa_ref[...], preferred_element_type=F32)
              + jax.nn.sigmoid(gb_ref[...]) * _mm(ob, wb_ref[...]))
    x1 = x_ref[...] + _mm(merged, wo_ref[...])
    x1_ref[...] = x1
    h2 = x1 * lax.rsqrt(jnp.mean(x1 * x1, axis=-1, keepdims=True) + RMS_EPS) * gf_ref[...]
    h2_ref[...] = h2
    logits = _mm_hi(h2, wr_ref[...]) + br_ref[...]
    lane = _iota(logits.shape, 1)
    big = 4 * LANES
    is_grp = lane < N_GROUPS
    lg = jnp.where(is_grp, logits, NEG_INF)
    mg = jnp.max(lg, axis=-1, keepdims=True)
    gidx = jnp.min(jnp.where(lg == mg, lane, big), axis=-1, keepdims=True)
    pg = 1.0 / jnp.sum(jnp.where(is_grp, jnp.exp(lg - mg), 0.0), axis=-1, keepdims=True)
    in_grp = (lane >= N_GROUPS) & (((lane - N_GROUPS) >> 3) == gidx) & (lane < N_GROUPS + N_EXPERTS)
    le = jnp.where(in_grp, logits, NEG_INF)
    m1 = jnp.max(le, axis=-1, keepdims=True)
    i1 = jnp.min(jnp.where(le == m1, lane, big), axis=-1, keepdims=True)
    le2 = jnp.where(lane == i1, NEG_INF, le)
    m2 = jnp.max(le2, axis=-1, keepdims=True)
    i2 = jnp.min(jnp.where(le2 == m2, lane, big), axis=-1, keepdims=True)
    e2 = jnp.exp(m2 - m1)
    inv = pg / (1.0 + e2)
    route_ref[...] = jnp.where(lane == 0, (i1 - N_GROUPS).astype(F32),
                               jnp.where(lane == 1, (i2 - N_GROUPS).astype(F32),
                                         jnp.where(lane == 2, inv, jnp.where(lane == 3, inv * e2, 0.0))))


def _merge(x2, oa, yn, bonus, g, ga, gb, ln_w, ln_b, w_a, w_b, w_o, g_ffn, w_grp, b_grp, w_exp, b_exp, tm=256):
    n = x2.shape[0]
    w_r = jnp.pad(jnp.concatenate([w_grp, w_exp], axis=1), ((0, 0), (0, LANES - N_GROUPS - N_EXPERTS)))
    b_r = jnp.pad(jnp.concatenate([b_grp, b_exp]), (0, LANES - N_GROUPS - N_EXPERTS)).reshape(1, LANES)
    row = lambda wd: pl.BlockSpec((tm, wd), lambda i: (i, 0))
    vec = lambda a: a.reshape(1, -1)
    wa, wb, wo = w_a.astype(BF16), w_b.astype(BF16), w_o.astype(BF16)
    return pl.pallas_call(
        _merge_kernel,
        grid=(n // tm,),
        in_specs=[row(D_MODEL), row(WIDTH_A), row(WIDTH_B), row(WIDTH_B), row(WIDTH_B), row(D_MODEL), row(D_MODEL),
                  _full((1, WIDTH_B)), _full((1, WIDTH_B)), _full(wa.shape), _full(wb.shape), _full(wo.shape),
                  _full((1, D_MODEL)), _full(w_r.shape), _full((1, LANES))],
        out_specs=[row(D_MODEL), row(D_MODEL), row(LANES)],
        out_shape=[jax.ShapeDtypeStruct((n, D_MODEL), F32), jax.ShapeDtypeStruct((n, D_MODEL), F32),
                   jax.ShapeDtypeStruct((n, LANES), F32)],
        compiler_params=_params(("parallel",)),
        name="merge",
    )(x2, oa, yn, bonus, g, ga, gb, vec(ln_w), vec(ln_b), wa, wb, wo, vec(g_ffn), w_r, b_r)


def _gather_start(idx_ref, src_hbm, buf, sem, slot, rows):
    def body(r, carry):
        pltpu.make_async_copy(src_hbm.at[pl.ds(idx_ref[0, 0, r], 1)], buf.at[slot, pl.ds(r, 1)], sem.at[slot]).start()
        return carry
    lax.fori_loop(0, rows, body, 0, unroll=8)


def _gather_wait(src_hbm, buf, sem, slot, rows):
    pltpu.make_async_copy(src_hbm.at[pl.ds(0, rows)], buf.at[slot], sem.at[slot]).wait()


def _gather_pipeline(idx_ref, idx_next_ref, src_hbm, buf, sem, rows):
    i = pl.program_id(0)
    slot = i % 2

    @pl.when(i == 0)
    def _():
        _gather_start(idx_ref, src_hbm, buf, sem, 0, rows)

    @pl.when(i + 1 < pl.num_programs(0))
    def _():
        _gather_start(idx_next_ref, src_hbm, buf, sem, 1 - slot, rows)

    _gather_wait(src_hbm, buf, sem, slot, rows)
    return slot


def _moe_kernel(be_ref, tok_ref, tok_next_ref, gate_ref, h2_hbm, w13_ref, w2_ref, y_ref, buf, sem):
    slot = _gather_pipeline(tok_ref, tok_next_ref, h2_hbm, buf, sem, MOE_BLOCK)
    xb = buf[slot]
    a = _mm(xb, w13_ref[0])
    hid = jax.nn.silu(a[:, 0:D_EXPERT]) * a[:, D_EXPERT:2 * D_EXPERT]
    y_ref[...] = _mm(hid, w2_ref[0]) * gate_ref[...]


def _moe(h2, blk_expert, buf_tok, buf_gate, w13, w2):
    n_pad = buf_tok.shape[0]
    n_blk = n_pad // MOE_BLOCK
    tok3 = buf_tok.reshape(n_blk, 1, MOE_BLOCK)
    smem_blk = lambda f: pl.BlockSpec((1, 1, MOE_BLOCK), f, memory_space=pltpu.SMEM)
    grid_spec = pltpu.PrefetchScalarGridSpec(
        num_scalar_prefetch=1,
        grid=(n_blk,),
        in_specs=[smem_blk(lambda i, be: (i, 0, 0)),
                  smem_blk(lambda i, be: (jnp.minimum(i + 1, n_blk - 1), 0, 0)),
                  pl.BlockSpec((MOE_BLOCK, 1), lambda i, be: (i, 0)),
                  pl.BlockSpec(memory_space=pl.ANY),
                  pl.BlockSpec((1, D_MODEL, 2 * D_EXPERT), lambda i, be: (be[i], 0, 0)),
                  pl.BlockSpec((1, D_EXPERT, D_MODEL), lambda i, be: (be[i], 0, 0))],
        out_specs=pl.BlockSpec((MOE_BLOCK, D_MODEL), lambda i, be: (i, 0)),
        scratch_shapes=[pltpu.VMEM((2, MOE_BLOCK, D_MODEL), F32), pltpu.SemaphoreType.DMA((2,))],
    )
    return pl.pallas_call(
        _moe_kernel,
        grid_spec=grid_spec,
        out_shape=jax.ShapeDtypeStruct((n_pad, D_MODEL), F32),
        compiler_params=_params(("arbitrary",)),
        name="moe",
    )(blk_expert, tok3, tok3, buf_gate.reshape(n_pad, 1), h2, w13, w2)


def _final_kernel(pos_ref, pos_next_ref, x1_ref, g_ref, y_hbm, o_ref, buf, sem, *, tm):
    slot = _gather_pipeline(pos_ref, pos_next_ref, y_hbm, buf, sem, 2 * tm)
    x = x1_ref[...] + buf[slot, 0:tm] + buf[slot, tm:2 * tm]
    o_ref[...] = x * lax.rsqrt(jnp.mean(x * x, axis=-1, keepdims=True) + RMS_EPS) * g_ref[...]


def _final(x1, y_pad, pos, g_final, tm=256):
    n = x1.shape[0]
    nt = n // tm
    pos3 = pos.reshape(nt, tm, TOP_K).transpose(0, 2, 1).reshape(nt, 1, TOP_K * tm)
    smem_blk = lambda f: pl.BlockSpec((1, 1, TOP_K * tm), f, memory_space=pltpu.SMEM)
    return pl.pallas_call(
        functools.partial(_final_kernel, tm=tm),
        grid=(nt,),
        in_specs=[smem_blk(lambda i: (i, 0, 0)), smem_blk(lambda i: (jnp.minimum(i + 1, nt - 1), 0, 0)),
                  pl.BlockSpec((tm, D_MODEL), lambda i: (i, 0)), _full((1, D_MODEL)),
                  pl.BlockSpec(memory_space=pl.ANY)],
        out_specs=pl.BlockSpec((tm, D_MODEL), lambda i: (i, 0)),
        out_shape=jax.ShapeDtypeStruct((n, D_MODEL), F32),
        scratch_shapes=[pltpu.VMEM((2, TOP_K * tm, D_MODEL), F32), pltpu.SemaphoreType.DMA((2,))],
        compiler_params=_params(("arbitrary",)),
        name="final",
    )(pos3, pos3, x1, g_final.reshape(1, D_MODEL), y_pad)


def _dispatch_plan(route, n_tok):
    e_flat = route[:, 0:TOP_K].astype(I32).reshape(n_tok * TOP_K)
    gate_flat = route[:, TOP_K:2 * TOP_K].reshape(n_tok * TOP_K)
    onehot = (e_flat[:, None] == jnp.arange(N_EXPERTS, dtype=I32)[None, :]).astype(I32)
    csum = jnp.cumsum(onehot, axis=0)
    counts = csum[-1]
    rank = jnp.sum(jnp.where(onehot > 0, csum - 1, 0), axis=1)
    padded = (counts + MOE_BLOCK - 1) // MOE_BLOCK * MOE_BLOCK
    pad_end = jnp.cumsum(padded)
    pad_start = pad_end - padded
    dest = pad_start[e_flat] + rank
    n_pad = n_tok * TOP_K + N_EXPERTS * MOE_BLOCK
    tok_flat = jnp.repeat(jnp.arange(n_tok, dtype=I32), TOP_K)
    buf_tok = jnp.zeros((n_pad,), I32).at[dest].set(tok_flat)
    buf_gate = jnp.zeros((n_pad,), F32).at[dest].set(gate_flat)
    n_blk = n_pad // MOE_BLOCK
    blk_expert = jnp.minimum(jnp.searchsorted(pad_end, jnp.arange(n_blk, dtype=I32) * MOE_BLOCK, side="right"),
                             N_EXPERTS - 1).astype(I32)
    return buf_tok, buf_gate, blk_expert, dest.reshape(n_tok, TOP_K)


def kernel(x, positions, g_mix, w_in, cmp_pe_k, cmp_w1_k, cmp_b1_k, cmp_w2_k, cmp_b2_k, cmp_pe_v, cmp_w1_v, cmp_b1_v, cmp_w2_v, cmp_b2_v, rwkv_mu, rwkv_w0, rwkv_w2, rwkv_a0, rwkv_a2, rwkv_g2, rwkv_k_k, rwkv_k_a, rwkv_r_k, rwkv_ln_w, rwkv_ln_b, w_proj_a, w_proj_b, w_out, g_ffn, w_grp, b_grp, w_exp, b_exp, e_w1, e_w3, e_w2, g_final):
    bsz, s, _ = x.shape
    assert w_in.shape[0] == 1, "single-layer stack"
    n_tok = bsz * s
    x2 = x.reshape(n_tok, D_MODEL)
    half = HEAD_DIM // 2
    freqs = jnp.power(ROPE_THETA, -jnp.arange(half, dtype=F32) / half)
    ang = positions.astype(F32).reshape(n_tok, 1) * freqs
    cs = jnp.tile(jnp.cos(ang), (1, 4))
    sn = jnp.tile(jnp.concatenate([-jnp.sin(ang), jnp.sin(ang)], axis=1), (1, 2))
    q, kc, ksl, kwn, vc, vsl, vwn, ng, prw, ga, gb = _inproj(x2, g_mix[0], cs, sn, w_in[0])
    k_c = _compress(kc, cmp_pe_k[0], cmp_w1_k[0], cmp_b1_k[0], cmp_w2_k[0], cmp_b2_k[0], bsz, s, 0.0)
    v_c = _compress(vc, cmp_pe_v[0], cmp_w1_v[0], cmp_b1_v[0], cmp_w2_v[0], cmp_b2_v[0], bsz, s, 1.0)
    o_a = _nsa(q, k_c, v_c, ksl, vsl, kwn, vwn, ng, bsz, s).reshape(n_tok, WIDTH_A)
    r, lw, k, v, kn, kb, g = _rwkv_prep(prw, rwkv_mu[0], rwkv_w0[0], rwkv_w2[0], rwkv_a0[0], rwkv_a2[0],
                                        rwkv_g2[0], rwkv_k_k[0], rwkv_k_a[0], s)
    qh, y0, bonus, gm, hm = _rwkv_chunk(r, lw, k, v, kn, kb, rwkv_r_k[0], bsz, s)
    yn = _rwkv_scan(qh, y0, gm, hm, bsz, s)
    x1, h2, route = _merge(x2, o_a, yn.reshape(n_tok, WIDTH_B), bonus.reshape(n_tok, WIDTH_B), g, ga, gb,
                           rwkv_ln_w[0], rwkv_ln_b[0], w_proj_a[0], w_proj_b[0], w_out[0], g_ffn[0],
                           w_grp[0], b_grp[0], w_exp[0], b_exp[0])
    buf_tok, buf_gate, blk_expert, pos = _dispatch_plan(route, n_tok)
    w13 = jnp.concatenate([e_w1[0], e_w3[0]], axis=-1).astype(BF16)
    y_pad = _moe(h2, blk_expert, buf_tok, buf_gate, w13, e_w2[0].astype(BF16))
    out = _final(x1, y_pad, pos, g_final)
    return out.reshape(bsz, s, D_MODEL)
```

```python
import functools

import numpy as np
import jax
import jax.numpy as jnp
from jax import lax
from jax.experimental import pallas as pl
from jax.experimental.pallas import tpu as pltpu

F32 = jnp.float32
BF16 = jnp.bfloat16
I32 = jnp.int32

D_MODEL = 1024
N_HEADS_A = 8
N_KV_GROUPS = 2
HEADS_PER_GROUP = N_HEADS_A // N_KV_GROUPS
HEAD_DIM = 64
L_CMP = 32
D_CMP = 16
CMP_HIDDEN = 256
L_SLC = 64
N_SELECT = 16
WINDOW = 512
Q_BLOCK = 128
ROPE_THETA = 10000.0
FORCE_SCORE = 1e4
NEG_INF = -1e30
N_HEADS_B = 8
HEAD_DIM_B = 64
W_LORA = 64
A_LORA = 64
G_LORA = 128
GN_EPS = 64e-5
N_GROUPS = 4
EXPERTS_PER_GROUP = 8
N_EXPERTS = N_GROUPS * EXPERTS_PER_GROUP
TOP_K = 2
D_EXPERT = 256
MOE_BLOCK = 256
RMS_EPS = 1e-6
WIDTH_A = N_HEADS_A * HEAD_DIM
KV_WIDTH = N_KV_GROUPS * HEAD_DIM
WIDTH_B = N_HEADS_B * HEAD_DIM_B
C_RWKV = 3 * WIDTH_B + W_LORA + A_LORA + G_LORA

LANES = 128
CHUNK = 64
SUB = 16
KEY_TILE = 1024
BLOCKS_PER_TILE = KEY_TILE // L_SLC
VMEM_LIMIT = 56 * 1024 * 1024

HI = lax.Precision.HIGHEST


def _mm(a, b):
    return jnp.dot(a.astype(BF16), b.astype(BF16), preferred_element_type=F32)


def _mm_nt(a, b):
    return lax.dot_general(a.astype(BF16), b.astype(BF16), (((1,), (1,)), ((), ())), preferred_element_type=F32)


def _mm_tn(a, b):
    return lax.dot_general(a.astype(BF16), b.astype(BF16), (((0,), (0,)), ((), ())), preferred_element_type=F32)


def _mm_hi(a, b):
    return jnp.dot(a, b, preferred_element_type=F32, precision=HI)


def _iota(shape, dim):
    return lax.broadcasted_iota(I32, shape, dim)


def _params(sem):
    return pltpu.CompilerParams(dimension_semantics=sem, vmem_limit_bytes=VMEM_LIMIT)


def _full(shape):
    nd = len(shape)
    return pl.BlockSpec(shape, lambda *_: (0,) * nd)


Q_PAD = N_HEADS_A * LANES
KV_PAD = N_KV_GROUPS * LANES


def _inproj_kernel(x_ref, g_ref, cs_ref, sn_ref, wr_ref, ws_ref, wv_ref, wn_ref, ww_ref, wg_ref,
                   q_ref, kc_ref, ksl_ref, kwn_ref, vc_ref, vsl_ref, vwn_ref, ng_ref, prw_ref, ga_ref, gb_ref):
    x = x_ref[...]
    h = (x * lax.rsqrt(jnp.mean(x * x, axis=-1, keepdims=True) + RMS_EPS) * g_ref[...]).astype(BF16)
    n_rope = wr_ref.shape[1] // LANES
    cs = jnp.concatenate([cs_ref[...]] * n_rope, axis=1)
    sn = jnp.concatenate([sn_ref[...]] * n_rope, axis=1)
    ro = (jnp.dot(h, wr_ref[...], preferred_element_type=F32) * cs
          + jnp.dot(h, ws_ref[...], preferred_element_type=F32) * sn)
    q_ref[...] = (ro[:, 0:Q_PAD] * (HEAD_DIM ** -0.5)).astype(BF16)
    o = Q_PAD
    kc_ref[...] = ro[:, o:o + KV_WIDTH]
    tm = x_ref.shape[0]
    blk = ((pl.program_id(0) * tm + _iota((tm, 1), 0)) >> 6) & (BLOCKS_PER_TILE - 1)
    code = jnp.where((_iota((1, KV_PAD), 1) & (LANES - 1)) == HEAD_DIM + blk, NEG_INF, 0.0)
    ksl_ref[...] = (ro[:, o + KV_WIDTH:o + KV_WIDTH + KV_PAD] + code).astype(BF16)
    kwn_ref[...] = ro[:, o + KV_WIDTH + KV_PAD:o + KV_WIDTH + 2 * KV_PAD].astype(BF16)
    v = jnp.dot(h, wv_ref[...], preferred_element_type=F32)
    vc_ref[...] = v[:, 0:KV_WIDTH]
    ones = ((_iota((1, 2 * KV_PAD), 1) & (LANES - 1)) >= HEAD_DIM).astype(F32)
    vp = v[:, KV_WIDTH:KV_WIDTH + 2 * KV_PAD] + ones
    vsl_ref[...] = vp[:, 0:KV_PAD].astype(BF16)
    vwn_ref[...] = vp[:, KV_PAD:2 * KV_PAD].astype(BF16)
    ng_ref[...] = jnp.dot(h, wn_ref[...], preferred_element_type=F32)
    prw_ref[...] = jnp.dot(h, ww_ref[...], preferred_element_type=F32)
    gg = jnp.dot(h, wg_ref[...], preferred_element_type=F32)
    ga_ref[...] = gg[:, 0:D_MODEL]
    gb_ref[...] = gg[:, D_MODEL:2 * D_MODEL]


def _pad_heads(w):
    d, wd = w.shape
    return jnp.pad(w.reshape(d, wd // HEAD_DIM, HEAD_DIM), ((0, 0), (0, 0), (0, LANES - HEAD_DIM))).reshape(d, -1)


def _swap_halves(w):
    d, wd = w.shape
    return w.reshape(d, wd // HEAD_DIM, 2, HEAD_DIM // 2)[:, :, ::-1, :].reshape(d, wd)


def _inproj(x2, g_mix, cs, sn, w_in, tm=256):
    n = x2.shape[0]
    o = 0
    cols = {}
    for name, wd in (("q", WIDTH_A), ("kc", KV_WIDTH), ("vc", KV_WIDTH), ("ksl", KV_WIDTH), ("vsl", KV_WIDTH),
                     ("kwn", KV_WIDTH), ("vwn", KV_WIDTH), ("ng", 3 * N_HEADS_A), ("rw", C_RWKV),
                     ("ga", D_MODEL), ("gb", D_MODEL)):
        cols[name] = w_in[:, o:o + wd]
        o += wd
    rope_cols = lambda f: jnp.concatenate(
        [_pad_heads(f(cols["q"])), f(cols["kc"]), _pad_heads(f(cols["ksl"])), _pad_heads(f(cols["kwn"]))], axis=1)
    w_rope = rope_cols(lambda w: w)
    w_swap = rope_cols(_swap_halves)
    w_v = jnp.concatenate([cols["vc"], _pad_heads(cols["vsl"]), _pad_heads(cols["vwn"])], axis=1)
    w_ng = jnp.pad(cols["ng"], ((0, 0), (0, LANES - 3 * N_HEADS_A)))
    w_gate = jnp.concatenate([cols["ga"], cols["gb"]], axis=1)
    ws = [w.astype(BF16) for w in (w_rope, w_swap, w_v, w_ng, cols["rw"], w_gate)]
    row = lambda wd: pl.BlockSpec((tm, wd), lambda i: (i, 0))
    outs = [(Q_PAD, BF16), (KV_WIDTH, F32), (KV_PAD, BF16), (KV_PAD, BF16), (KV_WIDTH, F32), (KV_PAD, BF16),
            (KV_PAD, BF16), (LANES, F32), (C_RWKV, F32), (D_MODEL, F32), (D_MODEL, F32)]
    return pl.pallas_call(
        _inproj_kernel,
        grid=(n // tm,),
        in_specs=[row(D_MODEL), _full((1, D_MODEL)), row(LANES), row(LANES)] + [_full(w.shape) for w in ws],
        out_specs=[row(wd) for wd, _ in outs],
        out_shape=[jax.ShapeDtypeStruct((n, wd), dt) for wd, dt in outs],
        compiler_params=_params(("parallel",)),
        name="inproj",
    )(x2, g_mix.reshape(1, D_MODEL), cs, sn, *ws)


def _compress_kernel(c_ref, pe_ref, w1_ref, b1_ref, w2_ref, b2_ref, o_ref, *, pad_value):
    half = D_CMP * HEAD_DIM
    c = c_ref[0, 0]
    w1 = w1_ref[...]
    z1 = _mm(c, w1[0:half])
    z2 = _mm(c, w1[half:2 * half])
    z2 = jnp.concatenate([z2[1:], z2[:1]], axis=0)
    pb = _mm(pe_ref[...], w1)[0:1] + b1_ref[...]
    hid = jax.nn.gelu(z1 + z2 + pb)
    out = _mm(hid, w2_ref[...]) + b2_ref[...]
    o_ref[0, 0] = jnp.concatenate([out, jnp.full(out.shape, pad_value, F32)], axis=1).astype(o_ref.dtype)


def _compress(kv, pe, w1, b1, w2, b2, bsz, s, pad_value):
    nch = s // D_CMP
    c = kv.reshape(bsz, nch, D_CMP, N_KV_GROUPS, HEAD_DIM).transpose(0, 3, 1, 2, 4).reshape(
        bsz, N_KV_GROUPS, nch, D_CMP * HEAD_DIM)
    pe8 = jnp.broadcast_to(pe.reshape(1, L_CMP * HEAD_DIM), (8, L_CMP * HEAD_DIM))
    return pl.pallas_call(
        functools.partial(_compress_kernel, pad_value=pad_value),
        grid=(bsz, N_KV_GROUPS),
        in_specs=[pl.BlockSpec((1, 1, nch, D_CMP * HEAD_DIM), lambda b, g: (b, g, 0, 0)),
                  _full(pe8.shape), _full(w1.shape), _full((1, CMP_HIDDEN)), _full(w2.shape), _full((1, HEAD_DIM))],
        out_specs=pl.BlockSpec((1, 1, nch, LANES), lambda b, g: (b, g, 0, 0)),
        out_shape=jax.ShapeDtypeStruct((bsz, N_KV_GROUPS, nch, LANES), BF16),
        compiler_params=_params(("parallel", "parallel")),
        name="compress",
    )(c, pe8, w1, b1.reshape(1, CMP_HIDDEN), w2, b2.reshape(1, HEAD_DIM))


def _nsa_kernel(q_ref, kc_ref, vc_ref, ksl_ref, vsl_ref, kwn_ref, vwn_ref, ng_ref, ov_ref, o_ref, *, n_top):
    n_cmp = kc_ref.shape[2]
    n_slc = ov_ref.shape[1]
    s0 = pl.program_id(1) * Q_BLOCK
    hq = HEADS_PER_GROUP * Q_BLOCK
    t_row = s0 + (_iota((hq, 1), 0) & (Q_BLOCK - 1))
    t_q = s0 + _iota((Q_BLOCK, 1), 0)
    sig = jax.nn.sigmoid(ng_ref[0])
    win_start = pl.multiple_of(jnp.maximum(s0 - WINDOW, 0), LANES)
    n_tiles = s0 // KEY_TILE + 1
    groups = range(N_KV_GROUPS)
    lanes = lambda g: slice(g * LANES, (g + 1) * LANES)
    qg = [jnp.concatenate([q_ref[0, :, lanes(g * HEADS_PER_GROUP + h)] for h in range(HEADS_PER_GROUP)], axis=0)
          for g in groups]

    o_c, sel = [], []
    for g in groups:
        s_c = _mm_nt(qg[g], kc_ref[0, g])
        valid_c = (_iota((1, n_cmp), 1) * D_CMP + (L_CMP - 1)) <= t_row
        s_c = jnp.where(valid_c, s_c, NEG_INF)
        e = jnp.where(valid_c, jnp.exp(s_c - jnp.max(s_c, axis=-1, keepdims=True)), 0.0)
        l = jnp.sum(e, axis=-1, keepdims=True)
        p_c = e * (1.0 / jnp.where(l > 0.0, l, 1.0))
        o_c.append(_mm(p_c, vc_ref[0, g]))

        p_sum = p_c[0:Q_BLOCK]
        for h in range(1, HEADS_PER_GROUP):
            p_sum = p_sum + p_c[h * Q_BLOCK:(h + 1) * Q_BLOCK]
        imp_t = _mm_hi(p_sum, ov_ref[...]).T
        j = _iota((n_slc, Q_BLOCK), 0)
        cur = (s0 + _iota((n_slc, Q_BLOCK), 1)) >> 6
        forced = (j == 0) | (j == cur) | (j == cur - 1)
        score = jnp.where(forced, FORCE_SCORE, jnp.where(j <= cur, imp_t, -1.0))
        sel_t = jnp.zeros((n_slc, Q_BLOCK), F32)
        for _ in range(n_top):
            m = jnp.max(score, axis=0, keepdims=True)
            first = jnp.min(jnp.where(score == m, j, n_slc), axis=0, keepdims=True)
            hit = j == first
            sel_t = jnp.where(hit, 1.0, sel_t)
            score = jnp.where(hit, -3e38, score)
        sel.append(1.0 - sel_t.T)

    lane_q = _iota((Q_BLOCK, LANES), 1)
    in_code = (lane_q >= HEAD_DIM) & (lane_q < HEAD_DIM + BLOCKS_PER_TILE)

    def sel_step(kt, carry, diagonal):
        k0 = pl.multiple_of(kt * KEY_TILE, KEY_TILE)
        shift = (HEAD_DIM - BLOCKS_PER_TILE * kt) & (LANES - 1)
        out = []
        for g in groups:
            m_i, acc = carry[2 * g], carry[2 * g + 1]
            code = jnp.where(in_code, pltpu.roll(sel[g], shift, 1), 0.0).astype(BF16)
            qa = qg[g] + jnp.concatenate([code] * HEADS_PER_GROUP, axis=0)
            s = _mm_nt(qa, ksl_ref[0, pl.ds(k0, KEY_TILE), lanes(g)])
            if diagonal:
                bias = jnp.where((k0 + _iota((Q_BLOCK, KEY_TILE), 1)) <= t_q, 0.0, NEG_INF)
                s = (s.reshape(HEADS_PER_GROUP, Q_BLOCK, KEY_TILE) + bias[None]).reshape(hq, KEY_TILE)
            m_new = jnp.maximum(m_i, jnp.max(s, axis=-1, keepdims=True))
            p = jnp.exp(s - m_new)
            acc = jnp.exp(m_i - m_new) * acc + _mm(p, vsl_ref[0, pl.ds(k0, KEY_TILE), lanes(g)])
            out += [m_new, acc]
        return tuple(out)

    init = (jnp.full((hq, 1), NEG_INF, F32), jnp.zeros((hq, LANES), F32)) * N_KV_GROUPS
    fin = lax.fori_loop(0, n_tiles - 1, lambda kt, c: sel_step(kt, c, False), init)
    fin = sel_step(n_tiles - 1, fin, True)

    heads_out = []
    for g in groups:
        acc_s = fin[2 * g + 1]
        o_s = acc_s * (1.0 / acc_s[:, HEAD_DIM:HEAD_DIM + 1])
        kw = kwn_ref[0, pl.ds(win_start, WINDOW + Q_BLOCK), lanes(g)]
        vw = vwn_ref[0, pl.ds(win_start, WINDOW + Q_BLOCK), lanes(g)]
        pos_w = win_start + _iota((1, WINDOW + Q_BLOCK), 1)
        s_w = jnp.where((pos_w <= t_row) & (pos_w > t_row - WINDOW), _mm_nt(qg[g], kw), NEG_INF)
        acc_w = _mm(jnp.exp(s_w - jnp.max(s_w, axis=-1, keepdims=True)), vw)
        o_w = acc_w * (1.0 / acc_w[:, HEAD_DIM:HEAD_DIM + 1])
        for h in range(HEADS_PER_GROUP):
            hh = g * HEADS_PER_GROUP + h
            r = slice(h * Q_BLOCK, (h + 1) * Q_BLOCK)
            mix = (sig[:, 3 * hh:3 * hh + 1] * o_c[g][r] + sig[:, 3 * hh + 1:3 * hh + 2] * o_s[r]
                   + sig[:, 3 * hh + 2:3 * hh + 3] * o_w[r])
            heads_out.append(mix[:, 0:HEAD_DIM])
    o_ref[0] = jnp.concatenate(heads_out, axis=1).astype(o_ref.dtype)


def _nsa(q, kc, vc, ksl, vsl, kwn, vwn, ng, bsz, s):
    n_cmp = s // D_CMP
    n_slc = s // L_SLC
    ii = np.arange(n_cmp)[:, None]
    jj = np.arange(n_slc)[None, :]
    assert n_slc <= LANES and s % KEY_TILE == 0
    overlap = (ii * D_CMP < (jj + 1) * L_SLC) & (ii * D_CMP + L_CMP > jj * L_SLC)
    overlap = jnp.asarray(np.pad(overlap, ((0, 0), (0, LANES - n_slc))), F32)
    seq = lambda wd: pl.BlockSpec((1, s, wd), lambda b, i: (b, 0, 0))
    blk = lambda wd: pl.BlockSpec((1, Q_BLOCK, wd), lambda b, i: (b, i, 0))
    cmp_spec = pl.BlockSpec((1, N_KV_GROUPS, n_cmp, LANES), lambda b, i: (b, 0, 0, 0))
    r3 = lambda a: a.reshape(bsz, s, a.shape[-1])
    return pl.pallas_call(
        functools.partial(_nsa_kernel, n_top=min(N_SELECT, n_slc)),
        grid=(bsz, s // Q_BLOCK),
        in_specs=[blk(Q_PAD), cmp_spec, cmp_spec, seq(KV_PAD), seq(KV_PAD), seq(KV_PAD), seq(KV_PAD),
                  blk(LANES), _full(overlap.shape)],
        out_specs=blk(WIDTH_A),
        out_shape=jax.ShapeDtypeStruct((bsz, s, WIDTH_A), BF16),
        compiler_params=_params(("parallel", "arbitrary")),
        name="nsa",
    )(r3(q), kc, vc, r3(ksl), r3(vsl), r3(kwn), r3(vwn), r3(ng), overlap)


def _rwkv_prep_kernel(p_ref, pv_ref, mu_ref, w0_ref, w2_ref, a0_ref, a2_ref, g2_ref, kk_ref, ka_ref, bd_ref,
                      r_ref, lw_ref, k_ref, v_ref, kn_ref, kb_ref, g_ref, *, tiles_per_seq):
    p = p_ref[...]
    first = (pl.program_id(0) % tiles_per_seq) == 0
    prev_last = jnp.where(first, 0.0, pv_ref[7:8, :])
    p_prev = jnp.concatenate([prev_last, p[:-1]], axis=0)
    xs = p + (p_prev - p) * mu_ref[...]
    wb = WIDTH_B
    r = xs[:, 0:wb]
    k = xs[:, wb:2 * wb]
    v = xs[:, 2 * wb:3 * wb]
    o = 3 * wb
    xw = xs[:, o:o + W_LORA]
    xa = xs[:, o + W_LORA:o + W_LORA + A_LORA]
    xg = xs[:, o + W_LORA + A_LORA:o + W_LORA + A_LORA + G_LORA]
    z = w0_ref[...] + _mm_hi(jnp.tanh(xw), w2_ref[...])
    w_log = -(jnp.maximum(-z, 0.0) + jnp.log(1.0 + jnp.exp(-jnp.abs(z)))) - 0.5
    a = jax.nn.sigmoid(a0_ref[...] + _mm_hi(xa, a2_ref[...]))
    kk = k * kk_ref[...]
    ss = _mm_hi(kk * kk, bd_ref[...])
    kn = kk * (1.0 / jnp.maximum(jnp.sqrt(ss), 1e-12))
    r_ref[...] = r
    lw_ref[...] = -jnp.exp(w_log)
    k_ref[...] = k * (1.0 + (a - 1.0) * ka_ref[...])
    v_ref[...] = v
    kn_ref[...] = kn
    kb_ref[...] = kn * a
    g_ref[...] = _mm(jax.nn.sigmoid(xg), g2_ref[...])


def _rwkv_prep(prw, mu, w0, w2, a0, a2, g2, k_k, k_a, s, tm=256):
    n = prw.shape[0]
    hid = np.arange(WIDTH_B) // HEAD_DIM_B
    bd = jnp.asarray(hid[:, None] == hid[None, :], F32)
    row = lambda wd: pl.BlockSpec((tm, wd), lambda i: (i, 0))
    vec = lambda a: a.reshape(1, -1)
    return pl.pallas_call(
        functools.partial(_rwkv_prep_kernel, tiles_per_seq=s // tm),
        grid=(n // tm,),
        in_specs=[row(C_RWKV), pl.BlockSpec((8, C_RWKV), lambda i: (jnp.maximum(i * (tm // 8) - 1, 0), 0)),
                  _full((1, C_RWKV)), _full((1, WIDTH_B)), _full(w2.shape), _full((1, WIDTH_B)), _full(a2.shape),
                  _full(g2.shape), _full((1, WIDTH_B)), _full((1, WIDTH_B)), _full(bd.shape)],
        out_specs=[row(WIDTH_B)] * 7,
        out_shape=[jax.ShapeDtypeStruct((n, WIDTH_B), F32)] * 7,
        compiler_params=_params(("parallel",)),
        name="rwkv_prep",
    )(prw, prw, vec(mu), vec(w0), w2, vec(a0), a2, g2, vec(k_k), vec(k_a), bd)


def _rwkv_chunk_kernel(r_ref, lw_ref, k_ref, v_ref, kn_ref, kb_ref, rk_ref,
                       qh_ref, y0_ref, bonus_ref, g_ref, h_ref, *, n_sub):
    L = CHUNK
    hd = HEAD_DIM_B
    ti = _iota((L, L), 0)
    si = _iota((L, L), 1)
    strict = si < ti
    incl = si <= ti
    same_sub = (ti // SUB) == (si // SUB)
    eye = (ti == si).astype(F32)
    tri = incl.astype(F32)
    at, bt, kt, rt, bl, kl, vv, rkr, p_tot = [], [], [], [], [], [], [], [], []
    for j in range(n_sub):
        rows = slice(j * L, (j + 1) * L)
        lw = lw_ref[0, rows, :]
        cs = _mm_hi(tri, lw)
        p_inv = jnp.exp(-cs)
        p_end = jnp.exp(cs[L - 1:L, :] - cs)
        r = r_ref[0, rows, :]
        k = k_ref[0, rows, :]
        kb = kb_ref[0, rows, :]
        at.append(-kn_ref[0, rows, :] * jnp.exp(cs - lw))
        bt.append(kb * p_inv)
        kt.append(k * p_inv)
        rt.append(r * jnp.exp(cs))
        bl.append(kb * p_end)
        kl.append(k * p_end)
        vv.append(v_ref[0, rows, :])
        rkr.append(r * k * rk_ref[...])
        p_tot.append(jnp.exp(cs[L - 1:L, :]))
    units = [(j, h) for j in range(n_sub) for h in range(N_HEADS_B)]
    sl = lambda arr, u: arr[u[0]][:, u[1] * hd:(u[1] + 1) * hd]
    aa = [_mm_nt(jnp.concatenate([sl(at, u), sl(rt, u)], axis=0),
                 jnp.concatenate([sl(bt, u), sl(kt, u)], axis=0)) for u in units]
    a_ab = [jnp.where(strict, x[0:L, 0:L], 0.0) for x in aa]
    a_low = [jnp.concatenate([jnp.where(strict, x[0:L, L:2 * L], 0.0),
                              jnp.where(incl, x[L:2 * L, L:2 * L], 0.0)], axis=0) for x in aa]
    a_rb = [jnp.where(incl, x[L:2 * L, 0:L], 0.0) for x in aa]
    av = [_mm(x, sl(vv, u)) for x, u in zip(a_low, units)]
    kv = [_mm_tn(sl(kl, u), sl(vv, u)) for u in units]
    pw = [jnp.where(same_sub, x, 0.0) for x in a_ab]
    t = [eye + x for x in pw]
    for _ in range(3):
        pw = [_mm(x, x) for x in pw]
        t = [x + _mm(x, y) for x, y in zip(t, pw)]
    width = SUB
    while width < L:
        m = ((ti // width) == (si // width) + 1) & ((ti // (2 * width)) == (si // (2 * width)))
        ot = [_mm(jnp.where(m, x, 0.0), y) for x, y in zip(a_ab, t)]
        t = [x + _mm(x, y) for x, y in zip(t, ot)]
        width *= 2
    wu = [_mm(x, jnp.concatenate([sl(at, u), y[0:L]], axis=1)) for x, y, u in zip(t, av, units)]
    qy = [jnp.concatenate([sl(rt, u), y[L:2 * L]], axis=1) + _mm(x, w)
          for x, y, w, u in zip(a_rb, av, wu, units)]
    gh = [_mm_tn(sl(bl, u), w) + jnp.concatenate([eye * sl(p_tot, u), y], axis=1)
          for w, y, u in zip(wu, kv, units)]
    bonus = [jnp.sum(sl(rkr, u), axis=-1, keepdims=True) * sl(vv, u) for u in units]
    heads = lambda xs, j, c: jnp.concatenate([x[:, c] for x in xs[j * N_HEADS_B:(j + 1) * N_HEADS_B]], axis=1)
    lo, hi = slice(0, hd), slice(hd, 2 * hd)
    for j in range(n_sub):
        rows = slice(j * L, (j + 1) * L)
        qh_ref[0, rows, :] = heads(qy, j, lo)
        y0_ref[0, rows, :] = heads(qy, j, hi)
        bonus_ref[0, rows, :] = heads(bonus, j, slice(None))
        g_ref[0, j] = heads(gh, j, lo)
        h_ref[0, j] = heads(gh, j, hi)


def _rwkv_chunk(r, lw, k, v, kn, kb, r_k, bsz, s, n_sub=2):
    nch = s // CHUNK
    blk = pl.BlockSpec((1, n_sub * CHUNK, WIDTH_B), lambda b, c: (b, c, 0))
    mat = pl.BlockSpec((1, n_sub, HEAD_DIM_B, WIDTH_B), lambda b, c: (b, c, 0, 0))
    r3 = lambda a: a.reshape(bsz, s, WIDTH_B)
    seq_sds = jax.ShapeDtypeStruct((bsz, s, WIDTH_B), F32)
    mat_sds = jax.ShapeDtypeStruct((bsz, nch, HEAD_DIM_B, WIDTH_B), F32)
    return pl.pallas_call(
        functools.partial(_rwkv_chunk_kernel, n_sub=n_sub),
        grid=(bsz, nch // n_sub),
        in_specs=[blk] * 6 + [_full((1, WIDTH_B))],
        out_specs=[blk, blk, blk, mat, mat],
        out_shape=[seq_sds, seq_sds, seq_sds, mat_sds, mat_sds],
        compiler_params=_params(("parallel", "parallel")),
        name="rwkv_chunk",
    )(r3(r), r3(lw), r3(k), r3(v), r3(kn), r3(kb), r_k.reshape(1, WIDTH_B))


def _rwkv_scan_kernel(qh_ref, y0_ref, g_ref, h_ref, yn_ref, st_ref):
    hd = HEAD_DIM_B

    @pl.when(pl.program_id(0) == 0)
    def _():
        st_ref[...] = jnp.zeros_like(st_ref)

    for b in range(qh_ref.shape[0]):
        ys = []
        for h in range(N_HEADS_B):
            c = slice(h * hd, (h + 1) * hd)
            st = st_ref[b * N_HEADS_B + h]
            y = _mm_hi(qh_ref[b, :, c], st) + y0_ref[b, :, c]
            st_ref[b * N_HEADS_B + h] = _mm_hi(g_ref[b, 0, :, c], st) + h_ref[b, 0, :, c]
            mean = jnp.mean(y, axis=-1, keepdims=True)
            d = y - mean
            var = jnp.mean(d * d, axis=-1, keepdims=True)
            ys.append(d * lax.rsqrt(var + GN_EPS))
        yn_ref[b] = jnp.concatenate(ys, axis=1)


def _rwkv_scan(qh, y0, g, h, bsz, s):
    nch = s // CHUNK
    blk = pl.BlockSpec((bsz, CHUNK, WIDTH_B), lambda c: (0, c, 0))
    mat = pl.BlockSpec((bsz, 1, HEAD_DIM_B, WIDTH_B), lambda c: (0, c, 0, 0))
    return pl.pallas_call(
        _rwkv_scan_kernel,
        grid=(nch,),
        in_specs=[blk, blk, mat, mat],
        out_specs=blk,
        out_shape=jax.ShapeDtypeStruct((bsz, s, WIDTH_B), F32),
        scratch_shapes=[pltpu.VMEM((bsz * N_HEADS_B, HEAD_DIM_B, HEAD_DIM_B), F32)],
        compiler_params=_params(("arbitrary",)),
        name="rwkv_scan",
    )(qh, y0, g, h)


SUBLANES = 8
TOKEN_TILE = (SUBLANES, D_MODEL // SUBLANES)
assert TOKEN_TILE[1] == LANES


def _token_tile_spec(rows):
    return pl.BlockSpec((rows,) + TOKEN_TILE, lambda i, *_: (i, 0, 0))


def _store_token_tiles(ref, x):
    for c in range(SUBLANES):
        ref[:, c, :] = x[:, c * LANES:(c + 1) * LANES]


def _load_token_tiles(ref, slot, start, rows):
    return jnp.concatenate([ref[slot, pl.ds(start, rows), c, :] for c in range(SUBLANES)], axis=1)


def _merge_kernel(x_ref, oa_ref, yn_ref, bonus_ref, g_ref, ga_ref, gb_ref, lnw_ref, lnb_ref, wa_ref, wb_ref,
                  wo_ref, gf_ref, wr_ref, br_ref, x1_ref, h2_ref, route_ref):
    ob = (yn_ref[...] * lnw_ref[...] + lnb_ref[...] + bonus_ref[...]) * g_ref[...]
    merged = (jax.nn.sigmoid(ga_ref[...]) * jnp.dot(oa_ref[...], wa_ref[...], preferred_element_type=F32)
              + jax.nn.sigmoid(gb_ref[...]) * _mm(ob, wb_ref[...]))
    x1 = x_ref[...] + _mm(merged, wo_ref[...])
    x1_ref[...] = x1
    h2 = x1 * lax.rsqrt(jnp.mean(x1 * x1, axis=-1, keepdims=True) + RMS_EPS) * gf_ref[...]
    _store_token_tiles(h2_ref, h2)
    logits = _mm_hi(h2, wr_ref[...]) + br_ref[...]
    lane = _iota(logits.shape, 1)
    big = 4 * LANES
    is_grp = lane < N_GROUPS
    lg = jnp.where(is_grp, logits, NEG_INF)
    mg = jnp.max(lg, axis=-1, keepdims=True)
    gidx = jnp.min(jnp.where(lg == mg, lane, big), axis=-1, keepdims=True)
    pg = 1.0 / jnp.sum(jnp.where(is_grp, jnp.exp(lg - mg), 0.0), axis=-1, keepdims=True)
    in_grp = (lane >= N_GROUPS) & (((lane - N_GROUPS) >> 3) == gidx) & (lane < N_GROUPS + N_EXPERTS)
    le = jnp.where(in_grp, logits, NEG_INF)
    m1 = jnp.max(le, axis=-1, keepdims=True)
    i1 = jnp.min(jnp.where(le == m1, lane, big), axis=-1, keepdims=True)
    le2 = jnp.where(lane == i1, NEG_INF, le)
    m2 = jnp.max(le2, axis=-1, keepdims=True)
    i2 = jnp.min(jnp.where(le2 == m2, lane, big), axis=-1, keepdims=True)
    e2 = jnp.exp(m2 - m1)
    inv = pg / (1.0 + e2)
    route_ref[...] = jnp.where(lane == 0, (i1 - N_GROUPS).astype(F32),
                               jnp.where(lane == 1, (i2 - N_GROUPS).astype(F32),
                                         jnp.where(lane == 2, inv, jnp.where(lane == 3, inv * e2, 0.0))))


def _merge(x2, oa, yn, bonus, g, ga, gb, ln_w, ln_b, w_a, w_b, w_o, g_ffn, w_grp, b_grp, w_exp, b_exp, tm=256):
    n = x2.shape[0]
    w_r = jnp.pad(jnp.concatenate([w_grp, w_exp], axis=1), ((0, 0), (0, LANES - N_GROUPS - N_EXPERTS)))
    b_r = jnp.pad(jnp.concatenate([b_grp, b_exp]), (0, LANES - N_GROUPS - N_EXPERTS)).reshape(1, LANES)
    row = lambda wd: pl.BlockSpec((tm, wd), lambda i: (i, 0))
    vec = lambda a: a.reshape(1, -1)
    wa, wb, wo = w_a.astype(BF16), w_b.astype(BF16), w_o.astype(BF16)
    return pl.pallas_call(
        _merge_kernel,
        grid=(n // tm,),
        in_specs=[row(D_MODEL), row(WIDTH_A), row(WIDTH_B), row(WIDTH_B), row(WIDTH_B), row(D_MODEL), row(D_MODEL),
                  _full((1, WIDTH_B)), _full((1, WIDTH_B)), _full(wa.shape), _full(wb.shape), _full(wo.shape),
                  _full((1, D_MODEL)), _full(w_r.shape), _full((1, LANES))],
        out_specs=[row(D_MODEL), _token_tile_spec(tm), row(LANES)],
        out_shape=[jax.ShapeDtypeStruct((n, D_MODEL), F32), jax.ShapeDtypeStruct((n,) + TOKEN_TILE, F32),
                   jax.ShapeDtypeStruct((n, LANES), F32)],
        compiler_params=_params(("parallel",)),
        name="merge",
    )(x2, oa, yn, bonus, g, ga, gb, vec(ln_w), vec(ln_b), wa, wb, wo, vec(g_ffn), w_r, b_r)


DMA_UNROLL = 8


def _gather_start(idx_ref, src_hbm, buf, sem, slot, rows):
    def body(r8, carry):
        for u in range(DMA_UNROLL):
            r = r8 * DMA_UNROLL + u
            pltpu.make_async_copy(src_hbm.at[pl.ds(idx_ref[0, 0, r], 1)], buf.at[slot, pl.ds(r, 1)],
                                  sem.at[slot]).start(priority=u % 2)
        return carry
    lax.fori_loop(0, rows // DMA_UNROLL, body, 0)


def _gather_wait(src_hbm, buf, sem, slot, rows):
    pltpu.make_async_copy(src_hbm.at[pl.ds(0, rows)], buf.at[slot], sem.at[slot]).wait()


def _gather_pipeline(idx_ref, idx_next_ref, src_hbm, buf, sem, rows):
    i = pl.program_id(0)
    slot = i % 2

    @pl.when(i == 0)
    def _():
        _gather_start(idx_ref, src_hbm, buf, sem, 0, rows)

    @pl.when(i + 1 < pl.num_programs(0))
    def _():
        _gather_start(idx_next_ref, src_hbm, buf, sem, 1 - slot, rows)

    _gather_wait(src_hbm, buf, sem, slot, rows)
    return slot


def _moe_kernel(be_ref, tok_ref, tok_next_ref, h2_hbm, w13_ref, w2_ref, y_ref, buf, sem):
    slot = _gather_pipeline(tok_ref, tok_next_ref, h2_hbm, buf, sem, MOE_BLOCK)
    xb = _load_token_tiles(buf, slot, 0, MOE_BLOCK)
    a = _mm(xb, w13_ref[0])
    hid = jax.nn.silu(a[:, 0:D_EXPERT]) * a[:, D_EXPERT:2 * D_EXPERT]
    _store_token_tiles(y_ref, _mm(hid, w2_ref[0]))


def _moe(h2, blk_expert, buf_tok, w13, w2):
    n_pad = buf_tok.shape[0]
    n_blk = n_pad // MOE_BLOCK
    tok3 = buf_tok.reshape(n_blk, 1, MOE_BLOCK)
    smem_blk = lambda f: pl.BlockSpec((1, 1, MOE_BLOCK), f, memory_space=pltpu.SMEM)
    grid_spec = pltpu.PrefetchScalarGridSpec(
        num_scalar_prefetch=1,
        grid=(n_blk,),
        in_specs=[smem_blk(lambda i, be: (i, 0, 0)),
                  smem_blk(lambda i, be: (jnp.minimum(i + 1, n_blk - 1), 0, 0)),
                  pl.BlockSpec(memory_space=pl.ANY),
                  pl.BlockSpec((1, D_MODEL, 2 * D_EXPERT), lambda i, be: (be[i], 0, 0)),
                  pl.BlockSpec((1, D_EXPERT, D_MODEL), lambda i, be: (be[i], 0, 0))],
        out_specs=_token_tile_spec(MOE_BLOCK),
        scratch_shapes=[pltpu.VMEM((2, MOE_BLOCK) + TOKEN_TILE, F32), pltpu.SemaphoreType.DMA((2,))],
    )
    return pl.pallas_call(
        _moe_kernel,
        grid_spec=grid_spec,
        out_shape=jax.ShapeDtypeStruct((n_pad,) + TOKEN_TILE, F32),
        compiler_params=_params(("arbitrary",)),
        name="moe",
    )(blk_expert, tok3, tok3, h2, w13, w2)


def _final_kernel(pos_ref, pos_next_ref, x1_ref, route_ref, g_ref, y_hbm, o_ref, buf, sem, *, tm):
    slot = _gather_pipeline(pos_ref, pos_next_ref, y_hbm, buf, sem, TOP_K * tm)
    x = x1_ref[...]
    for k in range(TOP_K):
        x = x + route_ref[:, TOP_K + k:TOP_K + k + 1] * _load_token_tiles(buf, slot, k * tm, tm)
    o_ref[...] = x * lax.rsqrt(jnp.mean(x * x, axis=-1, keepdims=True) + RMS_EPS) * g_ref[...]


def _final(x1, route, y_pad, pos, g_final, tm=256):
    n = x1.shape[0]
    nt = n // tm
    pos3 = pos.reshape(nt, tm, TOP_K).transpose(0, 2, 1).reshape(nt, 1, TOP_K * tm)
    smem_blk = lambda f: pl.BlockSpec((1, 1, TOP_K * tm), f, memory_space=pltpu.SMEM)
    row = lambda wd: pl.BlockSpec((tm, wd), lambda i: (i, 0))
    return pl.pallas_call(
        functools.partial(_final_kernel, tm=tm),
        grid=(nt,),
        in_specs=[smem_blk(lambda i: (i, 0, 0)), smem_blk(lambda i: (jnp.minimum(i + 1, nt - 1), 0, 0)),
                  row(D_MODEL), row(LANES), _full((1, D_MODEL)), pl.BlockSpec(memory_space=pl.ANY)],
        out_specs=row(D_MODEL),
        out_shape=jax.ShapeDtypeStruct((n, D_MODEL), F32),
        scratch_shapes=[pltpu.VMEM((2, TOP_K * tm) + TOKEN_TILE, F32), pltpu.SemaphoreType.DMA((2,))],
        compiler_params=_params(("arbitrary",)),
        name="final",
    )(pos3, pos3, x1, route, g_final.reshape(1, D_MODEL), y_pad)


def _dispatch_plan(route, n_tok):
    e_flat = route[:, 0:TOP_K].astype(I32).reshape(n_tok * TOP_K)
    onehot = (e_flat[:, None] == jnp.arange(N_EXPERTS, dtype=I32)[None, :]).astype(I32)
    csum = jnp.cumsum(onehot, axis=0)
    counts = csum[-1]
    rank = jnp.sum(jnp.where(onehot > 0, csum - 1, 0), axis=1)
    padded = (counts + MOE_BLOCK - 1) // MOE_BLOCK * MOE_BLOCK
    pad_end = jnp.cumsum(padded)
    pad_start = pad_end - padded
    dest = pad_start[e_flat] + rank
    n_pad = n_tok * TOP_K + N_EXPERTS * MOE_BLOCK
    tok_flat = jnp.repeat(jnp.arange(n_tok, dtype=I32), TOP_K)
    buf_tok = jnp.zeros((n_pad,), I32).at[dest].set(tok_flat)
    n_blk = n_pad // MOE_BLOCK
    blk_expert = jnp.minimum(jnp.searchsorted(pad_end, jnp.arange(n_blk, dtype=I32) * MOE_BLOCK, side="right"),
                             N_EXPERTS - 1).astype(I32)
    return buf_tok, blk_expert, dest.reshape(n_tok, TOP_K)


def kernel(x, positions, g_mix, w_in, cmp_pe_k, cmp_w1_k, cmp_b1_k, cmp_w2_k, cmp_b2_k, cmp_pe_v, cmp_w1_v, cmp_b1_v, cmp_w2_v, cmp_b2_v, rwkv_mu, rwkv_w0, rwkv_w2, rwkv_a0, rwkv_a2, rwkv_g2, rwkv_k_k, rwkv_k_a, rwkv_r_k, rwkv_ln_w, rwkv_ln_b, w_proj_a, w_proj_b, w_out, g_ffn, w_grp, b_grp, w_exp, b_exp, e_w1, e_w3, e_w2, g_final):
    bsz, s, _ = x.shape
    assert w_in.shape[0] == 1, "single-layer stack"
    n_tok = bsz * s
    x2 = x.reshape(n_tok, D_MODEL)
    half = HEAD_DIM // 2
    freqs = jnp.power(ROPE_THETA, -jnp.arange(half, dtype=F32) / half)
    ang = positions.astype(F32).reshape(n_tok, 1) * freqs
    cs = jnp.tile(jnp.cos(ang), (1, 4))
    sn = jnp.tile(jnp.concatenate([-jnp.sin(ang), jnp.sin(ang)], axis=1), (1, 2))
    q, kc, ksl, kwn, vc, vsl, vwn, ng, prw, ga, gb = _inproj(x2, g_mix[0], cs, sn, w_in[0])
    k_c = _compress(kc, cmp_pe_k[0], cmp_w1_k[0], cmp_b1_k[0], cmp_w2_k[0], cmp_b2_k[0], bsz, s, 0.0)
    v_c = _compress(vc, cmp_pe_v[0], cmp_w1_v[0], cmp_b1_v[0], cmp_w2_v[0], cmp_b2_v[0], bsz, s, 1.0)
    o_a = _nsa(q, k_c, v_c, ksl, vsl, kwn, vwn, ng, bsz, s).reshape(n_tok, WIDTH_A)
    r, lw, k, v, kn, kb, g = _rwkv_prep(prw, rwkv_mu[0], rwkv_w0[0], rwkv_w2[0], rwkv_a0[0], rwkv_a2[0],
                                        rwkv_g2[0], rwkv_k_k[0], rwkv_k_a[0], s)
    qh, y0, bonus, gm, hm = _rwkv_chunk(r, lw, k, v, kn, kb, rwkv_r_k[0], bsz, s)
    yn = _rwkv_scan(qh, y0, gm, hm, bsz, s)
    x1, h2, route = _merge(x2, o_a, yn.reshape(n_tok, WIDTH_B), bonus.reshape(n_tok, WIDTH_B), g, ga, gb,
                           rwkv_ln_w[0], rwkv_ln_b[0], w_proj_a[0], w_proj_b[0], w_out[0], g_ffn[0],
                           w_grp[0], b_grp[0], w_exp[0], b_exp[0])
    buf_tok, blk_expert, pos = _dispatch_plan(route, n_tok)
    w13 = jnp.concatenate([e_w1[0], e_w3[0]], axis=-1).astype(BF16)
    y_pad = _moe(h2, blk_expert, buf_tok, w13, e_w2[0].astype(BF16))
    out = _final(x1, route, y_pad, pos, g_final)
    return out.reshape(bsz, s, D_MODEL)
```

```python
import functools

import numpy as np
import jax
import jax.numpy as jnp
from jax import lax
from jax.experimental import pallas as pl
from jax.experimental.pallas import tpu as pltpu

F32 = jnp.float32
BF16 = jnp.bfloat16
I32 = jnp.int32

D_MODEL = 1024
N_HEADS_A = 8
N_KV_GROUPS = 2
HEADS_PER_GROUP = N_HEADS_A // N_KV_GROUPS
HEAD_DIM = 64
L_CMP = 32
D_CMP = 16
CMP_HIDDEN = 256
L_SLC = 64
N_SELECT = 16
WINDOW = 512
Q_BLOCK = 128
ROPE_THETA = 10000.0
FORCE_SCORE = 1e4
NEG_INF = -1e30
N_HEADS_B = 8
HEAD_DIM_B = 64
W_LORA = 64
A_LORA = 64
G_LORA = 128
GN_EPS = 64e-5
N_GROUPS = 4
EXPERTS_PER_GROUP = 8
N_EXPERTS = N_GROUPS * EXPERTS_PER_GROUP
TOP_K = 2
D_EXPERT = 256
MOE_BLOCK = 256
RMS_EPS = 1e-6
WIDTH_A = N_HEADS_A * HEAD_DIM
KV_WIDTH = N_KV_GROUPS * HEAD_DIM
WIDTH_B = N_HEADS_B * HEAD_DIM_B
C_RWKV = 3 * WIDTH_B + W_LORA + A_LORA + G_LORA

LANES = 128
CHUNK = 64
SUB = 16
KEY_TILE = 1024
BLOCKS_PER_TILE = KEY_TILE // L_SLC
VMEM_LIMIT = 56 * 1024 * 1024

HI = lax.Precision.HIGHEST


def _mm(a, b):
    return jnp.dot(a.astype(BF16), b.astype(BF16), preferred_element_type=F32)


def _mm_nt(a, b):
    return lax.dot_general(a.astype(BF16), b.astype(BF16), (((1,), (1,)), ((), ())), preferred_element_type=F32)


def _mm_tn(a, b):
    return lax.dot_general(a.astype(BF16), b.astype(BF16), (((0,), (0,)), ((), ())), preferred_element_type=F32)


def _mm_hi(a, b):
    return jnp.dot(a, b, preferred_element_type=F32, precision=HI)


def _iota(shape, dim):
    return lax.broadcasted_iota(I32, shape, dim)


def _params(sem):
    return pltpu.CompilerParams(dimension_semantics=sem, vmem_limit_bytes=VMEM_LIMIT)


def _full(shape):
    nd = len(shape)
    return pl.BlockSpec(shape, lambda *_: (0,) * nd)


Q_PAD = N_HEADS_A * LANES
KV_PAD = N_KV_GROUPS * LANES


def _inproj_kernel(x_ref, g_ref, cs_ref, sn_ref, wr_ref, wv_ref, wn_ref, ww_ref, wg_ref,
                   q_ref, kc_ref, ksl_ref, kwn_ref, vc_ref, vsl_ref, vwn_ref, ng_ref, prw_ref, ga_ref, gb_ref):
    x = x_ref[...]
    h = (x * lax.rsqrt(jnp.mean(x * x, axis=-1, keepdims=True) + RMS_EPS) * g_ref[...]).astype(BF16)
    pr = jnp.dot(h, wr_ref[...], preferred_element_type=F32)
    cs = cs_ref[...]
    sn = sn_ref[...]
    first_half = (_iota((1, LANES), 1) & (HEAD_DIM // 2)) == 0
    chunks = []
    for c in range(pr.shape[1] // LANES):
        x = pr[:, c * LANES:(c + 1) * LANES]
        swapped = jnp.where(first_half, pltpu.roll(x, LANES - HEAD_DIM // 2, 1), pltpu.roll(x, HEAD_DIM // 2, 1))
        chunks.append(x * cs + swapped * sn)
    ro = jnp.concatenate(chunks, axis=1)
    q_ref[...] = (ro[:, 0:Q_PAD] * (HEAD_DIM ** -0.5)).astype(BF16)
    o = Q_PAD
    kc_ref[...] = ro[:, o:o + KV_WIDTH]
    tm = x_ref.shape[0]
    blk = ((pl.program_id(0) * tm + _iota((tm, 1), 0)) >> 6) & (BLOCKS_PER_TILE - 1)
    code = jnp.where((_iota((1, KV_PAD), 1) & (LANES - 1)) == HEAD_DIM + blk, NEG_INF, 0.0)
    ksl_ref[...] = (ro[:, o + KV_WIDTH:o + KV_WIDTH + KV_PAD] + code).astype(BF16)
    kwn_ref[...] = ro[:, o + KV_WIDTH + KV_PAD:o + KV_WIDTH + 2 * KV_PAD].astype(BF16)
    v = jnp.dot(h, wv_ref[...], preferred_element_type=F32)
    vc_ref[...] = v[:, 0:KV_WIDTH]
    ones = ((_iota((1, 2 * KV_PAD), 1) & (LANES - 1)) >= HEAD_DIM).astype(F32)
    vp = v[:, KV_WIDTH:KV_WIDTH + 2 * KV_PAD] + ones
    vsl_ref[...] = vp[:, 0:KV_PAD].astype(BF16)
    vwn_ref[...] = vp[:, KV_PAD:2 * KV_PAD].astype(BF16)
    ng_ref[...] = jnp.dot(h, wn_ref[...], preferred_element_type=F32)
    prw_ref[...] = jnp.dot(h, ww_ref[...], preferred_element_type=F32)
    gg = jnp.dot(h, wg_ref[...], preferred_element_type=F32)
    ga_ref[...] = gg[:, 0:D_MODEL]
    gb_ref[...] = gg[:, D_MODEL:2 * D_MODEL]


def _pad_heads(w):
    d, wd = w.shape
    return jnp.pad(w.reshape(d, wd // HEAD_DIM, HEAD_DIM), ((0, 0), (0, 0), (0, LANES - HEAD_DIM))).reshape(d, -1)


def _inproj(x2, g_mix, cs, sn, w_in, tm=256):
    n = x2.shape[0]
    o = 0
    cols = {}
    for name, wd in (("q", WIDTH_A), ("kc", KV_WIDTH), ("vc", KV_WIDTH), ("ksl", KV_WIDTH), ("vsl", KV_WIDTH),
                     ("kwn", KV_WIDTH), ("vwn", KV_WIDTH), ("ng", 3 * N_HEADS_A), ("rw", C_RWKV),
                     ("ga", D_MODEL), ("gb", D_MODEL)):
        cols[name] = w_in[:, o:o + wd]
        o += wd
    w_rope = jnp.concatenate(
        [_pad_heads(cols["q"]), cols["kc"], _pad_heads(cols["ksl"]), _pad_heads(cols["kwn"])], axis=1)
    w_v = jnp.concatenate([cols["vc"], _pad_heads(cols["vsl"]), _pad_heads(cols["vwn"])], axis=1)
    w_ng = jnp.pad(cols["ng"], ((0, 0), (0, LANES - 3 * N_HEADS_A)))
    w_gate = jnp.concatenate([cols["ga"], cols["gb"]], axis=1)
    ws = [w.astype(BF16) for w in (w_rope, w_v, w_ng, cols["rw"], w_gate)]
    row = lambda wd: pl.BlockSpec((tm, wd), lambda i: (i, 0))
    outs = [(Q_PAD, BF16), (KV_WIDTH, F32), (KV_PAD, BF16), (KV_PAD, BF16), (KV_WIDTH, F32), (KV_PAD, BF16),
            (KV_PAD, BF16), (LANES, F32), (C_RWKV, F32), (D_MODEL, F32), (D_MODEL, F32)]
    return pl.pallas_call(
        _inproj_kernel,
        grid=(n // tm,),
        in_specs=[row(D_MODEL), _full((1, D_MODEL)), row(LANES), row(LANES)] + [_full(w.shape) for w in ws],
        out_specs=[row(wd) for wd, _ in outs],
        out_shape=[jax.ShapeDtypeStruct((n, wd), dt) for wd, dt in outs],
        compiler_params=_params(("parallel",)),
        name="inproj",
    )(x2, g_mix.reshape(1, D_MODEL), cs, sn, *ws)


def _compress_kernel(c_ref, pe_ref, w1_ref, b1_ref, w2_ref, b2_ref, o_ref, *, pad_value):
    half = D_CMP * HEAD_DIM
    c = c_ref[0, 0]
    w1 = w1_ref[...]
    z1 = _mm(c, w1[0:half])
    z2 = _mm(c, w1[half:2 * half])
    z2 = jnp.concatenate([z2[1:], z2[:1]], axis=0)
    pb = _mm(pe_ref[...], w1)[0:1] + b1_ref[...]
    hid = jax.nn.gelu(z1 + z2 + pb)
    out = _mm(hid, w2_ref[...]) + b2_ref[...]
    o_ref[0, 0] = jnp.concatenate([out, jnp.full(out.shape, pad_value, F32)], axis=1).astype(o_ref.dtype)


def _compress(kv, pe, w1, b1, w2, b2, bsz, s, pad_value):
    nch = s // D_CMP
    c = kv.reshape(bsz, nch, D_CMP, N_KV_GROUPS, HEAD_DIM).transpose(0, 3, 1, 2, 4).reshape(
        bsz, N_KV_GROUPS, nch, D_CMP * HEAD_DIM)
    pe8 = jnp.broadcast_to(pe.reshape(1, L_CMP * HEAD_DIM), (8, L_CMP * HEAD_DIM))
    return pl.pallas_call(
        functools.partial(_compress_kernel, pad_value=pad_value),
        grid=(bsz, N_KV_GROUPS),
        in_specs=[pl.BlockSpec((1, 1, nch, D_CMP * HEAD_DIM), lambda b, g: (b, g, 0, 0)),
                  _full(pe8.shape), _full(w1.shape), _full((1, CMP_HIDDEN)), _full(w2.shape), _full((1, HEAD_DIM))],
        out_specs=pl.BlockSpec((1, 1, nch, LANES), lambda b, g: (b, g, 0, 0)),
        out_shape=jax.ShapeDtypeStruct((bsz, N_KV_GROUPS, nch, LANES), BF16),
        compiler_params=_params(("parallel", "parallel")),
        name="compress",
    )(c, pe8, w1, b1.reshape(1, CMP_HIDDEN), w2, b2.reshape(1, HEAD_DIM))


def _nsa_kernel(q_ref, kc_ref, vc_ref, ksl_ref, vsl_ref, kwn_ref, vwn_ref, ng_ref, ov_ref, o_ref, *, n_top):
    n_cmp = kc_ref.shape[2]
    n_slc = ov_ref.shape[1]
    s0 = pl.program_id(1) * Q_BLOCK
    hq = HEADS_PER_GROUP * Q_BLOCK
    t_row = s0 + (_iota((hq, 1), 0) & (Q_BLOCK - 1))
    t_q = s0 + _iota((Q_BLOCK, 1), 0)
    sig = jax.nn.sigmoid(ng_ref[0])
    win_start = pl.multiple_of(jnp.maximum(s0 - WINDOW, 0), LANES)
    n_tiles = s0 // KEY_TILE + 1
    groups = range(N_KV_GROUPS)
    lanes = lambda g: slice(g * LANES, (g + 1) * LANES)
    qg = [jnp.concatenate([q_ref[0, :, lanes(g * HEADS_PER_GROUP + h)] for h in range(HEADS_PER_GROUP)], axis=0)
          for g in groups]

    o_c, sel = [], []
    for g in groups:
        s_c = _mm_nt(qg[g], kc_ref[0, g])
        valid_c = (_iota((1, n_cmp), 1) * D_CMP + (L_CMP - 1)) <= t_row
        s_c = jnp.where(valid_c, s_c, NEG_INF)
        e = jnp.where(valid_c, jnp.exp(s_c - jnp.max(s_c, axis=-1, keepdims=True)), 0.0)
        l = jnp.sum(e, axis=-1, keepdims=True)
        p_c = e * (1.0 / jnp.where(l > 0.0, l, 1.0))
        o_c.append(_mm(p_c, vc_ref[0, g]))

        p_sum = p_c[0:Q_BLOCK]
        for h in range(1, HEADS_PER_GROUP):
            p_sum = p_sum + p_c[h * Q_BLOCK:(h + 1) * Q_BLOCK]
        imp_t = _mm_hi(p_sum, ov_ref[...]).T
        j = _iota((n_slc, Q_BLOCK), 0)
        cur = (s0 + _iota((n_slc, Q_BLOCK), 1)) >> 6
        forced = (j == 0) | (j == cur) | (j == cur - 1)
        score = jnp.where(forced, FORCE_SCORE, jnp.where(j <= cur, imp_t, -1.0))
        sel_t = jnp.zeros((n_slc, Q_BLOCK), F32)
        for _ in range(n_top):
            m = jnp.max(score, axis=0, keepdims=True)
            first = jnp.min(jnp.where(score == m, j, n_slc), axis=0, keepdims=True)
            hit = j == first
            sel_t = jnp.where(hit, 1.0, sel_t)
            score = jnp.where(hit, -3e38, score)
        sel.append(1.0 - sel_t.T)

    lane_q = _iota((Q_BLOCK, LANES), 1)
    in_code = (lane_q >= HEAD_DIM) & (lane_q < HEAD_DIM + BLOCKS_PER_TILE)

    def sel_step(kt, carry, diagonal):
        k0 = pl.multiple_of(kt * KEY_TILE, KEY_TILE)
        shift = (HEAD_DIM - BLOCKS_PER_TILE * kt) & (LANES - 1)
        out = []
        for g in groups:
            m_i, acc = carry[2 * g], carry[2 * g + 1]
            code = jnp.where(in_code, pltpu.roll(sel[g], shift, 1), 0.0).astype(BF16)
            qa = qg[g] + jnp.concatenate([code] * HEADS_PER_GROUP, axis=0)
            s = _mm_nt(qa, ksl_ref[0, pl.ds(k0, KEY_TILE), lanes(g)])
            if diagonal:
                bias = jnp.where((k0 + _iota((Q_BLOCK, KEY_TILE), 1)) <= t_q, 0.0, NEG_INF)
                s = (s.reshape(HEADS_PER_GROUP, Q_BLOCK, KEY_TILE) + bias[None]).reshape(hq, KEY_TILE)
            m_new = jnp.maximum(m_i, jnp.max(s, axis=-1, keepdims=True))
            p = jnp.exp(s - m_new)
            acc = jnp.exp(m_i - m_new) * acc + _mm(p, vsl_ref[0, pl.ds(k0, KEY_TILE), lanes(g)])
            out += [m_new, acc]
        return tuple(out)

    init = (jnp.full((hq, 1), NEG_INF, F32), jnp.zeros((hq, LANES), F32)) * N_KV_GROUPS
    fin = lax.fori_loop(0, n_tiles - 1, lambda kt, c: sel_step(kt, c, False), init)
    fin = sel_step(n_tiles - 1, fin, True)

    heads_out = []
    for g in groups:
        acc_s = fin[2 * g + 1]
        o_s = acc_s * (1.0 / acc_s[:, HEAD_DIM:HEAD_DIM + 1])
        kw = kwn_ref[0, pl.ds(win_start, WINDOW + Q_BLOCK), lanes(g)]
        vw = vwn_ref[0, pl.ds(win_start, WINDOW + Q_BLOCK), lanes(g)]
        pos_w = win_start + _iota((1, WINDOW + Q_BLOCK), 1)
        s_w = jnp.where((pos_w <= t_row) & (pos_w > t_row - WINDOW), _mm_nt(qg[g], kw), NEG_INF)
        acc_w = _mm(jnp.exp(s_w - jnp.max(s_w, axis=-1, keepdims=True)), vw)
        o_w = acc_w * (1.0 / acc_w[:, HEAD_DIM:HEAD_DIM + 1])
        for h in range(HEADS_PER_GROUP):
            hh = g * HEADS_PER_GROUP + h
            r = slice(h * Q_BLOCK, (h + 1) * Q_BLOCK)
            mix = (sig[:, 3 * hh:3 * hh + 1] * o_c[g][r] + sig[:, 3 * hh + 1:3 * hh + 2] * o_s[r]
                   + sig[:, 3 * hh + 2:3 * hh + 3] * o_w[r])
            heads_out.append(mix[:, 0:HEAD_DIM])
    o_ref[0] = jnp.concatenate(heads_out, axis=1).astype(o_ref.dtype)


def _nsa(q, kc, vc, ksl, vsl, kwn, vwn, ng, bsz, s):
    n_cmp = s // D_CMP
    n_slc = s // L_SLC
    ii = np.arange(n_cmp)[:, None]
    jj = np.arange(n_slc)[None, :]
    assert n_slc <= LANES and s % KEY_TILE == 0
    overlap = (ii * D_CMP < (jj + 1) * L_SLC) & (ii * D_CMP + L_CMP > jj * L_SLC)
    overlap = jnp.asarray(np.pad(overlap, ((0, 0), (0, LANES - n_slc))), F32)
    seq = lambda wd: pl.BlockSpec((1, s, wd), lambda b, i: (b, 0, 0))
    blk = lambda wd: pl.BlockSpec((1, Q_BLOCK, wd), lambda b, i: (b, i, 0))
    cmp_spec = pl.BlockSpec((1, N_KV_GROUPS, n_cmp, LANES), lambda b, i: (b, 0, 0, 0))
    r3 = lambda a: a.reshape(bsz, s, a.shape[-1])
    return pl.pallas_call(
        functools.partial(_nsa_kernel, n_top=min(N_SELECT, n_slc)),
        grid=(bsz, s // Q_BLOCK),
        in_specs=[blk(Q_PAD), cmp_spec, cmp_spec, seq(KV_PAD), seq(KV_PAD), seq(KV_PAD), seq(KV_PAD),
                  blk(LANES), _full(overlap.shape)],
        out_specs=blk(WIDTH_A),
        out_shape=jax.ShapeDtypeStruct((bsz, s, WIDTH_A), BF16),
        compiler_params=_params(("parallel", "arbitrary")),
        name="nsa",
    )(r3(q), kc, vc, r3(ksl), r3(vsl), r3(kwn), r3(vwn), r3(ng), overlap)


def _rwkv_prep_kernel(p_ref, pv_ref, mu_ref, w0_ref, w2_ref, a0_ref, a2_ref, g2_ref, kk_ref, ka_ref, bd_ref,
                      r_ref, lw_ref, k_ref, v_ref, kn_ref, kb_ref, g_ref, *, tiles_per_seq):
    p = p_ref[...]
    first = (pl.program_id(0) % tiles_per_seq) == 0
    prev_last = jnp.where(first, 0.0, pv_ref[7:8, :])
    p_prev = jnp.concatenate([prev_last, p[:-1]], axis=0)
    xs = p + (p_prev - p) * mu_ref[...]
    wb = WIDTH_B
    r = xs[:, 0:wb]
    k = xs[:, wb:2 * wb]
    v = xs[:, 2 * wb:3 * wb]
    o = 3 * wb
    xw = xs[:, o:o + W_LORA]
    xa = xs[:, o + W_LORA:o + W_LORA + A_LORA]
    xg = xs[:, o + W_LORA + A_LORA:o + W_LORA + A_LORA + G_LORA]
    z = w0_ref[...] + _mm_hi(jnp.tanh(xw), w2_ref[...])
    w_log = -(jnp.maximum(-z, 0.0) + jnp.log(1.0 + jnp.exp(-jnp.abs(z)))) - 0.5
    a = jax.nn.sigmoid(a0_ref[...] + _mm_hi(xa, a2_ref[...]))
    kk = k * kk_ref[...]
    ss = _mm_hi(kk * kk, bd_ref[...])
    kn = kk * (1.0 / jnp.maximum(jnp.sqrt(ss), 1e-12))
    r_ref[...] = r
    lw_ref[...] = -jnp.exp(w_log)
    k_ref[...] = k * (1.0 + (a - 1.0) * ka_ref[...])
    v_ref[...] = v
    kn_ref[...] = kn
    kb_ref[...] = kn * a
    g_ref[...] = _mm(jax.nn.sigmoid(xg), g2_ref[...])


def _rwkv_prep(prw, mu, w0, w2, a0, a2, g2, k_k, k_a, s, tm=256):
    n = prw.shape[0]
    hid = np.arange(WIDTH_B) // HEAD_DIM_B
    bd = jnp.asarray(hid[:, None] == hid[None, :], F32)
    row = lambda wd: pl.BlockSpec((tm, wd), lambda i: (i, 0))
    vec = lambda a: a.reshape(1, -1)
    return pl.pallas_call(
        functools.partial(_rwkv_prep_kernel, tiles_per_seq=s // tm),
        grid=(n // tm,),
        in_specs=[row(C_RWKV), pl.BlockSpec((8, C_RWKV), lambda i: (jnp.maximum(i * (tm // 8) - 1, 0), 0)),
                  _full((1, C_RWKV)), _full((1, WIDTH_B)), _full(w2.shape), _full((1, WIDTH_B)), _full(a2.shape),
                  _full(g2.shape), _full((1, WIDTH_B)), _full((1, WIDTH_B)), _full(bd.shape)],
        out_specs=[row(WIDTH_B)] * 7,
        out_shape=[jax.ShapeDtypeStruct((n, WIDTH_B), F32)] * 7,
        compiler_params=_params(("parallel",)),
        name="rwkv_prep",
    )(prw, prw, vec(mu), vec(w0), w2, vec(a0), a2, g2, vec(k_k), vec(k_a), bd)


def _rwkv_chunk_kernel(r_ref, lw_ref, k_ref, v_ref, kn_ref, kb_ref, rk_ref,
                       bonus_ref, tm_ref, *, n_sub):
    L = CHUNK
    hd = HEAD_DIM_B
    ti = _iota((L, L), 0)
    si = _iota((L, L), 1)
    strict = si < ti
    incl = si <= ti
    same_sub = (ti // SUB) == (si // SUB)
    eye = (ti == si).astype(F32)
    tri = incl.astype(F32)
    at, bt, kt, rt, bl, kl, vv, rkr, p_tot = [], [], [], [], [], [], [], [], []
    for j in range(n_sub):
        rows = slice(j * L, (j + 1) * L)
        lw = lw_ref[0, rows, :]
        cs = _mm_hi(tri, lw)
        p_inv = jnp.exp(-cs)
        p_end = jnp.exp(cs[L - 1:L, :] - cs)
        r = r_ref[0, rows, :]
        k = k_ref[0, rows, :]
        kb = kb_ref[0, rows, :]
        at.append(-kn_ref[0, rows, :] * jnp.exp(cs - lw))
        bt.append(kb * p_inv)
        kt.append(k * p_inv)
        rt.append(r * jnp.exp(cs))
        bl.append(kb * p_end)
        kl.append(k * p_end)
        vv.append(v_ref[0, rows, :])
        rkr.append(r * k * rk_ref[...])
        p_tot.append(jnp.exp(cs[L - 1:L, :]))
    units = [(j, h) for j in range(n_sub) for h in range(N_HEADS_B)]
    sl = lambda arr, u: arr[u[0]][:, u[1] * hd:(u[1] + 1) * hd]
    aa = [_mm_nt(jnp.concatenate([sl(at, u), sl(rt, u)], axis=0),
                 jnp.concatenate([sl(bt, u), sl(kt, u)], axis=0)) for u in units]
    a_ab = [jnp.where(strict, x[0:L, 0:L], 0.0) for x in aa]
    a_low = [jnp.concatenate([jnp.where(strict, x[0:L, L:2 * L], 0.0),
                              jnp.where(incl, x[L:2 * L, L:2 * L], 0.0)], axis=0) for x in aa]
    a_rb = [jnp.where(incl, x[L:2 * L, 0:L], 0.0) for x in aa]
    av = [_mm(x, sl(vv, u)) for x, u in zip(a_low, units)]
    kv = [_mm_tn(sl(kl, u), sl(vv, u)) for u in units]
    pw = [jnp.where(same_sub, x, 0.0) for x in a_ab]
    t = [eye + x for x in pw]
    for _ in range(3):
        pw = [_mm(x, x) for x in pw]
        t = [x + _mm(x, y) for x, y in zip(t, pw)]
    width = SUB
    while width < L:
        m = ((ti // width) == (si // width) + 1) & ((ti // (2 * width)) == (si // (2 * width)))
        ot = [_mm(jnp.where(m, x, 0.0), y) for x, y in zip(a_ab, t)]
        t = [x + _mm(x, y) for x, y in zip(t, ot)]
        width *= 2
    wu = [_mm(x, jnp.concatenate([sl(at, u), y[0:L]], axis=1)) for x, y, u in zip(t, av, units)]
    qy = [jnp.concatenate([sl(rt, u), y[L:2 * L]], axis=1) + _mm(x, w)
          for x, y, w, u in zip(a_rb, av, wu, units)]
    gh = [_mm_tn(sl(bl, u), w) + jnp.concatenate([eye * sl(p_tot, u), y], axis=1)
          for w, y, u in zip(wu, kv, units)]
    bonus = [jnp.sum(sl(rkr, u), axis=-1, keepdims=True) * sl(vv, u) for u in units]
    for j in range(n_sub):
        bonus_ref[0, j * L:(j + 1) * L, :] = jnp.concatenate(bonus[j * N_HEADS_B:(j + 1) * N_HEADS_B], axis=1)
    for x, y, (j, h) in zip(qy, gh, units):
        tm_ref[0, j, h, 0:L, :] = x
        tm_ref[0, j, h, L:2 * L, :] = y


def _rwkv_chunk(r, lw, k, v, kn, kb, r_k, bsz, s, n_sub=2):
    nch = s // CHUNK
    blk = pl.BlockSpec((1, n_sub * CHUNK, WIDTH_B), lambda b, c: (b, c, 0))
    mat = pl.BlockSpec((1, n_sub, N_HEADS_B, 2 * CHUNK, 2 * HEAD_DIM_B), lambda b, c: (b, c, 0, 0, 0))
    r3 = lambda a: a.reshape(bsz, s, WIDTH_B)
    return pl.pallas_call(
        functools.partial(_rwkv_chunk_kernel, n_sub=n_sub),
        grid=(bsz, nch // n_sub),
        in_specs=[blk] * 6 + [_full((1, WIDTH_B))],
        out_specs=[blk, mat],
        out_shape=[jax.ShapeDtypeStruct((bsz, s, WIDTH_B), F32),
                   jax.ShapeDtypeStruct((bsz, nch, N_HEADS_B, 2 * CHUNK, 2 * HEAD_DIM_B), F32)],
        compiler_params=_params(("parallel", "parallel")),
        name="rwkv_chunk",
    )(r3(r), r3(lw), r3(k), r3(v), r3(kn), r3(kb), r_k.reshape(1, WIDTH_B))


def _rwkv_scan_kernel(tm_ref, yn_ref, st_ref):
    L = CHUNK
    hd = HEAD_DIM_B

    @pl.when(pl.program_id(0) == 0)
    def _():
        eye = (_iota((hd, hd), 0) == _iota((hd, hd), 1)).astype(F32)
        for i in range(st_ref.shape[0]):
            st_ref[i] = jnp.concatenate([jnp.zeros((hd, hd), F32), eye], axis=0)

    for b in range(tm_ref.shape[0]):
        res = [_mm_hi(tm_ref[b, 0, h], st_ref[b * N_HEADS_B + h]) for h in range(N_HEADS_B)]
        ys = []
        for h in range(N_HEADS_B):
            st_ref[b * N_HEADS_B + h, 0:hd, :] = res[h][L:2 * L]
            y = res[h][0:L]
            mean = jnp.mean(y, axis=-1, keepdims=True)
            d = y - mean
            var = jnp.mean(d * d, axis=-1, keepdims=True)
            ys.append(d * lax.rsqrt(var + GN_EPS))
        yn_ref[b] = jnp.concatenate(ys, axis=1)


def _rwkv_scan(tmat, bsz, s):
    nch = s // CHUNK
    return pl.pallas_call(
        _rwkv_scan_kernel,
        grid=(nch,),
        in_specs=[pl.BlockSpec((bsz, 1, N_HEADS_B, 2 * CHUNK, 2 * HEAD_DIM_B), lambda c: (0, c, 0, 0, 0))],
        out_specs=pl.BlockSpec((bsz, CHUNK, WIDTH_B), lambda c: (0, c, 0)),
        out_shape=jax.ShapeDtypeStruct((bsz, s, WIDTH_B), F32),
        scratch_shapes=[pltpu.VMEM((bsz * N_HEADS_B, 2 * HEAD_DIM_B, HEAD_DIM_B), F32)],
        compiler_params=_params(("arbitrary",)),
        name="rwkv_scan",
    )(tmat)


SUBLANES = 8
TOKEN_TILE = (SUBLANES, D_MODEL // SUBLANES)
assert TOKEN_TILE[1] == LANES


def _token_tile_spec(rows):
    return pl.BlockSpec((rows,) + TOKEN_TILE, lambda i, *_: (i, 0, 0))


def _store_token_tiles(ref, x):
    for c in range(SUBLANES):
        ref[:, c, :] = x[:, c * LANES:(c + 1) * LANES]


def _gather_buffer(rows):
    return pltpu.VMEM((2, rows // SUBLANES, SUBLANES, SUBLANES, LANES), F32)


def _gathered_rows(buf, slot, start, rows):
    t0, nt = start // SUBLANES, rows // SUBLANES
    return jnp.concatenate([buf[slot, t0:t0 + nt, c].reshape(rows, LANES) for c in range(SUBLANES)], axis=1)


def _merge_kernel(x_ref, oa_ref, yn_ref, bonus_ref, g_ref, ga_ref, gb_ref, lnw_ref, lnb_ref, wa_ref, wb_ref,
                  wo_ref, gf_ref, wr_ref, br_ref, x1_ref, h2_ref, route_ref):
    ob = (yn_ref[...] * lnw_ref[...] + lnb_ref[...] + bonus_ref[...]) * g_ref[...]
    merged = (jax.nn.sigmoid(ga_ref[...]) * jnp.dot(oa_ref[...], wa_ref[...], preferred_element_type=F32)
              + jax.nn.sigmoid(gb_ref[...]) * _mm(ob, wb_ref[...]))
    x1 = x_ref[...] + _mm(merged, wo_ref[...])
    x1_ref[...] = x1
    h2 = x1 * lax.rsqrt(jnp.mean(x1 * x1, axis=-1, keepdims=True) + RMS_EPS) * gf_ref[...]
    _store_token_tiles(h2_ref, h2)
    logits = _mm_hi(h2, wr_ref[...]) + br_ref[...]
    lane = _iota(logits.shape, 1)
    big = 4 * LANES
    is_grp = lane < N_GROUPS
    lg = jnp.where(is_grp, logits, NEG_INF)
    mg = jnp.max(lg, axis=-1, keepdims=True)
    gidx = jnp.min(jnp.where(lg == mg, lane, big), axis=-1, keepdims=True)
    pg = 1.0 / jnp.sum(jnp.where(is_grp, jnp.exp(lg - mg), 0.0), axis=-1, keepdims=True)
    in_grp = (lane >= N_GROUPS) & (((lane - N_GROUPS) >> 3) == gidx) & (lane < N_GROUPS + N_EXPERTS)
    le = jnp.where(in_grp, logits, NEG_INF)
    m1 = jnp.max(le, axis=-1, keepdims=True)
    i1 = jnp.min(jnp.where(le == m1, lane, big), axis=-1, keepdims=True)
    le2 = jnp.where(lane == i1, NEG_INF, le)
    m2 = jnp.max(le2, axis=-1, keepdims=True)
    i2 = jnp.min(jnp.where(le2 == m2, lane, big), axis=-1, keepdims=True)
    e2 = jnp.exp(m2 - m1)
    inv = pg / (1.0 + e2)
    route_ref[...] = jnp.where(lane == 0, (i1 - N_GROUPS).astype(F32),
                               jnp.where(lane == 1, (i2 - N_GROUPS).astype(F32),
                                         jnp.where(lane == 2, inv, jnp.where(lane == 3, inv * e2, 0.0))))


def _merge(x2, oa, yn, bonus, g, ga, gb, ln_w, ln_b, w_a, w_b, w_o, g_ffn, w_grp, b_grp, w_exp, b_exp, tm=256):
    n = x2.shape[0]
    w_r = jnp.pad(jnp.concatenate([w_grp, w_exp], axis=1), ((0, 0), (0, LANES - N_GROUPS - N_EXPERTS)))
    b_r = jnp.pad(jnp.concatenate([b_grp, b_exp]), (0, LANES - N_GROUPS - N_EXPERTS)).reshape(1, LANES)
    row = lambda wd: pl.BlockSpec((tm, wd), lambda i: (i, 0))
    vec = lambda a: a.reshape(1, -1)
    wa, wb, wo = w_a.astype(BF16), w_b.astype(BF16), w_o.astype(BF16)
    return pl.pallas_call(
        _merge_kernel,
        grid=(n // tm,),
        in_specs=[row(D_MODEL), row(WIDTH_A), row(WIDTH_B), row(WIDTH_B), row(WIDTH_B), row(D_MODEL), row(D_MODEL),
                  _full((1, WIDTH_B)), _full((1, WIDTH_B)), _full(wa.shape), _full(wb.shape), _full(wo.shape),
                  _full((1, D_MODEL)), _full(w_r.shape), _full((1, LANES))],
        out_specs=[row(D_MODEL), _token_tile_spec(tm), row(LANES)],
        out_shape=[jax.ShapeDtypeStruct((n, D_MODEL), F32), jax.ShapeDtypeStruct((n,) + TOKEN_TILE, F32),
                   jax.ShapeDtypeStruct((n, LANES), F32)],
        compiler_params=_params(("parallel",)),
        name="merge",
    )(x2, oa, yn, bonus, g, ga, gb, vec(ln_w), vec(ln_b), wa, wb, wo, vec(g_ffn), w_r, b_r)


def _gather_start(idx_ref, src_hbm, buf, sem, slot, rows):
    def body(t, carry):
        for u in range(SUBLANES):
            pltpu.make_async_copy(src_hbm.at[pl.ds(idx_ref[0, 0, t * SUBLANES + u], 1)],
                                  buf.at[slot, pl.ds(t, 1), :, u], sem.at[slot]).start(priority=u % 2)
        return carry
    lax.fori_loop(0, rows // SUBLANES, body, 0)


def _gather_wait(buf, sem, slot):
    pltpu.make_async_copy(buf.at[slot], buf.at[slot], sem.at[slot]).wait()


def _gather_pipeline(idx_ref, idx_next_ref, src_hbm, buf, sem, rows):
    i = pl.program_id(0)
    slot = i % 2

    @pl.when(i == 0)
    def _():
        _gather_start(idx_ref, src_hbm, buf, sem, 0, rows)

    @pl.when(i + 1 < pl.num_programs(0))
    def _():
        _gather_start(idx_next_ref, src_hbm, buf, sem, 1 - slot, rows)

    _gather_wait(buf, sem, slot)
    return slot


def _moe_kernel(be_ref, tok_ref, tok_next_ref, h2_hbm, w13_ref, w2_ref, y_ref, buf, sem):
    slot = _gather_pipeline(tok_ref, tok_next_ref, h2_hbm, buf, sem, MOE_BLOCK)
    xb = _gathered_rows(buf, slot, 0, MOE_BLOCK)
    a = _mm(xb, w13_ref[0])
    hid = jax.nn.silu(a[:, 0:D_EXPERT]) * a[:, D_EXPERT:2 * D_EXPERT]
    _store_token_tiles(y_ref, _mm(hid, w2_ref[0]))


def _moe(h2, blk_expert, buf_tok, w13, w2):
    n_pad = buf_tok.shape[0]
    n_blk = n_pad // MOE_BLOCK
    tok3 = buf_tok.reshape(n_blk, 1, MOE_BLOCK)
    smem_blk = lambda f: pl.BlockSpec((1, 1, MOE_BLOCK), f, memory_space=pltpu.SMEM)
    grid_spec = pltpu.PrefetchScalarGridSpec(
        num_scalar_prefetch=1,
        grid=(n_blk,),
        in_specs=[smem_blk(lambda i, be: (i, 0, 0)),
                  smem_blk(lambda i, be: (jnp.minimum(i + 1, n_blk - 1), 0, 0)),
                  pl.BlockSpec(memory_space=pl.ANY),
                  pl.BlockSpec((1, D_MODEL, 2 * D_EXPERT), lambda i, be: (be[i], 0, 0)),
                  pl.BlockSpec((1, D_EXPERT, D_MODEL), lambda i, be: (be[i], 0, 0))],
        out_specs=_token_tile_spec(MOE_BLOCK),
        scratch_shapes=[_gather_buffer(MOE_BLOCK), pltpu.SemaphoreType.DMA((2,))],
    )
    return pl.pallas_call(
        _moe_kernel,
        grid_spec=grid_spec,
        out_shape=jax.ShapeDtypeStruct((n_pad,) + TOKEN_TILE, F32),
        compiler_params=_params(("arbitrary",)),
        name="moe",
    )(blk_expert, tok3, tok3, h2, w13, w2)


def _final_kernel(pos_ref, pos_next_ref, x1_ref, route_ref, g_ref, y_hbm, o_ref, buf, sem, *, tm):
    slot = _gather_pipeline(pos_ref, pos_next_ref, y_hbm, buf, sem, TOP_K * tm)
    x = x1_ref[...]
    for k in range(TOP_K):
        x = x + route_ref[:, TOP_K + k:TOP_K + k + 1] * _gathered_rows(buf, slot, k * tm, tm)
    o_ref[...] = x * lax.rsqrt(jnp.mean(x * x, axis=-1, keepdims=True) + RMS_EPS) * g_ref[...]


def _final(x1, route, y_pad, pos, g_final, tm=256):
    n = x1.shape[0]
    nt = n // tm
    pos3 = pos.reshape(nt, tm, TOP_K).transpose(0, 2, 1).reshape(nt, 1, TOP_K * tm)
    smem_blk = lambda f: pl.BlockSpec((1, 1, TOP_K * tm), f, memory_space=pltpu.SMEM)
    row = lambda wd: pl.BlockSpec((tm, wd), lambda i: (i, 0))
    return pl.pallas_call(
        functools.partial(_final_kernel, tm=tm),
        grid=(nt,),
        in_specs=[smem_blk(lambda i: (i, 0, 0)), smem_blk(lambda i: (jnp.minimum(i + 1, nt - 1), 0, 0)),
                  row(D_MODEL), row(LANES), _full((1, D_MODEL)), pl.BlockSpec(memory_space=pl.ANY)],
        out_specs=row(D_MODEL),
        out_shape=jax.ShapeDtypeStruct((n, D_MODEL), F32),
        scratch_shapes=[_gather_buffer(TOP_K * tm), pltpu.SemaphoreType.DMA((2,))],
        compiler_params=_params(("arbitrary",)),
        name="final",
    )(pos3, pos3, x1, route, g_final.reshape(1, D_MODEL), y_pad)


def _dispatch_plan(route, n_tok):
    e_flat = route[:, 0:TOP_K].astype(I32).reshape(n_tok * TOP_K)
    onehot = (e_flat[:, None] == jnp.arange(N_EXPERTS, dtype=I32)[None, :]).astype(I32)
    csum = jnp.cumsum(onehot, axis=0)
    counts = csum[-1]
    rank = jnp.sum(jnp.where(onehot > 0, csum - 1, 0), axis=1)
    padded = (counts + MOE_BLOCK - 1) // MOE_BLOCK * MOE_BLOCK
    pad_end = jnp.cumsum(padded)
    pad_start = pad_end - padded
    dest = pad_start[e_flat] + rank
    n_pad = n_tok * TOP_K + N_EXPERTS * MOE_BLOCK
    tok_flat = jnp.repeat(jnp.arange(n_tok, dtype=I32), TOP_K)
    buf_tok = jnp.zeros((n_pad,), I32).at[dest].set(tok_flat)
    n_blk = n_pad // MOE_BLOCK
    blk_expert = jnp.minimum(jnp.searchsorted(pad_end, jnp.arange(n_blk, dtype=I32) * MOE_BLOCK, side="right"),
                             N_EXPERTS - 1).astype(I32)
    return buf_tok, blk_expert, dest.reshape(n_tok, TOP_K)


def kernel(x, positions, g_mix, w_in, cmp_pe_k, cmp_w1_k, cmp_b1_k, cmp_w2_k, cmp_b2_k, cmp_pe_v, cmp_w1_v, cmp_b1_v, cmp_w2_v, cmp_b2_v, rwkv_mu, rwkv_w0, rwkv_w2, rwkv_a0, rwkv_a2, rwkv_g2, rwkv_k_k, rwkv_k_a, rwkv_r_k, rwkv_ln_w, rwkv_ln_b, w_proj_a, w_proj_b, w_out, g_ffn, w_grp, b_grp, w_exp, b_exp, e_w1, e_w3, e_w2, g_final):
    bsz, s, _ = x.shape
    assert w_in.shape[0] == 1, "single-layer stack"
    n_tok = bsz * s
    x2 = x.reshape(n_tok, D_MODEL)
    half = HEAD_DIM // 2
    freqs = jnp.power(ROPE_THETA, -jnp.arange(half, dtype=F32) / half)
    ang = positions.astype(F32).reshape(n_tok, 1) * freqs
    cs = jnp.tile(jnp.cos(ang), (1, 4))
    sn = jnp.tile(jnp.concatenate([-jnp.sin(ang), jnp.sin(ang)], axis=1), (1, 2))
    q, kc, ksl, kwn, vc, vsl, vwn, ng, prw, ga, gb = _inproj(x2, g_mix[0], cs, sn, w_in[0])
    k_c = _compress(kc, cmp_pe_k[0], cmp_w1_k[0], cmp_b1_k[0], cmp_w2_k[0], cmp_b2_k[0], bsz, s, 0.0)
    v_c = _compress(vc, cmp_pe_v[0], cmp_w1_v[0], cmp_b1_v[0], cmp_w2_v[0], cmp_b2_v[0], bsz, s, 1.0)
    o_a = _nsa(q, k_c, v_c, ksl, vsl, kwn, vwn, ng, bsz, s).reshape(n_tok, WIDTH_A)
    r, lw, k, v, kn, kb, g = _rwkv_prep(prw, rwkv_mu[0], rwkv_w0[0], rwkv_w2[0], rwkv_a0[0], rwkv_a2[0],
                                        rwkv_g2[0], rwkv_k_k[0], rwkv_k_a[0], s)
    bonus, tmat = _rwkv_chunk(r, lw, k, v, kn, kb, rwkv_r_k[0], bsz, s)
    yn = _rwkv_scan(tmat, bsz, s)
    x1, h2, route = _merge(x2, o_a, yn.reshape(n_tok, WIDTH_B), bonus.reshape(n_tok, WIDTH_B), g, ga, gb,
                           rwkv_ln_w[0], rwkv_ln_b[0], w_proj_a[0], w_proj_b[0], w_out[0], g_ffn[0],
                           w_grp[0], b_grp[0], w_exp[0], b_exp[0])
    buf_tok, blk_expert, pos = _dispatch_plan(route, n_tok)
    w13 = jnp.concatenate([e_w1[0], e_w3[0]], axis=-1).astype(BF16)
    y_pad = _moe(h2, blk_expert, buf_tok, w13, e_w2[0].astype(BF16))
    out = _final(x1, route, y_pad, pos, g_final)
    return out.reshape(bsz, s, D_MODEL)
```

```python
import functools

import numpy as np
import jax
import jax.numpy as jnp
from jax import lax
from jax.experimental import pallas as pl
from jax.experimental.pallas import tpu as pltpu

F32 = jnp.float32
BF16 = jnp.bfloat16
I32 = jnp.int32

D_MODEL = 1024
N_HEADS_A = 8
N_KV_GROUPS = 2
HEADS_PER_GROUP = N_HEADS_A // N_KV_GROUPS
HEAD_DIM = 64
L_CMP = 32
D_CMP = 16
CMP_HIDDEN = 256
L_SLC = 64
N_SELECT = 16
WINDOW = 512
Q_BLOCK = 128
ROPE_THETA = 10000.0
FORCE_SCORE = 1e4
NEG_INF = -1e30
N_HEADS_B = 8
HEAD_DIM_B = 64
W_LORA = 64
A_LORA = 64
G_LORA = 128
GN_EPS = 64e-5
N_GROUPS = 4
EXPERTS_PER_GROUP = 8
N_EXPERTS = N_GROUPS * EXPERTS_PER_GROUP
TOP_K = 2
D_EXPERT = 256
MOE_BLOCK = 256
RMS_EPS = 1e-6
WIDTH_A = N_HEADS_A * HEAD_DIM
KV_WIDTH = N_KV_GROUPS * HEAD_DIM
WIDTH_B = N_HEADS_B * HEAD_DIM_B
C_RWKV = 3 * WIDTH_B + W_LORA + A_LORA + G_LORA

LANES = 128
CHUNK = 64
SUB = 16
KEY_TILE = 1024
BLOCKS_PER_TILE = KEY_TILE // L_SLC
VMEM_LIMIT = 56 * 1024 * 1024

HI = lax.Precision.HIGHEST


def _mm(a, b):
    return jnp.dot(a.astype(BF16), b.astype(BF16), preferred_element_type=F32)


def _mm_nt(a, b):
    return lax.dot_general(a.astype(BF16), b.astype(BF16), (((1,), (1,)), ((), ())), preferred_element_type=F32)


def _mm_tn(a, b):
    return lax.dot_general(a.astype(BF16), b.astype(BF16), (((0,), (0,)), ((), ())), preferred_element_type=F32)


def _mm_hi(a, b):
    return jnp.dot(a, b, preferred_element_type=F32, precision=HI)


def _mm_exact_rhs(a, b):
    a1 = a.astype(BF16)
    r1 = a - a1.astype(F32)
    a2 = r1.astype(BF16)
    a3 = (r1 - a2.astype(F32)).astype(BF16)
    bb = b.astype(BF16)
    dot = lambda x: jnp.dot(x, bb, preferred_element_type=F32)
    return dot(a1) + dot(a2) + dot(a3)


def _iota(shape, dim):
    return lax.broadcasted_iota(I32, shape, dim)


def _params(sem):
    return pltpu.CompilerParams(dimension_semantics=sem, vmem_limit_bytes=VMEM_LIMIT)


def _full(shape):
    nd = len(shape)
    return pl.BlockSpec(shape, lambda *_: (0,) * nd)


Q_PAD = N_HEADS_A * LANES
KV_PAD = N_KV_GROUPS * LANES


def _inproj_kernel(x_ref, g_ref, cs_ref, sn_ref, wr_ref, wv_ref, wn_ref, ww_ref, wg_ref,
                   q_ref, kc_ref, ksl_ref, kwn_ref, vc_ref, vsl_ref, vwn_ref, ng_ref, prw_ref, ga_ref, gb_ref):
    x = x_ref[...]
    h = (x * lax.rsqrt(jnp.mean(x * x, axis=-1, keepdims=True) + RMS_EPS) * g_ref[...]).astype(BF16)
    pr = jnp.dot(h, wr_ref[...], preferred_element_type=F32)
    cs = cs_ref[...]
    sn = sn_ref[...]
    first_half = (_iota((1, LANES), 1) & (HEAD_DIM // 2)) == 0
    chunks = []
    for c in range(pr.shape[1] // LANES):
        x = pr[:, c * LANES:(c + 1) * LANES]
        swapped = jnp.where(first_half, pltpu.roll(x, LANES - HEAD_DIM // 2, 1), pltpu.roll(x, HEAD_DIM // 2, 1))
        chunks.append(x * cs + swapped * sn)
    ro = jnp.concatenate(chunks, axis=1)
    q_ref[...] = (ro[:, 0:Q_PAD] * (HEAD_DIM ** -0.5)).astype(BF16)
    o = Q_PAD
    kc_ref[...] = ro[:, o:o + KV_WIDTH]
    tm = x_ref.shape[0]
    blk = ((pl.program_id(0) * tm + _iota((tm, 1), 0)) >> 6) & (BLOCKS_PER_TILE - 1)
    code = jnp.where((_iota((1, KV_PAD), 1) & (LANES - 1)) == HEAD_DIM + blk, NEG_INF, 0.0)
    ksl_ref[...] = (ro[:, o + KV_WIDTH:o + KV_WIDTH + KV_PAD] + code).astype(BF16)
    kwn_ref[...] = ro[:, o + KV_WIDTH + KV_PAD:o + KV_WIDTH + 2 * KV_PAD].astype(BF16)
    v = jnp.dot(h, wv_ref[...], preferred_element_type=F32)
    vc_ref[...] = v[:, 0:KV_WIDTH]
    ones = ((_iota((1, 2 * KV_PAD), 1) & (LANES - 1)) >= HEAD_DIM).astype(F32)
    vp = v[:, KV_WIDTH:KV_WIDTH + 2 * KV_PAD] + ones
    vsl_ref[...] = vp[:, 0:KV_PAD].astype(BF16)
    vwn_ref[...] = vp[:, KV_PAD:2 * KV_PAD].astype(BF16)
    ng_ref[...] = jnp.dot(h, wn_ref[...], preferred_element_type=F32)
    prw_ref[...] = jnp.dot(h, ww_ref[...], preferred_element_type=F32)
    gg = jnp.dot(h, wg_ref[...], preferred_element_type=F32)
    ga_ref[...] = gg[:, 0:D_MODEL]
    gb_ref[...] = gg[:, D_MODEL:2 * D_MODEL]


def _pad_heads(w):
    d, wd = w.shape
    return jnp.pad(w.reshape(d, wd // HEAD_DIM, HEAD_DIM), ((0, 0), (0, 0), (0, LANES - HEAD_DIM))).reshape(d, -1)


def _inproj(x2, g_mix, cs, sn, w_in, tm=256):
    n = x2.shape[0]
    o = 0
    cols = {}
    for name, wd in (("q", WIDTH_A), ("kc", KV_WIDTH), ("vc", KV_WIDTH), ("ksl", KV_WIDTH), ("vsl", KV_WIDTH),
                     ("kwn", KV_WIDTH), ("vwn", KV_WIDTH), ("ng", 3 * N_HEADS_A), ("rw", C_RWKV),
                     ("ga", D_MODEL), ("gb", D_MODEL)):
        cols[name] = w_in[:, o:o + wd]
        o += wd
    w_rope = jnp.concatenate(
        [_pad_heads(cols["q"]), cols["kc"], _pad_heads(cols["ksl"]), _pad_heads(cols["kwn"])], axis=1)
    w_v = jnp.concatenate([cols["vc"], _pad_heads(cols["vsl"]), _pad_heads(cols["vwn"])], axis=1)
    w_ng = jnp.pad(cols["ng"], ((0, 0), (0, LANES - 3 * N_HEADS_A)))
    w_gate = jnp.concatenate([cols["ga"], cols["gb"]], axis=1)
    ws = [w.astype(BF16) for w in (w_rope, w_v, w_ng, cols["rw"], w_gate)]
    row = lambda wd: pl.BlockSpec((tm, wd), lambda i: (i, 0))
    outs = [(Q_PAD, BF16), (KV_WIDTH, F32), (KV_PAD, BF16), (KV_PAD, BF16), (KV_WIDTH, F32), (KV_PAD, BF16),
            (KV_PAD, BF16), (LANES, F32), (C_RWKV, F32), (D_MODEL, F32), (D_MODEL, F32)]
    return pl.pallas_call(
        _inproj_kernel,
        grid=(n // tm,),
        in_specs=[row(D_MODEL), _full((1, D_MODEL)), row(LANES), row(LANES)] + [_full(w.shape) for w in ws],
        out_specs=[row(wd) for wd, _ in outs],
        out_shape=[jax.ShapeDtypeStruct((n, wd), dt) for wd, dt in outs],
        compiler_params=_params(("parallel",)),
        name="inproj",
    )(x2, g_mix.reshape(1, D_MODEL), cs, sn, *ws)


def _compress_kernel(c_ref, pe_ref, w1_ref, b1_ref, w2_ref, b2_ref, o_ref, *, pad_value):
    half = D_CMP * HEAD_DIM
    c = c_ref[0, 0]
    w1 = w1_ref[...]
    z1 = _mm(c, w1[0:half])
    z2 = _mm(c, w1[half:2 * half])
    z2 = jnp.concatenate([z2[1:], z2[:1]], axis=0)
    pb = _mm(pe_ref[...], w1)[0:1] + b1_ref[...]
    hid = jax.nn.gelu(z1 + z2 + pb)
    out = _mm(hid, w2_ref[...]) + b2_ref[...]
    o_ref[0, 0] = jnp.concatenate([out, jnp.full(out.shape, pad_value, F32)], axis=1).astype(o_ref.dtype)


def _compress(kv, pe, w1, b1, w2, b2, bsz, s, pad_value):
    nch = s // D_CMP
    c = kv.reshape(bsz, nch, D_CMP, N_KV_GROUPS, HEAD_DIM).transpose(0, 3, 1, 2, 4).reshape(
        bsz, N_KV_GROUPS, nch, D_CMP * HEAD_DIM)
    pe8 = jnp.broadcast_to(pe.reshape(1, L_CMP * HEAD_DIM), (8, L_CMP * HEAD_DIM))
    return pl.pallas_call(
        functools.partial(_compress_kernel, pad_value=pad_value),
        grid=(bsz, N_KV_GROUPS),
        in_specs=[pl.BlockSpec((1, 1, nch, D_CMP * HEAD_DIM), lambda b, g: (b, g, 0, 0)),
                  _full(pe8.shape), _full(w1.shape), _full((1, CMP_HIDDEN)), _full(w2.shape), _full((1, HEAD_DIM))],
        out_specs=pl.BlockSpec((1, 1, nch, LANES), lambda b, g: (b, g, 0, 0)),
        out_shape=jax.ShapeDtypeStruct((bsz, N_KV_GROUPS, nch, LANES), BF16),
        compiler_params=_params(("parallel", "parallel")),
        name="compress",
    )(c, pe8, w1, b1.reshape(1, CMP_HIDDEN), w2, b2.reshape(1, HEAD_DIM))


def _nsa_kernel(q_ref, kc_ref, vc_ref, ksl_ref, vsl_ref, kwn_ref, vwn_ref, ng_ref, ov_ref, o_ref, *, n_top):
    n_cmp = kc_ref.shape[2]
    n_slc = ov_ref.shape[1]
    s0 = pl.program_id(1) * Q_BLOCK
    hq = HEADS_PER_GROUP * Q_BLOCK
    t_q = s0 + _iota((Q_BLOCK, 1), 0)
    sig = jax.nn.sigmoid(ng_ref[0])
    win_start = pl.multiple_of(jnp.maximum(s0 - WINDOW, 0), LANES)
    n_tiles = s0 // KEY_TILE + 1
    groups = range(N_KV_GROUPS)
    lanes = lambda g: slice(g * LANES, (g + 1) * LANES)
    qg = [jnp.concatenate([q_ref[0, :, lanes(g * HEADS_PER_GROUP + h)] for h in range(HEADS_PER_GROUP)], axis=0)
          for g in groups]

    def add_bias(s, bias):
        return (s.reshape(HEADS_PER_GROUP, Q_BLOCK, bias.shape[1]) + bias[None]).reshape(hq, bias.shape[1])

    cmp_bias = jnp.where((_iota((1, n_cmp), 1) * D_CMP + (L_CMP - 1)) <= t_q, 0.0, NEG_INF)
    has_cmp = jnp.concatenate([(t_q >= L_CMP - 1).astype(F32)] * HEADS_PER_GROUP, axis=0)
    pos_w = win_start + _iota((1, WINDOW + Q_BLOCK), 1)
    win_bias = jnp.where((pos_w <= t_q) & (pos_w > t_q - WINDOW), 0.0, NEG_INF)

    s_c = [add_bias(_mm_nt(qg[g], kc_ref[0, g]), cmp_bias) for g in groups]
    s_w = [add_bias(_mm_nt(qg[g], kwn_ref[0, pl.ds(win_start, WINDOW + Q_BLOCK), lanes(g)]), win_bias)
           for g in groups]
    e_c = [jnp.exp(x - jnp.max(x, axis=-1, keepdims=True)) for x in s_c]
    p_c = [x * (has_cmp / jnp.sum(x, axis=-1, keepdims=True)) for x in e_c]
    e_w = [jnp.exp(x - jnp.max(x, axis=-1, keepdims=True)) for x in s_w]
    o_c = [_mm(p_c[g], vc_ref[0, g]) for g in groups]
    acc_w = [_mm(e_w[g], vwn_ref[0, pl.ds(win_start, WINDOW + Q_BLOCK), lanes(g)]) for g in groups]
    o_win = [x * (1.0 / x[:, HEAD_DIM:HEAD_DIM + 1]) for x in acc_w]

    imp_t = []
    for g in groups:
        p_sum = p_c[g][0:Q_BLOCK]
        for h in range(1, HEADS_PER_GROUP):
            p_sum = p_sum + p_c[g][h * Q_BLOCK:(h + 1) * Q_BLOCK]
        imp_t.append(_mm_exact_rhs(p_sum, ov_ref[...]).T)
    j = _iota((n_slc, Q_BLOCK), 0)
    cur = (s0 + _iota((n_slc, Q_BLOCK), 1)) >> 6
    forced = (j == 0) | (j == cur) | (j == cur - 1)
    score = [jnp.where(forced, -3e38, jnp.where(j <= cur, x, -1.0)) for x in imp_t]
    sel_t = [forced.astype(F32) for _ in groups]
    for _ in range(n_top - 3):
        for g in groups:
            m = jnp.max(score[g], axis=0, keepdims=True)
            first = jnp.min(jnp.where(score[g] == m, j, n_slc), axis=0, keepdims=True)
            hit = j == first
            sel_t[g] = jnp.where(hit, 1.0, sel_t[g])
            score[g] = jnp.where(hit, -3e38, score[g])
    sel = [1.0 - x.T for x in sel_t]

    lane_q = _iota((Q_BLOCK, LANES), 1)
    in_code = (lane_q >= HEAD_DIM) & (lane_q < HEAD_DIM + BLOCKS_PER_TILE)

    def sel_step(kt, carry, diagonal):
        k0 = pl.multiple_of(kt * KEY_TILE, KEY_TILE)
        shift = (HEAD_DIM - BLOCKS_PER_TILE * kt) & (LANES - 1)
        s = []
        for g in groups:
            code = jnp.where(in_code, pltpu.roll(sel[g], shift, 1), 0.0).astype(BF16)
            qa = qg[g] + jnp.concatenate([code] * HEADS_PER_GROUP, axis=0)
            s.append(_mm_nt(qa, ksl_ref[0, pl.ds(k0, KEY_TILE), lanes(g)]))
        if diagonal:
            causal = jnp.where((k0 + _iota((Q_BLOCK, KEY_TILE), 1)) <= t_q, 0.0, NEG_INF)
            s = [add_bias(x, causal) for x in s]
        m_new = [jnp.maximum(carry[2 * g], jnp.max(s[g], axis=-1, keepdims=True)) for g in groups]
        p = [jnp.exp(s[g] - m_new[g]) for g in groups]
        pv = [_mm(p[g], vsl_ref[0, pl.ds(k0, KEY_TILE), lanes(g)]) for g in groups]
        out = []
        for g in groups:
            out += [m_new[g], jnp.exp(carry[2 * g] - m_new[g]) * carry[2 * g + 1] + pv[g]]
        return tuple(out)

    init = (jnp.full((hq, 1), NEG_INF, F32), jnp.zeros((hq, LANES), F32)) * N_KV_GROUPS
    fin = lax.fori_loop(0, n_tiles - 1, lambda kt, c: sel_step(kt, c, False), init)
    fin = sel_step(n_tiles - 1, fin, True)

    heads_out = []
    for g in groups:
        acc_s = fin[2 * g + 1]
        o_s = acc_s * (1.0 / acc_s[:, HEAD_DIM:HEAD_DIM + 1])
        o_w = o_win[g]
        for h in range(HEADS_PER_GROUP):
            hh = g * HEADS_PER_GROUP + h
            r = slice(h * Q_BLOCK, (h + 1) * Q_BLOCK)
            mix = (sig[:, 3 * hh:3 * hh + 1] * o_c[g][r] + sig[:, 3 * hh + 1:3 * hh + 2] * o_s[r]
                   + sig[:, 3 * hh + 2:3 * hh + 3] * o_w[r])
            heads_out.append(mix[:, 0:HEAD_DIM])
    o_ref[0] = jnp.concatenate(heads_out, axis=1).astype(o_ref.dtype)


def _nsa(q, kc, vc, ksl, vsl, kwn, vwn, ng, bsz, s):
    n_cmp = s // D_CMP
    n_slc = s // L_SLC
    ii = np.arange(n_cmp)[:, None]
    jj = np.arange(n_slc)[None, :]
    assert n_slc <= LANES and s % KEY_TILE == 0 and min(N_SELECT, n_slc) >= 3
    overlap = (ii * D_CMP < (jj + 1) * L_SLC) & (ii * D_CMP + L_CMP > jj * L_SLC)
    overlap = jnp.asarray(np.pad(overlap, ((0, 0), (0, LANES - n_slc))), BF16)
    seq = lambda wd: pl.BlockSpec((1, s, wd), lambda b, i: (b, 0, 0))
    blk = lambda wd: pl.BlockSpec((1, Q_BLOCK, wd), lambda b, i: (b, i, 0))
    cmp_spec = pl.BlockSpec((1, N_KV_GROUPS, n_cmp, LANES), lambda b, i: (b, 0, 0, 0))
    r3 = lambda a: a.reshape(bsz, s, a.shape[-1])
    return pl.pallas_call(
        functools.partial(_nsa_kernel, n_top=min(N_SELECT, n_slc)),
        grid=(bsz, s // Q_BLOCK),
        in_specs=[blk(Q_PAD), cmp_spec, cmp_spec, seq(KV_PAD), seq(KV_PAD), seq(KV_PAD), seq(KV_PAD),
                  blk(LANES), _full(overlap.shape)],
        out_specs=blk(WIDTH_A),
        out_shape=jax.ShapeDtypeStruct((bsz, s, WIDTH_A), BF16),
        compiler_params=_params(("parallel", "arbitrary")),
        name="nsa",
    )(r3(q), kc, vc, r3(ksl), r3(vsl), r3(kwn), r3(vwn), r3(ng), overlap)


def _rwkv_prep_kernel(p_ref, pv_ref, mu_ref, w0_ref, w2_ref, a0_ref, a2_ref, g2_ref, kk_ref, ka_ref, bd_ref,
                      r_ref, lw_ref, k_ref, v_ref, kn_ref, kb_ref, g_ref, *, tiles_per_seq):
    p = p_ref[...]
    first = (pl.program_id(0) % tiles_per_seq) == 0
    prev_last = jnp.where(first, 0.0, pv_ref[7:8, :])
    p_prev = jnp.concatenate([prev_last, p[:-1]], axis=0)
    xs = p + (p_prev - p) * mu_ref[...]
    wb = WIDTH_B
    r = xs[:, 0:wb]
    k = xs[:, wb:2 * wb]
    v = xs[:, 2 * wb:3 * wb]
    o = 3 * wb
    xw = xs[:, o:o + W_LORA]
    xa = xs[:, o + W_LORA:o + W_LORA + A_LORA]
    xg = xs[:, o + W_LORA + A_LORA:o + W_LORA + A_LORA + G_LORA]
    z = w0_ref[...] + _mm_hi(jnp.tanh(xw), w2_ref[...])
    w_log = -(jnp.maximum(-z, 0.0) + jnp.log(1.0 + jnp.exp(-jnp.abs(z)))) - 0.5
    a = jax.nn.sigmoid(a0_ref[...] + _mm_hi(xa, a2_ref[...]))
    kk = k * kk_ref[...]
    ss = _mm_hi(kk * kk, bd_ref[...])
    kn = kk * (1.0 / jnp.maximum(jnp.sqrt(ss), 1e-12))
    r_ref[...] = r
    lw_ref[...] = -jnp.exp(w_log)
    k_ref[...] = k * (1.0 + (a - 1.0) * ka_ref[...])
    v_ref[...] = v
    kn_ref[...] = kn
    kb_ref[...] = kn * a
    g_ref[...] = _mm(jax.nn.sigmoid(xg), g2_ref[...])


def _rwkv_prep(prw, mu, w0, w2, a0, a2, g2, k_k, k_a, s, tm=256):
    n = prw.shape[0]
    hid = np.arange(WIDTH_B) // HEAD_DIM_B
    bd = jnp.asarray(hid[:, None] == hid[None, :], F32)
    row = lambda wd: pl.BlockSpec((tm, wd), lambda i: (i, 0))
    vec = lambda a: a.reshape(1, -1)
    return pl.pallas_call(
        functools.partial(_rwkv_prep_kernel, tiles_per_seq=s // tm),
        grid=(n // tm,),
        in_specs=[row(C_RWKV), pl.BlockSpec((8, C_RWKV), lambda i: (jnp.maximum(i * (tm // 8) - 1, 0), 0)),
                  _full((1, C_RWKV)), _full((1, WIDTH_B)), _full(w2.shape), _full((1, WIDTH_B)), _full(a2.shape),
                  _full(g2.shape), _full((1, WIDTH_B)), _full((1, WIDTH_B)), _full(bd.shape)],
        out_specs=[row(WIDTH_B)] * 7,
        out_shape=[jax.ShapeDtypeStruct((n, WIDTH_B), F32)] * 7,
        compiler_params=_params(("parallel",)),
        name="rwkv_prep",
    )(prw, prw, vec(mu), vec(w0), w2, vec(a0), a2, g2, vec(k_k), vec(k_a), bd)


def _rwkv_chunk_kernel(r_ref, lw_ref, k_ref, v_ref, kn_ref, kb_ref, rk_ref,
                       bonus_ref, tm_ref, *, n_sub):
    L = CHUNK
    hd = HEAD_DIM_B
    ti = _iota((L, L), 0)
    si = _iota((L, L), 1)
    strict = si < ti
    incl = si <= ti
    same_sub = (ti // SUB) == (si // SUB)
    eye = (ti == si).astype(F32)
    tri = incl.astype(F32)
    at, bt, kt, rt, bl, kl, vv, rkr, p_tot = [], [], [], [], [], [], [], [], []
    for j in range(n_sub):
        rows = slice(j * L, (j + 1) * L)
        lw = lw_ref[0, rows, :]
        cs = _mm_hi(tri, lw)
        p_inv = jnp.exp(-cs)
        p_end = jnp.exp(cs[L - 1:L, :] - cs)
        r = r_ref[0, rows, :]
        k = k_ref[0, rows, :]
        kb = kb_ref[0, rows, :]
        at.append(-kn_ref[0, rows, :] * jnp.exp(cs - lw))
        bt.append(kb * p_inv)
        kt.append(k * p_inv)
        rt.append(r * jnp.exp(cs))
        bl.append(kb * p_end)
        kl.append(k * p_end)
        vv.append(v_ref[0, rows, :])
        rkr.append(r * k * rk_ref[...])
        p_tot.append(jnp.exp(cs[L - 1:L, :]))
    units = [(j, h) for j in range(n_sub) for h in range(N_HEADS_B)]
    sl = lambda arr, u: arr[u[0]][:, u[1] * hd:(u[1] + 1) * hd]
    aa = [_mm_nt(jnp.concatenate([sl(at, u), sl(rt, u)], axis=0),
                 jnp.concatenate([sl(bt, u), sl(kt, u)], axis=0)) for u in units]
    a_ab = [jnp.where(strict, x[0:L, 0:L], 0.0) for x in aa]
    a_low = [jnp.concatenate([jnp.where(strict, x[0:L, L:2 * L], 0.0),
                              jnp.where(incl, x[L:2 * L, L:2 * L], 0.0)], axis=0) for x in aa]
    a_rb = [jnp.where(incl, x[L:2 * L, 0:L], 0.0) for x in aa]
    av = [_mm(x, sl(vv, u)) for x, u in zip(a_low, units)]
    kv = [_mm_tn(sl(kl, u), sl(vv, u)) for u in units]
    pw = [jnp.where(same_sub, x, 0.0) for x in a_ab]
    t = [eye + x for x in pw]
    for _ in range(3):
        pw = [_mm(x, x) for x in pw]
        t = [x + _mm(x, y) for x, y in zip(t, pw)]
    width = SUB
    while width < L:
        m = ((ti // width) == (si // width) + 1) & ((ti // (2 * width)) == (si // (2 * width)))
        ot = [_mm(jnp.where(m, x, 0.0), y) for x, y in zip(a_ab, t)]
        t = [x + _mm(x, y) for x, y in zip(t, ot)]
        width *= 2
    wu = [_mm(x, jnp.concatenate([sl(at, u), y[0:L]], axis=1)) for x, y, u in zip(t, av, units)]
    qy = [jnp.concatenate([sl(rt, u), y[L:2 * L]], axis=1) + _mm(x, w)
          for x, y, w, u in zip(a_rb, av, wu, units)]
    gh = [_mm_tn(sl(bl, u), w) + jnp.concatenate([eye * sl(p_tot, u), y], axis=1)
          for w, y, u in zip(wu, kv, units)]
    bonus = [jnp.sum(sl(rkr, u), axis=-1, keepdims=True) * sl(vv, u) for u in units]
    for j in range(n_sub):
        bonus_ref[0, j * L:(j + 1) * L, :] = jnp.concatenate(bonus[j * N_HEADS_B:(j + 1) * N_HEADS_B], axis=1)
    for x, y, (j, h) in zip(qy, gh, units):
        tm_ref[0, j, h, 0:L, :] = x
        tm_ref[0, j, h, L:2 * L, :] = y


def _rwkv_chunk(r, lw, k, v, kn, kb, r_k, bsz, s, n_sub=2):
    nch = s // CHUNK
    blk = pl.BlockSpec((1, n_sub * CHUNK, WIDTH_B), lambda b, c: (b, c, 0))
    mat = pl.BlockSpec((1, n_sub, N_HEADS_B, 2 * CHUNK, 2 * HEAD_DIM_B), lambda b, c: (b, c, 0, 0, 0))
    r3 = lambda a: a.reshape(bsz, s, WIDTH_B)
    return pl.pallas_call(
        functools.partial(_rwkv_chunk_kernel, n_sub=n_sub),
        grid=(bsz, nch // n_sub),
        in_specs=[blk] * 6 + [_full((1, WIDTH_B))],
        out_specs=[blk, mat],
        out_shape=[jax.ShapeDtypeStruct((bsz, s, WIDTH_B), F32),
                   jax.ShapeDtypeStruct((bsz, nch, N_HEADS_B, 2 * CHUNK, 2 * HEAD_DIM_B), F32)],
        compiler_params=_params(("parallel", "parallel")),
        name="rwkv_chunk",
    )(r3(r), r3(lw), r3(k), r3(v), r3(kn), r3(kb), r_k.reshape(1, WIDTH_B))


def _rwkv_scan_kernel(tm_ref, yn_ref, st_ref):
    L = CHUNK
    hd = HEAD_DIM_B

    @pl.when(pl.program_id(0) == 0)
    def _():
        eye = (_iota((hd, hd), 0) == _iota((hd, hd), 1)).astype(F32)
        for i in range(st_ref.shape[0]):
            st_ref[i] = jnp.concatenate([jnp.zeros((hd, hd), F32), eye], axis=0)

    for b in range(tm_ref.shape[0]):
        res = [_mm_hi(tm_ref[b, 0, h], st_ref[b * N_HEADS_B + h]) for h in range(N_HEADS_B)]
        ys = []
        for h in range(N_HEADS_B):
            st_ref[b * N_HEADS_B + h, 0:hd, :] = res[h][L:2 * L]
            y = res[h][0:L]
            mean = jnp.mean(y, axis=-1, keepdims=True)
            d = y - mean
            var = jnp.mean(d * d, axis=-1, keepdims=True)
            ys.append(d * lax.rsqrt(var + GN_EPS))
        yn_ref[b] = jnp.concatenate(ys, axis=1)


def _rwkv_scan(tmat, bsz, s):
    nch = s // CHUNK
    return pl.pallas_call(
        _rwkv_scan_kernel,
        grid=(nch,),
        in_specs=[pl.BlockSpec((bsz, 1, N_HEADS_B, 2 * CHUNK, 2 * HEAD_DIM_B), lambda c: (0, c, 0, 0, 0))],
        out_specs=pl.BlockSpec((bsz, CHUNK, WIDTH_B), lambda c: (0, c, 0)),
        out_shape=jax.ShapeDtypeStruct((bsz, s, WIDTH_B), F32),
        scratch_shapes=[pltpu.VMEM((bsz * N_HEADS_B, 2 * HEAD_DIM_B, HEAD_DIM_B), F32)],
        compiler_params=_params(("arbitrary",)),
        name="rwkv_scan",
    )(tmat)


SUBLANES = 8
TOKEN_TILE = (SUBLANES, D_MODEL // SUBLANES)
assert TOKEN_TILE[1] == LANES


def _token_tile_spec(rows):
    return pl.BlockSpec((rows,) + TOKEN_TILE, lambda i, *_: (i, 0, 0))


def _store_token_tiles(ref, x):
    for c in range(SUBLANES):
        ref[:, c, :] = x[:, c * LANES:(c + 1) * LANES]


def _gather_buffer(rows):
    return pltpu.VMEM((2, rows // SUBLANES, SUBLANES, SUBLANES, LANES), F32)


def _gathered_rows(buf, slot, start, rows):
    t0, nt = start // SUBLANES, rows // SUBLANES
    return jnp.concatenate([buf[slot, t0:t0 + nt, c].reshape(rows, LANES) for c in range(SUBLANES)], axis=1)


def _merge_kernel(x_ref, oa_ref, yn_ref, bonus_ref, g_ref, ga_ref, gb_ref, lnw_ref, lnb_ref, wa_ref, wb_ref,
                  wo_ref, gf_ref, wr_ref, br_ref, x1_ref, h2_ref, route_ref):
    ob = (yn_ref[...] * lnw_ref[...] + lnb_ref[...] + bonus_ref[...]) * g_ref[...]
    merged = (jax.nn.sigmoid(ga_ref[...]) * jnp.dot(oa_ref[...], wa_ref[...], preferred_element_type=F32)
              + jax.nn.sigmoid(gb_ref[...]) * _mm(ob, wb_ref[...]))
    x1 = x_ref[...] + _mm(merged, wo_ref[...])
    x1_ref[...] = x1
    h2 = x1 * lax.rsqrt(jnp.mean(x1 * x1, axis=-1, keepdims=True) + RMS_EPS) * gf_ref[...]
    _store_token_tiles(h2_ref, h2)
    logits = _mm_hi(h2, wr_ref[...]) + br_ref[...]
    lane = _iota(logits.shape, 1)
    big = 4 * LANES
    is_grp = lane < N_GROUPS
    lg = jnp.where(is_grp, logits, NEG_INF)
    mg = jnp.max(lg, axis=-1, keepdims=True)
    gidx = jnp.min(jnp.where(lg == mg, lane, big), axis=-1, keepdims=True)
    pg = 1.0 / jnp.sum(jnp.where(is_grp, jnp.exp(lg - mg), 0.0), axis=-1, keepdims=True)
    in_grp = (lane >= N_GROUPS) & (((lane - N_GROUPS) >> 3) == gidx) & (lane < N_GROUPS + N_EXPERTS)
    le = jnp.where(in_grp, logits, NEG_INF)
    m1 = jnp.max(le, axis=-1, keepdims=True)
    i1 = jnp.min(jnp.where(le == m1, lane, big), axis=-1, keepdims=True)
    le2 = jnp.where(lane == i1, NEG_INF, le)
    m2 = jnp.max(le2, axis=-1, keepdims=True)
    i2 = jnp.min(jnp.where(le2 == m2, lane, big), axis=-1, keepdims=True)
    e2 = jnp.exp(m2 - m1)
    inv = pg / (1.0 + e2)
    route_ref[...] = jnp.where(lane == 0, (i1 - N_GROUPS).astype(F32),
                               jnp.where(lane == 1, (i2 - N_GROUPS).astype(F32),
                                         jnp.where(lane == 2, inv, jnp.where(lane == 3, inv * e2, 0.0))))


def _merge(x2, oa, yn, bonus, g, ga, gb, ln_w, ln_b, w_a, w_b, w_o, g_ffn, w_grp, b_grp, w_exp, b_exp, tm=256):
    n = x2.shape[0]
    w_r = jnp.pad(jnp.concatenate([w_grp, w_exp], axis=1), ((0, 0), (0, LANES - N_GROUPS - N_EXPERTS)))
    b_r = jnp.pad(jnp.concatenate([b_grp, b_exp]), (0, LANES - N_GROUPS - N_EXPERTS)).reshape(1, LANES)
    row = lambda wd: pl.BlockSpec((tm, wd), lambda i: (i, 0))
    vec = lambda a: a.reshape(1, -1)
    wa, wb, wo = w_a.astype(BF16), w_b.astype(BF16), w_o.astype(BF16)
    return pl.pallas_call(
        _merge_kernel,
        grid=(n // tm,),
        in_specs=[row(D_MODEL), row(WIDTH_A), row(WIDTH_B), row(WIDTH_B), row(WIDTH_B), row(D_MODEL), row(D_MODEL),
                  _full((1, WIDTH_B)), _full((1, WIDTH_B)), _full(wa.shape), _full(wb.shape), _full(wo.shape),
                  _full((1, D_MODEL)), _full(w_r.shape), _full((1, LANES))],
        out_specs=[row(D_MODEL), _token_tile_spec(tm), row(LANES)],
        out_shape=[jax.ShapeDtypeStruct((n, D_MODEL), F32), jax.ShapeDtypeStruct((n,) + TOKEN_TILE, F32),
                   jax.ShapeDtypeStruct((n, LANES), F32)],
        compiler_params=_params(("parallel",)),
        name="merge",
    )(x2, oa, yn, bonus, g, ga, gb, vec(ln_w), vec(ln_b), wa, wb, wo, vec(g_ffn), w_r, b_r)


def _gather_start(idx_ref, src_hbm, buf, sem, slot, rows):
    def body(t, carry):
        for u in range(SUBLANES):
            pltpu.make_async_copy(src_hbm.at[pl.ds(idx_ref[0, 0, t * SUBLANES + u], 1)],
                                  buf.at[slot, pl.ds(t, 1), :, u], sem.at[slot]).start(priority=u % 2)
        return carry
    lax.fori_loop(0, rows // SUBLANES, body, 0)


def _gather_wait(buf, sem, slot):
    pltpu.make_async_copy(buf.at[slot], buf.at[slot], sem.at[slot]).wait()


def _gather_pipeline(idx_ref, idx_next_ref, src_hbm, buf, sem, rows):
    i = pl.program_id(0)
    slot = i % 2

    @pl.when(i == 0)
    def _():
        _gather_start(idx_ref, src_hbm, buf, sem, 0, rows)

    @pl.when(i + 1 < pl.num_programs(0))
    def _():
        _gather_start(idx_next_ref, src_hbm, buf, sem, 1 - slot, rows)

    _gather_wait(buf, sem, slot)
    return slot


def _moe_kernel(be_ref, tok_ref, tok_next_ref, h2_hbm, w13_ref, w2_ref, y_ref, buf, sem):
    slot = _gather_pipeline(tok_ref, tok_next_ref, h2_hbm, buf, sem, MOE_BLOCK)
    xb = _gathered_rows(buf, slot, 0, MOE_BLOCK)
    a = _mm(xb, w13_ref[0])
    hid = jax.nn.silu(a[:, 0:D_EXPERT]) * a[:, D_EXPERT:2 * D_EXPERT]
    _store_token_tiles(y_ref, _mm(hid, w2_ref[0]))


def _moe(h2, blk_expert, buf_tok, w13, w2):
    n_pad = buf_tok.shape[0]
    n_blk = n_pad // MOE_BLOCK
    tok3 = buf_tok.reshape(n_blk, 1, MOE_BLOCK)
    smem_blk = lambda f: pl.BlockSpec((1, 1, MOE_BLOCK), f, memory_space=pltpu.SMEM)
    grid_spec = pltpu.PrefetchScalarGridSpec(
        num_scalar_prefetch=1,
        grid=(n_blk,),
        in_specs=[smem_blk(lambda i, be: (i, 0, 0)),
                  smem_blk(lambda i, be: (jnp.minimum(i + 1, n_blk - 1), 0, 0)),
                  pl.BlockSpec(memory_space=pl.ANY),
                  pl.BlockSpec((1, D_MODEL, 2 * D_EXPERT), lambda i, be: (be[i], 0, 0)),
                  pl.BlockSpec((1, D_EXPERT, D_MODEL), lambda i, be: (be[i], 0, 0))],
        out_specs=_token_tile_spec(MOE_BLOCK),
        scratch_shapes=[_gather_buffer(MOE_BLOCK), pltpu.SemaphoreType.DMA((2,))],
    )
    return pl.pallas_call(
        _moe_kernel,
        grid_spec=grid_spec,
        out_shape=jax.ShapeDtypeStruct((n_pad,) + TOKEN_TILE, F32),
        compiler_params=_params(("arbitrary",)),
        name="moe",
    )(blk_expert, tok3, tok3, h2, w13, w2)


def _final_kernel(pos_ref, pos_next_ref, x1_ref, route_ref, g_ref, y_hbm, o_ref, buf, sem, *, tm):
    slot = _gather_pipeline(pos_ref, pos_next_ref, y_hbm, buf, sem, TOP_K * tm)
    x = x1_ref[...]
    for k in range(TOP_K):
        x = x + route_ref[:, TOP_K + k:TOP_K + k + 1] * _gathered_rows(buf, slot, k * tm, tm)
    o_ref[...] = x * lax.rsqrt(jnp.mean(x * x, axis=-1, keepdims=True) + RMS_EPS) * g_ref[...]


def _final(x1, route, y_pad, pos, g_final, tm=256):
    n = x1.shape[0]
    nt = n // tm
    pos3 = pos.reshape(nt, tm, TOP_K).transpose(0, 2, 1).reshape(nt, 1, TOP_K * tm)
    smem_blk = lambda f: pl.BlockSpec((1, 1, TOP_K * tm), f, memory_space=pltpu.SMEM)
    row = lambda wd: pl.BlockSpec((tm, wd), lambda i: (i, 0))
    return pl.pallas_call(
        functools.partial(_final_kernel, tm=tm),
        grid=(nt,),
        in_specs=[smem_blk(lambda i: (i, 0, 0)), smem_blk(lambda i: (jnp.minimum(i + 1, nt - 1), 0, 0)),
                  row(D_MODEL), row(LANES), _full((1, D_MODEL)), pl.BlockSpec(memory_space=pl.ANY)],
        out_specs=row(D_MODEL),
        out_shape=jax.ShapeDtypeStruct((n, D_MODEL), F32),
        scratch_shapes=[_gather_buffer(TOP_K * tm), pltpu.SemaphoreType.DMA((2,))],
        compiler_params=_params(("arbitrary",)),
        name="final",
    )(pos3, pos3, x1, route, g_final.reshape(1, D_MODEL), y_pad)


def _dispatch_plan(route, n_tok):
    e_flat = route[:, 0:TOP_K].astype(I32).reshape(n_tok * TOP_K)
    onehot = (e_flat[:, None] == jnp.arange(N_EXPERTS, dtype=I32)[None, :]).astype(I32)
    csum = jnp.cumsum(onehot, axis=0)
    counts = csum[-1]
    rank = jnp.sum(jnp.where(onehot > 0, csum - 1, 0), axis=1)
    padded = (counts + MOE_BLOCK - 1) // MOE_BLOCK * MOE_BLOCK
    pad_end = jnp.cumsum(padded)
    pad_start = pad_end - padded
    dest = pad_start[e_flat] + rank
    n_pad = n_tok * TOP_K + N_EXPERTS * MOE_BLOCK
    tok_flat = jnp.repeat(jnp.arange(n_tok, dtype=I32), TOP_K)
    buf_tok = jnp.zeros((n_pad,), I32).at[dest].set(tok_flat)
    n_blk = n_pad // MOE_BLOCK
    blk_expert = jnp.minimum(jnp.searchsorted(pad_end, jnp.arange(n_blk, dtype=I32) * MOE_BLOCK, side="right"),
                             N_EXPERTS - 1).astype(I32)
    return buf_tok, blk_expert, dest.reshape(n_tok, TOP_K)


def kernel(x, positions, g_mix, w_in, cmp_pe_k, cmp_w1_k, cmp_b1_k, cmp_w2_k, cmp_b2_k, cmp_pe_v, cmp_w1_v, cmp_b1_v, cmp_w2_v, cmp_b2_v, rwkv_mu, rwkv_w0, rwkv_w2, rwkv_a0, rwkv_a2, rwkv_g2, rwkv_k_k, rwkv_k_a, rwkv_r_k, rwkv_ln_w, rwkv_ln_b, w_proj_a, w_proj_b, w_out, g_ffn, w_grp, b_grp, w_exp, b_exp, e_w1, e_w3, e_w2, g_final):
    bsz, s, _ = x.shape
    assert w_in.shape[0] == 1, "single-layer stack"
    n_tok = bsz * s
    x2 = x.reshape(n_tok, D_MODEL)
    half = HEAD_DIM // 2
    freqs = jnp.power(ROPE_THETA, -jnp.arange(half, dtype=F32) / half)
    ang = positions.astype(F32).reshape(n_tok, 1) * freqs
    cs = jnp.tile(jnp.cos(ang), (1, 4))
    sn = jnp.tile(jnp.concatenate([-jnp.sin(ang), jnp.sin(ang)], axis=1), (1, 2))
    q, kc, ksl, kwn, vc, vsl, vwn, ng, prw, ga, gb = _inproj(x2, g_mix[0], cs, sn, w_in[0])
    k_c = _compress(kc, cmp_pe_k[0], cmp_w1_k[0], cmp_b1_k[0], cmp_w2_k[0], cmp_b2_k[0], bsz, s, 0.0)
    v_c = _compress(vc, cmp_pe_v[0], cmp_w1_v[0], cmp_b1_v[0], cmp_w2_v[0], cmp_b2_v[0], bsz, s, 1.0)
    o_a = _nsa(q, k_c, v_c, ksl, vsl, kwn, vwn, ng, bsz, s).reshape(n_tok, WIDTH_A)
    r, lw, k, v, kn, kb, g = _rwkv_prep(prw, rwkv_mu[0], rwkv_w0[0], rwkv_w2[0], rwkv_a0[0], rwkv_a2[0],
                                        rwkv_g2[0], rwkv_k_k[0], rwkv_k_a[0], s)
    bonus, tmat = _rwkv_chunk(r, lw, k, v, kn, kb, rwkv_r_k[0], bsz, s)
    yn = _rwkv_scan(tmat, bsz, s)
    x1, h2, route = _merge(x2, o_a, yn.reshape(n_tok, WIDTH_B), bonus.reshape(n_tok, WIDTH_B), g, ga, gb,
                           rwkv_ln_w[0], rwkv_ln_b[0], w_proj_a[0], w_proj_b[0], w_out[0], g_ffn[0],
                           w_grp[0], b_grp[0], w_exp[0], b_exp[0])
    buf_tok, blk_expert, pos = _dispatch_plan(route, n_tok)
    w13 = jnp.concatenate([e_w1[0], e_w3[0]], axis=-1).astype(BF16)
    y_pad = _moe(h2, blk_expert, buf_tok, w13, e_w2[0].astype(BF16))
    out = _final(x1, route, y_pad, pos, g_final)
    return out.reshape(bsz, s, D_MODEL)
```

```python
import functools

import numpy as np
import jax
import jax.numpy as jnp
from jax import lax
from jax.experimental import pallas as pl
from jax.experimental.pallas import tpu as pltpu

F32 = jnp.float32
BF16 = jnp.bfloat16
I32 = jnp.int32

D_MODEL = 1024
N_HEADS_A = 8
N_KV_GROUPS = 2
HEADS_PER_GROUP = N_HEADS_A // N_KV_GROUPS
HEAD_DIM = 64
L_CMP = 32
D_CMP = 16
CMP_HIDDEN = 256
L_SLC = 64
N_SELECT = 16
WINDOW = 512
Q_BLOCK = 128
ROPE_THETA = 10000.0
FORCE_SCORE = 1e4
NEG_INF = -1e30
N_HEADS_B = 8
HEAD_DIM_B = 64
W_LORA = 64
A_LORA = 64
G_LORA = 128
GN_EPS = 64e-5
N_GROUPS = 4
EXPERTS_PER_GROUP = 8
N_EXPERTS = N_GROUPS * EXPERTS_PER_GROUP
TOP_K = 2
D_EXPERT = 256
MOE_BLOCK = 256
RMS_EPS = 1e-6
WIDTH_A = N_HEADS_A * HEAD_DIM
KV_WIDTH = N_KV_GROUPS * HEAD_DIM
WIDTH_B = N_HEADS_B * HEAD_DIM_B
C_RWKV = 3 * WIDTH_B + W_LORA + A_LORA + G_LORA

LANES = 128
CHUNK = 64
SUB = 16
KEY_TILE = 1024
BLOCKS_PER_TILE = KEY_TILE // L_SLC
VMEM_LIMIT = 56 * 1024 * 1024

HI = lax.Precision.HIGHEST


def _mm(a, b):
    return jnp.dot(a.astype(BF16), b.astype(BF16), preferred_element_type=F32)


def _mm_nt(a, b):
    return lax.dot_general(a.astype(BF16), b.astype(BF16), (((1,), (1,)), ((), ())), preferred_element_type=F32)


def _mm_tn(a, b):
    return lax.dot_general(a.astype(BF16), b.astype(BF16), (((0,), (0,)), ((), ())), preferred_element_type=F32)


def _mm_hi(a, b):
    return jnp.dot(a, b, preferred_element_type=F32, precision=HI)


def _mm_exact_rhs(a, b):
    a1 = a.astype(BF16)
    r1 = a - a1.astype(F32)
    a2 = r1.astype(BF16)
    a3 = (r1 - a2.astype(F32)).astype(BF16)
    bb = b.astype(BF16)
    dot = lambda x: jnp.dot(x, bb, preferred_element_type=F32)
    return dot(a1) + dot(a2) + dot(a3)


def _iota(shape, dim):
    return lax.broadcasted_iota(I32, shape, dim)


def _params(sem):
    return pltpu.CompilerParams(dimension_semantics=sem, vmem_limit_bytes=VMEM_LIMIT)


def _full(shape):
    nd = len(shape)
    return pl.BlockSpec(shape, lambda *_: (0,) * nd)


Q_PAD = N_HEADS_A * LANES
KV_PAD = N_KV_GROUPS * LANES


def _inproj_kernel(x_ref, g_ref, cs_ref, sn_ref, wr_ref, wv_ref, wn_ref, ww_ref, wg_ref,
                   q_ref, kc_ref, ksl_ref, kwn_ref, vc_ref, vsl_ref, vwn_ref, ng_ref, prw_ref, ga_ref, gb_ref):
    x = x_ref[...]
    h = (x * lax.rsqrt(jnp.mean(x * x, axis=-1, keepdims=True) + RMS_EPS) * g_ref[...]).astype(BF16)
    pr = jnp.dot(h, wr_ref[...], preferred_element_type=F32)
    cs = cs_ref[...]
    sn = sn_ref[...]
    first_half = (_iota((1, LANES), 1) & (HEAD_DIM // 2)) == 0
    chunks = []
    for c in range(pr.shape[1] // LANES):
        x = pr[:, c * LANES:(c + 1) * LANES]
        swapped = jnp.where(first_half, pltpu.roll(x, LANES - HEAD_DIM // 2, 1), pltpu.roll(x, HEAD_DIM // 2, 1))
        chunks.append(x * cs + swapped * sn)
    ro = jnp.concatenate(chunks, axis=1)
    q_ref[...] = (ro[:, 0:Q_PAD] * (HEAD_DIM ** -0.5)).astype(BF16)
    o = Q_PAD
    kc_ref[...] = ro[:, o:o + KV_WIDTH]
    tm = x_ref.shape[0]
    blk = ((pl.program_id(0) * tm + _iota((tm, 1), 0)) >> 6) & (BLOCKS_PER_TILE - 1)
    code = jnp.where((_iota((1, KV_PAD), 1) & (LANES - 1)) == HEAD_DIM + blk, NEG_INF, 0.0)
    ksl_ref[...] = (ro[:, o + KV_WIDTH:o + KV_WIDTH + KV_PAD] + code).astype(BF16)
    kwn_ref[...] = ro[:, o + KV_WIDTH + KV_PAD:o + KV_WIDTH + 2 * KV_PAD].astype(BF16)
    v = jnp.dot(h, wv_ref[...], preferred_element_type=F32)
    vc_ref[...] = v[:, 0:KV_WIDTH]
    ones = ((_iota((1, 2 * KV_PAD), 1) & (LANES - 1)) >= HEAD_DIM).astype(F32)
    vp = v[:, KV_WIDTH:KV_WIDTH + 2 * KV_PAD] + ones
    vsl_ref[...] = vp[:, 0:KV_PAD].astype(BF16)
    vwn_ref[...] = vp[:, KV_PAD:2 * KV_PAD].astype(BF16)
    ng_ref[...] = jnp.dot(h, wn_ref[...], preferred_element_type=F32)
    prw_ref[...] = jnp.dot(h, ww_ref[...], preferred_element_type=F32)
    gg = jnp.dot(h, wg_ref[...], preferred_element_type=F32)
    ga_ref[...] = gg[:, 0:D_MODEL]
    gb_ref[...] = gg[:, D_MODEL:2 * D_MODEL]


def _pad_heads(w):
    d, wd = w.shape
    return jnp.pad(w.reshape(d, wd // HEAD_DIM, HEAD_DIM), ((0, 0), (0, 0), (0, LANES - HEAD_DIM))).reshape(d, -1)


def _inproj(x2, g_mix, cs, sn, w_in, tm=256):
    n = x2.shape[0]
    o = 0
    cols = {}
    for name, wd in (("q", WIDTH_A), ("kc", KV_WIDTH), ("vc", KV_WIDTH), ("ksl", KV_WIDTH), ("vsl", KV_WIDTH),
                     ("kwn", KV_WIDTH), ("vwn", KV_WIDTH), ("ng", 3 * N_HEADS_A), ("rw", C_RWKV),
                     ("ga", D_MODEL), ("gb", D_MODEL)):
        cols[name] = w_in[:, o:o + wd]
        o += wd
    w_rope = jnp.concatenate(
        [_pad_heads(cols["q"]), cols["kc"], _pad_heads(cols["ksl"]), _pad_heads(cols["kwn"])], axis=1)
    w_v = jnp.concatenate([cols["vc"], _pad_heads(cols["vsl"]), _pad_heads(cols["vwn"])], axis=1)
    w_ng = jnp.pad(cols["ng"], ((0, 0), (0, LANES - 3 * N_HEADS_A)))
    w_gate = jnp.concatenate([cols["ga"], cols["gb"]], axis=1)
    ws = [w.astype(BF16) for w in (w_rope, w_v, w_ng, cols["rw"], w_gate)]
    row = lambda wd: pl.BlockSpec((tm, wd), lambda i: (i, 0))
    outs = [(Q_PAD, BF16), (KV_WIDTH, F32), (KV_PAD, BF16), (KV_PAD, BF16), (KV_WIDTH, F32), (KV_PAD, BF16),
            (KV_PAD, BF16), (LANES, F32), (C_RWKV, F32), (D_MODEL, F32), (D_MODEL, F32)]
    return pl.pallas_call(
        _inproj_kernel,
        grid=(n // tm,),
        in_specs=[row(D_MODEL), _full((1, D_MODEL)), row(LANES), row(LANES)] + [_full(w.shape) for w in ws],
        out_specs=[row(wd) for wd, _ in outs],
        out_shape=[jax.ShapeDtypeStruct((n, wd), dt) for wd, dt in outs],
        compiler_params=_params(("parallel",)),
        name="inproj",
    )(x2, g_mix.reshape(1, D_MODEL), cs, sn, *ws)


def _compress_kernel(c_ref, pe_ref, w1_ref, b1_ref, w2_ref, b2_ref, o_ref, *, pad_value):
    half = D_CMP * HEAD_DIM
    c = c_ref[0, 0]
    w1 = w1_ref[...]
    z1 = _mm(c, w1[0:half])
    z2 = _mm(c, w1[half:2 * half])
    z2 = jnp.concatenate([z2[1:], z2[:1]], axis=0)
    pb = _mm(pe_ref[...], w1)[0:1] + b1_ref[...]
    hid = jax.nn.gelu(z1 + z2 + pb)
    out = _mm(hid, w2_ref[...]) + b2_ref[...]
    o_ref[0, 0] = jnp.concatenate([out, jnp.full(out.shape, pad_value, F32)], axis=1).astype(o_ref.dtype)


def _compress(kv, pe, w1, b1, w2, b2, bsz, s, pad_value):
    nch = s // D_CMP
    c = kv.reshape(bsz, nch, D_CMP, N_KV_GROUPS, HEAD_DIM).transpose(0, 3, 1, 2, 4).reshape(
        bsz, N_KV_GROUPS, nch, D_CMP * HEAD_DIM)
    pe8 = jnp.broadcast_to(pe.reshape(1, L_CMP * HEAD_DIM), (8, L_CMP * HEAD_DIM))
    return pl.pallas_call(
        functools.partial(_compress_kernel, pad_value=pad_value),
        grid=(bsz, N_KV_GROUPS),
        in_specs=[pl.BlockSpec((1, 1, nch, D_CMP * HEAD_DIM), lambda b, g: (b, g, 0, 0)),
                  _full(pe8.shape), _full(w1.shape), _full((1, CMP_HIDDEN)), _full(w2.shape), _full((1, HEAD_DIM))],
        out_specs=pl.BlockSpec((1, 1, nch, LANES), lambda b, g: (b, g, 0, 0)),
        out_shape=jax.ShapeDtypeStruct((bsz, N_KV_GROUPS, nch, LANES), BF16),
        compiler_params=_params(("parallel", "parallel")),
        name="compress",
    )(c, pe8, w1, b1.reshape(1, CMP_HIDDEN), w2, b2.reshape(1, HEAD_DIM))


def _nsa_kernel(q_ref, kc_ref, vc_ref, ksl_ref, vsl_ref, kwn_ref, vwn_ref, ng_ref, ov_ref, o_ref, *, n_top):
    n_cmp = kc_ref.shape[2]
    n_slc = ov_ref.shape[1]
    s0 = pl.program_id(1) * Q_BLOCK
    hq = HEADS_PER_GROUP * Q_BLOCK
    t_q = s0 + _iota((Q_BLOCK, 1), 0)
    sig = jax.nn.sigmoid(ng_ref[0])
    win_start = pl.multiple_of(jnp.maximum(s0 - WINDOW, 0), LANES)
    n_tiles = s0 // KEY_TILE + 1
    groups = range(N_KV_GROUPS)
    lanes = lambda g: slice(g * LANES, (g + 1) * LANES)
    qg = [jnp.concatenate([q_ref[0, :, lanes(g * HEADS_PER_GROUP + h)] for h in range(HEADS_PER_GROUP)], axis=0)
          for g in groups]

    def add_bias(s, bias):
        return (s.reshape(HEADS_PER_GROUP, Q_BLOCK, bias.shape[1]) + bias[None]).reshape(hq, bias.shape[1])

    cmp_bias = jnp.where((_iota((1, n_cmp), 1) * D_CMP + (L_CMP - 1)) <= t_q, 0.0, NEG_INF)
    has_cmp = jnp.concatenate([(t_q >= L_CMP - 1).astype(F32)] * HEADS_PER_GROUP, axis=0)
    pos_w = win_start + _iota((1, WINDOW + Q_BLOCK), 1)
    win_bias = jnp.where((pos_w <= t_q) & (pos_w > t_q - WINDOW), 0.0, NEG_INF)

    s_c = [add_bias(_mm_nt(qg[g], kc_ref[0, g]), cmp_bias) for g in groups]
    s_w = [add_bias(_mm_nt(qg[g], kwn_ref[0, pl.ds(win_start, WINDOW + Q_BLOCK), lanes(g)]), win_bias)
           for g in groups]
    e_c = [jnp.exp(x - jnp.max(x, axis=-1, keepdims=True)) for x in s_c]
    p_c = [x * (has_cmp / jnp.sum(x, axis=-1, keepdims=True)) for x in e_c]
    e_w = [jnp.exp(x - jnp.max(x, axis=-1, keepdims=True)) for x in s_w]
    o_c = [_mm(p_c[g], vc_ref[0, g]) for g in groups]
    acc_w = [_mm(e_w[g], vwn_ref[0, pl.ds(win_start, WINDOW + Q_BLOCK), lanes(g)]) for g in groups]
    o_win = [x * (1.0 / x[:, HEAD_DIM:HEAD_DIM + 1]) for x in acc_w]

    imp_t = []
    for g in groups:
        p_sum = p_c[g][0:Q_BLOCK]
        for h in range(1, HEADS_PER_GROUP):
            p_sum = p_sum + p_c[g][h * Q_BLOCK:(h + 1) * Q_BLOCK]
        imp_t.append(_mm_exact_rhs(p_sum, ov_ref[...]).T)
    j = _iota((n_slc, Q_BLOCK), 0)
    cur = (s0 + _iota((n_slc, Q_BLOCK), 1)) >> 6
    forced = (j == 0) | (j == cur) | (j == cur - 1)
    score = [jnp.where(forced, -3e38, jnp.where(j <= cur, x, -1.0)) for x in imp_t]
    sel_t = [forced.astype(F32) for _ in groups]
    for _ in range(n_top - 3):
        for g in groups:
            m = jnp.max(score[g], axis=0, keepdims=True)
            first = jnp.min(jnp.where(score[g] == m, j, n_slc), axis=0, keepdims=True)
            hit = j == first
            sel_t[g] = jnp.where(hit, 1.0, sel_t[g])
            score[g] = jnp.where(hit, -3e38, score[g])
    sel = [1.0 - x.T for x in sel_t]

    lane_q = _iota((Q_BLOCK, LANES), 1)
    in_code = (lane_q >= HEAD_DIM) & (lane_q < HEAD_DIM + BLOCKS_PER_TILE)

    def sel_step(kt, carry, diagonal):
        k0 = pl.multiple_of(kt * KEY_TILE, KEY_TILE)
        shift = (HEAD_DIM - BLOCKS_PER_TILE * kt) & (LANES - 1)
        s = []
        for g in groups:
            code = jnp.where(in_code, pltpu.roll(sel[g], shift, 1), 0.0).astype(BF16)
            qa = qg[g] + jnp.concatenate([code] * HEADS_PER_GROUP, axis=0)
            s.append(_mm_nt(qa, ksl_ref[0, pl.ds(k0, KEY_TILE), lanes(g)]))
        if diagonal:
            causal = jnp.where((k0 + _iota((Q_BLOCK, KEY_TILE), 1)) <= t_q, 0.0, NEG_INF)
            s = [add_bias(x, causal) for x in s]
        m_new = [jnp.maximum(carry[2 * g], jnp.max(s[g], axis=-1, keepdims=True)) for g in groups]
        p = [jnp.exp(s[g] - m_new[g]) for g in groups]
        pv = [_mm(p[g], vsl_ref[0, pl.ds(k0, KEY_TILE), lanes(g)]) for g in groups]
        out = []
        for g in groups:
            out += [m_new[g], jnp.exp(carry[2 * g] - m_new[g]) * carry[2 * g + 1] + pv[g]]
        return tuple(out)

    init = (jnp.full((hq, 1), NEG_INF, F32), jnp.zeros((hq, LANES), F32)) * N_KV_GROUPS
    fin = lax.fori_loop(0, n_tiles - 1, lambda kt, c: sel_step(kt, c, False), init)
    fin = sel_step(n_tiles - 1, fin, True)

    heads_out = []
    for g in groups:
        acc_s = fin[2 * g + 1]
        o_s = acc_s * (1.0 / acc_s[:, HEAD_DIM:HEAD_DIM + 1])
        o_w = o_win[g]
        for h in range(HEADS_PER_GROUP):
            hh = g * HEADS_PER_GROUP + h
            r = slice(h * Q_BLOCK, (h + 1) * Q_BLOCK)
            mix = (sig[:, 3 * hh:3 * hh + 1] * o_c[g][r] + sig[:, 3 * hh + 1:3 * hh + 2] * o_s[r]
                   + sig[:, 3 * hh + 2:3 * hh + 3] * o_w[r])
            heads_out.append(mix[:, 0:HEAD_DIM])
    o_ref[0] = jnp.concatenate(heads_out, axis=1).astype(o_ref.dtype)


def _nsa(q, kc, vc, ksl, vsl, kwn, vwn, ng, bsz, s):
    n_cmp = s // D_CMP
    n_slc = s // L_SLC
    ii = np.arange(n_cmp)[:, None]
    jj = np.arange(n_slc)[None, :]
    assert n_slc <= LANES and s % KEY_TILE == 0 and min(N_SELECT, n_slc) >= 3
    overlap = (ii * D_CMP < (jj + 1) * L_SLC) & (ii * D_CMP + L_CMP > jj * L_SLC)
    overlap = jnp.asarray(np.pad(overlap, ((0, 0), (0, LANES - n_slc))), BF16)
    seq = lambda wd: pl.BlockSpec((1, s, wd), lambda b, i: (b, 0, 0))
    blk = lambda wd: pl.BlockSpec((1, Q_BLOCK, wd), lambda b, i: (b, i, 0))
    cmp_spec = pl.BlockSpec((1, N_KV_GROUPS, n_cmp, LANES), lambda b, i: (b, 0, 0, 0))
    r3 = lambda a: a.reshape(bsz, s, a.shape[-1])
    return pl.pallas_call(
        functools.partial(_nsa_kernel, n_top=min(N_SELECT, n_slc)),
        grid=(bsz, s // Q_BLOCK),
        in_specs=[blk(Q_PAD), cmp_spec, cmp_spec, seq(KV_PAD), seq(KV_PAD), seq(KV_PAD), seq(KV_PAD),
                  blk(LANES), _full(overlap.shape)],
        out_specs=blk(WIDTH_A),
        out_shape=jax.ShapeDtypeStruct((bsz, s, WIDTH_A), BF16),
        compiler_params=_params(("parallel", "arbitrary")),
        name="nsa",
    )(r3(q), kc, vc, r3(ksl), r3(vsl), r3(kwn), r3(vwn), r3(ng), overlap)


def _rwkv_prep_kernel(p_ref, pv_ref, mu_ref, w0_ref, w2_ref, a0_ref, a2_ref, g2_ref, kk_ref, ka_ref, bd_ref,
                      r_ref, lw_ref, k_ref, v_ref, kn_ref, kb_ref, g_ref, *, tiles_per_seq):
    p = p_ref[...]
    first = (pl.program_id(0) % tiles_per_seq) == 0
    prev_last = jnp.where(first, 0.0, pv_ref[7:8, :])
    p_prev = jnp.concatenate([prev_last, p[:-1]], axis=0)
    xs = p + (p_prev - p) * mu_ref[...]
    wb = WIDTH_B
    r = xs[:, 0:wb]
    k = xs[:, wb:2 * wb]
    v = xs[:, 2 * wb:3 * wb]
    o = 3 * wb
    xw = xs[:, o:o + W_LORA]
    xa = xs[:, o + W_LORA:o + W_LORA + A_LORA]
    xg = xs[:, o + W_LORA + A_LORA:o + W_LORA + A_LORA + G_LORA]
    z = w0_ref[...] + _mm_hi(jnp.tanh(xw), w2_ref[...])
    w_log = -(jnp.maximum(-z, 0.0) + jnp.log(1.0 + jnp.exp(-jnp.abs(z)))) - 0.5
    a = jax.nn.sigmoid(a0_ref[...] + _mm_hi(xa, a2_ref[...]))
    kk = k * kk_ref[...]
    ss = _mm_hi(kk * kk, bd_ref[...])
    kn = kk * (1.0 / jnp.maximum(jnp.sqrt(ss), 1e-12))
    r_ref[...] = r
    lw_ref[...] = -jnp.exp(w_log)
    k_ref[...] = k * (1.0 + (a - 1.0) * ka_ref[...])
    v_ref[...] = v
    kn_ref[...] = kn
    kb_ref[...] = kn * a
    g_ref[...] = _mm(jax.nn.sigmoid(xg), g2_ref[...])


def _rwkv_prep(prw, mu, w0, w2, a0, a2, g2, k_k, k_a, s, tm=256):
    n = prw.shape[0]
    hid = np.arange(WIDTH_B) // HEAD_DIM_B
    bd = jnp.asarray(hid[:, None] == hid[None, :], F32)
    row = lambda wd: pl.BlockSpec((tm, wd), lambda i: (i, 0))
    vec = lambda a: a.reshape(1, -1)
    return pl.pallas_call(
        functools.partial(_rwkv_prep_kernel, tiles_per_seq=s // tm),
        grid=(n // tm,),
        in_specs=[row(C_RWKV), pl.BlockSpec((8, C_RWKV), lambda i: (jnp.maximum(i * (tm // 8) - 1, 0), 0)),
                  _full((1, C_RWKV)), _full((1, WIDTH_B)), _full(w2.shape), _full((1, WIDTH_B)), _full(a2.shape),
                  _full(g2.shape), _full((1, WIDTH_B)), _full((1, WIDTH_B)), _full(bd.shape)],
        out_specs=[row(WIDTH_B)] * 7,
        out_shape=[jax.ShapeDtypeStruct((n, WIDTH_B), F32)] * 7,
        compiler_params=_params(("parallel",)),
        name="rwkv_prep",
    )(prw, prw, vec(mu), vec(w0), w2, vec(a0), a2, g2, vec(k_k), vec(k_a), bd)


def _rwkv_chunk_kernel(r_ref, lw_ref, k_ref, v_ref, kn_ref, kb_ref, rk_ref,
                       bonus_ref, tm_ref, *, n_sub):
    L = CHUNK
    hd = HEAD_DIM_B
    ti = _iota((L, L), 0)
    si = _iota((L, L), 1)
    strict = si < ti
    incl = si <= ti
    same_sub = (ti // SUB) == (si // SUB)
    eye = (ti == si).astype(F32)
    tri = incl.astype(F32)
    at, bt, kt, rt, bl, kl, vv, rkr, p_tot = [], [], [], [], [], [], [], [], []
    for j in range(n_sub):
        rows = slice(j * L, (j + 1) * L)
        lw = lw_ref[0, rows, :]
        cs = _mm_hi(tri, lw)
        p_inv = jnp.exp(-cs)
        p_end = jnp.exp(cs[L - 1:L, :] - cs)
        r = r_ref[0, rows, :]
        k = k_ref[0, rows, :]
        kb = kb_ref[0, rows, :]
        at.append(-kn_ref[0, rows, :] * jnp.exp(cs - lw))
        bt.append(kb * p_inv)
        kt.append(k * p_inv)
        rt.append(r * jnp.exp(cs))
        bl.append(kb * p_end)
        kl.append(k * p_end)
        vv.append(v_ref[0, rows, :])
        rkr.append(r * k * rk_ref[...])
        p_tot.append(jnp.exp(cs[L - 1:L, :]))
    units = [(j, h) for j in range(n_sub) for h in range(N_HEADS_B)]
    sl = lambda arr, u: arr[u[0]][:, u[1] * hd:(u[1] + 1) * hd]
    aa = [_mm_nt(jnp.concatenate([sl(at, u), sl(rt, u)], axis=0),
                 jnp.concatenate([sl(bt, u), sl(kt, u)], axis=0)) for u in units]
    a_ab = [jnp.where(strict, x[0:L, 0:L], 0.0) for x in aa]
    a_low = [jnp.concatenate([jnp.where(strict, x[0:L, L:2 * L], 0.0),
                              jnp.where(incl, x[L:2 * L, L:2 * L], 0.0)], axis=0) for x in aa]
    a_rb = [jnp.where(incl, x[L:2 * L, 0:L], 0.0) for x in aa]
    av = [_mm(x, sl(vv, u)) for x, u in zip(a_low, units)]
    kv = [_mm_tn(sl(kl, u), sl(vv, u)) for u in units]
    pw = [jnp.where(same_sub, x, 0.0) for x in a_ab]
    t = [eye + x for x in pw]
    for _ in range(3):
        pw = [_mm(x, x) for x in pw]
        t = [x + _mm(x, y) for x, y in zip(t, pw)]
    width = SUB
    while width < L:
        m = ((ti // width) == (si // width) + 1) & ((ti // (2 * width)) == (si // (2 * width)))
        ot = [_mm(jnp.where(m, x, 0.0), y) for x, y in zip(a_ab, t)]
        t = [x + _mm(x, y) for x, y in zip(t, ot)]
        width *= 2
    wu = [_mm(x, jnp.concatenate([sl(at, u), y[0:L]], axis=1)) for x, y, u in zip(t, av, units)]
    qy = [jnp.concatenate([sl(rt, u), y[L:2 * L]], axis=1) + _mm(x, w)
          for x, y, w, u in zip(a_rb, av, wu, units)]
    gh = [_mm_tn(sl(bl, u), w) + jnp.concatenate([eye * sl(p_tot, u), y], axis=1)
          for w, y, u in zip(wu, kv, units)]
    bonus = [jnp.sum(sl(rkr, u), axis=-1, keepdims=True) * sl(vv, u) for u in units]
    for j in range(n_sub):
        bonus_ref[0, j * L:(j + 1) * L, :] = jnp.concatenate(bonus[j * N_HEADS_B:(j + 1) * N_HEADS_B], axis=1)
    for x, y, (j, h) in zip(qy, gh, units):
        tm_ref[0, j, h, 0:L, :] = x
        tm_ref[0, j, h, L:2 * L, :] = y


def _rwkv_chunk(r, lw, k, v, kn, kb, r_k, bsz, s, n_sub=2):
    nch = s // CHUNK
    blk = pl.BlockSpec((1, n_sub * CHUNK, WIDTH_B), lambda b, c: (b, c, 0))
    mat = pl.BlockSpec((1, n_sub, N_HEADS_B, 2 * CHUNK, 2 * HEAD_DIM_B), lambda b, c: (b, c, 0, 0, 0))
    r3 = lambda a: a.reshape(bsz, s, WIDTH_B)
    return pl.pallas_call(
        functools.partial(_rwkv_chunk_kernel, n_sub=n_sub),
        grid=(bsz, nch // n_sub),
        in_specs=[blk] * 6 + [_full((1, WIDTH_B))],
        out_specs=[blk, mat],
        out_shape=[jax.ShapeDtypeStruct((bsz, s, WIDTH_B), F32),
                   jax.ShapeDtypeStruct((bsz, nch, N_HEADS_B, 2 * CHUNK, 2 * HEAD_DIM_B), F32)],
        compiler_params=_params(("parallel", "parallel")),
        name="rwkv_chunk",
    )(r3(r), r3(lw), r3(k), r3(v), r3(kn), r3(kb), r_k.reshape(1, WIDTH_B))


def _rwkv_scan_kernel(tm_ref, yn_ref, st_ref):
    L = CHUNK
    hd = HEAD_DIM_B

    @pl.when(pl.program_id(0) == 0)
    def _():
        eye = (_iota((hd, hd), 0) == _iota((hd, hd), 1)).astype(F32)
        for i in range(st_ref.shape[0]):
            st_ref[i] = jnp.concatenate([jnp.zeros((hd, hd), F32), eye], axis=0)

    for b in range(tm_ref.shape[0]):
        res = [_mm_hi(tm_ref[b, 0, h], st_ref[b * N_HEADS_B + h]) for h in range(N_HEADS_B)]
        ys = []
        for h in range(N_HEADS_B):
            st_ref[b * N_HEADS_B + h, 0:hd, :] = res[h][L:2 * L]
            y = res[h][0:L]
            mean = jnp.mean(y, axis=-1, keepdims=True)
            d = y - mean
            var = jnp.mean(d * d, axis=-1, keepdims=True)
            ys.append(d * lax.rsqrt(var + GN_EPS))
        yn_ref[b] = jnp.concatenate(ys, axis=1)


def _rwkv_scan(tmat, bsz, s):
    nch = s // CHUNK
    return pl.pallas_call(
        _rwkv_scan_kernel,
        grid=(nch,),
        in_specs=[pl.BlockSpec((bsz, 1, N_HEADS_B, 2 * CHUNK, 2 * HEAD_DIM_B), lambda c: (0, c, 0, 0, 0))],
        out_specs=pl.BlockSpec((bsz, CHUNK, WIDTH_B), lambda c: (0, c, 0)),
        out_shape=jax.ShapeDtypeStruct((bsz, s, WIDTH_B), F32),
        scratch_shapes=[pltpu.VMEM((bsz * N_HEADS_B, 2 * HEAD_DIM_B, HEAD_DIM_B), F32)],
        compiler_params=_params(("arbitrary",)),
        name="rwkv_scan",
    )(tmat)


SUBLANES = 8
TOKEN_TILE = (SUBLANES, D_MODEL // SUBLANES)
assert TOKEN_TILE[1] == LANES


def _token_tile_spec(rows):
    return pl.BlockSpec((rows,) + TOKEN_TILE, lambda i, *_: (i, 0, 0))


def _store_token_tiles(ref, x):
    for c in range(SUBLANES):
        ref[:, c, :] = x[:, c * LANES:(c + 1) * LANES]


def _gather_buffer(rows):
    return pltpu.VMEM((2, rows // SUBLANES, SUBLANES, SUBLANES, LANES), F32)


def _gathered_rows(buf, slot, start, rows):
    t0, nt = start // SUBLANES, rows // SUBLANES
    return jnp.concatenate([buf[slot, t0:t0 + nt, c].reshape(rows, LANES) for c in range(SUBLANES)], axis=1)


def _merge_kernel(x_ref, oa_ref, yn_ref, bonus_ref, g_ref, ga_ref, gb_ref, lnw_ref, lnb_ref, wa_ref, wb_ref,
                  wo_ref, gf_ref, wr_ref, br_ref, x1_ref, h2_ref, route_ref, cnt_ref, run_ref):
    @pl.when(pl.program_id(0) == 0)
    def _():
        run_ref[...] = jnp.zeros_like(run_ref)

    ob = (yn_ref[...] * lnw_ref[...] + lnb_ref[...] + bonus_ref[...]) * g_ref[...]
    merged = (jax.nn.sigmoid(ga_ref[...]) * jnp.dot(oa_ref[...], wa_ref[...], preferred_element_type=F32)
              + jax.nn.sigmoid(gb_ref[...]) * _mm(ob, wb_ref[...]))
    x1 = x_ref[...] + _mm(merged, wo_ref[...])
    x1_ref[...] = x1
    h2 = x1 * lax.rsqrt(jnp.mean(x1 * x1, axis=-1, keepdims=True) + RMS_EPS) * gf_ref[...]
    _store_token_tiles(h2_ref, h2)
    logits = _mm_hi(h2, wr_ref[...]) + br_ref[...]
    lane = _iota(logits.shape, 1)
    big = 4 * LANES
    is_grp = lane < N_GROUPS
    lg = jnp.where(is_grp, logits, NEG_INF)
    mg = jnp.max(lg, axis=-1, keepdims=True)
    gidx = jnp.min(jnp.where(lg == mg, lane, big), axis=-1, keepdims=True)
    pg = 1.0 / jnp.sum(jnp.where(is_grp, jnp.exp(lg - mg), 0.0), axis=-1, keepdims=True)
    in_grp = (lane >= N_GROUPS) & (((lane - N_GROUPS) >> 3) == gidx) & (lane < N_GROUPS + N_EXPERTS)
    le = jnp.where(in_grp, logits, NEG_INF)
    m1 = jnp.max(le, axis=-1, keepdims=True)
    i1 = jnp.min(jnp.where(le == m1, lane, big), axis=-1, keepdims=True)
    le2 = jnp.where(lane == i1, NEG_INF, le)
    m2 = jnp.max(le2, axis=-1, keepdims=True)
    i2 = jnp.min(jnp.where(le2 == m2, lane, big), axis=-1, keepdims=True)
    e2 = jnp.exp(m2 - m1)
    inv = pg / (1.0 + e2)
    tm = logits.shape[0]
    hot1 = (lane == i1 - N_GROUPS).astype(F32)
    hot2 = (lane == i2 - N_GROUPS).astype(F32)
    both = hot1 + hot2
    earlier = (_iota((tm, tm), 1) < _iota((tm, tm), 0)).astype(BF16)
    before = jnp.dot(earlier, both.astype(BF16), preferred_element_type=F32) + run_ref[...]
    rank1 = jnp.sum(hot1 * before, axis=-1, keepdims=True)
    rank2 = jnp.sum(hot2 * before, axis=-1, keepdims=True)
    run_ref[...] = run_ref[...] + jnp.sum(both, axis=0, keepdims=True)
    cnt_ref[...] = run_ref[...]
    cols = ((i1 - N_GROUPS).astype(F32), (i2 - N_GROUPS).astype(F32), inv, inv * e2, rank1, rank2)
    route = jnp.zeros_like(logits)
    for c, v in enumerate(cols):
        route = jnp.where(lane == c, v, route)
    route_ref[...] = route


def _merge(x2, oa, yn, bonus, g, ga, gb, ln_w, ln_b, w_a, w_b, w_o, g_ffn, w_grp, b_grp, w_exp, b_exp, tm=256):
    n = x2.shape[0]
    w_r = jnp.pad(jnp.concatenate([w_grp, w_exp], axis=1), ((0, 0), (0, LANES - N_GROUPS - N_EXPERTS)))
    b_r = jnp.pad(jnp.concatenate([b_grp, b_exp]), (0, LANES - N_GROUPS - N_EXPERTS)).reshape(1, LANES)
    row = lambda wd: pl.BlockSpec((tm, wd), lambda i: (i, 0))
    vec = lambda a: a.reshape(1, -1)
    wa, wb, wo = w_a.astype(BF16), w_b.astype(BF16), w_o.astype(BF16)
    return pl.pallas_call(
        _merge_kernel,
        grid=(n // tm,),
        in_specs=[row(D_MODEL), row(WIDTH_A), row(WIDTH_B), row(WIDTH_B), row(WIDTH_B), row(D_MODEL), row(D_MODEL),
                  _full((1, WIDTH_B)), _full((1, WIDTH_B)), _full(wa.shape), _full(wb.shape), _full(wo.shape),
                  _full((1, D_MODEL)), _full(w_r.shape), _full((1, LANES))],
        out_specs=[row(D_MODEL), _token_tile_spec(tm), row(LANES), _full((1, LANES))],
        out_shape=[jax.ShapeDtypeStruct((n, D_MODEL), F32), jax.ShapeDtypeStruct((n,) + TOKEN_TILE, F32),
                   jax.ShapeDtypeStruct((n, LANES), F32), jax.ShapeDtypeStruct((1, LANES), F32)],
        scratch_shapes=[pltpu.VMEM((1, LANES), F32)],
        compiler_params=_params(("arbitrary",)),
        name="merge",
    )(x2, oa, yn, bonus, g, ga, gb, vec(ln_w), vec(ln_b), wa, wb, wo, vec(g_ffn), w_r, b_r)


def _gather_start(idx_ref, src_hbm, buf, sem, slot, rows):
    def body(t, carry):
        for u in range(SUBLANES):
            pltpu.make_async_copy(src_hbm.at[pl.ds(idx_ref[0, 0, t * SUBLANES + u], 1)],
                                  buf.at[slot, pl.ds(t, 1), :, u], sem.at[slot]).start(priority=u % 2)
        return carry
    lax.fori_loop(0, rows // SUBLANES, body, 0)


def _gather_wait(buf, sem, slot):
    pltpu.make_async_copy(buf.at[slot], buf.at[slot], sem.at[slot]).wait()


def _gather_pipeline(idx_ref, idx_next_ref, src_hbm, buf, sem, rows):
    i = pl.program_id(0)
    slot = i % 2

    @pl.when(i == 0)
    def _():
        _gather_start(idx_ref, src_hbm, buf, sem, 0, rows)

    @pl.when(i + 1 < pl.num_programs(0))
    def _():
        _gather_start(idx_next_ref, src_hbm, buf, sem, 1 - slot, rows)

    _gather_wait(buf, sem, slot)
    return slot


def _dispatch_kernel(pos_ref, h2_hbm, zero_hbm, x_hbm, sem, *, tm):
    del zero_hbm
    i = pl.program_id(0)
    slot = i % 2

    def body(t, carry):
        for u in range(SUBLANES):
            r = t * SUBLANES + u
            for k in range(TOP_K):
                p = pos_ref[0, 0, k * tm + r]
                pltpu.make_async_copy(h2_hbm.at[pl.ds(i * tm + r, 1)], x_hbm.at[pl.ds(p >> 3, 1), :, p & (SUBLANES - 1)],
                                      sem.at[slot]).start(priority=(u + k) % 2)
        return carry
    lax.fori_loop(0, tm // SUBLANES, body, 0)

    def wait(which):
        rows = h2_hbm.at[pl.ds(0, TOP_K * tm)]
        pltpu.make_async_copy(rows, rows, sem.at[which]).wait()

    @pl.when(i > 0)
    def _():
        wait(1 - slot)

    @pl.when(i == pl.num_programs(0) - 1)
    def _():
        wait(slot)


def _dispatch(h2, pos, n_pad, tm=256):
    n = h2.shape[0]
    nt = n // tm
    pos3 = pos.reshape(nt, tm, TOP_K).transpose(0, 2, 1).reshape(nt, 1, TOP_K * tm)
    tiles = (n_pad // SUBLANES, SUBLANES, SUBLANES, LANES)
    any_spec = pl.BlockSpec(memory_space=pl.ANY)
    return pl.pallas_call(
        functools.partial(_dispatch_kernel, tm=tm),
        grid=(nt,),
        in_specs=[pl.BlockSpec((1, 1, TOP_K * tm), lambda i: (i, 0, 0), memory_space=pltpu.SMEM), any_spec, any_spec],
        out_specs=any_spec,
        out_shape=jax.ShapeDtypeStruct(tiles, F32),
        scratch_shapes=[pltpu.SemaphoreType.DMA((2,))],
        input_output_aliases={2: 0},
        compiler_params=_params(("arbitrary",)),
        name="dispatch",
    )(pos3, h2, jnp.zeros(tiles, F32))


def _moe_kernel(be_ref, x_ref, w13_ref, w2_ref, y_ref):
    xb = jnp.concatenate([x_ref[:, c].reshape(MOE_BLOCK, LANES) for c in range(SUBLANES)], axis=1)
    a = _mm(xb, w13_ref[0])
    hid = jax.nn.silu(a[:, 0:D_EXPERT]) * a[:, D_EXPERT:2 * D_EXPERT]
    _store_token_tiles(y_ref, _mm(hid, w2_ref[0]))


def _moe(x_pad, blk_expert, w13, w2):
    n_blk = blk_expert.shape[0]
    grid_spec = pltpu.PrefetchScalarGridSpec(
        num_scalar_prefetch=1,
        grid=(n_blk,),
        in_specs=[pl.BlockSpec((MOE_BLOCK // SUBLANES, SUBLANES, SUBLANES, LANES), lambda i, be: (i, 0, 0, 0)),
                  pl.BlockSpec((1, D_MODEL, 2 * D_EXPERT), lambda i, be: (be[i], 0, 0)),
                  pl.BlockSpec((1, D_EXPERT, D_MODEL), lambda i, be: (be[i], 0, 0))],
        out_specs=_token_tile_spec(MOE_BLOCK),
    )
    return pl.pallas_call(
        _moe_kernel,
        grid_spec=grid_spec,
        out_shape=jax.ShapeDtypeStruct((n_blk * MOE_BLOCK,) + TOKEN_TILE, F32),
        compiler_params=_params(("parallel",)),
        name="moe",
    )(blk_expert, x_pad, w13, w2)


def _final_kernel(pos_ref, pos_next_ref, x1_ref, route_ref, g_ref, y_hbm, o_ref, buf, sem, *, tm):
    slot = _gather_pipeline(pos_ref, pos_next_ref, y_hbm, buf, sem, TOP_K * tm)
    x = x1_ref[...]
    for k in range(TOP_K):
        x = x + route_ref[:, TOP_K + k:TOP_K + k + 1] * _gathered_rows(buf, slot, k * tm, tm)
    o_ref[...] = x * lax.rsqrt(jnp.mean(x * x, axis=-1, keepdims=True) + RMS_EPS) * g_ref[...]


def _final(x1, route, y_pad, pos, g_final, tm=256):
    n = x1.shape[0]
    nt = n // tm
    pos3 = pos.reshape(nt, tm, TOP_K).transpose(0, 2, 1).reshape(nt, 1, TOP_K * tm)
    smem_blk = lambda f: pl.BlockSpec((1, 1, TOP_K * tm), f, memory_space=pltpu.SMEM)
    row = lambda wd: pl.BlockSpec((tm, wd), lambda i: (i, 0))
    return pl.pallas_call(
        functools.partial(_final_kernel, tm=tm),
        grid=(nt,),
        in_specs=[smem_blk(lambda i: (i, 0, 0)), smem_blk(lambda i: (jnp.minimum(i + 1, nt - 1), 0, 0)),
                  row(D_MODEL), row(LANES), _full((1, D_MODEL)), pl.BlockSpec(memory_space=pl.ANY)],
        out_specs=row(D_MODEL),
        out_shape=jax.ShapeDtypeStruct((n, D_MODEL), F32),
        scratch_shapes=[_gather_buffer(TOP_K * tm), pltpu.SemaphoreType.DMA((2,))],
        compiler_params=_params(("arbitrary",)),
        name="final",
    )(pos3, pos3, x1, route, g_final.reshape(1, D_MODEL), y_pad)


def _dispatch_plan(route, counts, n_tok):
    expert = route[:, 0:TOP_K].astype(I32)
    rank = route[:, 2 * TOP_K:3 * TOP_K].astype(I32)
    cnt = counts[0, 0:N_EXPERTS].astype(I32)
    padded = (cnt + MOE_BLOCK - 1) // MOE_BLOCK * MOE_BLOCK
    pad_end = jnp.cumsum(padded)
    pos = (pad_end - padded)[expert] + rank
    n_pad = n_tok * TOP_K + N_EXPERTS * MOE_BLOCK
    n_blk = n_pad // MOE_BLOCK
    blk_expert = jnp.minimum(jnp.searchsorted(pad_end, jnp.arange(n_blk, dtype=I32) * MOE_BLOCK, side="right"),
                             N_EXPERTS - 1).astype(I32)
    return pos, blk_expert, n_pad


def kernel(x, positions, g_mix, w_in, cmp_pe_k, cmp_w1_k, cmp_b1_k, cmp_w2_k, cmp_b2_k, cmp_pe_v, cmp_w1_v, cmp_b1_v, cmp_w2_v, cmp_b2_v, rwkv_mu, rwkv_w0, rwkv_w2, rwkv_a0, rwkv_a2, rwkv_g2, rwkv_k_k, rwkv_k_a, rwkv_r_k, rwkv_ln_w, rwkv_ln_b, w_proj_a, w_proj_b, w_out, g_ffn, w_grp, b_grp, w_exp, b_exp, e_w1, e_w3, e_w2, g_final):
    bsz, s, _ = x.shape
    assert w_in.shape[0] == 1, "single-layer stack"
    n_tok = bsz * s
    x2 = x.reshape(n_tok, D_MODEL)
    half = HEAD_DIM // 2
    freqs = jnp.power(ROPE_THETA, -jnp.arange(half, dtype=F32) / half)
    ang = positions.astype(F32).reshape(n_tok, 1) * freqs
    cs = jnp.tile(jnp.cos(ang), (1, 4))
    sn = jnp.tile(jnp.concatenate([-jnp.sin(ang), jnp.sin(ang)], axis=1), (1, 2))
    q, kc, ksl, kwn, vc, vsl, vwn, ng, prw, ga, gb = _inproj(x2, g_mix[0], cs, sn, w_in[0])
    k_c = _compress(kc, cmp_pe_k[0], cmp_w1_k[0], cmp_b1_k[0], cmp_w2_k[0], cmp_b2_k[0], bsz, s, 0.0)
    v_c = _compress(vc, cmp_pe_v[0], cmp_w1_v[0], cmp_b1_v[0], cmp_w2_v[0], cmp_b2_v[0], bsz, s, 1.0)
    o_a = _nsa(q, k_c, v_c, ksl, vsl, kwn, vwn, ng, bsz, s).reshape(n_tok, WIDTH_A)
    r, lw, k, v, kn, kb, g = _rwkv_prep(prw, rwkv_mu[0], rwkv_w0[0], rwkv_w2[0], rwkv_a0[0], rwkv_a2[0],
                                        rwkv_g2[0], rwkv_k_k[0], rwkv_k_a[0], s)
    bonus, tmat = _rwkv_chunk(r, lw, k, v, kn, kb, rwkv_r_k[0], bsz, s)
    yn = _rwkv_scan(tmat, bsz, s)
    x1, h2, route, counts = _merge(x2, o_a, yn.reshape(n_tok, WIDTH_B), bonus.reshape(n_tok, WIDTH_B), g, ga, gb,
                           rwkv_ln_w[0], rwkv_ln_b[0], w_proj_a[0], w_proj_b[0], w_out[0], g_ffn[0],
                           w_grp[0], b_grp[0], w_exp[0], b_exp[0])
    pos, blk_expert, n_pad = _dispatch_plan(route, counts, n_tok)
    w13 = jnp.concatenate([e_w1[0], e_w3[0]], axis=-1).astype(BF16)
    y_pad = _moe(_dispatch(h2, pos, n_pad), blk_expert, w13, e_w2[0].astype(BF16))
    out = _final(x1, route, y_pad, pos, g_final)
    return out.reshape(bsz, s, D_MODEL)
```

```python
import functools

import numpy as np
import jax
import jax.numpy as jnp
from jax import lax
from jax.experimental import pallas as pl
from jax.experimental.pallas import tpu as pltpu

F32 = jnp.float32
BF16 = jnp.bfloat16
I32 = jnp.int32

D_MODEL = 1024
N_HEADS_A = 8
N_KV_GROUPS = 2
HEADS_PER_GROUP = N_HEADS_A // N_KV_GROUPS
HEAD_DIM = 64
L_CMP = 32
D_CMP = 16
CMP_HIDDEN = 256
L_SLC = 64
N_SELECT = 16
WINDOW = 512
Q_BLOCK = 128
ROPE_THETA = 10000.0
FORCE_SCORE = 1e4
NEG_INF = -1e30
N_HEADS_B = 8
HEAD_DIM_B = 64
W_LORA = 64
A_LORA = 64
G_LORA = 128
GN_EPS = 64e-5
N_GROUPS = 4
EXPERTS_PER_GROUP = 8
N_EXPERTS = N_GROUPS * EXPERTS_PER_GROUP
TOP_K = 2
D_EXPERT = 256
MOE_BLOCK = 256
RMS_EPS = 1e-6
WIDTH_A = N_HEADS_A * HEAD_DIM
KV_WIDTH = N_KV_GROUPS * HEAD_DIM
WIDTH_B = N_HEADS_B * HEAD_DIM_B
C_RWKV = 3 * WIDTH_B + W_LORA + A_LORA + G_LORA

LANES = 128
CHUNK = 64
SUB = 16
KEY_TILE = 1024
BLOCKS_PER_TILE = KEY_TILE // L_SLC
VMEM_LIMIT = 56 * 1024 * 1024

HI = lax.Precision.HIGHEST


def _mm(a, b):
    return jnp.dot(a.astype(BF16), b.astype(BF16), preferred_element_type=F32)


def _mm_nt(a, b):
    return lax.dot_general(a.astype(BF16), b.astype(BF16), (((1,), (1,)), ((), ())), preferred_element_type=F32)


def _mm_tn(a, b):
    return lax.dot_general(a.astype(BF16), b.astype(BF16), (((0,), (0,)), ((), ())), preferred_element_type=F32)


def _mm_hi(a, b):
    return jnp.dot(a, b, preferred_element_type=F32, precision=HI)


def _mm_exact_rhs(a, b):
    a1 = a.astype(BF16)
    r1 = a - a1.astype(F32)
    a2 = r1.astype(BF16)
    a3 = (r1 - a2.astype(F32)).astype(BF16)
    bb = b.astype(BF16)
    dot = lambda x: jnp.dot(x, bb, preferred_element_type=F32)
    return dot(a1) + dot(a2) + dot(a3)


def _iota(shape, dim):
    return lax.broadcasted_iota(I32, shape, dim)


def _params(sem):
    return pltpu.CompilerParams(dimension_semantics=sem, vmem_limit_bytes=VMEM_LIMIT)


def _full(shape):
    nd = len(shape)
    return pl.BlockSpec(shape, lambda *_: (0,) * nd)


Q_PAD = N_HEADS_A * LANES
KV_PAD = N_KV_GROUPS * LANES


def _inproj_kernel(x_ref, g_ref, cs_ref, sn_ref, wr_ref, wv_ref, wn_ref, ww_ref, wg_ref,
                   q_ref, kc_ref, ksl_ref, kwn_ref, vc_ref, vsl_ref, vwn_ref, ng_ref, prw_ref, ga_ref, gb_ref):
    x = x_ref[...]
    h = (x * lax.rsqrt(jnp.mean(x * x, axis=-1, keepdims=True) + RMS_EPS) * g_ref[...]).astype(BF16)
    pr = jnp.dot(h, wr_ref[...], preferred_element_type=F32)
    cs = cs_ref[...]
    sn = sn_ref[...]
    first_half = (_iota((1, LANES), 1) & (HEAD_DIM // 2)) == 0
    chunks = []
    for c in range(pr.shape[1] // LANES):
        x = pr[:, c * LANES:(c + 1) * LANES]
        swapped = jnp.where(first_half, pltpu.roll(x, LANES - HEAD_DIM // 2, 1), pltpu.roll(x, HEAD_DIM // 2, 1))
        chunks.append(x * cs + swapped * sn)
    ro = jnp.concatenate(chunks, axis=1)
    q_ref[...] = (ro[:, 0:Q_PAD] * (HEAD_DIM ** -0.5)).astype(BF16)
    o = Q_PAD
    kc_ref[...] = ro[:, o:o + KV_WIDTH]
    tm = x_ref.shape[0]
    blk = ((pl.program_id(0) * tm + _iota((tm, 1), 0)) >> 6) & (BLOCKS_PER_TILE - 1)
    code = jnp.where((_iota((1, KV_PAD), 1) & (LANES - 1)) == HEAD_DIM + blk, NEG_INF, 0.0)
    ksl_ref[...] = (ro[:, o + KV_WIDTH:o + KV_WIDTH + KV_PAD] + code).astype(BF16)
    kwn_ref[...] = ro[:, o + KV_WIDTH + KV_PAD:o + KV_WIDTH + 2 * KV_PAD].astype(BF16)
    v = jnp.dot(h, wv_ref[...], preferred_element_type=F32)
    vc_ref[...] = v[:, 0:KV_WIDTH]
    ones = ((_iota((1, 2 * KV_PAD), 1) & (LANES - 1)) >= HEAD_DIM).astype(F32)
    vp = v[:, KV_WIDTH:KV_WIDTH + 2 * KV_PAD] + ones
    vsl_ref[...] = vp[:, 0:KV_PAD].astype(BF16)
    vwn_ref[...] = vp[:, KV_PAD:2 * KV_PAD].astype(BF16)
    ng_ref[...] = jnp.dot(h, wn_ref[...], preferred_element_type=F32)
    prw_ref[...] = jnp.dot(h, ww_ref[...], preferred_element_type=F32)
    gg = jnp.dot(h, wg_ref[...], preferred_element_type=F32)
    ga_ref[...] = gg[:, 0:D_MODEL]
    gb_ref[...] = gg[:, D_MODEL:2 * D_MODEL]


def _pad_heads(w):
    d, wd = w.shape
    return jnp.pad(w.reshape(d, wd // HEAD_DIM, HEAD_DIM), ((0, 0), (0, 0), (0, LANES - HEAD_DIM))).reshape(d, -1)


def _inproj(x2, g_mix, cs, sn, w_in, tm=256):
    n = x2.shape[0]
    o = 0
    cols = {}
    for name, wd in (("q", WIDTH_A), ("kc", KV_WIDTH), ("vc", KV_WIDTH), ("ksl", KV_WIDTH), ("vsl", KV_WIDTH),
                     ("kwn", KV_WIDTH), ("vwn", KV_WIDTH), ("ng", 3 * N_HEADS_A), ("rw", C_RWKV),
                     ("ga", D_MODEL), ("gb", D_MODEL)):
        cols[name] = w_in[:, o:o + wd]
        o += wd
    w_rope = jnp.concatenate(
        [_pad_heads(cols["q"]), cols["kc"], _pad_heads(cols["ksl"]), _pad_heads(cols["kwn"])], axis=1)
    w_v = jnp.concatenate([cols["vc"], _pad_heads(cols["vsl"]), _pad_heads(cols["vwn"])], axis=1)
    w_ng = jnp.pad(cols["ng"], ((0, 0), (0, LANES - 3 * N_HEADS_A)))
    w_gate = jnp.concatenate([cols["ga"], cols["gb"]], axis=1)
    ws = [w.astype(BF16) for w in (w_rope, w_v, w_ng, cols["rw"], w_gate)]
    row = lambda wd: pl.BlockSpec((tm, wd), lambda i: (i, 0))
    outs = [(Q_PAD, BF16), (KV_WIDTH, F32), (KV_PAD, BF16), (KV_PAD, BF16), (KV_WIDTH, F32), (KV_PAD, BF16),
            (KV_PAD, BF16), (LANES, F32), (C_RWKV, F32), (D_MODEL, F32), (D_MODEL, F32)]
    return pl.pallas_call(
        _inproj_kernel,
        grid=(n // tm,),
        in_specs=[row(D_MODEL), _full((1, D_MODEL)), row(LANES), row(LANES)] + [_full(w.shape) for w in ws],
        out_specs=[row(wd) for wd, _ in outs],
        out_shape=[jax.ShapeDtypeStruct((n, wd), dt) for wd, dt in outs],
        compiler_params=_params(("parallel",)),
        name="inproj",
    )(x2, g_mix.reshape(1, D_MODEL), cs, sn, *ws)


def _compress_kernel(c_ref, pe_ref, w1_ref, b1_ref, w2_ref, b2_ref, o_ref, *, pad_value):
    half = D_CMP * HEAD_DIM
    c = c_ref[0, 0]
    w1 = w1_ref[...]
    z1 = _mm(c, w1[0:half])
    z2 = _mm(c, w1[half:2 * half])
    z2 = jnp.concatenate([z2[1:], z2[:1]], axis=0)
    pb = _mm(pe_ref[...], w1)[0:1] + b1_ref[...]
    hid = jax.nn.gelu(z1 + z2 + pb)
    out = _mm(hid, w2_ref[...]) + b2_ref[...]
    o_ref[0, 0] = jnp.concatenate([out, jnp.full(out.shape, pad_value, F32)], axis=1).astype(o_ref.dtype)


def _compress(kv, pe, w1, b1, w2, b2, bsz, s, pad_value):
    nch = s // D_CMP
    c = kv.reshape(bsz, nch, D_CMP, N_KV_GROUPS, HEAD_DIM).transpose(0, 3, 1, 2, 4).reshape(
        bsz, N_KV_GROUPS, nch, D_CMP * HEAD_DIM)
    pe8 = jnp.broadcast_to(pe.reshape(1, L_CMP * HEAD_DIM), (8, L_CMP * HEAD_DIM))
    return pl.pallas_call(
        functools.partial(_compress_kernel, pad_value=pad_value),
        grid=(bsz, N_KV_GROUPS),
        in_specs=[pl.BlockSpec((1, 1, nch, D_CMP * HEAD_DIM), lambda b, g: (b, g, 0, 0)),
                  _full(pe8.shape), _full(w1.shape), _full((1, CMP_HIDDEN)), _full(w2.shape), _full((1, HEAD_DIM))],
        out_specs=pl.BlockSpec((1, 1, nch, LANES), lambda b, g: (b, g, 0, 0)),
        out_shape=jax.ShapeDtypeStruct((bsz, N_KV_GROUPS, nch, LANES), BF16),
        compiler_params=_params(("parallel", "parallel")),
        name="compress",
    )(c, pe8, w1, b1.reshape(1, CMP_HIDDEN), w2, b2.reshape(1, HEAD_DIM))


def _nsa_kernel(q_ref, kc_ref, vc_ref, ksl_ref, vsl_ref, kwn_ref, vwn_ref, ng_ref, ov_ref, o_ref, *, n_top):
    n_cmp = kc_ref.shape[2]
    n_slc = ov_ref.shape[1]
    s0 = pl.program_id(1) * Q_BLOCK
    hq = HEADS_PER_GROUP * Q_BLOCK
    t_q = s0 + _iota((Q_BLOCK, 1), 0)
    sig = jax.nn.sigmoid(ng_ref[0])
    win_start = pl.multiple_of(jnp.maximum(s0 - WINDOW, 0), LANES)
    n_tiles = s0 // KEY_TILE + 1
    groups = range(N_KV_GROUPS)
    lanes = lambda g: slice(g * LANES, (g + 1) * LANES)
    qg = [jnp.concatenate([q_ref[0, :, lanes(g * HEADS_PER_GROUP + h)] for h in range(HEADS_PER_GROUP)], axis=0)
          for g in groups]

    def add_bias(s, bias):
        return (s.reshape(HEADS_PER_GROUP, Q_BLOCK, bias.shape[1]) + bias[None]).reshape(hq, bias.shape[1])

    cmp_bias = jnp.where((_iota((1, n_cmp), 1) * D_CMP + (L_CMP - 1)) <= t_q, 0.0, NEG_INF)
    has_cmp = jnp.concatenate([(t_q >= L_CMP - 1).astype(F32)] * HEADS_PER_GROUP, axis=0)
    pos_w = win_start + _iota((1, WINDOW + Q_BLOCK), 1)
    win_bias = jnp.where((pos_w <= t_q) & (pos_w > t_q - WINDOW), 0.0, NEG_INF)

    s_c = [add_bias(_mm_nt(qg[g], kc_ref[0, g]), cmp_bias) for g in groups]
    s_w = [add_bias(_mm_nt(qg[g], kwn_ref[0, pl.ds(win_start, WINDOW + Q_BLOCK), lanes(g)]), win_bias)
           for g in groups]
    e_c = [jnp.exp(x - jnp.max(x, axis=-1, keepdims=True)) for x in s_c]
    p_c = [x * (has_cmp / jnp.sum(x, axis=-1, keepdims=True)) for x in e_c]
    e_w = [jnp.exp(x - jnp.max(x, axis=-1, keepdims=True)) for x in s_w]
    o_c = [_mm(p_c[g], vc_ref[0, g]) for g in groups]
    acc_w = [_mm(e_w[g], vwn_ref[0, pl.ds(win_start, WINDOW + Q_BLOCK), lanes(g)]) for g in groups]
    o_win = [x * (1.0 / x[:, HEAD_DIM:HEAD_DIM + 1]) for x in acc_w]

    imp_t = []
    for g in groups:
        p_sum = p_c[g][0:Q_BLOCK]
        for h in range(1, HEADS_PER_GROUP):
            p_sum = p_sum + p_c[g][h * Q_BLOCK:(h + 1) * Q_BLOCK]
        imp_t.append(_mm_exact_rhs(p_sum, ov_ref[...]).T)
    j = _iota((n_slc, Q_BLOCK), 0)
    cur = (s0 + _iota((n_slc, Q_BLOCK), 1)) >> 6
    forced = (j == 0) | (j == cur) | (j == cur - 1)
    score = [jnp.where(forced, -3e38, jnp.where(j <= cur, x, -1.0)) for x in imp_t]
    sel_t = [forced.astype(F32) for _ in groups]
    for _ in range(n_top - 3):
        for g in groups:
            m = jnp.max(score[g], axis=0, keepdims=True)
            first = jnp.min(jnp.where(score[g] == m, j, n_slc), axis=0, keepdims=True)
            hit = j == first
            sel_t[g] = jnp.where(hit, 1.0, sel_t[g])
            score[g] = jnp.where(hit, -3e38, score[g])
    sel = [1.0 - x.T for x in sel_t]

    lane_q = _iota((Q_BLOCK, LANES), 1)
    in_code = (lane_q >= HEAD_DIM) & (lane_q < HEAD_DIM + BLOCKS_PER_TILE)

    def sel_step(kt, carry, diagonal):
        k0 = pl.multiple_of(kt * KEY_TILE, KEY_TILE)
        shift = (HEAD_DIM - BLOCKS_PER_TILE * kt) & (LANES - 1)
        s = []
        for g in groups:
            code = jnp.where(in_code, pltpu.roll(sel[g], shift, 1), 0.0).astype(BF16)
            qa = qg[g] + jnp.concatenate([code] * HEADS_PER_GROUP, axis=0)
            s.append(_mm_nt(qa, ksl_ref[0, pl.ds(k0, KEY_TILE), lanes(g)]))
        if diagonal:
            causal = jnp.where((k0 + _iota((Q_BLOCK, KEY_TILE), 1)) <= t_q, 0.0, NEG_INF)
            s = [add_bias(x, causal) for x in s]
        m_new = [jnp.maximum(carry[2 * g], jnp.max(s[g], axis=-1, keepdims=True)) for g in groups]
        p = [jnp.exp(s[g] - m_new[g]) for g in groups]
        pv = [_mm(p[g], vsl_ref[0, pl.ds(k0, KEY_TILE), lanes(g)]) for g in groups]
        out = []
        for g in groups:
            out += [m_new[g], jnp.exp(carry[2 * g] - m_new[g]) * carry[2 * g + 1] + pv[g]]
        return tuple(out)

    init = (jnp.full((hq, 1), NEG_INF, F32), jnp.zeros((hq, LANES), F32)) * N_KV_GROUPS
    fin = lax.fori_loop(0, n_tiles - 1, lambda kt, c: sel_step(kt, c, False), init)
    fin = sel_step(n_tiles - 1, fin, True)

    heads_out = []
    for g in groups:
        acc_s = fin[2 * g + 1]
        o_s = acc_s * (1.0 / acc_s[:, HEAD_DIM:HEAD_DIM + 1])
        o_w = o_win[g]
        for h in range(HEADS_PER_GROUP):
            hh = g * HEADS_PER_GROUP + h
            r = slice(h * Q_BLOCK, (h + 1) * Q_BLOCK)
            mix = (sig[:, 3 * hh:3 * hh + 1] * o_c[g][r] + sig[:, 3 * hh + 1:3 * hh + 2] * o_s[r]
                   + sig[:, 3 * hh + 2:3 * hh + 3] * o_w[r])
            heads_out.append(mix[:, 0:HEAD_DIM])
    o_ref[0] = jnp.concatenate(heads_out, axis=1).astype(o_ref.dtype)


def _nsa(q, kc, vc, ksl, vsl, kwn, vwn, ng, bsz, s):
    n_cmp = s // D_CMP
    n_slc = s // L_SLC
    ii = np.arange(n_cmp)[:, None]
    jj = np.arange(n_slc)[None, :]
    assert n_slc <= LANES and s % KEY_TILE == 0 and min(N_SELECT, n_slc) >= 3
    overlap = (ii * D_CMP < (jj + 1) * L_SLC) & (ii * D_CMP + L_CMP > jj * L_SLC)
    overlap = jnp.asarray(np.pad(overlap, ((0, 0), (0, LANES - n_slc))), BF16)
    seq = lambda wd: pl.BlockSpec((1, s, wd), lambda b, i: (b, 0, 0))
    blk = lambda wd: pl.BlockSpec((1, Q_BLOCK, wd), lambda b, i: (b, i, 0))
    cmp_spec = pl.BlockSpec((1, N_KV_GROUPS, n_cmp, LANES), lambda b, i: (b, 0, 0, 0))
    r3 = lambda a: a.reshape(bsz, s, a.shape[-1])
    return pl.pallas_call(
        functools.partial(_nsa_kernel, n_top=min(N_SELECT, n_slc)),
        grid=(bsz, s // Q_BLOCK),
        in_specs=[blk(Q_PAD), cmp_spec, cmp_spec, seq(KV_PAD), seq(KV_PAD), seq(KV_PAD), seq(KV_PAD),
                  blk(LANES), _full(overlap.shape)],
        out_specs=blk(WIDTH_A),
        out_shape=jax.ShapeDtypeStruct((bsz, s, WIDTH_A), BF16),
        compiler_params=_params(("parallel", "arbitrary")),
        name="nsa",
    )(r3(q), kc, vc, r3(ksl), r3(vsl), r3(kwn), r3(vwn), r3(ng), overlap)


def _rwkv_prep_kernel(p_ref, pv_ref, mu_ref, w0_ref, w2_ref, a0_ref, a2_ref, g2_ref, kk_ref, ka_ref, bd_ref,
                      r_ref, lw_ref, k_ref, v_ref, kn_ref, kb_ref, g_ref, *, tiles_per_seq):
    p = p_ref[...]
    first = (pl.program_id(0) % tiles_per_seq) == 0
    prev_last = jnp.where(first, 0.0, pv_ref[7:8, :])
    p_prev = jnp.concatenate([prev_last, p[:-1]], axis=0)
    xs = p + (p_prev - p) * mu_ref[...]
    wb = WIDTH_B
    r = xs[:, 0:wb]
    k = xs[:, wb:2 * wb]
    v = xs[:, 2 * wb:3 * wb]
    o = 3 * wb
    xw = xs[:, o:o + W_LORA]
    xa = xs[:, o + W_LORA:o + W_LORA + A_LORA]
    xg = xs[:, o + W_LORA + A_LORA:o + W_LORA + A_LORA + G_LORA]
    z = w0_ref[...] + _mm_hi(jnp.tanh(xw), w2_ref[...])
    w_log = -(jnp.maximum(-z, 0.0) + jnp.log(1.0 + jnp.exp(-jnp.abs(z)))) - 0.5
    a = jax.nn.sigmoid(a0_ref[...] + _mm_hi(xa, a2_ref[...]))
    kk = k * kk_ref[...]
    ss = _mm_hi(kk * kk, bd_ref[...])
    kn = kk * (1.0 / jnp.maximum(jnp.sqrt(ss), 1e-12))
    r_ref[...] = r
    lw_ref[...] = -jnp.exp(w_log)
    k_ref[...] = k * (1.0 + (a - 1.0) * ka_ref[...])
    v_ref[...] = v
    kn_ref[...] = kn
    kb_ref[...] = kn * a
    g_ref[...] = _mm(jax.nn.sigmoid(xg), g2_ref[...])


def _rwkv_prep(prw, mu, w0, w2, a0, a2, g2, k_k, k_a, s, tm=256):
    n = prw.shape[0]
    hid = np.arange(WIDTH_B) // HEAD_DIM_B
    bd = jnp.asarray(hid[:, None] == hid[None, :], F32)
    row = lambda wd: pl.BlockSpec((tm, wd), lambda i: (i, 0))
    vec = lambda a: a.reshape(1, -1)
    return pl.pallas_call(
        functools.partial(_rwkv_prep_kernel, tiles_per_seq=s // tm),
        grid=(n // tm,),
        in_specs=[row(C_RWKV), pl.BlockSpec((8, C_RWKV), lambda i: (jnp.maximum(i * (tm // 8) - 1, 0), 0)),
                  _full((1, C_RWKV)), _full((1, WIDTH_B)), _full(w2.shape), _full((1, WIDTH_B)), _full(a2.shape),
                  _full(g2.shape), _full((1, WIDTH_B)), _full((1, WIDTH_B)), _full(bd.shape)],
        out_specs=[row(WIDTH_B)] * 7,
        out_shape=[jax.ShapeDtypeStruct((n, WIDTH_B), F32)] * 7,
        compiler_params=_params(("parallel",)),
        name="rwkv_prep",
    )(prw, prw, vec(mu), vec(w0), w2, vec(a0), a2, g2, vec(k_k), vec(k_a), bd)


def _rwkv_chunk_kernel(r_ref, lw_ref, k_ref, v_ref, kn_ref, kb_ref, rk_ref,
                       bonus_ref, tm_ref, *, n_sub):
    L = CHUNK
    hd = HEAD_DIM_B
    ti = _iota((L, L), 0)
    si = _iota((L, L), 1)
    strict = si < ti
    incl = si <= ti
    same_sub = (ti // SUB) == (si // SUB)
    eye = (ti == si).astype(F32)
    tri = incl.astype(F32)
    at, bt, kt, rt, bl, kl, vv, rkr, p_tot = [], [], [], [], [], [], [], [], []
    for j in range(n_sub):
        rows = slice(j * L, (j + 1) * L)
        lw = lw_ref[0, rows, :]
        cs = _mm_hi(tri, lw)
        p_inv = jnp.exp(-cs)
        p_end = jnp.exp(cs[L - 1:L, :] - cs)
        r = r_ref[0, rows, :]
        k = k_ref[0, rows, :]
        kb = kb_ref[0, rows, :]
        at.append(-kn_ref[0, rows, :] * jnp.exp(cs - lw))
        bt.append(kb * p_inv)
        kt.append(k * p_inv)
        rt.append(r * jnp.exp(cs))
        bl.append(kb * p_end)
        kl.append(k * p_end)
        vv.append(v_ref[0, rows, :])
        rkr.append(r * k * rk_ref[...])
        p_tot.append(jnp.exp(cs[L - 1:L, :]))
    units = [(j, h) for j in range(n_sub) for h in range(N_HEADS_B)]
    sl = lambda arr, u: arr[u[0]][:, u[1] * hd:(u[1] + 1) * hd]
    aa = [_mm_nt(jnp.concatenate([sl(at, u), sl(rt, u)], axis=0),
                 jnp.concatenate([sl(bt, u), sl(kt, u)], axis=0)) for u in units]
    a_ab = [jnp.where(strict, x[0:L, 0:L], 0.0) for x in aa]
    a_low = [jnp.concatenate([jnp.where(strict, x[0:L, L:2 * L], 0.0),
                              jnp.where(incl, x[L:2 * L, L:2 * L], 0.0)], axis=0) for x in aa]
    a_rb = [jnp.where(incl, x[L:2 * L, 0:L], 0.0) for x in aa]
    av = [_mm(x, sl(vv, u)) for x, u in zip(a_low, units)]
    kv = [_mm_tn(sl(kl, u), sl(vv, u)) for u in units]
    pw = [jnp.where(same_sub, x, 0.0) for x in a_ab]
    t = [eye + x for x in pw]
    for _ in range(3):
        pw = [_mm(x, x) for x in pw]
        t = [x + _mm(x, y) for x, y in zip(t, pw)]
    width = SUB
    while width < L:
        m = ((ti // width) == (si // width) + 1) & ((ti // (2 * width)) == (si // (2 * width)))
        ot = [_mm(jnp.where(m, x, 0.0), y) for x, y in zip(a_ab, t)]
        t = [x + _mm(x, y) for x, y in zip(t, ot)]
        width *= 2
    wu = [_mm(x, jnp.concatenate([sl(at, u), y[0:L]], axis=1)) for x, y, u in zip(t, av, units)]
    qy = [jnp.concatenate([sl(rt, u), y[L:2 * L]], axis=1) + _mm(x, w)
          for x, y, w, u in zip(a_rb, av, wu, units)]
    gh = [_mm_tn(sl(bl, u), w) + jnp.concatenate([eye * sl(p_tot, u), y], axis=1)
          for w, y, u in zip(wu, kv, units)]
    bonus = [jnp.sum(sl(rkr, u), axis=-1, keepdims=True) * sl(vv, u) for u in units]
    for j in range(n_sub):
        bonus_ref[0, j * L:(j + 1) * L, :] = jnp.concatenate(bonus[j * N_HEADS_B:(j + 1) * N_HEADS_B], axis=1)
    for x, y, (j, h) in zip(qy, gh, units):
        tm_ref[0, j, h, 0:L, :] = x
        tm_ref[0, j, h, L:2 * L, :] = y


def _rwkv_chunk(r, lw, k, v, kn, kb, r_k, bsz, s, n_sub=2):
    nch = s // CHUNK
    blk = pl.BlockSpec((1, n_sub * CHUNK, WIDTH_B), lambda b, c: (b, c, 0))
    mat = pl.BlockSpec((1, n_sub, N_HEADS_B, 2 * CHUNK, 2 * HEAD_DIM_B), lambda b, c: (b, c, 0, 0, 0))
    r3 = lambda a: a.reshape(bsz, s, WIDTH_B)
    return pl.pallas_call(
        functools.partial(_rwkv_chunk_kernel, n_sub=n_sub),
        grid=(bsz, nch // n_sub),
        in_specs=[blk] * 6 + [_full((1, WIDTH_B))],
        out_specs=[blk, mat],
        out_shape=[jax.ShapeDtypeStruct((bsz, s, WIDTH_B), F32),
                   jax.ShapeDtypeStruct((bsz, nch, N_HEADS_B, 2 * CHUNK, 2 * HEAD_DIM_B), F32)],
        compiler_params=_params(("parallel", "parallel")),
        name="rwkv_chunk",
    )(r3(r), r3(lw), r3(k), r3(v), r3(kn), r3(kb), r_k.reshape(1, WIDTH_B))


def _rwkv_scan_kernel(tm_ref, yn_ref, st_ref):
    L = CHUNK
    hd = HEAD_DIM_B

    @pl.when(pl.program_id(0) == 0)
    def _():
        eye = (_iota((hd, hd), 0) == _iota((hd, hd), 1)).astype(F32)
        for i in range(st_ref.shape[0]):
            st_ref[i] = jnp.concatenate([jnp.zeros((hd, hd), F32), eye], axis=0)

    for b in range(tm_ref.shape[0]):
        res = [_mm_hi(tm_ref[b, 0, h], st_ref[b * N_HEADS_B + h]) for h in range(N_HEADS_B)]
        ys = []
        for h in range(N_HEADS_B):
            st_ref[b * N_HEADS_B + h, 0:hd, :] = res[h][L:2 * L]
            y = res[h][0:L]
            mean = jnp.mean(y, axis=-1, keepdims=True)
            d = y - mean
            var = jnp.mean(d * d, axis=-1, keepdims=True)
            ys.append(d * lax.rsqrt(var + GN_EPS))
        yn_ref[b] = jnp.concatenate(ys, axis=1)


def _rwkv_scan(tmat, bsz, s):
    nch = s // CHUNK
    return pl.pallas_call(
        _rwkv_scan_kernel,
        grid=(nch,),
        in_specs=[pl.BlockSpec((bsz, 1, N_HEADS_B, 2 * CHUNK, 2 * HEAD_DIM_B), lambda c: (0, c, 0, 0, 0))],
        out_specs=pl.BlockSpec((bsz, CHUNK, WIDTH_B), lambda c: (0, c, 0)),
        out_shape=jax.ShapeDtypeStruct((bsz, s, WIDTH_B), F32),
        scratch_shapes=[pltpu.VMEM((bsz * N_HEADS_B, 2 * HEAD_DIM_B, HEAD_DIM_B), F32)],
        compiler_params=_params(("arbitrary",)),
        name="rwkv_scan",
    )(tmat)


SUBLANES = 8
TOKEN_TILE = (SUBLANES, D_MODEL // SUBLANES)
assert TOKEN_TILE[1] == LANES


def _token_tile_spec(rows):
    return pl.BlockSpec((rows,) + TOKEN_TILE, lambda i, *_: (i, 0, 0))


def _store_token_tiles(ref, x):
    for c in range(SUBLANES):
        ref[:, c, :] = x[:, c * LANES:(c + 1) * LANES]


def _gather_buffer(rows):
    return pltpu.VMEM((2, rows // SUBLANES, SUBLANES, SUBLANES, LANES), F32)


def _gathered_rows(buf, slot, start, rows):
    t0, nt = start // SUBLANES, rows // SUBLANES
    return jnp.concatenate([buf[slot, t0:t0 + nt, c].reshape(rows, LANES) for c in range(SUBLANES)], axis=1)


def _merge_kernel(x_ref, oa_ref, yn_ref, bonus_ref, g_ref, ga_ref, gb_ref, lnw_ref, lnb_ref, wa_ref, wb_ref,
                  wo_ref, gf_ref, wr_ref, br_ref, x1_ref, h2_ref, route_ref, cnt_ref, run_ref):
    @pl.when(pl.program_id(0) == 0)
    def _():
        run_ref[...] = jnp.zeros_like(run_ref)

    ob = (yn_ref[...] * lnw_ref[...] + lnb_ref[...] + bonus_ref[...]) * g_ref[...]
    merged = (jax.nn.sigmoid(ga_ref[...]) * jnp.dot(oa_ref[...], wa_ref[...], preferred_element_type=F32)
              + jax.nn.sigmoid(gb_ref[...]) * _mm(ob, wb_ref[...]))
    x1 = x_ref[...] + _mm(merged, wo_ref[...])
    x1_ref[...] = x1
    h2 = x1 * lax.rsqrt(jnp.mean(x1 * x1, axis=-1, keepdims=True) + RMS_EPS) * gf_ref[...]
    _store_token_tiles(h2_ref, h2)
    logits = _mm_hi(h2, wr_ref[...]) + br_ref[...]
    lane = _iota(logits.shape, 1)
    big = 4 * LANES
    is_grp = lane < N_GROUPS
    lg = jnp.where(is_grp, logits, NEG_INF)
    mg = jnp.max(lg, axis=-1, keepdims=True)
    gidx = jnp.min(jnp.where(lg == mg, lane, big), axis=-1, keepdims=True)
    pg = 1.0 / jnp.sum(jnp.where(is_grp, jnp.exp(lg - mg), 0.0), axis=-1, keepdims=True)
    in_grp = (lane >= N_GROUPS) & (((lane - N_GROUPS) >> 3) == gidx) & (lane < N_GROUPS + N_EXPERTS)
    le = jnp.where(in_grp, logits, NEG_INF)
    m1 = jnp.max(le, axis=-1, keepdims=True)
    i1 = jnp.min(jnp.where(le == m1, lane, big), axis=-1, keepdims=True)
    le2 = jnp.where(lane == i1, NEG_INF, le)
    m2 = jnp.max(le2, axis=-1, keepdims=True)
    i2 = jnp.min(jnp.where(le2 == m2, lane, big), axis=-1, keepdims=True)
    e2 = jnp.exp(m2 - m1)
    inv = pg / (1.0 + e2)
    tm = logits.shape[0]
    hot1 = (lane == i1 - N_GROUPS).astype(F32)
    hot2 = (lane == i2 - N_GROUPS).astype(F32)
    both = hot1 + hot2
    earlier = (_iota((tm, tm), 1) < _iota((tm, tm), 0)).astype(BF16)
    before = jnp.dot(earlier, both.astype(BF16), preferred_element_type=F32) + run_ref[...]
    rank1 = jnp.sum(hot1 * before, axis=-1, keepdims=True)
    rank2 = jnp.sum(hot2 * before, axis=-1, keepdims=True)
    run_ref[...] = run_ref[...] + jnp.sum(both, axis=0, keepdims=True)
    cnt_ref[...] = run_ref[...]
    cols = ((i1 - N_GROUPS).astype(F32), (i2 - N_GROUPS).astype(F32), inv, inv * e2, rank1, rank2)
    route = jnp.zeros_like(logits)
    for c, v in enumerate(cols):
        route = jnp.where(lane == c, v, route)
    route_ref[...] = route


def _merge(x2, oa, yn, bonus, g, ga, gb, ln_w, ln_b, w_a, w_b, w_o, g_ffn, w_grp, b_grp, w_exp, b_exp, tm=256):
    n = x2.shape[0]
    w_r = jnp.pad(jnp.concatenate([w_grp, w_exp], axis=1), ((0, 0), (0, LANES - N_GROUPS - N_EXPERTS)))
    b_r = jnp.pad(jnp.concatenate([b_grp, b_exp]), (0, LANES - N_GROUPS - N_EXPERTS)).reshape(1, LANES)
    row = lambda wd: pl.BlockSpec((tm, wd), lambda i: (i, 0))
    vec = lambda a: a.reshape(1, -1)
    wa, wb, wo = w_a.astype(BF16), w_b.astype(BF16), w_o.astype(BF16)
    return pl.pallas_call(
        _merge_kernel,
        grid=(n // tm,),
        in_specs=[row(D_MODEL), row(WIDTH_A), row(WIDTH_B), row(WIDTH_B), row(WIDTH_B), row(D_MODEL), row(D_MODEL),
                  _full((1, WIDTH_B)), _full((1, WIDTH_B)), _full(wa.shape), _full(wb.shape), _full(wo.shape),
                  _full((1, D_MODEL)), _full(w_r.shape), _full((1, LANES))],
        out_specs=[row(D_MODEL), _token_tile_spec(tm), row(LANES), _full((1, LANES))],
        out_shape=[jax.ShapeDtypeStruct((n, D_MODEL), F32), jax.ShapeDtypeStruct((n,) + TOKEN_TILE, F32),
                   jax.ShapeDtypeStruct((n, LANES), F32), jax.ShapeDtypeStruct((1, LANES), F32)],
        scratch_shapes=[pltpu.VMEM((1, LANES), F32)],
        compiler_params=_params(("arbitrary",)),
        name="merge",
    )(x2, oa, yn, bonus, g, ga, gb, vec(ln_w), vec(ln_b), wa, wb, wo, vec(g_ffn), w_r, b_r)


def _gather_start(idx_ref, src_hbm, buf, sem, slot, rows):
    def body(t, carry):
        for u in range(SUBLANES):
            pltpu.make_async_copy(src_hbm.at[pl.ds(idx_ref[0, 0, t * SUBLANES + u], 1)],
                                  buf.at[slot, pl.ds(t, 1), :, u], sem.at[slot]).start(priority=u % 2)
        return carry
    lax.fori_loop(0, rows // SUBLANES, body, 0)


def _gather_wait(buf, sem, slot):
    pltpu.make_async_copy(buf.at[slot], buf.at[slot], sem.at[slot]).wait()


def _gather_pipeline(idx_ref, idx_next_ref, src_hbm, buf, sem, rows):
    i = pl.program_id(0)
    slot = i % 2

    @pl.when(i == 0)
    def _():
        _gather_start(idx_ref, src_hbm, buf, sem, 0, rows)

    @pl.when(i + 1 < pl.num_programs(0))
    def _():
        _gather_start(idx_next_ref, src_hbm, buf, sem, 1 - slot, rows)

    _gather_wait(buf, sem, slot)
    return slot


def _dispatch_kernel(pos_ref, h2_ref, zero_hbm, x_hbm, sem, *, tm):
    del zero_hbm

    def body(t, carry):
        for u in range(SUBLANES):
            r = t * SUBLANES + u
            for k in range(TOP_K):
                p = pos_ref[0, 0, k * tm + r]
                pltpu.make_async_copy(h2_ref.at[pl.ds(r, 1)], x_hbm.at[pl.ds(p >> 3, 1), :, p & (SUBLANES - 1)],
                                      sem.at[k]).start(priority=(u + k) % 2)
        return carry
    lax.fori_loop(0, tm // SUBLANES, body, 0)
    for k in range(TOP_K):
        pltpu.make_async_copy(h2_ref, h2_ref, sem.at[k]).wait()


def _dispatch(h2, pos, n_pad, tm=256):
    n = h2.shape[0]
    nt = n // tm
    pos3 = pos.reshape(nt, tm, TOP_K).transpose(0, 2, 1).reshape(nt, 1, TOP_K * tm)
    tiles = (n_pad // SUBLANES, SUBLANES, SUBLANES, LANES)
    any_spec = pl.BlockSpec(memory_space=pl.ANY)
    return pl.pallas_call(
        functools.partial(_dispatch_kernel, tm=tm),
        grid=(nt,),
        in_specs=[pl.BlockSpec((1, 1, TOP_K * tm), lambda i: (i, 0, 0), memory_space=pltpu.SMEM),
                  _token_tile_spec(tm), any_spec],
        out_specs=any_spec,
        out_shape=jax.ShapeDtypeStruct(tiles, F32),
        scratch_shapes=[pltpu.SemaphoreType.DMA((TOP_K,))],
        input_output_aliases={2: 0},
        compiler_params=_params(("arbitrary",)),
        name="dispatch",
    )(pos3, h2, jnp.zeros(tiles, F32))


def _moe_kernel(be_ref, x_ref, w13_ref, w2_ref, y_ref):
    xb = jnp.concatenate([x_ref[:, c].reshape(MOE_BLOCK, LANES) for c in range(SUBLANES)], axis=1)
    a = _mm(xb, w13_ref[0])
    hid = jax.nn.silu(a[:, 0:D_EXPERT]) * a[:, D_EXPERT:2 * D_EXPERT]
    _store_token_tiles(y_ref, _mm(hid, w2_ref[0]))


def _moe(x_pad, blk_expert, w13, w2):
    n_blk = blk_expert.shape[0]
    grid_spec = pltpu.PrefetchScalarGridSpec(
        num_scalar_prefetch=1,
        grid=(n_blk,),
        in_specs=[pl.BlockSpec((MOE_BLOCK // SUBLANES, SUBLANES, SUBLANES, LANES), lambda i, be: (i, 0, 0, 0)),
                  pl.BlockSpec((1, D_MODEL, 2 * D_EXPERT), lambda i, be: (be[i], 0, 0)),
                  pl.BlockSpec((1, D_EXPERT, D_MODEL), lambda i, be: (be[i], 0, 0))],
        out_specs=_token_tile_spec(MOE_BLOCK),
    )
    return pl.pallas_call(
        _moe_kernel,
        grid_spec=grid_spec,
        out_shape=jax.ShapeDtypeStruct((n_blk * MOE_BLOCK,) + TOKEN_TILE, F32),
        compiler_params=_params(("parallel",)),
        name="moe",
    )(blk_expert, x_pad, w13, w2)


def _final_kernel(pos_ref, pos_next_ref, x1_ref, route_ref, g_ref, y_hbm, o_ref, buf, sem, *, tm):
    slot = _gather_pipeline(pos_ref, pos_next_ref, y_hbm, buf, sem, TOP_K * tm)
    x = x1_ref[...]
    for k in range(TOP_K):
        x = x + route_ref[:, TOP_K + k:TOP_K + k + 1] * _gathered_rows(buf, slot, k * tm, tm)
    o_ref[...] = x * lax.rsqrt(jnp.mean(x * x, axis=-1, keepdims=True) + RMS_EPS) * g_ref[...]


def _final(x1, route, y_pad, pos, g_final, tm=256):
    n = x1.shape[0]
    nt = n // tm
    pos3 = pos.reshape(nt, tm, TOP_K).transpose(0, 2, 1).reshape(nt, 1, TOP_K * tm)
    smem_blk = lambda f: pl.BlockSpec((1, 1, TOP_K * tm), f, memory_space=pltpu.SMEM)
    row = lambda wd: pl.BlockSpec((tm, wd), lambda i: (i, 0))
    return pl.pallas_call(
        functools.partial(_final_kernel, tm=tm),
        grid=(nt,),
        in_specs=[smem_blk(lambda i: (i, 0, 0)), smem_blk(lambda i: (jnp.minimum(i + 1, nt - 1), 0, 0)),
                  row(D_MODEL), row(LANES), _full((1, D_MODEL)), pl.BlockSpec(memory_space=pl.ANY)],
        out_specs=row(D_MODEL),
        out_shape=jax.ShapeDtypeStruct((n, D_MODEL), F32),
        scratch_shapes=[_gather_buffer(TOP_K * tm), pltpu.SemaphoreType.DMA((2,))],
        compiler_params=_params(("arbitrary",)),
        name="final",
    )(pos3, pos3, x1, route, g_final.reshape(1, D_MODEL), y_pad)


def _dispatch_plan(route, counts, n_tok):
    expert = route[:, 0:TOP_K].astype(I32)
    rank = route[:, 2 * TOP_K:3 * TOP_K].astype(I32)
    cnt = counts[0, 0:N_EXPERTS].astype(I32)
    padded = (cnt + MOE_BLOCK - 1) // MOE_BLOCK * MOE_BLOCK
    ids = jnp.arange(N_EXPERTS, dtype=I32)
    pad_end = jnp.sum(jnp.where(ids[None, :] <= ids[:, None], padded[None, :], 0), axis=1)
    pad_start = pad_end - padded
    pos = jnp.sum(jnp.where(expert[..., None] == ids, pad_start, 0), axis=-1) + rank
    n_pad = n_tok * TOP_K + N_EXPERTS * MOE_BLOCK
    n_blk = n_pad // MOE_BLOCK
    blk_start = jnp.arange(n_blk, dtype=I32) * MOE_BLOCK
    blk_expert = jnp.minimum(jnp.sum((pad_end[None, :] <= blk_start[:, None]).astype(I32), axis=1), N_EXPERTS - 1)
    return pos, blk_expert, n_pad


def kernel(x, positions, g_mix, w_in, cmp_pe_k, cmp_w1_k, cmp_b1_k, cmp_w2_k, cmp_b2_k, cmp_pe_v, cmp_w1_v, cmp_b1_v, cmp_w2_v, cmp_b2_v, rwkv_mu, rwkv_w0, rwkv_w2, rwkv_a0, rwkv_a2, rwkv_g2, rwkv_k_k, rwkv_k_a, rwkv_r_k, rwkv_ln_w, rwkv_ln_b, w_proj_a, w_proj_b, w_out, g_ffn, w_grp, b_grp, w_exp, b_exp, e_w1, e_w3, e_w2, g_final):
    bsz, s, _ = x.shape
    assert w_in.shape[0] == 1, "single-layer stack"
    n_tok = bsz * s
    x2 = x.reshape(n_tok, D_MODEL)
    half = HEAD_DIM // 2
    freqs = jnp.power(ROPE_THETA, -jnp.arange(half, dtype=F32) / half)
    ang = positions.astype(F32).reshape(n_tok, 1) * freqs
    cs = jnp.tile(jnp.cos(ang), (1, 4))
    sn = jnp.tile(jnp.concatenate([-jnp.sin(ang), jnp.sin(ang)], axis=1), (1, 2))
    q, kc, ksl, kwn, vc, vsl, vwn, ng, prw, ga, gb = _inproj(x2, g_mix[0], cs, sn, w_in[0])
    k_c = _compress(kc, cmp_pe_k[0], cmp_w1_k[0], cmp_b1_k[0], cmp_w2_k[0], cmp_b2_k[0], bsz, s, 0.0)
    v_c = _compress(vc, cmp_pe_v[0], cmp_w1_v[0], cmp_b1_v[0], cmp_w2_v[0], cmp_b2_v[0], bsz, s, 1.0)
    o_a = _nsa(q, k_c, v_c, ksl, vsl, kwn, vwn, ng, bsz, s).reshape(n_tok, WIDTH_A)
    r, lw, k, v, kn, kb, g = _rwkv_prep(prw, rwkv_mu[0], rwkv_w0[0], rwkv_w2[0], rwkv_a0[0], rwkv_a2[0],
                                        rwkv_g2[0], rwkv_k_k[0], rwkv_k_a[0], s)
    bonus, tmat = _rwkv_chunk(r, lw, k, v, kn, kb, rwkv_r_k[0], bsz, s)
    yn = _rwkv_scan(tmat, bsz, s)
    x1, h2, route, counts = _merge(x2, o_a, yn.reshape(n_tok, WIDTH_B), bonus.reshape(n_tok, WIDTH_B), g, ga, gb,
                           rwkv_ln_w[0], rwkv_ln_b[0], w_proj_a[0], w_proj_b[0], w_out[0], g_ffn[0],
                           w_grp[0], b_grp[0], w_exp[0], b_exp[0])
    pos, blk_expert, n_pad = _dispatch_plan(route, counts, n_tok)
    w13 = jnp.concatenate([e_w1[0], e_w3[0]], axis=-1).astype(BF16)
    y_pad = _moe(_dispatch(h2, pos, n_pad), blk_expert, w13, e_w2[0].astype(BF16))
    out = _final(x1, route, y_pad, pos, g_final)
    return out.reshape(bsz, s, D_MODEL)
```

```python
import functools

import numpy as np
import jax
import jax.numpy as jnp
from jax import lax
from jax.experimental import pallas as pl
from jax.experimental.pallas import tpu as pltpu

F32 = jnp.float32
BF16 = jnp.bfloat16
I32 = jnp.int32

D_MODEL = 1024
N_HEADS_A = 8
N_KV_GROUPS = 2
HEADS_PER_GROUP = N_HEADS_A // N_KV_GROUPS
HEAD_DIM = 64
L_CMP = 32
D_CMP = 16
CMP_HIDDEN = 256
L_SLC = 64
N_SELECT = 16
WINDOW = 512
Q_BLOCK = 128
ROPE_THETA = 10000.0
FORCE_SCORE = 1e4
NEG_INF = -1e30
N_HEADS_B = 8
HEAD_DIM_B = 64
W_LORA = 64
A_LORA = 64
G_LORA = 128
GN_EPS = 64e-5
N_GROUPS = 4
EXPERTS_PER_GROUP = 8
N_EXPERTS = N_GROUPS * EXPERTS_PER_GROUP
TOP_K = 2
D_EXPERT = 256
MOE_BLOCK = 256
RMS_EPS = 1e-6
WIDTH_A = N_HEADS_A * HEAD_DIM
KV_WIDTH = N_KV_GROUPS * HEAD_DIM
WIDTH_B = N_HEADS_B * HEAD_DIM_B
C_RWKV = 3 * WIDTH_B + W_LORA + A_LORA + G_LORA

LANES = 128
CHUNK = 64
SUB = 16
KEY_TILE = 1024
BLOCKS_PER_TILE = KEY_TILE // L_SLC
VMEM_LIMIT = 56 * 1024 * 1024

HI = lax.Precision.HIGHEST


def _mm(a, b):
    return jnp.dot(a.astype(BF16), b.astype(BF16), preferred_element_type=F32)


def _mm_nt(a, b):
    return lax.dot_general(a.astype(BF16), b.astype(BF16), (((1,), (1,)), ((), ())), preferred_element_type=F32)


def _mm_tn(a, b):
    return lax.dot_general(a.astype(BF16), b.astype(BF16), (((0,), (0,)), ((), ())), preferred_element_type=F32)


def _mm_hi(a, b):
    return jnp.dot(a, b, preferred_element_type=F32, precision=HI)


def _bf16_pieces(a):
    a1 = a.astype(BF16)
    r1 = a - a1.astype(F32)
    a2 = r1.astype(BF16)
    return a1, a2, (r1 - a2.astype(F32)).astype(BF16)


def _mm_exact_rhs(a, b):
    bb = b.astype(BF16)
    return sum(jnp.dot(x, bb, preferred_element_type=F32) for x in _bf16_pieces(a))


def _mm_exact_lhs(a, b):
    aa = a.astype(BF16)
    return sum(jnp.dot(aa, x, preferred_element_type=F32) for x in _bf16_pieces(b))


def _iota(shape, dim):
    return lax.broadcasted_iota(I32, shape, dim)


def _params(sem):
    return pltpu.CompilerParams(dimension_semantics=sem, vmem_limit_bytes=VMEM_LIMIT)


def _full(shape):
    nd = len(shape)
    return pl.BlockSpec(shape, lambda *_: (0,) * nd)


Q_PAD = N_HEADS_A * LANES
KV_PAD = N_KV_GROUPS * LANES


def _inproj_kernel(x_ref, g_ref, cs_ref, sn_ref, wr_ref, wv_ref, wn_ref, ww_ref, wg_ref,
                   q_ref, kc_ref, ksl_ref, kwn_ref, vc_ref, vsl_ref, vwn_ref, ng_ref, prw_ref, ga_ref, gb_ref):
    x = x_ref[...]
    h = (x * lax.rsqrt(jnp.mean(x * x, axis=-1, keepdims=True) + RMS_EPS) * g_ref[...]).astype(BF16)
    pr = jnp.dot(h, wr_ref[...], preferred_element_type=F32)
    cs = cs_ref[...]
    sn = sn_ref[...]
    first_half = (_iota((1, LANES), 1) & (HEAD_DIM // 2)) == 0
    chunks = []
    for c in range(pr.shape[1] // LANES):
        x = pr[:, c * LANES:(c + 1) * LANES]
        swapped = jnp.where(first_half, pltpu.roll(x, LANES - HEAD_DIM // 2, 1), pltpu.roll(x, HEAD_DIM // 2, 1))
        chunks.append(x * cs + swapped * sn)
    ro = jnp.concatenate(chunks, axis=1)
    q_ref[...] = (ro[:, 0:Q_PAD] * (HEAD_DIM ** -0.5)).astype(BF16)
    o = Q_PAD
    kc_ref[...] = ro[:, o:o + KV_WIDTH]
    tm = x_ref.shape[0]
    blk = ((pl.program_id(0) * tm + _iota((tm, 1), 0)) >> 6) & (BLOCKS_PER_TILE - 1)
    code = jnp.where((_iota((1, KV_PAD), 1) & (LANES - 1)) == HEAD_DIM + blk, NEG_INF, 0.0)
    ksl_ref[...] = (ro[:, o + KV_WIDTH:o + KV_WIDTH + KV_PAD] + code).astype(BF16)
    kwn_ref[...] = ro[:, o + KV_WIDTH + KV_PAD:o + KV_WIDTH + 2 * KV_PAD].astype(BF16)
    v = jnp.dot(h, wv_ref[...], preferred_element_type=F32)
    vc_ref[...] = v[:, 0:KV_WIDTH]
    ones = ((_iota((1, 2 * KV_PAD), 1) & (LANES - 1)) >= HEAD_DIM).astype(F32)
    vp = v[:, KV_WIDTH:KV_WIDTH + 2 * KV_PAD] + ones
    vsl_ref[...] = vp[:, 0:KV_PAD].astype(BF16)
    vwn_ref[...] = vp[:, KV_PAD:2 * KV_PAD].astype(BF16)
    ng_ref[...] = jnp.dot(h, wn_ref[...], preferred_element_type=F32)
    prw_ref[...] = jnp.dot(h, ww_ref[...], preferred_element_type=F32)
    gg = jnp.dot(h, wg_ref[...], preferred_element_type=F32)
    ga_ref[...] = gg[:, 0:D_MODEL]
    gb_ref[...] = gg[:, D_MODEL:2 * D_MODEL]


def _pad_heads(w):
    d, wd = w.shape
    return jnp.pad(w.reshape(d, wd // HEAD_DIM, HEAD_DIM), ((0, 0), (0, 0), (0, LANES - HEAD_DIM))).reshape(d, -1)


def _inproj(x2, g_mix, cs, sn, w_in, tm=256):
    n = x2.shape[0]
    o = 0
    cols = {}
    for name, wd in (("q", WIDTH_A), ("kc", KV_WIDTH), ("vc", KV_WIDTH), ("ksl", KV_WIDTH), ("vsl", KV_WIDTH),
                     ("kwn", KV_WIDTH), ("vwn", KV_WIDTH), ("ng", 3 * N_HEADS_A), ("rw", C_RWKV),
                     ("ga", D_MODEL), ("gb", D_MODEL)):
        cols[name] = w_in[:, o:o + wd]
        o += wd
    w_rope = jnp.concatenate(
        [_pad_heads(cols["q"]), cols["kc"], _pad_heads(cols["ksl"]), _pad_heads(cols["kwn"])], axis=1)
    w_v = jnp.concatenate([cols["vc"], _pad_heads(cols["vsl"]), _pad_heads(cols["vwn"])], axis=1)
    w_ng = jnp.pad(cols["ng"], ((0, 0), (0, LANES - 3 * N_HEADS_A)))
    w_gate = jnp.concatenate([cols["ga"], cols["gb"]], axis=1)
    ws = [w.astype(BF16) for w in (w_rope, w_v, w_ng, cols["rw"], w_gate)]
    row = lambda wd: pl.BlockSpec((tm, wd), lambda i: (i, 0))
    outs = [(Q_PAD, BF16), (KV_WIDTH, F32), (KV_PAD, BF16), (KV_PAD, BF16), (KV_WIDTH, F32), (KV_PAD, BF16),
            (KV_PAD, BF16), (LANES, F32), (C_RWKV, F32), (D_MODEL, F32), (D_MODEL, F32)]
    return pl.pallas_call(
        _inproj_kernel,
        grid=(n // tm,),
        in_specs=[row(D_MODEL), _full((1, D_MODEL)), row(LANES), row(LANES)] + [_full(w.shape) for w in ws],
        out_specs=[row(wd) for wd, _ in outs],
        out_shape=[jax.ShapeDtypeStruct((n, wd), dt) for wd, dt in outs],
        compiler_params=_params(("parallel",)),
        name="inproj",
    )(x2, g_mix.reshape(1, D_MODEL), cs, sn, *ws)


def _compress_kernel(c_ref, pe_ref, w1_ref, b1_ref, w2_ref, b2_ref, o_ref, *, pad_value):
    half = D_CMP * HEAD_DIM
    c = c_ref[0, 0]
    w1 = w1_ref[...]
    z1 = _mm(c, w1[0:half])
    z2 = _mm(c, w1[half:2 * half])
    z2 = jnp.concatenate([z2[1:], z2[:1]], axis=0)
    pb = _mm(pe_ref[...], w1)[0:1] + b1_ref[...]
    hid = jax.nn.gelu(z1 + z2 + pb)
    out = _mm(hid, w2_ref[...]) + b2_ref[...]
    o_ref[0, 0] = jnp.concatenate([out, jnp.full(out.shape, pad_value, F32)], axis=1).astype(o_ref.dtype)


def _compress(kv, pe, w1, b1, w2, b2, bsz, s, pad_value):
    nch = s // D_CMP
    c = kv.reshape(bsz, nch, D_CMP, N_KV_GROUPS, HEAD_DIM).transpose(0, 3, 1, 2, 4).reshape(
        bsz, N_KV_GROUPS, nch, D_CMP * HEAD_DIM)
    pe8 = jnp.broadcast_to(pe.reshape(1, L_CMP * HEAD_DIM), (8, L_CMP * HEAD_DIM))
    return pl.pallas_call(
        functools.partial(_compress_kernel, pad_value=pad_value),
        grid=(bsz, N_KV_GROUPS),
        in_specs=[pl.BlockSpec((1, 1, nch, D_CMP * HEAD_DIM), lambda b, g: (b, g, 0, 0)),
                  _full(pe8.shape), _full(w1.shape), _full((1, CMP_HIDDEN)), _full(w2.shape), _full((1, HEAD_DIM))],
        out_specs=pl.BlockSpec((1, 1, nch, LANES), lambda b, g: (b, g, 0, 0)),
        out_shape=jax.ShapeDtypeStruct((bsz, N_KV_GROUPS, nch, LANES), BF16),
        compiler_params=_params(("parallel", "parallel")),
        name="compress",
    )(c, pe8, w1, b1.reshape(1, CMP_HIDDEN), w2, b2.reshape(1, HEAD_DIM))


def _nsa_kernel(q_ref, kc_ref, vc_ref, ksl_ref, vsl_ref, kwn_ref, vwn_ref, ng_ref, ov_ref, o_ref, *, n_top):
    n_cmp = kc_ref.shape[2]
    n_slc = ov_ref.shape[1]
    s0 = pl.program_id(1) * Q_BLOCK
    hq = HEADS_PER_GROUP * Q_BLOCK
    t_q = s0 + _iota((Q_BLOCK, 1), 0)
    sig = jax.nn.sigmoid(ng_ref[0])
    win_start = pl.multiple_of(jnp.maximum(s0 - WINDOW, 0), LANES)
    n_tiles = s0 // KEY_TILE + 1
    groups = range(N_KV_GROUPS)
    lanes = lambda g: slice(g * LANES, (g + 1) * LANES)
    qg = [jnp.concatenate([q_ref[0, :, lanes(g * HEADS_PER_GROUP + h)] for h in range(HEADS_PER_GROUP)], axis=0)
          for g in groups]

    def add_bias(s, bias):
        return (s.reshape(HEADS_PER_GROUP, Q_BLOCK, bias.shape[1]) + bias[None]).reshape(hq, bias.shape[1])

    cmp_bias = jnp.where((_iota((1, n_cmp), 1) * D_CMP + (L_CMP - 1)) <= t_q, 0.0, NEG_INF)
    has_cmp = jnp.concatenate([(t_q >= L_CMP - 1).astype(F32)] * HEADS_PER_GROUP, axis=0)
    pos_w = win_start + _iota((1, WINDOW + Q_BLOCK), 1)
    win_bias = jnp.where((pos_w <= t_q) & (pos_w > t_q - WINDOW), 0.0, NEG_INF)

    s_c = [add_bias(_mm_nt(qg[g], kc_ref[0, g]), cmp_bias) for g in groups]
    s_w = [add_bias(_mm_nt(qg[g], kwn_ref[0, pl.ds(win_start, WINDOW + Q_BLOCK), lanes(g)]), win_bias)
           for g in groups]
    e_c = [jnp.exp(x - jnp.max(x, axis=-1, keepdims=True)) for x in s_c]
    p_c = [x * (has_cmp / jnp.sum(x, axis=-1, keepdims=True)) for x in e_c]
    e_w = [jnp.exp(x - jnp.max(x, axis=-1, keepdims=True)) for x in s_w]
    o_c = [_mm(p_c[g], vc_ref[0, g]) for g in groups]
    acc_w = [_mm(e_w[g], vwn_ref[0, pl.ds(win_start, WINDOW + Q_BLOCK), lanes(g)]) for g in groups]
    o_win = [x * (1.0 / x[:, HEAD_DIM:HEAD_DIM + 1]) for x in acc_w]

    imp_t = []
    for g in groups:
        p_sum = p_c[g][0:Q_BLOCK]
        for h in range(1, HEADS_PER_GROUP):
            p_sum = p_sum + p_c[g][h * Q_BLOCK:(h + 1) * Q_BLOCK]
        imp_t.append(_mm_exact_rhs(p_sum, ov_ref[...]).T)
    j = _iota((n_slc, Q_BLOCK), 0)
    cur = (s0 + _iota((n_slc, Q_BLOCK), 1)) >> 6
    forced = (j == 0) | (j == cur) | (j == cur - 1)
    score = [jnp.where(forced, -3e38, jnp.where(j <= cur, x, -1.0)) for x in imp_t]
    sel_t = [forced.astype(F32) for _ in groups]
    for _ in range(n_top - 3):
        for g in groups:
            m = jnp.max(score[g], axis=0, keepdims=True)
            first = jnp.min(jnp.where(score[g] == m, j, n_slc), axis=0, keepdims=True)
            hit = j == first
            sel_t[g] = jnp.where(hit, 1.0, sel_t[g])
            score[g] = jnp.where(hit, -3e38, score[g])
    sel = [1.0 - x.T for x in sel_t]

    lane_q = _iota((Q_BLOCK, LANES), 1)
    in_code = (lane_q >= HEAD_DIM) & (lane_q < HEAD_DIM + BLOCKS_PER_TILE)

    def sel_step(kt, carry, diagonal):
        k0 = pl.multiple_of(kt * KEY_TILE, KEY_TILE)
        shift = (HEAD_DIM - BLOCKS_PER_TILE * kt) & (LANES - 1)
        s = []
        for g in groups:
            code = jnp.where(in_code, pltpu.roll(sel[g], shift, 1), 0.0).astype(BF16)
            qa = qg[g] + jnp.concatenate([code] * HEADS_PER_GROUP, axis=0)
            s.append(_mm_nt(qa, ksl_ref[0, pl.ds(k0, KEY_TILE), lanes(g)]))
        if diagonal:
            causal = jnp.where((k0 + _iota((Q_BLOCK, KEY_TILE), 1)) <= t_q, 0.0, NEG_INF)
            s = [add_bias(x, causal) for x in s]
        m_new = [jnp.maximum(carry[2 * g], jnp.max(s[g], axis=-1, keepdims=True)) for g in groups]
        p = [jnp.exp(s[g] - m_new[g]) for g in groups]
        pv = [_mm(p[g], vsl_ref[0, pl.ds(k0, KEY_TILE), lanes(g)]) for g in groups]
        out = []
        for g in groups:
            out += [m_new[g], jnp.exp(carry[2 * g] - m_new[g]) * carry[2 * g + 1] + pv[g]]
        return tuple(out)

    init = (jnp.full((hq, 1), NEG_INF, F32), jnp.zeros((hq, LANES), F32)) * N_KV_GROUPS
    fin = lax.fori_loop(0, n_tiles - 1, lambda kt, c: sel_step(kt, c, False), init)
    fin = sel_step(n_tiles - 1, fin, True)

    heads_out = []
    for g in groups:
        acc_s = fin[2 * g + 1]
        o_s = acc_s * (1.0 / acc_s[:, HEAD_DIM:HEAD_DIM + 1])
        o_w = o_win[g]
        for h in range(HEADS_PER_GROUP):
            hh = g * HEADS_PER_GROUP + h
            r = slice(h * Q_BLOCK, (h + 1) * Q_BLOCK)
            mix = (sig[:, 3 * hh:3 * hh + 1] * o_c[g][r] + sig[:, 3 * hh + 1:3 * hh + 2] * o_s[r]
                   + sig[:, 3 * hh + 2:3 * hh + 3] * o_w[r])
            heads_out.append(mix[:, 0:HEAD_DIM])
    o_ref[0] = jnp.concatenate(heads_out, axis=1).astype(o_ref.dtype)


def _nsa(q, kc, vc, ksl, vsl, kwn, vwn, ng, bsz, s):
    n_cmp = s // D_CMP
    n_slc = s // L_SLC
    ii = np.arange(n_cmp)[:, None]
    jj = np.arange(n_slc)[None, :]
    assert n_slc <= LANES and s % KEY_TILE == 0 and min(N_SELECT, n_slc) >= 3
    overlap = (ii * D_CMP < (jj + 1) * L_SLC) & (ii * D_CMP + L_CMP > jj * L_SLC)
    overlap = jnp.asarray(np.pad(overlap, ((0, 0), (0, LANES - n_slc))), BF16)
    seq = lambda wd: pl.BlockSpec((1, s, wd), lambda b, i: (b, 0, 0))
    blk = lambda wd: pl.BlockSpec((1, Q_BLOCK, wd), lambda b, i: (b, i, 0))
    cmp_spec = pl.BlockSpec((1, N_KV_GROUPS, n_cmp, LANES), lambda b, i: (b, 0, 0, 0))
    r3 = lambda a: a.reshape(bsz, s, a.shape[-1])
    return pl.pallas_call(
        functools.partial(_nsa_kernel, n_top=min(N_SELECT, n_slc)),
        grid=(bsz, s // Q_BLOCK),
        in_specs=[blk(Q_PAD), cmp_spec, cmp_spec, seq(KV_PAD), seq(KV_PAD), seq(KV_PAD), seq(KV_PAD),
                  blk(LANES), _full(overlap.shape)],
        out_specs=blk(WIDTH_A),
        out_shape=jax.ShapeDtypeStruct((bsz, s, WIDTH_A), BF16),
        compiler_params=_params(("parallel", "arbitrary")),
        name="nsa",
    )(r3(q), kc, vc, r3(ksl), r3(vsl), r3(kwn), r3(vwn), r3(ng), overlap)


def _rwkv_prep_kernel(p_ref, pv_ref, mu_ref, w0_ref, w2_ref, a0_ref, a2_ref, g2_ref, kk_ref, ka_ref, bd_ref,
                      r_ref, lw_ref, k_ref, v_ref, kn_ref, kb_ref, g_ref, *, tiles_per_seq):
    p = p_ref[...]
    first = (pl.program_id(0) % tiles_per_seq) == 0
    prev_last = jnp.where(first, 0.0, pv_ref[7:8, :])
    p_prev = jnp.concatenate([prev_last, p[:-1]], axis=0)
    xs = p + (p_prev - p) * mu_ref[...]
    wb = WIDTH_B
    r = xs[:, 0:wb]
    k = xs[:, wb:2 * wb]
    v = xs[:, 2 * wb:3 * wb]
    o = 3 * wb
    xw = xs[:, o:o + W_LORA]
    xa = xs[:, o + W_LORA:o + W_LORA + A_LORA]
    xg = xs[:, o + W_LORA + A_LORA:o + W_LORA + A_LORA + G_LORA]
    z = w0_ref[...] + _mm_hi(jnp.tanh(xw), w2_ref[...])
    w_log = -(jnp.maximum(-z, 0.0) + jnp.log(1.0 + jnp.exp(-jnp.abs(z)))) - 0.5
    a = jax.nn.sigmoid(a0_ref[...] + _mm_hi(xa, a2_ref[...]))
    kk = k * kk_ref[...]
    ss = _mm_exact_rhs(kk * kk, bd_ref[...])
    kn = kk * (1.0 / jnp.maximum(jnp.sqrt(ss), 1e-12))
    r_ref[...] = r
    lw_ref[...] = -jnp.exp(w_log)
    k_ref[...] = k * (1.0 + (a - 1.0) * ka_ref[...])
    v_ref[...] = v
    kn_ref[...] = kn
    kb_ref[...] = kn * a
    g_ref[...] = _mm(jax.nn.sigmoid(xg), g2_ref[...])


def _rwkv_prep(prw, mu, w0, w2, a0, a2, g2, k_k, k_a, s, tm=256):
    n = prw.shape[0]
    hid = np.arange(WIDTH_B) // HEAD_DIM_B
    bd = jnp.asarray(hid[:, None] == hid[None, :], BF16)
    row = lambda wd: pl.BlockSpec((tm, wd), lambda i: (i, 0))
    vec = lambda a: a.reshape(1, -1)
    return pl.pallas_call(
        functools.partial(_rwkv_prep_kernel, tiles_per_seq=s // tm),
        grid=(n // tm,),
        in_specs=[row(C_RWKV), pl.BlockSpec((8, C_RWKV), lambda i: (jnp.maximum(i * (tm // 8) - 1, 0), 0)),
                  _full((1, C_RWKV)), _full((1, WIDTH_B)), _full(w2.shape), _full((1, WIDTH_B)), _full(a2.shape),
                  _full(g2.shape), _full((1, WIDTH_B)), _full((1, WIDTH_B)), _full(bd.shape)],
        out_specs=[row(WIDTH_B)] * 7,
        out_shape=[jax.ShapeDtypeStruct((n, WIDTH_B), F32)] * 7,
        compiler_params=_params(("parallel",)),
        name="rwkv_prep",
    )(prw, prw, vec(mu), vec(w0), w2, vec(a0), a2, g2, vec(k_k), vec(k_a), bd)


def _rwkv_chunk_kernel(r_ref, lw_ref, k_ref, v_ref, kn_ref, kb_ref, rk_ref,
                       bonus_ref, tm_ref, *, n_sub):
    L = CHUNK
    hd = HEAD_DIM_B
    ti = _iota((L, L), 0)
    si = _iota((L, L), 1)
    strict = si < ti
    incl = si <= ti
    same_sub = (ti // SUB) == (si // SUB)
    eye = (ti == si).astype(F32)
    tri = incl.astype(BF16)
    at, bt, kt, rt, bl, kl, vv, rkr, p_tot = [], [], [], [], [], [], [], [], []
    for j in range(n_sub):
        rows = slice(j * L, (j + 1) * L)
        lw = lw_ref[0, rows, :]
        cs = _mm_exact_lhs(tri, lw)
        p_inv = jnp.exp(-cs)
        p_end = jnp.exp(cs[L - 1:L, :] - cs)
        r = r_ref[0, rows, :]
        k = k_ref[0, rows, :]
        kb = kb_ref[0, rows, :]
        at.append(-kn_ref[0, rows, :] * jnp.exp(cs - lw))
        bt.append(kb * p_inv)
        kt.append(k * p_inv)
        rt.append(r * jnp.exp(cs))
        bl.append(kb * p_end)
        kl.append(k * p_end)
        vv.append(v_ref[0, rows, :])
        rkr.append(r * k * rk_ref[...])
        p_tot.append(jnp.exp(cs[L - 1:L, :]))
    units = [(j, h) for j in range(n_sub) for h in range(N_HEADS_B)]
    sl = lambda arr, u: arr[u[0]][:, u[1] * hd:(u[1] + 1) * hd]
    aa = [_mm_nt(jnp.concatenate([sl(at, u), sl(rt, u)], axis=0),
                 jnp.concatenate([sl(bt, u), sl(kt, u)], axis=0)) for u in units]
    a_ab = [jnp.where(strict, x[0:L, 0:L], 0.0) for x in aa]
    a_low = [jnp.concatenate([jnp.where(strict, x[0:L, L:2 * L], 0.0),
                              jnp.where(incl, x[L:2 * L, L:2 * L], 0.0)], axis=0) for x in aa]
    a_rb = [jnp.where(incl, x[L:2 * L, 0:L], 0.0) for x in aa]
    av = [_mm(x, sl(vv, u)) for x, u in zip(a_low, units)]
    kv = [_mm_tn(sl(kl, u), sl(vv, u)) for u in units]
    pw = [jnp.where(same_sub, x, 0.0) for x in a_ab]
    t = [eye + x for x in pw]
    for _ in range(3):
        pw = [_mm(x, x) for x in pw]
        t = [x + _mm(x, y) for x, y in zip(t, pw)]
    width = SUB
    while width < L:
        m = ((ti // width) == (si // width) + 1) & ((ti // (2 * width)) == (si // (2 * width)))
        ot = [_mm(jnp.where(m, x, 0.0), y) for x, y in zip(a_ab, t)]
        t = [x + _mm(x, y) for x, y in zip(t, ot)]
        width *= 2
    wu = [_mm(x, jnp.concatenate([sl(at, u), y[0:L]], axis=1)) for x, y, u in zip(t, av, units)]
    qy = [jnp.concatenate([sl(rt, u), y[L:2 * L]], axis=1) + _mm(x, w)
          for x, y, w, u in zip(a_rb, av, wu, units)]
    gh = [_mm_tn(sl(bl, u), w) + jnp.concatenate([eye * sl(p_tot, u), y], axis=1)
          for w, y, u in zip(wu, kv, units)]
    bonus = [jnp.sum(sl(rkr, u), axis=-1, keepdims=True) * sl(vv, u) for u in units]
    for j in range(n_sub):
        bonus_ref[0, j * L:(j + 1) * L, :] = jnp.concatenate(bonus[j * N_HEADS_B:(j + 1) * N_HEADS_B], axis=1)
    for x, y, (j, h) in zip(qy, gh, units):
        tm_ref[0, j, h, 0:L, :] = x
        tm_ref[0, j, h, L:2 * L, :] = y


def _rwkv_chunk(r, lw, k, v, kn, kb, r_k, bsz, s, n_sub=4):
    nch = s // CHUNK
    blk = pl.BlockSpec((1, n_sub * CHUNK, WIDTH_B), lambda b, c: (b, c, 0))
    mat = pl.BlockSpec((1, n_sub, N_HEADS_B, 2 * CHUNK, 2 * HEAD_DIM_B), lambda b, c: (b, c, 0, 0, 0))
    r3 = lambda a: a.reshape(bsz, s, WIDTH_B)
    return pl.pallas_call(
        functools.partial(_rwkv_chunk_kernel, n_sub=n_sub),
        grid=(bsz, nch // n_sub),
        in_specs=[blk] * 6 + [_full((1, WIDTH_B))],
        out_specs=[blk, mat],
        out_shape=[jax.ShapeDtypeStruct((bsz, s, WIDTH_B), F32),
                   jax.ShapeDtypeStruct((bsz, nch, N_HEADS_B, 2 * CHUNK, 2 * HEAD_DIM_B), F32)],
        compiler_params=_params(("parallel", "parallel")),
        name="rwkv_chunk",
    )(r3(r), r3(lw), r3(k), r3(v), r3(kn), r3(kb), r_k.reshape(1, WIDTH_B))


def _rwkv_scan_kernel(tm_ref, yn_ref, st_ref):
    L = CHUNK
    hd = HEAD_DIM_B

    @pl.when(pl.program_id(0) == 0)
    def _():
        eye = (_iota((hd, hd), 0) == _iota((hd, hd), 1)).astype(F32)
        for i in range(st_ref.shape[0]):
            st_ref[i] = jnp.concatenate([jnp.zeros((hd, hd), F32), eye], axis=0)

    for b in range(tm_ref.shape[0]):
        res = [_mm_hi(tm_ref[b, 0, h], st_ref[b * N_HEADS_B + h]) for h in range(N_HEADS_B)]
        ys = []
        for h in range(N_HEADS_B):
            st_ref[b * N_HEADS_B + h, 0:hd, :] = res[h][L:2 * L]
            y = res[h][0:L]
            mean = jnp.mean(y, axis=-1, keepdims=True)
            d = y - mean
            var = jnp.mean(d * d, axis=-1, keepdims=True)
            ys.append(d * lax.rsqrt(var + GN_EPS))
        yn_ref[b] = jnp.concatenate(ys, axis=1)


def _rwkv_scan(tmat, bsz, s):
    nch = s // CHUNK
    return pl.pallas_call(
        _rwkv_scan_kernel,
        grid=(nch,),
        in_specs=[pl.BlockSpec((bsz, 1, N_HEADS_B, 2 * CHUNK, 2 * HEAD_DIM_B), lambda c: (0, c, 0, 0, 0))],
        out_specs=pl.BlockSpec((bsz, CHUNK, WIDTH_B), lambda c: (0, c, 0)),
        out_shape=jax.ShapeDtypeStruct((bsz, s, WIDTH_B), F32),
        scratch_shapes=[pltpu.VMEM((bsz * N_HEADS_B, 2 * HEAD_DIM_B, HEAD_DIM_B), F32)],
        compiler_params=_params(("arbitrary",)),
        name="rwkv_scan",
    )(tmat)


SUBLANES = 8
TOKEN_TILE = (SUBLANES, D_MODEL // SUBLANES)
assert TOKEN_TILE[1] == LANES


def _token_tile_spec(rows):
    return pl.BlockSpec((rows,) + TOKEN_TILE, lambda i, *_: (i, 0, 0))


def _store_token_tiles(ref, x):
    for c in range(SUBLANES):
        ref[:, c, :] = x[:, c * LANES:(c + 1) * LANES]


def _gather_buffer(rows):
    return pltpu.VMEM((2, rows // SUBLANES, SUBLANES, SUBLANES, LANES), F32)


def _gathered_rows(buf, slot, start, rows):
    t0, nt = start // SUBLANES, rows // SUBLANES
    return jnp.concatenate([buf[slot, t0:t0 + nt, c].reshape(rows, LANES) for c in range(SUBLANES)], axis=1)


MERGE_SUB = 256


def _route(logits, run):
    tm = logits.shape[0]
    lane = _iota(logits.shape, 1)
    big = 4 * LANES
    is_grp = lane < N_GROUPS
    lg = jnp.where(is_grp, logits, NEG_INF)
    mg = jnp.max(lg, axis=-1, keepdims=True)
    gidx = jnp.min(jnp.where(lg == mg, lane, big), axis=-1, keepdims=True)
    pg = 1.0 / jnp.sum(jnp.where(is_grp, jnp.exp(lg - mg), 0.0), axis=-1, keepdims=True)
    in_grp = (lane >= N_GROUPS) & (((lane - N_GROUPS) >> 3) == gidx) & (lane < N_GROUPS + N_EXPERTS)
    le = jnp.where(in_grp, logits, NEG_INF)
    m1 = jnp.max(le, axis=-1, keepdims=True)
    i1 = jnp.min(jnp.where(le == m1, lane, big), axis=-1, keepdims=True)
    le2 = jnp.where(lane == i1, NEG_INF, le)
    m2 = jnp.max(le2, axis=-1, keepdims=True)
    i2 = jnp.min(jnp.where(le2 == m2, lane, big), axis=-1, keepdims=True)
    e2 = jnp.exp(m2 - m1)
    inv = pg / (1.0 + e2)
    hot1 = (lane == i1 - N_GROUPS).astype(F32)
    hot2 = (lane == i2 - N_GROUPS).astype(F32)
    both = hot1 + hot2
    earlier = (_iota((tm, tm), 1) < _iota((tm, tm), 0)).astype(BF16)
    before = jnp.dot(earlier, both.astype(BF16), preferred_element_type=F32) + run
    rank1 = jnp.sum(hot1 * before, axis=-1, keepdims=True)
    rank2 = jnp.sum(hot2 * before, axis=-1, keepdims=True)
    cols = ((i1 - N_GROUPS).astype(F32), (i2 - N_GROUPS).astype(F32), inv, inv * e2, rank1, rank2)
    route = jnp.zeros_like(logits)
    for c, v in enumerate(cols):
        route = jnp.where(lane == c, v, route)
    return route, run + jnp.sum(both, axis=0, keepdims=True)


def _merge_kernel(x_ref, oa_ref, yn_ref, bonus_ref, g_ref, ga_ref, gb_ref, lnw_ref, lnb_ref, wa_ref, wb_ref,
                  wo_ref, gf_ref, wr_ref, br_ref, x1_ref, h2_ref, route_ref, cnt_ref, run_ref):
    @pl.when(pl.program_id(0) == 0)
    def _():
        run_ref[...] = jnp.zeros_like(run_ref)

    subs = [slice(j * MERGE_SUB, (j + 1) * MERGE_SUB) for j in range(x_ref.shape[0] // MERGE_SUB)]
    ob = [(yn_ref[r, :] * lnw_ref[...] + lnb_ref[...] + bonus_ref[r, :]) * g_ref[r, :] for r in subs]
    pa = [jnp.dot(oa_ref[r, :], wa_ref[...], preferred_element_type=F32) for r in subs]
    pb = [_mm(x, wb_ref[...]) for x in ob]
    merged = [jax.nn.sigmoid(ga_ref[r, :]) * a + jax.nn.sigmoid(gb_ref[r, :]) * b for r, a, b in zip(subs, pa, pb)]
    x1 = [x_ref[r, :] + _mm(m, wo_ref[...]) for r, m in zip(subs, merged)]
    h2 = [x * lax.rsqrt(jnp.mean(x * x, axis=-1, keepdims=True) + RMS_EPS) * gf_ref[...] for x in x1]
    logits = []
    for x in h2:
        hi = x.astype(BF16)
        lo = (x - hi.astype(F32)).astype(BF16)
        logits.append(jnp.dot(jnp.concatenate([hi, hi, lo], axis=1), wr_ref[...], preferred_element_type=F32)
                      + br_ref[...])
    run = run_ref[...]
    for r, x, h, lg in zip(subs, x1, h2, logits):
        x1_ref[r, :] = x
        _store_token_tiles(h2_ref.at[r], h)
        route_ref[r, :], run = _route(lg, run)
    run_ref[...] = run
    cnt_ref[...] = run


def _merge(x2, oa, yn, bonus, g, ga, gb, ln_w, ln_b, w_a, w_b, w_o, g_ffn, w_grp, b_grp, w_exp, b_exp,
           tm=2 * MERGE_SUB):
    n = x2.shape[0]
    w_r = jnp.pad(jnp.concatenate([w_grp, w_exp], axis=1), ((0, 0), (0, LANES - N_GROUPS - N_EXPERTS)))
    w_hi = w_r.astype(BF16)
    w_lo = (w_r - w_hi.astype(F32)).astype(BF16)
    w_r = jnp.concatenate([w_hi, w_lo, w_hi], axis=0)
    b_r = jnp.pad(jnp.concatenate([b_grp, b_exp]), (0, LANES - N_GROUPS - N_EXPERTS)).reshape(1, LANES)
    row = lambda wd: pl.BlockSpec((tm, wd), lambda i: (i, 0))
    vec = lambda a: a.reshape(1, -1)
    wa, wb, wo = w_a.astype(BF16), w_b.astype(BF16), w_o.astype(BF16)
    return pl.pallas_call(
        _merge_kernel,
        grid=(n // tm,),
        in_specs=[row(D_MODEL), row(WIDTH_A), row(WIDTH_B), row(WIDTH_B), row(WIDTH_B), row(D_MODEL), row(D_MODEL),
                  _full((1, WIDTH_B)), _full((1, WIDTH_B)), _full(wa.shape), _full(wb.shape), _full(wo.shape),
                  _full((1, D_MODEL)), _full(w_r.shape), _full((1, LANES))],
        out_specs=[row(D_MODEL), _token_tile_spec(tm), row(LANES), _full((1, LANES))],
        out_shape=[jax.ShapeDtypeStruct((n, D_MODEL), F32), jax.ShapeDtypeStruct((n,) + TOKEN_TILE, F32),
                   jax.ShapeDtypeStruct((n, LANES), F32), jax.ShapeDtypeStruct((1, LANES), F32)],
        scratch_shapes=[pltpu.VMEM((1, LANES), F32)],
        compiler_params=_params(("arbitrary",)),
        name="merge",
    )(x2, oa, yn, bonus, g, ga, gb, vec(ln_w), vec(ln_b), wa, wb, wo, vec(g_ffn), w_r, b_r)


def _gather_start(idx_ref, src_hbm, buf, sem, slot, rows):
    def body(t, carry):
        for u in range(SUBLANES):
            pltpu.make_async_copy(src_hbm.at[pl.ds(idx_ref[0, 0, t * SUBLANES + u], 1)],
                                  buf.at[slot, pl.ds(t, 1), :, u], sem.at[slot]).start(priority=u % 2)
        return carry
    lax.fori_loop(0, rows // SUBLANES, body, 0)


def _gather_wait(buf, sem, slot):
    pltpu.make_async_copy(buf.at[slot], buf.at[slot], sem.at[slot]).wait()


def _gather_pipeline(idx_ref, idx_next_ref, src_hbm, buf, sem, rows):
    i = pl.program_id(0)
    slot = i % 2

    @pl.when(i == 0)
    def _():
        _gather_start(idx_ref, src_hbm, buf, sem, 0, rows)

    @pl.when(i + 1 < pl.num_programs(0))
    def _():
        _gather_start(idx_next_ref, src_hbm, buf, sem, 1 - slot, rows)

    _gather_wait(buf, sem, slot)
    return slot


def _dispatch_kernel(pos_ref, h2_ref, zero_hbm, x_hbm, sem, *, tm):
    del zero_hbm

    def body(t, carry):
        for u in range(SUBLANES):
            r = t * SUBLANES + u
            for k in range(TOP_K):
                p = pos_ref[0, 0, k * tm + r]
                pltpu.make_async_copy(h2_ref.at[pl.ds(r, 1)], x_hbm.at[pl.ds(p >> 3, 1), :, p & (SUBLANES - 1)],
                                      sem.at[k]).start(priority=(u + k) % 2)
        return carry
    lax.fori_loop(0, tm // SUBLANES, body, 0)
    for k in range(TOP_K):
        pltpu.make_async_copy(h2_ref, h2_ref, sem.at[k]).wait()


def _dispatch(h2, pos, n_pad, tm=256):
    n = h2.shape[0]
    nt = n // tm
    pos3 = pos.reshape(nt, tm, TOP_K).transpose(0, 2, 1).reshape(nt, 1, TOP_K * tm)
    tiles = (n_pad // SUBLANES, SUBLANES, SUBLANES, LANES)
    any_spec = pl.BlockSpec(memory_space=pl.ANY)
    return pl.pallas_call(
        functools.partial(_dispatch_kernel, tm=tm),
        grid=(nt,),
        in_specs=[pl.BlockSpec((1, 1, TOP_K * tm), lambda i: (i, 0, 0), memory_space=pltpu.SMEM),
                  _token_tile_spec(tm), any_spec],
        out_specs=any_spec,
        out_shape=jax.ShapeDtypeStruct(tiles, F32),
        scratch_shapes=[pltpu.SemaphoreType.DMA((TOP_K,))],
        input_output_aliases={2: 0},
        compiler_params=_params(("arbitrary",)),
        name="dispatch",
    )(pos3, h2, jnp.zeros(tiles, F32))


def _moe_kernel(be_ref, x_ref, w13_ref, w2_ref, y_ref):
    xb = jnp.concatenate([x_ref[:, c].reshape(MOE_BLOCK, LANES) for c in range(SUBLANES)], axis=1)
    a = _mm(xb, w13_ref[0])
    hid = jax.nn.silu(a[:, 0:D_EXPERT]) * a[:, D_EXPERT:2 * D_EXPERT]
    _store_token_tiles(y_ref, _mm(hid, w2_ref[0]))


def _moe(x_pad, blk_expert, w13, w2):
    n_blk = blk_expert.shape[0]
    grid_spec = pltpu.PrefetchScalarGridSpec(
        num_scalar_prefetch=1,
        grid=(n_blk,),
        in_specs=[pl.BlockSpec((MOE_BLOCK // SUBLANES, SUBLANES, SUBLANES, LANES), lambda i, be: (i, 0, 0, 0)),
                  pl.BlockSpec((1, D_MODEL, 2 * D_EXPERT), lambda i, be: (be[i], 0, 0)),
                  pl.BlockSpec((1, D_EXPERT, D_MODEL), lambda i, be: (be[i], 0, 0))],
        out_specs=_token_tile_spec(MOE_BLOCK),
    )
    return pl.pallas_call(
        _moe_kernel,
        grid_spec=grid_spec,
        out_shape=jax.ShapeDtypeStruct((n_blk * MOE_BLOCK,) + TOKEN_TILE, F32),
        compiler_params=_params(("parallel",)),
        name="moe",
    )(blk_expert, x_pad, w13, w2)


def _final_kernel(pos_ref, pos_next_ref, x1_ref, route_ref, g_ref, y_hbm, o_ref, buf, sem, *, tm):
    slot = _gather_pipeline(pos_ref, pos_next_ref, y_hbm, buf, sem, TOP_K * tm)
    x = x1_ref[...]
    for k in range(TOP_K):
        x = x + route_ref[:, TOP_K + k:TOP_K + k + 1] * _gathered_rows(buf, slot, k * tm, tm)
    o_ref[...] = x * lax.rsqrt(jnp.mean(x * x, axis=-1, keepdims=True) + RMS_EPS) * g_ref[...]


def _final(x1, route, y_pad, pos, g_final, tm=256):
    n = x1.shape[0]
    nt = n // tm
    pos3 = pos.reshape(nt, tm, TOP_K).transpose(0, 2, 1).reshape(nt, 1, TOP_K * tm)
    smem_blk = lambda f: pl.BlockSpec((1, 1, TOP_K * tm), f, memory_space=pltpu.SMEM)
    row = lambda wd: pl.BlockSpec((tm, wd), lambda i: (i, 0))
    return pl.pallas_call(
        functools.partial(_final_kernel, tm=tm),
        grid=(nt,),
        in_specs=[smem_blk(lambda i: (i, 0, 0)), smem_blk(lambda i: (jnp.minimum(i + 1, nt - 1), 0, 0)),
                  row(D_MODEL), row(LANES), _full((1, D_MODEL)), pl.BlockSpec(memory_space=pl.ANY)],
        out_specs=row(D_MODEL),
        out_shape=jax.ShapeDtypeStruct((n, D_MODEL), F32),
        scratch_shapes=[_gather_buffer(TOP_K * tm), pltpu.SemaphoreType.DMA((2,))],
        compiler_params=_params(("arbitrary",)),
        name="final",
    )(pos3, pos3, x1, route, g_final.reshape(1, D_MODEL), y_pad)


def _dispatch_plan(route, counts, n_tok):
    expert = route[:, 0:TOP_K].astype(I32)
    rank = route[:, 2 * TOP_K:3 * TOP_K].astype(I32)
    cnt = counts[0, 0:N_EXPERTS].astype(I32)
    padded = (cnt + MOE_BLOCK - 1) // MOE_BLOCK * MOE_BLOCK
    ids = jnp.arange(N_EXPERTS, dtype=I32)
    pad_end = jnp.sum(jnp.where(ids[None, :] <= ids[:, None], padded[None, :], 0), axis=1)
    pad_start = pad_end - padded
    pos = jnp.sum(jnp.where(expert[..., None] == ids, pad_start, 0), axis=-1) + rank
    n_pad = n_tok * TOP_K + N_EXPERTS * MOE_BLOCK
    n_blk = n_pad // MOE_BLOCK
    blk_start = jnp.arange(n_blk, dtype=I32) * MOE_BLOCK
    blk_expert = jnp.minimum(jnp.sum((pad_end[None, :] <= blk_start[:, None]).astype(I32), axis=1), N_EXPERTS - 1)
    return pos, blk_expert, n_pad


def kernel(x, positions, g_mix, w_in, cmp_pe_k, cmp_w1_k, cmp_b1_k, cmp_w2_k, cmp_b2_k, cmp_pe_v, cmp_w1_v, cmp_b1_v, cmp_w2_v, cmp_b2_v, rwkv_mu, rwkv_w0, rwkv_w2, rwkv_a0, rwkv_a2, rwkv_g2, rwkv_k_k, rwkv_k_a, rwkv_r_k, rwkv_ln_w, rwkv_ln_b, w_proj_a, w_proj_b, w_out, g_ffn, w_grp, b_grp, w_exp, b_exp, e_w1, e_w3, e_w2, g_final):
    bsz, s, _ = x.shape
    assert w_in.shape[0] == 1, "single-layer stack"
    n_tok = bsz * s
    x2 = x.reshape(n_tok, D_MODEL)
    half = HEAD_DIM // 2
    freqs = jnp.power(ROPE_THETA, -jnp.arange(half, dtype=F32) / half)
    ang = positions.astype(F32).reshape(n_tok, 1) * freqs
    cs = jnp.tile(jnp.cos(ang), (1, 4))
    sn = jnp.tile(jnp.concatenate([-jnp.sin(ang), jnp.sin(ang)], axis=1), (1, 2))
    q, kc, ksl, kwn, vc, vsl, vwn, ng, prw, ga, gb = _inproj(x2, g_mix[0], cs, sn, w_in[0])
    k_c = _compress(kc, cmp_pe_k[0], cmp_w1_k[0], cmp_b1_k[0], cmp_w2_k[0], cmp_b2_k[0], bsz, s, 0.0)
    v_c = _compress(vc, cmp_pe_v[0], cmp_w1_v[0], cmp_b1_v[0], cmp_w2_v[0], cmp_b2_v[0], bsz, s, 1.0)
    o_a = _nsa(q, k_c, v_c, ksl, vsl, kwn, vwn, ng, bsz, s).reshape(n_tok, WIDTH_A)
    r, lw, k, v, kn, kb, g = _rwkv_prep(prw, rwkv_mu[0], rwkv_w0[0], rwkv_w2[0], rwkv_a0[0], rwkv_a2[0],
                                        rwkv_g2[0], rwkv_k_k[0], rwkv_k_a[0], s)
    bonus, tmat = _rwkv_chunk(r, lw, k, v, kn, kb, rwkv_r_k[0], bsz, s)
    yn = _rwkv_scan(tmat, bsz, s)
    x1, h2, route, counts = _merge(x2, o_a, yn.reshape(n_tok, WIDTH_B), bonus.reshape(n_tok, WIDTH_B), g, ga, gb,
                           rwkv_ln_w[0], rwkv_ln_b[0], w_proj_a[0], w_proj_b[0], w_out[0], g_ffn[0],
                           w_grp[0], b_grp[0], w_exp[0], b_exp[0])
    pos, blk_expert, n_pad = _dispatch_plan(route, counts, n_tok)
    w13 = jnp.concatenate([e_w1[0], e_w3[0]], axis=-1).astype(BF16)
    y_pad = _moe(_dispatch(h2, pos, n_pad), blk_expert, w13, e_w2[0].astype(BF16))
    out = _final(x1, route, y_pad, pos, g_final)
    return out.reshape(bsz, s, D_MODEL)
```

```python
import functools

import numpy as np
import jax
import jax.numpy as jnp
from jax import lax
from jax.experimental import pallas as pl
from jax.experimental.pallas import tpu as pltpu

F32 = jnp.float32
BF16 = jnp.bfloat16
I32 = jnp.int32

D_MODEL = 1024
N_HEADS_A = 8
N_KV_GROUPS = 2
HEADS_PER_GROUP = N_HEADS_A // N_KV_GROUPS
HEAD_DIM = 64
L_CMP = 32
D_CMP = 16
CMP_HIDDEN = 256
L_SLC = 64
N_SELECT = 16
WINDOW = 512
Q_BLOCK = 128
ROPE_THETA = 10000.0
FORCE_SCORE = 1e4
NEG_INF = -1e30
N_HEADS_B = 8
HEAD_DIM_B = 64
W_LORA = 64
A_LORA = 64
G_LORA = 128
GN_EPS = 64e-5
N_GROUPS = 4
EXPERTS_PER_GROUP = 8
N_EXPERTS = N_GROUPS * EXPERTS_PER_GROUP
TOP_K = 2
D_EXPERT = 256
MOE_BLOCK = 256
RMS_EPS = 1e-6
WIDTH_A = N_HEADS_A * HEAD_DIM
KV_WIDTH = N_KV_GROUPS * HEAD_DIM
WIDTH_B = N_HEADS_B * HEAD_DIM_B
C_RWKV = 3 * WIDTH_B + W_LORA + A_LORA + G_LORA

LANES = 128
CHUNK = 64
SUB = 16
KEY_TILE = 1024
BLOCKS_PER_TILE = KEY_TILE // L_SLC
VMEM_LIMIT = 56 * 1024 * 1024

HI = lax.Precision.HIGHEST
LOG2_E = 1.4426950408889634


def _mm(a, b):
    return jnp.dot(a.astype(BF16), b.astype(BF16), preferred_element_type=F32)


def _mm_nt(a, b):
    return lax.dot_general(a.astype(BF16), b.astype(BF16), (((1,), (1,)), ((), ())), preferred_element_type=F32)


def _mm_tn(a, b):
    return lax.dot_general(a.astype(BF16), b.astype(BF16), (((0,), (0,)), ((), ())), preferred_element_type=F32)


def _mm_hi(a, b):
    return jnp.dot(a, b, preferred_element_type=F32, precision=HI)


def _mm_x3(a, b):
    a1 = a.astype(BF16)
    a2 = (a - a1.astype(F32)).astype(BF16)
    b1 = b.astype(BF16)
    b2 = (b - b1.astype(F32)).astype(BF16)
    dot = lambda x, y: jnp.dot(x, y, preferred_element_type=F32)
    return dot(a1, b1) + dot(a1, b2) + dot(a2, b1)


def _bf16_pieces(a):
    a1 = a.astype(BF16)
    r1 = a - a1.astype(F32)
    a2 = r1.astype(BF16)
    return a1, a2, (r1 - a2.astype(F32)).astype(BF16)


def _mm_exact_rhs(a, b):
    bb = b.astype(BF16)
    return sum(jnp.dot(x, bb, preferred_element_type=F32) for x in _bf16_pieces(a))


def _mm_exact_lhs(a, b):
    aa = a.astype(BF16)
    return sum(jnp.dot(aa, x, preferred_element_type=F32) for x in _bf16_pieces(b))


def _iota(shape, dim):
    return lax.broadcasted_iota(I32, shape, dim)


def _params(sem):
    return pltpu.CompilerParams(dimension_semantics=sem, vmem_limit_bytes=VMEM_LIMIT)


def _full(shape):
    nd = len(shape)
    return pl.BlockSpec(shape, lambda *_: (0,) * nd)


Q_PAD = N_HEADS_A * LANES
KV_PAD = N_KV_GROUPS * LANES


def _inproj_kernel(x_ref, g_ref, cs_ref, sn_ref, wr_ref, wv_ref, wn_ref, ww_ref, wg_ref,
                   q_ref, kc_ref, ksl_ref, kwn_ref, vc_ref, vsl_ref, vwn_ref, ng_ref, prw_ref, ga_ref, gb_ref):
    x = x_ref[...]
    h = (x * lax.rsqrt(jnp.mean(x * x, axis=-1, keepdims=True) + RMS_EPS) * g_ref[...]).astype(BF16)
    pr = jnp.dot(h, wr_ref[...], preferred_element_type=F32)
    cs = cs_ref[...]
    sn = sn_ref[...]
    first_half = (_iota((1, LANES), 1) & (HEAD_DIM // 2)) == 0
    chunks = []
    for c in range(pr.shape[1] // LANES):
        x = pr[:, c * LANES:(c + 1) * LANES]
        swapped = jnp.where(first_half, pltpu.roll(x, LANES - HEAD_DIM // 2, 1), pltpu.roll(x, HEAD_DIM // 2, 1))
        chunks.append(x * cs + swapped * sn)
    ro = jnp.concatenate(chunks, axis=1)
    q_ref[...] = (ro[:, 0:Q_PAD] * (HEAD_DIM ** -0.5 * LOG2_E)).astype(BF16)
    o = Q_PAD
    kc_ref[...] = ro[:, o:o + KV_WIDTH]
    tm = x_ref.shape[0]
    blk = ((pl.program_id(0) * tm + _iota((tm, 1), 0)) >> 6) & (BLOCKS_PER_TILE - 1)
    code = jnp.where((_iota((1, KV_PAD), 1) & (LANES - 1)) == HEAD_DIM + blk, NEG_INF, 0.0)
    ksl_ref[...] = (ro[:, o + KV_WIDTH:o + KV_WIDTH + KV_PAD] + code).astype(BF16)
    kwn_ref[...] = ro[:, o + KV_WIDTH + KV_PAD:o + KV_WIDTH + 2 * KV_PAD].astype(BF16)
    v = jnp.dot(h, wv_ref[...], preferred_element_type=F32)
    vc_ref[...] = v[:, 0:KV_WIDTH]
    ones = ((_iota((1, 2 * KV_PAD), 1) & (LANES - 1)) >= HEAD_DIM).astype(F32)
    vp = v[:, KV_WIDTH:KV_WIDTH + 2 * KV_PAD] + ones
    vsl_ref[...] = vp[:, 0:KV_PAD].astype(BF16)
    vwn_ref[...] = vp[:, KV_PAD:2 * KV_PAD].astype(BF16)
    ng_ref[...] = jnp.dot(h, wn_ref[...], preferred_element_type=F32)
    prw_ref[...] = jnp.dot(h, ww_ref[...], preferred_element_type=F32)
    gg = jnp.dot(h, wg_ref[...], preferred_element_type=F32)
    ga_ref[...] = gg[:, 0:D_MODEL]
    gb_ref[...] = gg[:, D_MODEL:2 * D_MODEL]


def _pad_heads(w):
    d, wd = w.shape
    return jnp.pad(w.reshape(d, wd // HEAD_DIM, HEAD_DIM), ((0, 0), (0, 0), (0, LANES - HEAD_DIM))).reshape(d, -1)


def _inproj(x2, g_mix, cs, sn, w_in, tm=256):
    n = x2.shape[0]
    o = 0
    cols = {}
    for name, wd in (("q", WIDTH_A), ("kc", KV_WIDTH), ("vc", KV_WIDTH), ("ksl", KV_WIDTH), ("vsl", KV_WIDTH),
                     ("kwn", KV_WIDTH), ("vwn", KV_WIDTH), ("ng", 3 * N_HEADS_A), ("rw", C_RWKV),
                     ("ga", D_MODEL), ("gb", D_MODEL)):
        cols[name] = w_in[:, o:o + wd]
        o += wd
    w_rope = jnp.concatenate(
        [_pad_heads(cols["q"]), cols["kc"], _pad_heads(cols["ksl"]), _pad_heads(cols["kwn"])], axis=1)
    w_v = jnp.concatenate([cols["vc"], _pad_heads(cols["vsl"]), _pad_heads(cols["vwn"])], axis=1)
    w_ng = jnp.pad(cols["ng"], ((0, 0), (0, LANES - 3 * N_HEADS_A)))
    w_gate = jnp.concatenate([cols["ga"], cols["gb"]], axis=1)
    ws = [w.astype(BF16) for w in (w_rope, w_v, w_ng, cols["rw"], w_gate)]
    row = lambda wd: pl.BlockSpec((tm, wd), lambda i: (i, 0))
    outs = [(Q_PAD, BF16), (KV_WIDTH, F32), (KV_PAD, BF16), (KV_PAD, BF16), (KV_WIDTH, F32), (KV_PAD, BF16),
            (KV_PAD, BF16), (LANES, F32), (C_RWKV, F32), (D_MODEL, F32), (D_MODEL, F32)]
    return pl.pallas_call(
        _inproj_kernel,
        grid=(n // tm,),
        in_specs=[row(D_MODEL), _full((1, D_MODEL)), row(LANES), row(LANES)] + [_full(w.shape) for w in ws],
        out_specs=[row(wd) for wd, _ in outs],
        out_shape=[jax.ShapeDtypeStruct((n, wd), dt) for wd, dt in outs],
        compiler_params=_params(("parallel",)),
        name="inproj",
    )(x2, g_mix.reshape(1, D_MODEL), cs, sn, *ws)


def _compress_kernel(kv_ref, pe_ref, w1_ref, wg_ref, b1_ref, w2_ref, b2_ref, o_ref, *, pad_value):
    nch = o_ref.shape[2]
    z = None
    for l in range(D_CMP):
        part = _mm(kv_ref[0, pl.ds(l, nch, stride=D_CMP), :], wg_ref[0, l])
        z = part if z is None else z + part
    z1 = z[:, 0:CMP_HIDDEN]
    z2 = z[:, CMP_HIDDEN:2 * CMP_HIDDEN]
    z2 = jnp.concatenate([z2[1:], z2[:1]], axis=0)
    pb = _mm(pe_ref[...], w1_ref[...])[0:1] + b1_ref[...]
    hid = jax.nn.gelu(z1 + z2 + pb)
    out = _mm(hid, w2_ref[...]) + b2_ref[...]
    o_ref[0, 0] = jnp.concatenate([out, jnp.full(out.shape, pad_value, F32)], axis=1).astype(o_ref.dtype)


def _compress(kv, pe, w1, b1, w2, b2, bsz, s, pad_value):
    nch = s // D_CMP
    pe8 = jnp.broadcast_to(pe.reshape(1, L_CMP * HEAD_DIM), (8, L_CMP * HEAD_DIM))
    w1r = w1.reshape(2, D_CMP, HEAD_DIM, CMP_HIDDEN)
    wcat = jnp.concatenate([w1r[0], w1r[1]], axis=-1)
    wg = jnp.stack([jnp.pad(wcat, ((0, 0), (g * HEAD_DIM, (N_KV_GROUPS - 1 - g) * HEAD_DIM), (0, 0)))
                    for g in range(N_KV_GROUPS)]).astype(BF16)
    return pl.pallas_call(
        functools.partial(_compress_kernel, pad_value=pad_value),
        grid=(bsz, N_KV_GROUPS),
        in_specs=[pl.BlockSpec((1, s, KV_WIDTH), lambda b, g: (b, 0, 0)),
                  _full(pe8.shape), _full(w1.shape),
                  pl.BlockSpec((1, D_CMP, KV_WIDTH, 2 * CMP_HIDDEN), lambda b, g: (g, 0, 0, 0)),
                  _full((1, CMP_HIDDEN)), _full(w2.shape), _full((1, HEAD_DIM))],
        out_specs=pl.BlockSpec((1, 1, nch, LANES), lambda b, g: (b, g, 0, 0)),
        out_shape=jax.ShapeDtypeStruct((bsz, N_KV_GROUPS, nch, LANES), BF16),
        compiler_params=_params(("parallel", "parallel")),
        name="compress",
    )(kv.reshape(bsz, s, KV_WIDTH), pe8, w1, wg, b1.reshape(1, CMP_HIDDEN), w2, b2.reshape(1, HEAD_DIM))


def _nsa_kernel(q_ref, kc_ref, vc_ref, ksl_ref, vsl_ref, kwn_ref, vwn_ref, ng_ref, ov_ref, o_ref, *, n_top):
    n_cmp = kc_ref.shape[2]
    n_slc = ov_ref.shape[1]
    s0 = pl.program_id(1) * Q_BLOCK
    hq = HEADS_PER_GROUP * Q_BLOCK
    t_q = s0 + _iota((Q_BLOCK, 1), 0)
    sig = jax.nn.sigmoid(ng_ref[0])
    win_start = pl.multiple_of(jnp.maximum(s0 - WINDOW, 0), LANES)
    n_tiles = s0 // KEY_TILE + 1
    groups = range(N_KV_GROUPS)
    lanes = lambda g: slice(g * LANES, (g + 1) * LANES)
    qg = [jnp.concatenate([q_ref[0, :, lanes(g * HEADS_PER_GROUP + h)] for h in range(HEADS_PER_GROUP)], axis=0)
          for g in groups]

    def add_bias(s, bias):
        return (s.reshape(HEADS_PER_GROUP, Q_BLOCK, bias.shape[1]) + bias[None]).reshape(hq, bias.shape[1])

    cmp_bias = jnp.where((_iota((1, n_cmp), 1) * D_CMP + (L_CMP - 1)) <= t_q, 0.0, NEG_INF)
    has_cmp = jnp.concatenate([(t_q >= L_CMP - 1).astype(F32)] * HEADS_PER_GROUP, axis=0)
    pos_w = win_start + _iota((1, WINDOW + Q_BLOCK), 1)
    win_bias = jnp.where((pos_w <= t_q) & (pos_w > t_q - WINDOW), 0.0, NEG_INF)

    s_c = [add_bias(_mm_nt(qg[g], kc_ref[0, g]), cmp_bias) for g in groups]
    s_w = [add_bias(_mm_nt(qg[g], kwn_ref[0, pl.ds(win_start, WINDOW + Q_BLOCK), lanes(g)]), win_bias)
           for g in groups]
    e_c = [jnp.exp2(x - jnp.max(x, axis=-1, keepdims=True)) for x in s_c]
    p_c = [x * (has_cmp / jnp.sum(x, axis=-1, keepdims=True)) for x in e_c]
    e_w = [jnp.exp2(x - jnp.max(x, axis=-1, keepdims=True)) for x in s_w]
    o_c = [_mm(p_c[g], vc_ref[0, g]) for g in groups]
    acc_w = [_mm(e_w[g], vwn_ref[0, pl.ds(win_start, WINDOW + Q_BLOCK), lanes(g)]) for g in groups]
    o_win = [x * (1.0 / x[:, HEAD_DIM:HEAD_DIM + 1]) for x in acc_w]

    imp_t = []
    for g in groups:
        p_sum = p_c[g][0:Q_BLOCK]
        for h in range(1, HEADS_PER_GROUP):
            p_sum = p_sum + p_c[g][h * Q_BLOCK:(h + 1) * Q_BLOCK]
        imp_t.append(_mm_exact_rhs(p_sum, ov_ref[...]).T)
    j = _iota((n_slc, Q_BLOCK), 0)
    cur = (s0 + _iota((n_slc, Q_BLOCK), 1)) >> 6
    forced = (j == 0) | (j == cur) | (j == cur - 1)
    score = [jnp.where(forced, -3e38, jnp.where(j <= cur, x, -1.0)) for x in imp_t]
    sel_t = [forced.astype(F32) for _ in groups]
    for _ in range(n_top - 3):
        for g in groups:
            m = jnp.max(score[g], axis=0, keepdims=True)
            first = jnp.min(jnp.where(score[g] == m, j, n_slc), axis=0, keepdims=True)
            hit = j == first
            sel_t[g] = jnp.where(hit, 1.0, sel_t[g])
            score[g] = jnp.where(hit, -3e38, score[g])
    sel = [1.0 - x.T for x in sel_t]

    lane_q = _iota((Q_BLOCK, LANES), 1)
    in_code = (lane_q >= HEAD_DIM) & (lane_q < HEAD_DIM + BLOCKS_PER_TILE)

    def sel_step(kt, carry, diagonal):
        k0 = pl.multiple_of(kt * KEY_TILE, KEY_TILE)
        shift = (HEAD_DIM - BLOCKS_PER_TILE * kt) & (LANES - 1)
        s = []
        for g in groups:
            code = jnp.where(in_code, pltpu.roll(sel[g], shift, 1), 0.0).astype(BF16)
            qa = qg[g] + jnp.concatenate([code] * HEADS_PER_GROUP, axis=0)
            s.append(_mm_nt(qa, ksl_ref[0, pl.ds(k0, KEY_TILE), lanes(g)]))
        if diagonal:
            causal = jnp.where((k0 + _iota((Q_BLOCK, KEY_TILE), 1)) <= t_q, 0.0, NEG_INF)
            s = [add_bias(x, causal) for x in s]
        m_new = [jnp.maximum(carry[2 * g], jnp.max(s[g], axis=-1, keepdims=True)) for g in groups]
        p = [jnp.exp2(s[g] - m_new[g]) for g in groups]
        pv = [_mm(p[g], vsl_ref[0, pl.ds(k0, KEY_TILE), lanes(g)]) for g in groups]
        out = []
        for g in groups:
            out += [m_new[g], jnp.exp2(carry[2 * g] - m_new[g]) * carry[2 * g + 1] + pv[g]]
        return tuple(out)

    init = (jnp.full((hq, 1), NEG_INF, F32), jnp.zeros((hq, LANES), F32)) * N_KV_GROUPS
    fin = lax.fori_loop(0, n_tiles - 1, lambda kt, c: sel_step(kt, c, False), init)
    fin = sel_step(n_tiles - 1, fin, True)

    heads_out = []
    for g in groups:
        acc_s = fin[2 * g + 1]
        o_s = acc_s * (1.0 / acc_s[:, HEAD_DIM:HEAD_DIM + 1])
        o_w = o_win[g]
        for h in range(HEADS_PER_GROUP):
            hh = g * HEADS_PER_GROUP + h
            r = slice(h * Q_BLOCK, (h + 1) * Q_BLOCK)
            mix = (sig[:, 3 * hh:3 * hh + 1] * o_c[g][r] + sig[:, 3 * hh + 1:3 * hh + 2] * o_s[r]
                   + sig[:, 3 * hh + 2:3 * hh + 3] * o_w[r])
            heads_out.append(mix[:, 0:HEAD_DIM])
    o_ref[0] = jnp.concatenate(heads_out, axis=1).astype(o_ref.dtype)


def _nsa(q, kc, vc, ksl, vsl, kwn, vwn, ng, bsz, s):
    n_cmp = s // D_CMP
    n_slc = s // L_SLC
    ii = np.arange(n_cmp)[:, None]
    jj = np.arange(n_slc)[None, :]
    assert n_slc <= LANES and s % KEY_TILE == 0 and min(N_SELECT, n_slc) >= 3
    overlap = (ii * D_CMP < (jj + 1) * L_SLC) & (ii * D_CMP + L_CMP > jj * L_SLC)
    overlap = jnp.asarray(np.pad(overlap, ((0, 0), (0, LANES - n_slc))), BF16)
    seq = lambda wd: pl.BlockSpec((1, s, wd), lambda b, i: (b, 0, 0))
    blk = lambda wd: pl.BlockSpec((1, Q_BLOCK, wd), lambda b, i: (b, i, 0))
    cmp_spec = pl.BlockSpec((1, N_KV_GROUPS, n_cmp, LANES), lambda b, i: (b, 0, 0, 0))
    r3 = lambda a: a.reshape(bsz, s, a.shape[-1])
    return pl.pallas_call(
        functools.partial(_nsa_kernel, n_top=min(N_SELECT, n_slc)),
        grid=(bsz, s // Q_BLOCK),
        in_specs=[blk(Q_PAD), cmp_spec, cmp_spec, seq(KV_PAD), seq(KV_PAD), seq(KV_PAD), seq(KV_PAD),
                  blk(LANES), _full(overlap.shape)],
        out_specs=blk(WIDTH_A),
        out_shape=jax.ShapeDtypeStruct((bsz, s, WIDTH_A), BF16),
        compiler_params=_params(("parallel", "arbitrary")),
        name="nsa",
    )(r3(q), kc, vc, r3(ksl), r3(vsl), r3(kwn), r3(vwn), r3(ng), overlap)


def _rwkv_prep_kernel(p_ref, pv_ref, mu_ref, w0_ref, w2_ref, a0_ref, a2_ref, g2_ref, kk_ref, ka_ref, bd_ref,
                      r_ref, lw_ref, k_ref, v_ref, kn_ref, kb_ref, g_ref, *, tiles_per_seq):
    p = p_ref[...]
    first = (pl.program_id(0) % tiles_per_seq) == 0
    prev_last = jnp.where(first, 0.0, pv_ref[7:8, :])
    p_prev = jnp.concatenate([prev_last, p[:-1]], axis=0)
    xs = p + (p_prev - p) * mu_ref[...]
    wb = WIDTH_B
    r = xs[:, 0:wb]
    k = xs[:, wb:2 * wb]
    v = xs[:, 2 * wb:3 * wb]
    o = 3 * wb
    xw = xs[:, o:o + W_LORA]
    xa = xs[:, o + W_LORA:o + W_LORA + A_LORA]
    xg = xs[:, o + W_LORA + A_LORA:o + W_LORA + A_LORA + G_LORA]
    z = w0_ref[...] + _mm_hi(jnp.tanh(xw), w2_ref[...])
    w_log = -(jnp.maximum(-z, 0.0) + jnp.log(1.0 + jnp.exp(-jnp.abs(z)))) - 0.5
    a = jax.nn.sigmoid(a0_ref[...] + _mm_hi(xa, a2_ref[...]))
    kk = k * kk_ref[...]
    ss = _mm_exact_rhs(kk * kk, bd_ref[...])
    kn = kk * (1.0 / jnp.maximum(jnp.sqrt(ss), 1e-12))
    r_ref[...] = r
    lw_ref[...] = -jnp.exp(w_log)
    k_ref[...] = k * (1.0 + (a - 1.0) * ka_ref[...])
    v_ref[...] = v
    kn_ref[...] = kn
    kb_ref[...] = kn * a
    g_ref[...] = _mm(jax.nn.sigmoid(xg), g2_ref[...])


def _rwkv_prep(prw, mu, w0, w2, a0, a2, g2, k_k, k_a, s, tm=256):
    n = prw.shape[0]
    hid = np.arange(WIDTH_B) // HEAD_DIM_B
    bd = jnp.asarray(hid[:, None] == hid[None, :], BF16)
    row = lambda wd: pl.BlockSpec((tm, wd), lambda i: (i, 0))
    vec = lambda a: a.reshape(1, -1)
    return pl.pallas_call(
        functools.partial(_rwkv_prep_kernel, tiles_per_seq=s // tm),
        grid=(n // tm,),
        in_specs=[row(C_RWKV), pl.BlockSpec((8, C_RWKV), lambda i: (jnp.maximum(i * (tm // 8) - 1, 0), 0)),
                  _full((1, C_RWKV)), _full((1, WIDTH_B)), _full(w2.shape), _full((1, WIDTH_B)), _full(a2.shape),
                  _full(g2.shape), _full((1, WIDTH_B)), _full((1, WIDTH_B)), _full(bd.shape)],
        out_specs=[row(WIDTH_B)] * 7,
        out_shape=[jax.ShapeDtypeStruct((n, WIDTH_B), F32)] * 7,
        compiler_params=_params(("parallel",)),
        name="rwkv_prep",
    )(prw, prw, vec(mu), vec(w0), w2, vec(a0), a2, g2, vec(k_k), vec(k_a), bd)


def _rwkv_chunk_kernel(r_ref, lw_ref, k_ref, v_ref, kn_ref, kb_ref, rk_ref,
                       bonus_ref, tm_ref, *, n_sub):
    L = CHUNK
    hd = HEAD_DIM_B
    ti = _iota((L, L), 0)
    si = _iota((L, L), 1)
    strict = si < ti
    incl = si <= ti
    same_sub = (ti // SUB) == (si // SUB)
    eye = (ti == si).astype(F32)
    tri = incl.astype(BF16)
    at, bt, kt, rt, bl, kl, vv, rkr, p_tot = [], [], [], [], [], [], [], [], []
    for j in range(n_sub):
        rows = slice(j * L, (j + 1) * L)
        lw = lw_ref[0, rows, :]
        cs = _mm_exact_lhs(tri, lw)
        p_inv = jnp.exp(-cs)
        p_end = jnp.exp(cs[L - 1:L, :] - cs)
        r = r_ref[0, rows, :]
        k = k_ref[0, rows, :]
        kb = kb_ref[0, rows, :]
        at.append(-kn_ref[0, rows, :] * jnp.exp(cs - lw))
        bt.append(kb * p_inv)
        kt.append(k * p_inv)
        rt.append(r * jnp.exp(cs))
        bl.append(kb * p_end)
        kl.append(k * p_end)
        vv.append(v_ref[0, rows, :])
        rkr.append(r * k * rk_ref[...])
        p_tot.append(jnp.exp(cs[L - 1:L, :]))
    units = [(j, h) for j in range(n_sub) for h in range(N_HEADS_B)]
    sl = lambda arr, u: arr[u[0]][:, u[1] * hd:(u[1] + 1) * hd]
    aa = [_mm_nt(jnp.concatenate([sl(at, u), sl(rt, u)], axis=0),
                 jnp.concatenate([sl(bt, u), sl(kt, u)], axis=0)) for u in units]
    a_ab = [jnp.where(strict, x[0:L, 0:L], 0.0) for x in aa]
    a_low = [jnp.concatenate([jnp.where(strict, x[0:L, L:2 * L], 0.0),
                              jnp.where(incl, x[L:2 * L, L:2 * L], 0.0)], axis=0) for x in aa]
    a_rb = [jnp.where(incl, x[L:2 * L, 0:L], 0.0) for x in aa]
    av = [_mm(x, sl(vv, u)) for x, u in zip(a_low, units)]
    kv = [_mm_tn(sl(kl, u), sl(vv, u)) for u in units]
    pw = [jnp.where(same_sub, x, 0.0) for x in a_ab]
    t = [eye + x for x in pw]
    for _ in range(3):
        pw = [_mm(x, x) for x in pw]
        t = [x + _mm(x, y) for x, y in zip(t, pw)]
    width = SUB
    while width < L:
        m = ((ti // width) == (si // width) + 1) & ((ti // (2 * width)) == (si // (2 * width)))
        ot = [_mm(jnp.where(m, x, 0.0), y) for x, y in zip(a_ab, t)]
        t = [x + _mm(x, y) for x, y in zip(t, ot)]
        width *= 2
    wu = [_mm(x, jnp.concatenate([sl(at, u), y[0:L]], axis=1)) for x, y, u in zip(t, av, units)]
    qy = [jnp.concatenate([sl(rt, u), y[L:2 * L]], axis=1) + _mm(x, w)
          for x, y, w, u in zip(a_rb, av, wu, units)]
    gh = [_mm_tn(sl(bl, u), w) + jnp.concatenate([eye * sl(p_tot, u), y], axis=1)
          for w, y, u in zip(wu, kv, units)]
    bonus = [jnp.sum(sl(rkr, u), axis=-1, keepdims=True) * sl(vv, u) for u in units]
    for j in range(n_sub):
        bonus_ref[0, j * L:(j + 1) * L, :] = jnp.concatenate(bonus[j * N_HEADS_B:(j + 1) * N_HEADS_B], axis=1)
    for x, y, (j, h) in zip(qy, gh, units):
        tm_ref[0, j, h, 0:L, :] = x
        tm_ref[0, j, h, L:2 * L, :] = y


def _rwkv_chunk(r, lw, k, v, kn, kb, r_k, bsz, s, n_sub=4):
    nch = s // CHUNK
    blk = pl.BlockSpec((1, n_sub * CHUNK, WIDTH_B), lambda b, c: (b, c, 0))
    mat = pl.BlockSpec((1, n_sub, N_HEADS_B, 2 * CHUNK, 2 * HEAD_DIM_B), lambda b, c: (b, c, 0, 0, 0))
    r3 = lambda a: a.reshape(bsz, s, WIDTH_B)
    return pl.pallas_call(
        functools.partial(_rwkv_chunk_kernel, n_sub=n_sub),
        grid=(bsz, nch // n_sub),
        in_specs=[blk] * 6 + [_full((1, WIDTH_B))],
        out_specs=[blk, mat],
        out_shape=[jax.ShapeDtypeStruct((bsz, s, WIDTH_B), F32),
                   jax.ShapeDtypeStruct((bsz, nch, N_HEADS_B, 2 * CHUNK, 2 * HEAD_DIM_B), F32)],
        compiler_params=_params(("parallel", "parallel")),
        name="rwkv_chunk",
    )(r3(r), r3(lw), r3(k), r3(v), r3(kn), r3(kb), r_k.reshape(1, WIDTH_B))


def _rwkv_scan_kernel(tm_ref, yn_ref, st_ref):
    L = CHUNK
    hd = HEAD_DIM_B

    @pl.when(pl.program_id(0) == 0)
    def _():
        eye = (_iota((hd, hd), 0) == _iota((hd, hd), 1)).astype(F32)
        for i in range(st_ref.shape[0]):
            st_ref[i] = jnp.concatenate([jnp.zeros((hd, hd), F32), eye], axis=0)

    for b in range(tm_ref.shape[0]):
        res = [_mm_x3(tm_ref[b, 0, h], st_ref[b * N_HEADS_B + h]) for h in range(N_HEADS_B)]
        ys = []
        for h in range(N_HEADS_B):
            st_ref[b * N_HEADS_B + h, 0:hd, :] = res[h][L:2 * L]
            y = res[h][0:L]
            mean = jnp.mean(y, axis=-1, keepdims=True)
            d = y - mean
            var = jnp.mean(d * d, axis=-1, keepdims=True)
            ys.append(d * lax.rsqrt(var + GN_EPS))
        yn_ref[b] = jnp.concatenate(ys, axis=1)


def _rwkv_scan(tmat, bsz, s):
    nch = s // CHUNK
    return pl.pallas_call(
        _rwkv_scan_kernel,
        grid=(nch,),
        in_specs=[pl.BlockSpec((bsz, 1, N_HEADS_B, 2 * CHUNK, 2 * HEAD_DIM_B), lambda c: (0, c, 0, 0, 0))],
        out_specs=pl.BlockSpec((bsz, CHUNK, WIDTH_B), lambda c: (0, c, 0)),
        out_shape=jax.ShapeDtypeStruct((bsz, s, WIDTH_B), F32),
        scratch_shapes=[pltpu.VMEM((bsz * N_HEADS_B, 2 * HEAD_DIM_B, HEAD_DIM_B), F32)],
        compiler_params=_params(("arbitrary",)),
        name="rwkv_scan",
    )(tmat)


SUBLANES = 8
TOKEN_TILE = (SUBLANES, D_MODEL // SUBLANES)
assert TOKEN_TILE[1] == LANES


def _token_tile_spec(rows):
    return pl.BlockSpec((rows,) + TOKEN_TILE, lambda i, *_: (i, 0, 0))


def _store_token_tiles(ref, x):
    for c in range(SUBLANES):
        ref[:, c, :] = x[:, c * LANES:(c + 1) * LANES]


def _gather_buffer(rows):
    return pltpu.VMEM((2, rows // SUBLANES, SUBLANES, SUBLANES, LANES), F32)


def _gathered_rows(buf, slot, start, rows):
    t0, nt = start // SUBLANES, rows // SUBLANES
    return jnp.concatenate([buf[slot, t0:t0 + nt, c].reshape(rows, LANES) for c in range(SUBLANES)], axis=1)


MERGE_SUB = 256


def _route(logits, run):
    tm = logits.shape[0]
    lane = _iota(logits.shape, 1)
    big = 4 * LANES
    is_grp = lane < N_GROUPS
    lg = jnp.where(is_grp, logits, NEG_INF)
    mg = jnp.max(lg, axis=-1, keepdims=True)
    gidx = jnp.min(jnp.where(lg == mg, lane, big), axis=-1, keepdims=True)
    pg = 1.0 / jnp.sum(jnp.where(is_grp, jnp.exp(lg - mg), 0.0), axis=-1, keepdims=True)
    in_grp = (lane >= N_GROUPS) & (((lane - N_GROUPS) >> 3) == gidx) & (lane < N_GROUPS + N_EXPERTS)
    le = jnp.where(in_grp, logits, NEG_INF)
    m1 = jnp.max(le, axis=-1, keepdims=True)
    i1 = jnp.min(jnp.where(le == m1, lane, big), axis=-1, keepdims=True)
    le2 = jnp.where(lane == i1, NEG_INF, le)
    m2 = jnp.max(le2, axis=-1, keepdims=True)
    i2 = jnp.min(jnp.where(le2 == m2, lane, big), axis=-1, keepdims=True)
    e2 = jnp.exp(m2 - m1)
    inv = pg / (1.0 + e2)
    hot1 = (lane == i1 - N_GROUPS).astype(F32)
    hot2 = (lane == i2 - N_GROUPS).astype(F32)
    both = hot1 + hot2
    earlier = (_iota((tm, tm), 1) < _iota((tm, tm), 0)).astype(BF16)
    before = jnp.dot(earlier, both.astype(BF16), preferred_element_type=F32) + run
    rank1 = jnp.sum(hot1 * before, axis=-1, keepdims=True)
    rank2 = jnp.sum(hot2 * before, axis=-1, keepdims=True)
    cols = ((i1 - N_GROUPS).astype(F32), (i2 - N_GROUPS).astype(F32), inv, inv * e2, rank1, rank2)
    route = jnp.zeros_like(logits)
    for c, v in enumerate(cols):
        route = jnp.where(lane == c, v, route)
    return route, run + jnp.sum(both, axis=0, keepdims=True)


def _merge_kernel(x_ref, oa_ref, yn_ref, bonus_ref, g_ref, ga_ref, gb_ref, lnw_ref, lnb_ref, wa_ref, wb_ref,
                  wo_ref, gf_ref, wr_ref, br_ref, x1_ref, h2_ref, route_ref, cnt_ref, run_ref):
    @pl.when(pl.program_id(0) == 0)
    def _():
        run_ref[...] = jnp.zeros_like(run_ref)

    subs = [slice(j * MERGE_SUB, (j + 1) * MERGE_SUB) for j in range(x_ref.shape[0] // MERGE_SUB)]
    ob = [(yn_ref[r, :] * lnw_ref[...] + lnb_ref[...] + bonus_ref[r, :]) * g_ref[r, :] for r in subs]
    pa = [jnp.dot(oa_ref[r, :], wa_ref[...], preferred_element_type=F32) for r in subs]
    pb = [_mm(x, wb_ref[...]) for x in ob]
    merged = [jax.nn.sigmoid(ga_ref[r, :]) * a + jax.nn.sigmoid(gb_ref[r, :]) * b for r, a, b in zip(subs, pa, pb)]
    x1 = [x_ref[r, :] + _mm(m, wo_ref[...]) for r, m in zip(subs, merged)]
    h2 = [x * lax.rsqrt(jnp.mean(x * x, axis=-1, keepdims=True) + RMS_EPS) * gf_ref[...] for x in x1]
    logits = []
    for x in h2:
        hi = x.astype(BF16)
        lo = (x - hi.astype(F32)).astype(BF16)
        logits.append(jnp.dot(jnp.concatenate([hi, hi, lo], axis=1), wr_ref[...], preferred_element_type=F32)
                      + br_ref[...])
    run = run_ref[...]
    for r, x, h, lg in zip(subs, x1, h2, logits):
        x1_ref[r, :] = x
        _store_token_tiles(h2_ref.at[r], h)
        route_ref[r, :], run = _route(lg, run)
    run_ref[...] = run
    cnt_ref[...] = run


def _merge(x2, oa, yn, bonus, g, ga, gb, ln_w, ln_b, w_a, w_b, w_o, g_ffn, w_grp, b_grp, w_exp, b_exp,
           tm=2 * MERGE_SUB):
    n = x2.shape[0]
    w_r = jnp.pad(jnp.concatenate([w_grp, w_exp], axis=1), ((0, 0), (0, LANES - N_GROUPS - N_EXPERTS)))
    w_hi = w_r.astype(BF16)
    w_lo = (w_r - w_hi.astype(F32)).astype(BF16)
    w_r = jnp.concatenate([w_hi, w_lo, w_hi], axis=0)
    b_r = jnp.pad(jnp.concatenate([b_grp, b_exp]), (0, LANES - N_GROUPS - N_EXPERTS)).reshape(1, LANES)
    row = lambda wd: pl.BlockSpec((tm, wd), lambda i: (i, 0))
    vec = lambda a: a.reshape(1, -1)
    wa, wb, wo = w_a.astype(BF16), w_b.astype(BF16), w_o.astype(BF16)
    return pl.pallas_call(
        _merge_kernel,
        grid=(n // tm,),
        in_specs=[row(D_MODEL), row(WIDTH_A), row(WIDTH_B), row(WIDTH_B), row(WIDTH_B), row(D_MODEL), row(D_MODEL),
                  _full((1, WIDTH_B)), _full((1, WIDTH_B)), _full(wa.shape), _full(wb.shape), _full(wo.shape),
                  _full((1, D_MODEL)), _full(w_r.shape), _full((1, LANES))],
        out_specs=[row(D_MODEL), _token_tile_spec(tm), row(LANES), _full((1, LANES))],
        out_shape=[jax.ShapeDtypeStruct((n, D_MODEL), F32), jax.ShapeDtypeStruct((n,) + TOKEN_TILE, F32),
                   jax.ShapeDtypeStruct((n, LANES), F32), jax.ShapeDtypeStruct((1, LANES), F32)],
        scratch_shapes=[pltpu.VMEM((1, LANES), F32)],
        compiler_params=_params(("arbitrary",)),
        name="merge",
    )(x2, oa, yn, bonus, g, ga, gb, vec(ln_w), vec(ln_b), wa, wb, wo, vec(g_ffn), w_r, b_r)


def _gather_start(idx_ref, src_hbm, buf, sem, slot, rows):
    def body(t, carry):
        for u in range(SUBLANES):
            pltpu.make_async_copy(src_hbm.at[pl.ds(idx_ref[0, 0, t * SUBLANES + u], 1)],
                                  buf.at[slot, pl.ds(t, 1), :, u], sem.at[slot]).start(priority=u % 2)
        return carry
    lax.fori_loop(0, rows // SUBLANES, body, 0)


def _gather_wait(buf, sem, slot):
    pltpu.make_async_copy(buf.at[slot], buf.at[slot], sem.at[slot]).wait()


def _gather_pipeline(idx_ref, idx_next_ref, src_hbm, buf, sem, rows):
    i = pl.program_id(0)
    slot = i % 2

    @pl.when(i == 0)
    def _():
        _gather_start(idx_ref, src_hbm, buf, sem, 0, rows)

    @pl.when(i + 1 < pl.num_programs(0))
    def _():
        _gather_start(idx_next_ref, src_hbm, buf, sem, 1 - slot, rows)

    _gather_wait(buf, sem, slot)
    return slot


def _dispatch_kernel(pos_ref, h2_ref, zero_hbm, x_hbm, sem, *, tm):
    del zero_hbm

    def body(t, carry):
        for u in range(SUBLANES):
            r = t * SUBLANES + u
            for k in range(TOP_K):
                p = pos_ref[0, 0, k * tm + r]
                pltpu.make_async_copy(h2_ref.at[pl.ds(r, 1)], x_hbm.at[pl.ds(p >> 3, 1), :, p & (SUBLANES - 1)],
                                      sem.at[k]).start(priority=(u + k) % 2)
        return carry
    lax.fori_loop(0, tm // SUBLANES, body, 0)
    for k in range(TOP_K):
        pltpu.make_async_copy(h2_ref, h2_ref, sem.at[k]).wait()


def _dispatch(h2, pos, n_pad, tm=256):
    n = h2.shape[0]
    nt = n // tm
    pos3 = pos.reshape(nt, tm, TOP_K).transpose(0, 2, 1).reshape(nt, 1, TOP_K * tm)
    tiles = (n_pad // SUBLANES, SUBLANES, SUBLANES, LANES)
    any_spec = pl.BlockSpec(memory_space=pl.ANY)
    return pl.pallas_call(
        functools.partial(_dispatch_kernel, tm=tm),
        grid=(nt,),
        in_specs=[pl.BlockSpec((1, 1, TOP_K * tm), lambda i: (i, 0, 0), memory_space=pltpu.SMEM),
                  _token_tile_spec(tm), any_spec],
        out_specs=any_spec,
        out_shape=jax.ShapeDtypeStruct(tiles, F32),
        scratch_shapes=[pltpu.SemaphoreType.DMA((TOP_K,))],
        input_output_aliases={2: 0},
        compiler_params=_params(("arbitrary",)),
        name="dispatch",
    )(pos3, h2, jnp.zeros(tiles, F32))


def _moe_kernel(be_ref, x_ref, w13_ref, w2_ref, y_ref):
    xb = jnp.concatenate([x_ref[:, c].reshape(MOE_BLOCK, LANES) for c in range(SUBLANES)], axis=1)
    a = _mm(xb, w13_ref[0])
    hid = jax.nn.silu(a[:, 0:D_EXPERT]) * a[:, D_EXPERT:2 * D_EXPERT]
    _store_token_tiles(y_ref, _mm(hid, w2_ref[0]))


def _moe(x_pad, blk_expert, w13, w2):
    n_blk = blk_expert.shape[0]
    grid_spec = pltpu.PrefetchScalarGridSpec(
        num_scalar_prefetch=1,
        grid=(n_blk,),
        in_specs=[pl.BlockSpec((MOE_BLOCK // SUBLANES, SUBLANES, SUBLANES, LANES), lambda i, be: (i, 0, 0, 0)),
                  pl.BlockSpec((1, D_MODEL, 2 * D_EXPERT), lambda i, be: (be[i], 0, 0)),
                  pl.BlockSpec((1, D_EXPERT, D_MODEL), lambda i, be: (be[i], 0, 0))],
        out_specs=_token_tile_spec(MOE_BLOCK),
    )
    return pl.pallas_call(
        _moe_kernel,
        grid_spec=grid_spec,
        out_shape=jax.ShapeDtypeStruct((n_blk * MOE_BLOCK,) + TOKEN_TILE, F32),
        compiler_params=_params(("parallel",)),
        name="moe",
    )(blk_expert, x_pad, w13, w2)


def _final_kernel(pos_ref, pos_next_ref, x1_ref, route_ref, g_ref, y_hbm, o_ref, buf, sem, *, tm):
    slot = _gather_pipeline(pos_ref, pos_next_ref, y_hbm, buf, sem, TOP_K * tm)
    x = x1_ref[...]
    for k in range(TOP_K):
        x = x + route_ref[:, TOP_K + k:TOP_K + k + 1] * _gathered_rows(buf, slot, k * tm, tm)
    o_ref[...] = x * lax.rsqrt(jnp.mean(x * x, axis=-1, keepdims=True) + RMS_EPS) * g_ref[...]


def _final(x1, route, y_pad, pos, g_final, tm=256):
    n = x1.shape[0]
    nt = n // tm
    pos3 = pos.reshape(nt, tm, TOP_K).transpose(0, 2, 1).reshape(nt, 1, TOP_K * tm)
    smem_blk = lambda f: pl.BlockSpec((1, 1, TOP_K * tm), f, memory_space=pltpu.SMEM)
    row = lambda wd: pl.BlockSpec((tm, wd), lambda i: (i, 0))
    return pl.pallas_call(
        functools.partial(_final_kernel, tm=tm),
        grid=(nt,),
        in_specs=[smem_blk(lambda i: (i, 0, 0)), smem_blk(lambda i: (jnp.minimum(i + 1, nt - 1), 0, 0)),
                  row(D_MODEL), row(LANES), _full((1, D_MODEL)), pl.BlockSpec(memory_space=pl.ANY)],
        out_specs=row(D_MODEL),
        out_shape=jax.ShapeDtypeStruct((n, D_MODEL), F32),
        scratch_shapes=[_gather_buffer(TOP_K * tm), pltpu.SemaphoreType.DMA((2,))],
        compiler_params=_params(("arbitrary",)),
        name="final",
    )(pos3, pos3, x1, route, g_final.reshape(1, D_MODEL), y_pad)


def _dispatch_plan(route, counts, n_tok):
    expert = route[:, 0:TOP_K].astype(I32)
    rank = route[:, 2 * TOP_K:3 * TOP_K].astype(I32)
    cnt = counts[0, 0:N_EXPERTS].astype(I32)
    padded = (cnt + MOE_BLOCK - 1) // MOE_BLOCK * MOE_BLOCK
    ids = jnp.arange(N_EXPERTS, dtype=I32)
    pad_end = jnp.sum(jnp.where(ids[None, :] <= ids[:, None], padded[None, :], 0), axis=1)
    pad_start = pad_end - padded
    pos = jnp.sum(jnp.where(expert[..., None] == ids, pad_start, 0), axis=-1) + rank
    n_pad = n_tok * TOP_K + N_EXPERTS * MOE_BLOCK
    n_blk = n_pad // MOE_BLOCK
    blk_start = jnp.arange(n_blk, dtype=I32) * MOE_BLOCK
    blk_expert = jnp.minimum(jnp.sum((pad_end[None, :] <= blk_start[:, None]).astype(I32), axis=1), N_EXPERTS - 1)
    return pos, blk_expert, n_pad


def kernel(x, positions, g_mix, w_in, cmp_pe_k, cmp_w1_k, cmp_b1_k, cmp_w2_k, cmp_b2_k, cmp_pe_v, cmp_w1_v, cmp_b1_v, cmp_w2_v, cmp_b2_v, rwkv_mu, rwkv_w0, rwkv_w2, rwkv_a0, rwkv_a2, rwkv_g2, rwkv_k_k, rwkv_k_a, rwkv_r_k, rwkv_ln_w, rwkv_ln_b, w_proj_a, w_proj_b, w_out, g_ffn, w_grp, b_grp, w_exp, b_exp, e_w1, e_w3, e_w2, g_final):
    bsz, s, _ = x.shape
    assert w_in.shape[0] == 1, "single-layer stack"
    n_tok = bsz * s
    x2 = x.reshape(n_tok, D_MODEL)
    half = HEAD_DIM // 2
    freqs = jnp.power(ROPE_THETA, -jnp.arange(half, dtype=F32) / half)
    ang = positions.astype(F32).reshape(n_tok, 1) * freqs
    cs = jnp.tile(jnp.cos(ang), (1, 4))
    sn = jnp.tile(jnp.concatenate([-jnp.sin(ang), jnp.sin(ang)], axis=1), (1, 2))
    q, kc, ksl, kwn, vc, vsl, vwn, ng, prw, ga, gb = _inproj(x2, g_mix[0], cs, sn, w_in[0])
    k_c = _compress(kc, cmp_pe_k[0], cmp_w1_k[0], cmp_b1_k[0], cmp_w2_k[0], cmp_b2_k[0], bsz, s, 0.0)
    v_c = _compress(vc, cmp_pe_v[0], cmp_w1_v[0], cmp_b1_v[0], cmp_w2_v[0], cmp_b2_v[0], bsz, s, 1.0)
    o_a = _nsa(q, k_c, v_c, ksl, vsl, kwn, vwn, ng, bsz, s).reshape(n_tok, WIDTH_A)
    r, lw, k, v, kn, kb, g = _rwkv_prep(prw, rwkv_mu[0], rwkv_w0[0], rwkv_w2[0], rwkv_a0[0], rwkv_a2[0],
                                        rwkv_g2[0], rwkv_k_k[0], rwkv_k_a[0], s)
    bonus, tmat = _rwkv_chunk(r, lw, k, v, kn, kb, rwkv_r_k[0], bsz, s)
    yn = _rwkv_scan(tmat, bsz, s)
    x1, h2, route, counts = _merge(x2, o_a, yn.reshape(n_tok, WIDTH_B), bonus.reshape(n_tok, WIDTH_B), g, ga, gb,
                           rwkv_ln_w[0], rwkv_ln_b[0], w_proj_a[0], w_proj_b[0], w_out[0], g_ffn[0],
                           w_grp[0], b_grp[0], w_exp[0], b_exp[0])
    pos, blk_expert, n_pad = _dispatch_plan(route, counts, n_tok)
    w13 = jnp.concatenate([e_w1[0], e_w3[0]], axis=-1).astype(BF16)
    y_pad = _moe(_dispatch(h2, pos, n_pad), blk_expert, w13, e_w2[0].astype(BF16))
    out = _final(x1, route, y_pad, pos, g_final)
    return out.reshape(bsz, s, D_MODEL)
```

```python
import functools

import numpy as np
import jax
import jax.numpy as jnp
from jax import lax
from jax.experimental import pallas as pl
from jax.experimental.pallas import tpu as pltpu

F32 = jnp.float32
BF16 = jnp.bfloat16
I32 = jnp.int32

D_MODEL = 1024
N_HEADS_A = 8
N_KV_GROUPS = 2
HEADS_PER_GROUP = N_HEADS_A // N_KV_GROUPS
HEAD_DIM = 64
L_CMP = 32
D_CMP = 16
CMP_HIDDEN = 256
L_SLC = 64
N_SELECT = 16
WINDOW = 512
Q_BLOCK = 128
ROPE_THETA = 10000.0
FORCE_SCORE = 1e4
NEG_INF = -1e30
N_HEADS_B = 8
HEAD_DIM_B = 64
W_LORA = 64
A_LORA = 64
G_LORA = 128
GN_EPS = 64e-5
N_GROUPS = 4
EXPERTS_PER_GROUP = 8
N_EXPERTS = N_GROUPS * EXPERTS_PER_GROUP
TOP_K = 2
D_EXPERT = 256
MOE_BLOCK = 256
RMS_EPS = 1e-6
WIDTH_A = N_HEADS_A * HEAD_DIM
KV_WIDTH = N_KV_GROUPS * HEAD_DIM
WIDTH_B = N_HEADS_B * HEAD_DIM_B
C_RWKV = 3 * WIDTH_B + W_LORA + A_LORA + G_LORA

LANES = 128
CHUNK = 64
SUB = 16
KEY_TILE = 1024
BLOCKS_PER_TILE = KEY_TILE // L_SLC
VMEM_LIMIT = 56 * 1024 * 1024

HI = lax.Precision.HIGHEST
LOG2_E = 1.4426950408889634


def _mm(a, b):
    return jnp.dot(a.astype(BF16), b.astype(BF16), preferred_element_type=F32)


def _mm_nt(a, b):
    return lax.dot_general(a.astype(BF16), b.astype(BF16), (((1,), (1,)), ((), ())), preferred_element_type=F32)


def _mm_tn(a, b):
    return lax.dot_general(a.astype(BF16), b.astype(BF16), (((0,), (0,)), ((), ())), preferred_element_type=F32)


def _mm_hi(a, b):
    return jnp.dot(a, b, preferred_element_type=F32, precision=HI)


def _mm_x3(a, b):
    a1 = a.astype(BF16)
    a2 = (a - a1.astype(F32)).astype(BF16)
    b1 = b.astype(BF16)
    b2 = (b - b1.astype(F32)).astype(BF16)
    dot = lambda x, y: jnp.dot(x, y, preferred_element_type=F32)
    return dot(a1, b1) + dot(a1, b2) + dot(a2, b1)


def _bf16_pieces(a):
    a1 = a.astype(BF16)
    r1 = a - a1.astype(F32)
    a2 = r1.astype(BF16)
    return a1, a2, (r1 - a2.astype(F32)).astype(BF16)


def _mm_exact_rhs(a, b):
    bb = b.astype(BF16)
    return sum(jnp.dot(x, bb, preferred_element_type=F32) for x in _bf16_pieces(a))


def _mm_exact_lhs(a, b):
    aa = a.astype(BF16)
    return sum(jnp.dot(aa, x, preferred_element_type=F32) for x in _bf16_pieces(b))


def _iota(shape, dim):
    return lax.broadcasted_iota(I32, shape, dim)


def _params(sem):
    return pltpu.CompilerParams(dimension_semantics=sem, vmem_limit_bytes=VMEM_LIMIT)


def _full(shape):
    nd = len(shape)
    return pl.BlockSpec(shape, lambda *_: (0,) * nd)


Q_PAD = N_HEADS_A * LANES
KV_PAD = N_KV_GROUPS * LANES


def _spread_heads(x, pad_value):
    low = _iota((1, LANES), 1) < HEAD_DIM
    out = []
    for c in range(x.shape[1] // LANES):
        pair = x[:, c * LANES:(c + 1) * LANES]
        out.append(jnp.where(low, pair, pad_value))
        out.append(jnp.where(low, pltpu.roll(pair, HEAD_DIM, 1), pad_value))
    return jnp.concatenate(out, axis=1)


def _inproj_kernel(x_ref, g_ref, cs_ref, sn_ref, wr_ref, wv_ref, wn_ref, ww_ref, wg_ref,
                   q_ref, kc_ref, ksl_ref, kwn_ref, vc_ref, vsl_ref, vwn_ref, ng_ref, prw_ref, ga_ref, gb_ref):
    x = x_ref[...]
    h = (x * lax.rsqrt(jnp.mean(x * x, axis=-1, keepdims=True) + RMS_EPS) * g_ref[...]).astype(BF16)
    pr = jnp.dot(h, wr_ref[...], preferred_element_type=F32)
    cs = cs_ref[...]
    sn = sn_ref[...]
    first_half = (_iota((1, LANES), 1) & (HEAD_DIM // 2)) == 0
    chunks = []
    for c in range(pr.shape[1] // LANES):
        x = pr[:, c * LANES:(c + 1) * LANES]
        swapped = jnp.where(first_half, pltpu.roll(x, LANES - HEAD_DIM // 2, 1), pltpu.roll(x, HEAD_DIM // 2, 1))
        chunks.append(x * cs + swapped * sn)
    ro = jnp.concatenate(chunks, axis=1)
    q_ref[...] = _spread_heads(ro[:, 0:WIDTH_A] * (HEAD_DIM ** -0.5 * LOG2_E), 0.0).astype(BF16)
    o = WIDTH_A
    kc_ref[...] = ro[:, o:o + KV_WIDTH]
    tm = x_ref.shape[0]
    blk = ((pl.program_id(0) * tm + _iota((tm, 1), 0)) >> 6) & (BLOCKS_PER_TILE - 1)
    code = jnp.where((_iota((1, KV_PAD), 1) & (LANES - 1)) == HEAD_DIM + blk, NEG_INF, 0.0)
    ksl_ref[...] = (_spread_heads(ro[:, o + KV_WIDTH:o + 2 * KV_WIDTH], 0.0) + code).astype(BF16)
    kwn_ref[...] = _spread_heads(ro[:, o + 2 * KV_WIDTH:o + 3 * KV_WIDTH], 0.0).astype(BF16)
    v = jnp.dot(h, wv_ref[...], preferred_element_type=F32)
    vc_ref[...] = v[:, 0:KV_WIDTH]
    vsl_ref[...] = _spread_heads(v[:, KV_WIDTH:2 * KV_WIDTH], 1.0).astype(BF16)
    vwn_ref[...] = _spread_heads(v[:, 2 * KV_WIDTH:3 * KV_WIDTH], 1.0).astype(BF16)
    ng_ref[...] = jnp.dot(h, wn_ref[...], preferred_element_type=F32)
    prw_ref[...] = jnp.dot(h, ww_ref[...], preferred_element_type=F32)
    gg = jnp.dot(h, wg_ref[...], preferred_element_type=F32)
    ga_ref[...] = gg[:, 0:D_MODEL]
    gb_ref[...] = gg[:, D_MODEL:2 * D_MODEL]


def _inproj(x2, g_mix, cs, sn, w_in, tm=256):
    n = x2.shape[0]
    o = 0
    cols = {}
    for name, wd in (("q", WIDTH_A), ("kc", KV_WIDTH), ("vc", KV_WIDTH), ("ksl", KV_WIDTH), ("vsl", KV_WIDTH),
                     ("kwn", KV_WIDTH), ("vwn", KV_WIDTH), ("ng", 3 * N_HEADS_A), ("rw", C_RWKV),
                     ("ga", D_MODEL), ("gb", D_MODEL)):
        cols[name] = w_in[:, o:o + wd]
        o += wd
    w_rope = jnp.concatenate([cols["q"], cols["kc"], cols["ksl"], cols["kwn"]], axis=1)
    w_v = jnp.concatenate([cols["vc"], cols["vsl"], cols["vwn"]], axis=1)
    w_ng = jnp.pad(cols["ng"], ((0, 0), (0, LANES - 3 * N_HEADS_A)))
    w_gate = jnp.concatenate([cols["ga"], cols["gb"]], axis=1)
    ws = [w.astype(BF16) for w in (w_rope, w_v, w_ng, cols["rw"], w_gate)]
    row = lambda wd: pl.BlockSpec((tm, wd), lambda i: (i, 0))
    outs = [(Q_PAD, BF16), (KV_WIDTH, F32), (KV_PAD, BF16), (KV_PAD, BF16), (KV_WIDTH, F32), (KV_PAD, BF16),
            (KV_PAD, BF16), (LANES, F32), (C_RWKV, F32), (D_MODEL, F32), (D_MODEL, F32)]
    return pl.pallas_call(
        _inproj_kernel,
        grid=(n // tm,),
        in_specs=[row(D_MODEL), _full((1, D_MODEL)), row(LANES), row(LANES)] + [_full(w.shape) for w in ws],
        out_specs=[row(wd) for wd, _ in outs],
        out_shape=[jax.ShapeDtypeStruct((n, wd), dt) for wd, dt in outs],
        compiler_params=_params(("parallel",)),
        name="inproj",
    )(x2, g_mix.reshape(1, D_MODEL), cs, sn, *ws)


def _compress_kernel(kv_ref, pe_ref, w1_ref, wg_ref, b1_ref, w2_ref, b2_ref, o_ref, *, pad_value):
    nch = o_ref.shape[2]
    z = None
    for l in range(D_CMP):
        part = _mm(kv_ref[0, pl.ds(l, nch, stride=D_CMP), :], wg_ref[0, l])
        z = part if z is None else z + part
    z1 = z[:, 0:CMP_HIDDEN]
    z2 = z[:, CMP_HIDDEN:2 * CMP_HIDDEN]
    z2 = jnp.concatenate([z2[1:], z2[:1]], axis=0)
    pb = _mm(pe_ref[...], w1_ref[...])[0:1] + b1_ref[...]
    hid = jax.nn.gelu(z1 + z2 + pb)
    out = _mm(hid, w2_ref[...]) + b2_ref[...]
    o_ref[0, 0] = jnp.concatenate([out, jnp.full(out.shape, pad_value, F32)], axis=1).astype(o_ref.dtype)


def _compress(kv, pe, w1, b1, w2, b2, bsz, s, pad_value):
    nch = s // D_CMP
    pe8 = jnp.broadcast_to(pe.reshape(1, L_CMP * HEAD_DIM), (8, L_CMP * HEAD_DIM))
    w1r = w1.reshape(2, D_CMP, HEAD_DIM, CMP_HIDDEN)
    wcat = jnp.concatenate([w1r[0], w1r[1]], axis=-1)
    wg = jnp.stack([jnp.pad(wcat, ((0, 0), (g * HEAD_DIM, (N_KV_GROUPS - 1 - g) * HEAD_DIM), (0, 0)))
                    for g in range(N_KV_GROUPS)]).astype(BF16)
    return pl.pallas_call(
        functools.partial(_compress_kernel, pad_value=pad_value),
        grid=(bsz, N_KV_GROUPS),
        in_specs=[pl.BlockSpec((1, s, KV_WIDTH), lambda b, g: (b, 0, 0)),
                  _full(pe8.shape), _full(w1.shape),
                  pl.BlockSpec((1, D_CMP, KV_WIDTH, 2 * CMP_HIDDEN), lambda b, g: (g, 0, 0, 0)),
                  _full((1, CMP_HIDDEN)), _full(w2.shape), _full((1, HEAD_DIM))],
        out_specs=pl.BlockSpec((1, 1, nch, LANES), lambda b, g: (b, g, 0, 0)),
        out_shape=jax.ShapeDtypeStruct((bsz, N_KV_GROUPS, nch, LANES), BF16),
        compiler_params=_params(("parallel", "parallel")),
        name="compress",
    )(kv.reshape(bsz, s, KV_WIDTH), pe8, w1, wg, b1.reshape(1, CMP_HIDDEN), w2, b2.reshape(1, HEAD_DIM))


def _nsa_kernel(q_ref, kc_ref, vc_ref, ksl_ref, vsl_ref, kwn_ref, vwn_ref, ng_ref, ov_ref, o_ref, *, n_top):
    n_cmp = kc_ref.shape[2]
    n_slc = ov_ref.shape[1]
    s0 = pl.program_id(1) * Q_BLOCK
    hq = HEADS_PER_GROUP * Q_BLOCK
    t_q = s0 + _iota((Q_BLOCK, 1), 0)
    sig = jax.nn.sigmoid(ng_ref[0])
    win_start = pl.multiple_of(jnp.maximum(s0 - WINDOW, 0), LANES)
    n_tiles = s0 // KEY_TILE + 1
    groups = range(N_KV_GROUPS)
    lanes = lambda g: slice(g * LANES, (g + 1) * LANES)
    qg = [jnp.concatenate([q_ref[0, :, lanes(g * HEADS_PER_GROUP + h)] for h in range(HEADS_PER_GROUP)], axis=0)
          for g in groups]

    def add_bias(s, bias):
        return (s.reshape(HEADS_PER_GROUP, Q_BLOCK, bias.shape[1]) + bias[None]).reshape(hq, bias.shape[1])

    cmp_bias = jnp.where((_iota((1, n_cmp), 1) * D_CMP + (L_CMP - 1)) <= t_q, 0.0, NEG_INF)
    has_cmp = jnp.concatenate([(t_q >= L_CMP - 1).astype(F32)] * HEADS_PER_GROUP, axis=0)
    pos_w = win_start + _iota((1, WINDOW + Q_BLOCK), 1)
    win_bias = jnp.where((pos_w <= t_q) & (pos_w > t_q - WINDOW), 0.0, NEG_INF)

    s_c = [add_bias(_mm_nt(qg[g], kc_ref[0, g]), cmp_bias) for g in groups]
    s_w = [add_bias(_mm_nt(qg[g], kwn_ref[0, pl.ds(win_start, WINDOW + Q_BLOCK), lanes(g)]), win_bias)
           for g in groups]
    e_c = [jnp.exp2(x - jnp.max(x, axis=-1, keepdims=True)) for x in s_c]
    p_c = [x * (has_cmp / jnp.sum(x, axis=-1, keepdims=True)) for x in e_c]
    e_w = [jnp.exp2(x - jnp.max(x, axis=-1, keepdims=True)) for x in s_w]
    o_c = [_mm(p_c[g], vc_ref[0, g]) for g in groups]
    acc_w = [_mm(e_w[g], vwn_ref[0, pl.ds(win_start, WINDOW + Q_BLOCK), lanes(g)]) for g in groups]
    o_win = [x * (1.0 / x[:, HEAD_DIM:HEAD_DIM + 1]) for x in acc_w]

    imp_t = []
    for g in groups:
        p_sum = p_c[g][0:Q_BLOCK]
        for h in range(1, HEADS_PER_GROUP):
            p_sum = p_sum + p_c[g][h * Q_BLOCK:(h + 1) * Q_BLOCK]
        imp_t.append(_mm_exact_rhs(p_sum, ov_ref[...]).T)
    j = _iota((n_slc, Q_BLOCK), 0)
    cur = (s0 + _iota((n_slc, Q_BLOCK), 1)) >> 6
    forced = (j == 0) | (j == cur) | (j == cur - 1)
    score = [jnp.where(forced, -3e38, jnp.where(j <= cur, x, -1.0)) for x in imp_t]
    sel_t = [forced.astype(F32) for _ in groups]
    for _ in range(n_top - 3):
        for g in groups:
            m = jnp.max(score[g], axis=0, keepdims=True)
            first = jnp.min(jnp.where(score[g] == m, j, n_slc), axis=0, keepdims=True)
            hit = j == first
            sel_t[g] = jnp.where(hit, 1.0, sel_t[g])
            score[g] = jnp.where(hit, -3e38, score[g])
    sel = [1.0 - x.T for x in sel_t]

    lane_q = _iota((Q_BLOCK, LANES), 1)
    in_code = (lane_q >= HEAD_DIM) & (lane_q < HEAD_DIM + BLOCKS_PER_TILE)

    def sel_step(kt, carry, diagonal):
        k0 = pl.multiple_of(kt * KEY_TILE, KEY_TILE)
        shift = (HEAD_DIM - BLOCKS_PER_TILE * kt) & (LANES - 1)
        s = []
        for g in groups:
            code = jnp.where(in_code, pltpu.roll(sel[g], shift, 1), 0.0).astype(BF16)
            qa = qg[g] + jnp.concatenate([code] * HEADS_PER_GROUP, axis=0)
            s.append(_mm_nt(qa, ksl_ref[0, pl.ds(k0, KEY_TILE), lanes(g)]))
        if diagonal:
            causal = jnp.where((k0 + _iota((Q_BLOCK, KEY_TILE), 1)) <= t_q, 0.0, NEG_INF)
            s = [add_bias(x, causal) for x in s]
        m_new = [jnp.maximum(carry[2 * g], jnp.max(s[g], axis=-1, keepdims=True)) for g in groups]
        p = [jnp.exp2(s[g] - m_new[g]) for g in groups]
        pv = [_mm(p[g], vsl_ref[0, pl.ds(k0, KEY_TILE), lanes(g)]) for g in groups]
        out = []
        for g in groups:
            out += [m_new[g], jnp.exp2(carry[2 * g] - m_new[g]) * carry[2 * g + 1] + pv[g]]
        return tuple(out)

    init = (jnp.full((hq, 1), NEG_INF, F32), jnp.zeros((hq, LANES), F32)) * N_KV_GROUPS
    fin = lax.fori_loop(0, n_tiles - 1, lambda kt, c: sel_step(kt, c, False), init)
    fin = sel_step(n_tiles - 1, fin, True)

    heads_out = []
    for g in groups:
        acc_s = fin[2 * g + 1]
        o_s = acc_s * (1.0 / acc_s[:, HEAD_DIM:HEAD_DIM + 1])
        o_w = o_win[g]
        for h in range(HEADS_PER_GROUP):
            hh = g * HEADS_PER_GROUP + h
            r = slice(h * Q_BLOCK, (h + 1) * Q_BLOCK)
            mix = (sig[:, 3 * hh:3 * hh + 1] * o_c[g][r] + sig[:, 3 * hh + 1:3 * hh + 2] * o_s[r]
                   + sig[:, 3 * hh + 2:3 * hh + 3] * o_w[r])
            heads_out.append(mix[:, 0:HEAD_DIM])
    o_ref[0] = jnp.concatenate(heads_out, axis=1).astype(o_ref.dtype)


def _nsa(q, kc, vc, ksl, vsl, kwn, vwn, ng, bsz, s):
    n_cmp = s // D_CMP
    n_slc = s // L_SLC
    ii = np.arange(n_cmp)[:, None]
    jj = np.arange(n_slc)[None, :]
    assert n_slc <= LANES and s % KEY_TILE == 0 and min(N_SELECT, n_slc) >= 3
    overlap = (ii * D_CMP < (jj + 1) * L_SLC) & (ii * D_CMP + L_CMP > jj * L_SLC)
    overlap = jnp.asarray(np.pad(overlap, ((0, 0), (0, LANES - n_slc))), BF16)
    seq = lambda wd: pl.BlockSpec((1, s, wd), lambda b, i: (b, 0, 0))
    blk = lambda wd: pl.BlockSpec((1, Q_BLOCK, wd), lambda b, i: (b, i, 0))
    cmp_spec = pl.BlockSpec((1, N_KV_GROUPS, n_cmp, LANES), lambda b, i: (b, 0, 0, 0))
    r3 = lambda a: a.reshape(bsz, s, a.shape[-1])
    return pl.pallas_call(
        functools.partial(_nsa_kernel, n_top=min(N_SELECT, n_slc)),
        grid=(bsz, s // Q_BLOCK),
        in_specs=[blk(Q_PAD), cmp_spec, cmp_spec, seq(KV_PAD), seq(KV_PAD), seq(KV_PAD), seq(KV_PAD),
                  blk(LANES), _full(overlap.shape)],
        out_specs=blk(WIDTH_A),
        out_shape=jax.ShapeDtypeStruct((bsz, s, WIDTH_A), BF16),
        compiler_params=_params(("parallel", "arbitrary")),
        name="nsa",
    )(r3(q), kc, vc, r3(ksl), r3(vsl), r3(kwn), r3(vwn), r3(ng), overlap)


def _rwkv_prep_kernel(p_ref, pv_ref, mu_ref, w0_ref, w2_ref, a0_ref, a2_ref, g2_ref, kk_ref, ka_ref, bd_ref,
                      r_ref, lw_ref, k_ref, v_ref, kn_ref, kb_ref, g_ref, *, tiles_per_seq):
    p = p_ref[...]
    first = (pl.program_id(0) % tiles_per_seq) == 0
    prev_last = jnp.where(first, 0.0, pv_ref[7:8, :])
    p_prev = jnp.concatenate([prev_last, p[:-1]], axis=0)
    xs = p + (p_prev - p) * mu_ref[...]
    wb = WIDTH_B
    r = xs[:, 0:wb]
    k = xs[:, wb:2 * wb]
    v = xs[:, 2 * wb:3 * wb]
    o = 3 * wb
    xw = xs[:, o:o + W_LORA]
    xa = xs[:, o + W_LORA:o + W_LORA + A_LORA]
    xg = xs[:, o + W_LORA + A_LORA:o + W_LORA + A_LORA + G_LORA]
    z = w0_ref[...] + _mm_hi(jnp.tanh(xw), w2_ref[...])
    w_log = -(jnp.maximum(-z, 0.0) + jnp.log(1.0 + jnp.exp(-jnp.abs(z)))) - 0.5
    a = jax.nn.sigmoid(a0_ref[...] + _mm_hi(xa, a2_ref[...]))
    kk = k * kk_ref[...]
    ss = _mm_exact_rhs(kk * kk, bd_ref[...])
    kn = kk * (1.0 / jnp.maximum(jnp.sqrt(ss), 1e-12))
    r_ref[...] = r
    lw_ref[...] = -jnp.exp(w_log)
    k_ref[...] = k * (1.0 + (a - 1.0) * ka_ref[...])
    v_ref[...] = v
    kn_ref[...] = kn
    kb_ref[...] = kn * a
    g_ref[...] = _mm(jax.nn.sigmoid(xg), g2_ref[...])


def _rwkv_prep(prw, mu, w0, w2, a0, a2, g2, k_k, k_a, s, tm=256):
    n = prw.shape[0]
    hid = np.arange(WIDTH_B) // HEAD_DIM_B
    bd = jnp.asarray(hid[:, None] == hid[None, :], BF16)
    row = lambda wd: pl.BlockSpec((tm, wd), lambda i: (i, 0))
    vec = lambda a: a.reshape(1, -1)
    return pl.pallas_call(
        functools.partial(_rwkv_prep_kernel, tiles_per_seq=s // tm),
        grid=(n // tm,),
        in_specs=[row(C_RWKV), pl.BlockSpec((8, C_RWKV), lambda i: (jnp.maximum(i * (tm // 8) - 1, 0), 0)),
                  _full((1, C_RWKV)), _full((1, WIDTH_B)), _full(w2.shape), _full((1, WIDTH_B)), _full(a2.shape),
                  _full(g2.shape), _full((1, WIDTH_B)), _full((1, WIDTH_B)), _full(bd.shape)],
        out_specs=[row(WIDTH_B)] * 7,
        out_shape=[jax.ShapeDtypeStruct((n, WIDTH_B), F32)] * 7,
        compiler_params=_params(("parallel",)),
        name="rwkv_prep",
    )(prw, prw, vec(mu), vec(w0), w2, vec(a0), a2, g2, vec(k_k), vec(k_a), bd)


def _rwkv_chunk_kernel(r_ref, lw_ref, k_ref, v_ref, kn_ref, kb_ref, rk_ref,
                       bonus_ref, tm_ref, *, n_sub):
    L = CHUNK
    hd = HEAD_DIM_B
    ti = _iota((L, L), 0)
    si = _iota((L, L), 1)
    strict = si < ti
    incl = si <= ti
    same_sub = (ti // SUB) == (si // SUB)
    eye = (ti == si).astype(F32)
    tri = incl.astype(BF16)
    at, bt, kt, rt, bl, kl, vv, rkr, p_tot = [], [], [], [], [], [], [], [], []
    for j in range(n_sub):
        rows = slice(j * L, (j + 1) * L)
        lw = lw_ref[0, rows, :]
        cs = _mm_exact_lhs(tri, lw)
        p_inv = jnp.exp(-cs)
        p_end = jnp.exp(cs[L - 1:L, :] - cs)
        r = r_ref[0, rows, :]
        k = k_ref[0, rows, :]
        kb = kb_ref[0, rows, :]
        at.append(-kn_ref[0, rows, :] * jnp.exp(cs - lw))
        bt.append(kb * p_inv)
        kt.append(k * p_inv)
        rt.append(r * jnp.exp(cs))
        bl.append(kb * p_end)
        kl.append(k * p_end)
        vv.append(v_ref[0, rows, :])
        rkr.append(r * k * rk_ref[...])
        p_tot.append(jnp.exp(cs[L - 1:L, :]))
    units = [(j, h) for j in range(n_sub) for h in range(N_HEADS_B)]
    sl = lambda arr, u: arr[u[0]][:, u[1] * hd:(u[1] + 1) * hd]
    aa = [_mm_nt(jnp.concatenate([sl(at, u), sl(rt, u)], axis=0),
                 jnp.concatenate([sl(bt, u), sl(kt, u)], axis=0)) for u in units]
    a_ab = [jnp.where(strict, x[0:L, 0:L], 0.0) for x in aa]
    a_low = [jnp.concatenate([jnp.where(strict, x[0:L, L:2 * L], 0.0),
                              jnp.where(incl, x[L:2 * L, L:2 * L], 0.0)], axis=0) for x in aa]
    a_rb = [jnp.where(incl, x[L:2 * L, 0:L], 0.0) for x in aa]
    av = [_mm(x, sl(vv, u)) for x, u in zip(a_low, units)]
    kv = [_mm_tn(sl(kl, u), sl(vv, u)) for u in units]
    pw = [jnp.where(same_sub, x, 0.0) for x in a_ab]
    t = [eye + x for x in pw]
    for _ in range(3):
        pw = [_mm(x, x) for x in pw]
        t = [x + _mm(x, y) for x, y in zip(t, pw)]
    width = SUB
    while width < L:
        m = ((ti // width) == (si // width) + 1) & ((ti // (2 * width)) == (si // (2 * width)))
        ot = [_mm(jnp.where(m, x, 0.0), y) for x, y in zip(a_ab, t)]
        t = [x + _mm(x, y) for x, y in zip(t, ot)]
        width *= 2
    wu = [_mm(x, jnp.concatenate([sl(at, u), y[0:L]], axis=1)) for x, y, u in zip(t, av, units)]
    qy = [jnp.concatenate([sl(rt, u), y[L:2 * L]], axis=1) + _mm(x, w)
          for x, y, w, u in zip(a_rb, av, wu, units)]
    gh = [_mm_tn(sl(bl, u), w) + jnp.concatenate([eye * sl(p_tot, u), y], axis=1)
          for w, y, u in zip(wu, kv, units)]
    bonus = [jnp.sum(sl(rkr, u), axis=-1, keepdims=True) * sl(vv, u) for u in units]
    for j in range(n_sub):
        bonus_ref[0, j * L:(j + 1) * L, :] = jnp.concatenate(bonus[j * N_HEADS_B:(j + 1) * N_HEADS_B], axis=1)
    for x, y, (j, h) in zip(qy, gh, units):
        tm_ref[0, j, h, 0:L, :] = x
        tm_ref[0, j, h, L:2 * L, :] = y


def _rwkv_chunk(r, lw, k, v, kn, kb, r_k, bsz, s, n_sub=4):
    nch = s // CHUNK
    blk = pl.BlockSpec((1, n_sub * CHUNK, WIDTH_B), lambda b, c: (b, c, 0))
    mat = pl.BlockSpec((1, n_sub, N_HEADS_B, 2 * CHUNK, 2 * HEAD_DIM_B), lambda b, c: (b, c, 0, 0, 0))
    r3 = lambda a: a.reshape(bsz, s, WIDTH_B)
    return pl.pallas_call(
        functools.partial(_rwkv_chunk_kernel, n_sub=n_sub),
        grid=(bsz, nch // n_sub),
        in_specs=[blk] * 6 + [_full((1, WIDTH_B))],
        out_specs=[blk, mat],
        out_shape=[jax.ShapeDtypeStruct((bsz, s, WIDTH_B), F32),
                   jax.ShapeDtypeStruct((bsz, nch, N_HEADS_B, 2 * CHUNK, 2 * HEAD_DIM_B), F32)],
        compiler_params=_params(("parallel", "parallel")),
        name="rwkv_chunk",
    )(r3(r), r3(lw), r3(k), r3(v), r3(kn), r3(kb), r_k.reshape(1, WIDTH_B))


def _rwkv_scan_kernel(tm_ref, yn_ref, st_ref):
    L = CHUNK
    hd = HEAD_DIM_B

    @pl.when(pl.program_id(0) == 0)
    def _():
        eye = (_iota((hd, hd), 0) == _iota((hd, hd), 1)).astype(F32)
        for i in range(st_ref.shape[0]):
            st_ref[i] = jnp.concatenate([jnp.zeros((hd, hd), F32), eye], axis=0)

    for b in range(tm_ref.shape[0]):
        res = [_mm_x3(tm_ref[b, 0, h], st_ref[b * N_HEADS_B + h]) for h in range(N_HEADS_B)]
        ys = []
        for h in range(N_HEADS_B):
            st_ref[b * N_HEADS_B + h, 0:hd, :] = res[h][L:2 * L]
            y = res[h][0:L]
            mean = jnp.mean(y, axis=-1, keepdims=True)
            d = y - mean
            var = jnp.mean(d * d, axis=-1, keepdims=True)
            ys.append(d * lax.rsqrt(var + GN_EPS))
        yn_ref[b] = jnp.concatenate(ys, axis=1)


def _rwkv_scan(tmat, bsz, s):
    nch = s // CHUNK
    return pl.pallas_call(
        _rwkv_scan_kernel,
        grid=(nch,),
        in_specs=[pl.BlockSpec((bsz, 1, N_HEADS_B, 2 * CHUNK, 2 * HEAD_DIM_B), lambda c: (0, c, 0, 0, 0))],
        out_specs=pl.BlockSpec((bsz, CHUNK, WIDTH_B), lambda c: (0, c, 0)),
        out_shape=jax.ShapeDtypeStruct((bsz, s, WIDTH_B), F32),
        scratch_shapes=[pltpu.VMEM((bsz * N_HEADS_B, 2 * HEAD_DIM_B, HEAD_DIM_B), F32)],
        compiler_params=_params(("arbitrary",)),
        name="rwkv_scan",
    )(tmat)


SUBLANES = 8
TOKEN_TILE = (SUBLANES, D_MODEL // SUBLANES)
assert TOKEN_TILE[1] == LANES


def _token_tile_spec(rows):
    return pl.BlockSpec((rows,) + TOKEN_TILE, lambda i, *_: (i, 0, 0))


def _store_token_tiles(ref, x):
    for c in range(SUBLANES):
        ref[:, c, :] = x[:, c * LANES:(c + 1) * LANES]


def _gather_buffer(rows):
    return pltpu.VMEM((2, rows // SUBLANES, SUBLANES, SUBLANES, LANES), F32)


def _gathered_rows(buf, slot, start, rows):
    t0, nt = start // SUBLANES, rows // SUBLANES
    return jnp.concatenate([buf[slot, t0:t0 + nt, c].reshape(rows, LANES) for c in range(SUBLANES)], axis=1)


MERGE_SUB = 256


def _route(logits, run):
    tm = logits.shape[0]
    lane = _iota(logits.shape, 1)
    big = 4 * LANES
    is_grp = lane < N_GROUPS
    lg = jnp.where(is_grp, logits, NEG_INF)
    mg = jnp.max(lg, axis=-1, keepdims=True)
    gidx = jnp.min(jnp.where(lg == mg, lane, big), axis=-1, keepdims=True)
    pg = 1.0 / jnp.sum(jnp.where(is_grp, jnp.exp(lg - mg), 0.0), axis=-1, keepdims=True)
    in_grp = (lane >= N_GROUPS) & (((lane - N_GROUPS) >> 3) == gidx) & (lane < N_GROUPS + N_EXPERTS)
    le = jnp.where(in_grp, logits, NEG_INF)
    m1 = jnp.max(le, axis=-1, keepdims=True)
    i1 = jnp.min(jnp.where(le == m1, lane, big), axis=-1, keepdims=True)
    le2 = jnp.where(lane == i1, NEG_INF, le)
    m2 = jnp.max(le2, axis=-1, keepdims=True)
    i2 = jnp.min(jnp.where(le2 == m2, lane, big), axis=-1, keepdims=True)
    e2 = jnp.exp(m2 - m1)
    inv = pg / (1.0 + e2)
    hot1 = (lane == i1 - N_GROUPS).astype(F32)
    hot2 = (lane == i2 - N_GROUPS).astype(F32)
    both = hot1 + hot2
    earlier = (_iota((tm, tm), 1) < _iota((tm, tm), 0)).astype(BF16)
    before = jnp.dot(earlier, both.astype(BF16), preferred_element_type=F32) + run
    rank1 = jnp.sum(hot1 * before, axis=-1, keepdims=True)
    rank2 = jnp.sum(hot2 * before, axis=-1, keepdims=True)
    cols = ((i1 - N_GROUPS).astype(F32), (i2 - N_GROUPS).astype(F32), inv, inv * e2, rank1, rank2)
    route = jnp.zeros_like(logits)
    for c, v in enumerate(cols):
        route = jnp.where(lane == c, v, route)
    return route, run + jnp.sum(both, axis=0, keepdims=True)


def _merge_kernel(x_ref, oa_ref, yn_ref, bonus_ref, g_ref, ga_ref, gb_ref, lnw_ref, lnb_ref, wa_ref, wb_ref,
                  wo_ref, gf_ref, wr_ref, br_ref, x1_ref, h2_ref, route_ref, cnt_ref, run_ref):
    @pl.when(pl.program_id(0) == 0)
    def _():
        run_ref[...] = jnp.zeros_like(run_ref)

    subs = [slice(j * MERGE_SUB, (j + 1) * MERGE_SUB) for j in range(x_ref.shape[0] // MERGE_SUB)]
    ob = [(yn_ref[r, :] * lnw_ref[...] + lnb_ref[...] + bonus_ref[r, :]) * g_ref[r, :] for r in subs]
    pa = [jnp.dot(oa_ref[r, :], wa_ref[...], preferred_element_type=F32) for r in subs]
    pb = [_mm(x, wb_ref[...]) for x in ob]
    merged = [jax.nn.sigmoid(ga_ref[r, :]) * a + jax.nn.sigmoid(gb_ref[r, :]) * b for r, a, b in zip(subs, pa, pb)]
    x1 = [x_ref[r, :] + _mm(m, wo_ref[...]) for r, m in zip(subs, merged)]
    h2 = [x * lax.rsqrt(jnp.mean(x * x, axis=-1, keepdims=True) + RMS_EPS) * gf_ref[...] for x in x1]
    logits = []
    for x in h2:
        hi = x.astype(BF16)
        lo = (x - hi.astype(F32)).astype(BF16)
        logits.append(jnp.dot(jnp.concatenate([hi, hi, lo], axis=1), wr_ref[...], preferred_element_type=F32)
                      + br_ref[...])
    run = run_ref[...]
    for r, x, h, lg in zip(subs, x1, h2, logits):
        x1_ref[r, :] = x
        _store_token_tiles(h2_ref.at[r], h)
        route_ref[r, :], run = _route(lg, run)
    run_ref[...] = run
    cnt_ref[...] = run


def _merge(x2, oa, yn, bonus, g, ga, gb, ln_w, ln_b, w_a, w_b, w_o, g_ffn, w_grp, b_grp, w_exp, b_exp,
           tm=2 * MERGE_SUB):
    n = x2.shape[0]
    w_r = jnp.pad(jnp.concatenate([w_grp, w_exp], axis=1), ((0, 0), (0, LANES - N_GROUPS - N_EXPERTS)))
    w_hi = w_r.astype(BF16)
    w_lo = (w_r - w_hi.astype(F32)).astype(BF16)
    w_r = jnp.concatenate([w_hi, w_lo, w_hi], axis=0)
    b_r = jnp.pad(jnp.concatenate([b_grp, b_exp]), (0, LANES - N_GROUPS - N_EXPERTS)).reshape(1, LANES)
    row = lambda wd: pl.BlockSpec((tm, wd), lambda i: (i, 0))
    vec = lambda a: a.reshape(1, -1)
    wa, wb, wo = w_a.astype(BF16), w_b.astype(BF16), w_o.astype(BF16)
    return pl.pallas_call(
        _merge_kernel,
        grid=(n // tm,),
        in_specs=[row(D_MODEL), row(WIDTH_A), row(WIDTH_B), row(WIDTH_B), row(WIDTH_B), row(D_MODEL), row(D_MODEL),
                  _full((1, WIDTH_B)), _full((1, WIDTH_B)), _full(wa.shape), _full(wb.shape), _full(wo.shape),
                  _full((1, D_MODEL)), _full(w_r.shape), _full((1, LANES))],
        out_specs=[row(D_MODEL), _token_tile_spec(tm), row(LANES), _full((1, LANES))],
        out_shape=[jax.ShapeDtypeStruct((n, D_MODEL), F32), jax.ShapeDtypeStruct((n,) + TOKEN_TILE, F32),
                   jax.ShapeDtypeStruct((n, LANES), F32), jax.ShapeDtypeStruct((1, LANES), F32)],
        scratch_shapes=[pltpu.VMEM((1, LANES), F32)],
        compiler_params=_params(("arbitrary",)),
        name="merge",
    )(x2, oa, yn, bonus, g, ga, gb, vec(ln_w), vec(ln_b), wa, wb, wo, vec(g_ffn), w_r, b_r)


def _gather_start(idx_ref, src_hbm, buf, sem, slot, rows):
    def body(t, carry):
        for u in range(SUBLANES):
            pltpu.make_async_copy(src_hbm.at[pl.ds(idx_ref[0, 0, t * SUBLANES + u], 1)],
                                  buf.at[slot, pl.ds(t, 1), :, u], sem.at[slot]).start(priority=u % 2)
        return carry
    lax.fori_loop(0, rows // SUBLANES, body, 0)


def _gather_wait(buf, sem, slot):
    pltpu.make_async_copy(buf.at[slot], buf.at[slot], sem.at[slot]).wait()


def _gather_pipeline(idx_ref, idx_next_ref, src_hbm, buf, sem, rows):
    i = pl.program_id(0)
    slot = i % 2

    @pl.when(i == 0)
    def _():
        _gather_start(idx_ref, src_hbm, buf, sem, 0, rows)

    @pl.when(i + 1 < pl.num_programs(0))
    def _():
        _gather_start(idx_next_ref, src_hbm, buf, sem, 1 - slot, rows)

    _gather_wait(buf, sem, slot)
    return slot


def _dispatch_kernel(end_ref, pos_ref, h2_ref, x_hbm, zeros, sem, zsem, *, tm):
    tile_rows = MOE_BLOCK // SUBLANES

    @pl.when(pl.program_id(0) == 0)
    def _():
        zeros[...] = jnp.zeros_like(zeros)

        def fill(block):
            cp = pltpu.make_async_copy(zeros, x_hbm.at[pl.ds(block * tile_rows, tile_rows)], zsem.at[0])
            cp.start()
            cp.wait()

        for e in range(N_EXPERTS):
            start = end_ref[e - 1] if e else 0

            @pl.when(end_ref[e] > start)
            def _():
                fill(end_ref[e] // MOE_BLOCK - 1)

        def tail(block, carry):
            fill(block)
            return carry
        lax.fori_loop(end_ref[N_EXPERTS - 1] // MOE_BLOCK, x_hbm.shape[0] // tile_rows, tail, 0)

    def body(t, carry):
        for u in range(SUBLANES):
            r = t * SUBLANES + u
            for k in range(TOP_K):
                p = pos_ref[0, 0, k * tm + r]
                pltpu.make_async_copy(h2_ref.at[pl.ds(r, 1)], x_hbm.at[pl.ds(p >> 3, 1), :, p & (SUBLANES - 1)],
                                      sem.at[k]).start(priority=(u + k) % 2)
        return carry
    lax.fori_loop(0, tm // SUBLANES, body, 0)
    for k in range(TOP_K):
        pltpu.make_async_copy(h2_ref, h2_ref, sem.at[k]).wait()


def _dispatch(h2, pos, pad_end, n_pad, tm=256):
    n = h2.shape[0]
    nt = n // tm
    pos3 = pos.reshape(nt, tm, TOP_K).transpose(0, 2, 1).reshape(nt, 1, TOP_K * tm)
    tiles = (n_pad // SUBLANES, SUBLANES, SUBLANES, LANES)
    grid_spec = pltpu.PrefetchScalarGridSpec(
        num_scalar_prefetch=1,
        grid=(nt,),
        in_specs=[pl.BlockSpec((1, 1, TOP_K * tm), lambda i, pe: (i, 0, 0), memory_space=pltpu.SMEM),
                  _token_tile_spec(tm)],
        out_specs=pl.BlockSpec(memory_space=pl.ANY),
        scratch_shapes=[pltpu.VMEM((MOE_BLOCK // SUBLANES, SUBLANES, SUBLANES, LANES), F32),
                        pltpu.SemaphoreType.DMA((TOP_K,)), pltpu.SemaphoreType.DMA((1,))],
    )
    return pl.pallas_call(
        functools.partial(_dispatch_kernel, tm=tm),
        grid_spec=grid_spec,
        out_shape=jax.ShapeDtypeStruct(tiles, F32),
        compiler_params=_params(("arbitrary",)),
        name="dispatch",
    )(pad_end, pos3, h2)


def _moe_kernel(be_ref, x_ref, w13_ref, w2_ref, y_ref):
    xb = jnp.concatenate([x_ref[:, c].reshape(MOE_BLOCK, LANES) for c in range(SUBLANES)], axis=1)
    a = _mm(xb, w13_ref[0])
    hid = jax.nn.silu(a[:, 0:D_EXPERT]) * a[:, D_EXPERT:2 * D_EXPERT]
    _store_token_tiles(y_ref, _mm(hid, w2_ref[0]))


def _moe(x_pad, blk_expert, w13, w2):
    n_blk = blk_expert.shape[0]
    grid_spec = pltpu.PrefetchScalarGridSpec(
        num_scalar_prefetch=1,
        grid=(n_blk,),
        in_specs=[pl.BlockSpec((MOE_BLOCK // SUBLANES, SUBLANES, SUBLANES, LANES), lambda i, be: (i, 0, 0, 0)),
                  pl.BlockSpec((1, D_MODEL, 2 * D_EXPERT), lambda i, be: (be[i], 0, 0)),
                  pl.BlockSpec((1, D_EXPERT, D_MODEL), lambda i, be: (be[i], 0, 0))],
        out_specs=_token_tile_spec(MOE_BLOCK),
    )
    return pl.pallas_call(
        _moe_kernel,
        grid_spec=grid_spec,
        out_shape=jax.ShapeDtypeStruct((n_blk * MOE_BLOCK,) + TOKEN_TILE, F32),
        compiler_params=_params(("parallel",)),
        name="moe",
    )(blk_expert, x_pad, w13, w2)


def _final_kernel(pos_ref, pos_next_ref, x1_ref, route_ref, g_ref, y_hbm, o_ref, buf, sem, *, tm):
    slot = _gather_pipeline(pos_ref, pos_next_ref, y_hbm, buf, sem, TOP_K * tm)
    x = x1_ref[...]
    for k in range(TOP_K):
        x = x + route_ref[:, TOP_K + k:TOP_K + k + 1] * _gathered_rows(buf, slot, k * tm, tm)
    o_ref[...] = x * lax.rsqrt(jnp.mean(x * x, axis=-1, keepdims=True) + RMS_EPS) * g_ref[...]


def _final(x1, route, y_pad, pos, g_final, tm=256):
    n = x1.shape[0]
    nt = n // tm
    pos3 = pos.reshape(nt, tm, TOP_K).transpose(0, 2, 1).reshape(nt, 1, TOP_K * tm)
    smem_blk = lambda f: pl.BlockSpec((1, 1, TOP_K * tm), f, memory_space=pltpu.SMEM)
    row = lambda wd: pl.BlockSpec((tm, wd), lambda i: (i, 0))
    return pl.pallas_call(
        functools.partial(_final_kernel, tm=tm),
        grid=(nt,),
        in_specs=[smem_blk(lambda i: (i, 0, 0)), smem_blk(lambda i: (jnp.minimum(i + 1, nt - 1), 0, 0)),
                  row(D_MODEL), row(LANES), _full((1, D_MODEL)), pl.BlockSpec(memory_space=pl.ANY)],
        out_specs=row(D_MODEL),
        out_shape=jax.ShapeDtypeStruct((n, D_MODEL), F32),
        scratch_shapes=[_gather_buffer(TOP_K * tm), pltpu.SemaphoreType.DMA((2,))],
        compiler_params=_params(("arbitrary",)),
        name="final",
    )(pos3, pos3, x1, route, g_final.reshape(1, D_MODEL), y_pad)


def _dispatch_plan(route, counts, n_tok):
    expert = route[:, 0:TOP_K].astype(I32)
    rank = route[:, 2 * TOP_K:3 * TOP_K].astype(I32)
    cnt = counts[0, 0:N_EXPERTS].astype(I32)
    padded = (cnt + MOE_BLOCK - 1) // MOE_BLOCK * MOE_BLOCK
    ids = jnp.arange(N_EXPERTS, dtype=I32)
    pad_end = jnp.sum(jnp.where(ids[None, :] <= ids[:, None], padded[None, :], 0), axis=1)
    pad_start = pad_end - padded
    pos = jnp.sum(jnp.where(expert[..., None] == ids, pad_start, 0), axis=-1) + rank
    n_pad = n_tok * TOP_K + N_EXPERTS * MOE_BLOCK
    n_blk = n_pad // MOE_BLOCK
    blk_start = jnp.arange(n_blk, dtype=I32) * MOE_BLOCK
    blk_expert = jnp.minimum(jnp.sum((pad_end[None, :] <= blk_start[:, None]).astype(I32), axis=1), N_EXPERTS - 1)
    return pos, blk_expert, pad_end, n_pad


def kernel(x, positions, g_mix, w_in, cmp_pe_k, cmp_w1_k, cmp_b1_k, cmp_w2_k, cmp_b2_k, cmp_pe_v, cmp_w1_v, cmp_b1_v, cmp_w2_v, cmp_b2_v, rwkv_mu, rwkv_w0, rwkv_w2, rwkv_a0, rwkv_a2, rwkv_g2, rwkv_k_k, rwkv_k_a, rwkv_r_k, rwkv_ln_w, rwkv_ln_b, w_proj_a, w_proj_b, w_out, g_ffn, w_grp, b_grp, w_exp, b_exp, e_w1, e_w3, e_w2, g_final):
    bsz, s, _ = x.shape
    assert w_in.shape[0] == 1, "single-layer stack"
    n_tok = bsz * s
    x2 = x.reshape(n_tok, D_MODEL)
    half = HEAD_DIM // 2
    freqs = jnp.power(ROPE_THETA, -jnp.arange(half, dtype=F32) / half)
    ang = positions.astype(F32).reshape(n_tok, 1) * freqs
    cs = jnp.tile(jnp.cos(ang), (1, 4))
    sn = jnp.tile(jnp.concatenate([-jnp.sin(ang), jnp.sin(ang)], axis=1), (1, 2))
    q, kc, ksl, kwn, vc, vsl, vwn, ng, prw, ga, gb = _inproj(x2, g_mix[0], cs, sn, w_in[0])
    k_c = _compress(kc, cmp_pe_k[0], cmp_w1_k[0], cmp_b1_k[0], cmp_w2_k[0], cmp_b2_k[0], bsz, s, 0.0)
    v_c = _compress(vc, cmp_pe_v[0], cmp_w1_v[0], cmp_b1_v[0], cmp_w2_v[0], cmp_b2_v[0], bsz, s, 1.0)
    o_a = _nsa(q, k_c, v_c, ksl, vsl, kwn, vwn, ng, bsz, s).reshape(n_tok, WIDTH_A)
    r, lw, k, v, kn, kb, g = _rwkv_prep(prw, rwkv_mu[0], rwkv_w0[0], rwkv_w2[0], rwkv_a0[0], rwkv_a2[0],
                                        rwkv_g2[0], rwkv_k_k[0], rwkv_k_a[0], s)
    bonus, tmat = _rwkv_chunk(r, lw, k, v, kn, kb, rwkv_r_k[0], bsz, s)
    yn = _rwkv_scan(tmat, bsz, s)
    x1, h2, route, counts = _merge(x2, o_a, yn.reshape(n_tok, WIDTH_B), bonus.reshape(n_tok, WIDTH_B), g, ga, gb,
                           rwkv_ln_w[0], rwkv_ln_b[0], w_proj_a[0], w_proj_b[0], w_out[0], g_ffn[0],
                           w_grp[0], b_grp[0], w_exp[0], b_exp[0])
    pos, blk_expert, pad_end, n_pad = _dispatch_plan(route, counts, n_tok)
    w13 = jnp.concatenate([e_w1[0], e_w3[0]], axis=-1).astype(BF16)
    y_pad = _moe(_dispatch(h2, pos, pad_end, n_pad), blk_expert, w13, e_w2[0].astype(BF16))
    out = _final(x1, route, y_pad, pos, g_final)
    return out.reshape(bsz, s, D_MODEL)
```

```python
import functools

import numpy as np
import jax
import jax.numpy as jnp
from jax import lax
from jax.experimental import pallas as pl
from jax.experimental.pallas import tpu as pltpu

F32 = jnp.float32
BF16 = jnp.bfloat16
I32 = jnp.int32

D_MODEL = 1024
N_HEADS_A = 8
N_KV_GROUPS = 2
HEADS_PER_GROUP = N_HEADS_A // N_KV_GROUPS
HEAD_DIM = 64
L_CMP = 32
D_CMP = 16
CMP_HIDDEN = 256
L_SLC = 64
N_SELECT = 16
WINDOW = 512
Q_BLOCK = 128
ROPE_THETA = 10000.0
FORCE_SCORE = 1e4
NEG_INF = -1e30
N_HEADS_B = 8
HEAD_DIM_B = 64
W_LORA = 64
A_LORA = 64
G_LORA = 128
GN_EPS = 64e-5
N_GROUPS = 4
EXPERTS_PER_GROUP = 8
N_EXPERTS = N_GROUPS * EXPERTS_PER_GROUP
TOP_K = 2
D_EXPERT = 256
MOE_BLOCK = 256
RMS_EPS = 1e-6
WIDTH_A = N_HEADS_A * HEAD_DIM
KV_WIDTH = N_KV_GROUPS * HEAD_DIM
WIDTH_B = N_HEADS_B * HEAD_DIM_B
C_RWKV = 3 * WIDTH_B + W_LORA + A_LORA + G_LORA

LANES = 128
CHUNK = 64
SUB = 16
KEY_TILE = 1024
BLOCKS_PER_TILE = KEY_TILE // L_SLC
VMEM_LIMIT = 56 * 1024 * 1024

HI = lax.Precision.HIGHEST
LOG2_E = 1.4426950408889634


def _mm(a, b):
    return jnp.dot(a.astype(BF16), b.astype(BF16), preferred_element_type=F32)


def _mm_nt(a, b):
    return lax.dot_general(a.astype(BF16), b.astype(BF16), (((1,), (1,)), ((), ())), preferred_element_type=F32)


def _mm_tn(a, b):
    return lax.dot_general(a.astype(BF16), b.astype(BF16), (((0,), (0,)), ((), ())), preferred_element_type=F32)


def _mm_hi(a, b):
    return jnp.dot(a, b, preferred_element_type=F32, precision=HI)


def _mm_x3(a, b):
    a1 = a.astype(BF16)
    a2 = (a - a1.astype(F32)).astype(BF16)
    b1 = b.astype(BF16)
    b2 = (b - b1.astype(F32)).astype(BF16)
    dot = lambda x, y: jnp.dot(x, y, preferred_element_type=F32)
    return dot(a1, b1) + dot(a1, b2) + dot(a2, b1)


def _bf16_pieces(a):
    a1 = a.astype(BF16)
    r1 = a - a1.astype(F32)
    a2 = r1.astype(BF16)
    return a1, a2, (r1 - a2.astype(F32)).astype(BF16)


def _mm_exact_rhs(a, b):
    bb = b.astype(BF16)
    return sum(jnp.dot(x, bb, preferred_element_type=F32) for x in _bf16_pieces(a))


def _mm_exact_lhs(a, b):
    aa = a.astype(BF16)
    return sum(jnp.dot(aa, x, preferred_element_type=F32) for x in _bf16_pieces(b))


def _iota(shape, dim):
    return lax.broadcasted_iota(I32, shape, dim)


def _params(sem):
    return pltpu.CompilerParams(dimension_semantics=sem, vmem_limit_bytes=VMEM_LIMIT)


def _full(shape):
    nd = len(shape)
    return pl.BlockSpec(shape, lambda *_: (0,) * nd)


Q_PAD = N_HEADS_A * LANES
KV_PAD = N_KV_GROUPS * LANES


def _spread_heads(x, pad_value):
    low = _iota((1, LANES), 1) < HEAD_DIM
    out = []
    for c in range(x.shape[1] // LANES):
        pair = x[:, c * LANES:(c + 1) * LANES]
        out.append(jnp.where(low, pair, pad_value))
        out.append(jnp.where(low, pltpu.roll(pair, HEAD_DIM, 1), pad_value))
    return jnp.concatenate(out, axis=1)


def _inproj_kernel(x_ref, g_ref, cs_ref, sn_ref, wr_ref, wv_ref, wn_ref, ww_ref, wg_ref,
                   q_ref, kc_ref, ksl_ref, kwn_ref, vc_ref, vsl_ref, vwn_ref, ng_ref, prw_ref, ga_ref, gb_ref):
    x = x_ref[...]
    h = (x * lax.rsqrt(jnp.mean(x * x, axis=-1, keepdims=True) + RMS_EPS) * g_ref[...]).astype(BF16)
    pr = jnp.dot(h, wr_ref[...], preferred_element_type=F32)
    cs = cs_ref[...]
    sn = sn_ref[...]
    first_half = (_iota((1, LANES), 1) & (HEAD_DIM // 2)) == 0
    chunks = []
    for c in range(pr.shape[1] // LANES):
        x = pr[:, c * LANES:(c + 1) * LANES]
        swapped = jnp.where(first_half, pltpu.roll(x, LANES - HEAD_DIM // 2, 1), pltpu.roll(x, HEAD_DIM // 2, 1))
        chunks.append(x * cs + swapped * sn)
    ro = jnp.concatenate(chunks, axis=1)
    q_ref[...] = _spread_heads(ro[:, 0:WIDTH_A] * (HEAD_DIM ** -0.5 * LOG2_E), 0.0).astype(BF16)
    o = WIDTH_A
    kc_ref[...] = ro[:, o:o + KV_WIDTH]
    tm = x_ref.shape[0]
    blk = ((pl.program_id(0) * tm + _iota((tm, 1), 0)) >> 6) & (BLOCKS_PER_TILE - 1)
    code = jnp.where((_iota((1, KV_PAD), 1) & (LANES - 1)) == HEAD_DIM + blk, NEG_INF, 0.0)
    ksl_ref[...] = (_spread_heads(ro[:, o + KV_WIDTH:o + 2 * KV_WIDTH], 0.0) + code).astype(BF16)
    kwn_ref[...] = _spread_heads(ro[:, o + 2 * KV_WIDTH:o + 3 * KV_WIDTH], 0.0).astype(BF16)
    v = jnp.dot(h, wv_ref[...], preferred_element_type=F32)
    vc_ref[...] = v[:, 0:KV_WIDTH]
    vsl_ref[...] = _spread_heads(v[:, KV_WIDTH:2 * KV_WIDTH], 1.0).astype(BF16)
    vwn_ref[...] = _spread_heads(v[:, 2 * KV_WIDTH:3 * KV_WIDTH], 1.0).astype(BF16)
    ng_ref[...] = jnp.dot(h, wn_ref[...], preferred_element_type=F32)
    prw_ref[...] = jnp.dot(h, ww_ref[...], preferred_element_type=F32)
    gg = jnp.dot(h, wg_ref[...], preferred_element_type=F32)
    ga_ref[...] = gg[:, 0:D_MODEL]
    gb_ref[...] = gg[:, D_MODEL:2 * D_MODEL]


def _inproj(x2, g_mix, cs, sn, w_in, tm=256):
    n = x2.shape[0]
    o = 0
    cols = {}
    for name, wd in (("q", WIDTH_A), ("kc", KV_WIDTH), ("vc", KV_WIDTH), ("ksl", KV_WIDTH), ("vsl", KV_WIDTH),
                     ("kwn", KV_WIDTH), ("vwn", KV_WIDTH), ("ng", 3 * N_HEADS_A), ("rw", C_RWKV),
                     ("ga", D_MODEL), ("gb", D_MODEL)):
        cols[name] = w_in[:, o:o + wd]
        o += wd
    w_rope = jnp.concatenate([cols["q"], cols["kc"], cols["ksl"], cols["kwn"]], axis=1)
    w_v = jnp.concatenate([cols["vc"], cols["vsl"], cols["vwn"]], axis=1)
    w_ng = jnp.pad(cols["ng"], ((0, 0), (0, LANES - 3 * N_HEADS_A)))
    w_gate = jnp.concatenate([cols["ga"], cols["gb"]], axis=1)
    ws = [w.astype(BF16) for w in (w_rope, w_v, w_ng, cols["rw"], w_gate)]
    row = lambda wd: pl.BlockSpec((tm, wd), lambda i: (i, 0))
    outs = [(Q_PAD, BF16), (KV_WIDTH, F32), (KV_PAD, BF16), (KV_PAD, BF16), (KV_WIDTH, F32), (KV_PAD, BF16),
            (KV_PAD, BF16), (LANES, F32), (C_RWKV, F32), (D_MODEL, F32), (D_MODEL, F32)]
    return pl.pallas_call(
        _inproj_kernel,
        grid=(n // tm,),
        in_specs=[row(D_MODEL), _full((1, D_MODEL)), row(LANES), row(LANES)] + [_full(w.shape) for w in ws],
        out_specs=[row(wd) for wd, _ in outs],
        out_shape=[jax.ShapeDtypeStruct((n, wd), dt) for wd, dt in outs],
        compiler_params=_params(("parallel",)),
        name="inproj",
    )(x2, g_mix.reshape(1, D_MODEL), cs, sn, *ws)


def _compress_kernel(kv_ref, pe_ref, w1_ref, wg_ref, b1_ref, w2_ref, b2_ref, o_ref, *, pad_value):
    nch = o_ref.shape[2]
    z = None
    for l in range(D_CMP):
        part = _mm(kv_ref[0, pl.ds(l, nch, stride=D_CMP), :], wg_ref[0, l])
        z = part if z is None else z + part
    z1 = z[:, 0:CMP_HIDDEN]
    z2 = z[:, CMP_HIDDEN:2 * CMP_HIDDEN]
    z2 = jnp.concatenate([z2[1:], z2[:1]], axis=0)
    pb = _mm(pe_ref[...], w1_ref[...])[0:1] + b1_ref[...]
    hid = jax.nn.gelu(z1 + z2 + pb)
    out = _mm(hid, w2_ref[...]) + b2_ref[...]
    o_ref[0, 0] = jnp.concatenate([out, jnp.full(out.shape, pad_value, F32)], axis=1).astype(o_ref.dtype)


def _compress(kv, pe, w1, b1, w2, b2, bsz, s, pad_value):
    nch = s // D_CMP
    pe8 = jnp.broadcast_to(pe.reshape(1, L_CMP * HEAD_DIM), (8, L_CMP * HEAD_DIM))
    w1r = w1.reshape(2, D_CMP, HEAD_DIM, CMP_HIDDEN)
    wcat = jnp.concatenate([w1r[0], w1r[1]], axis=-1)
    wg = jnp.stack([jnp.pad(wcat, ((0, 0), (g * HEAD_DIM, (N_KV_GROUPS - 1 - g) * HEAD_DIM), (0, 0)))
                    for g in range(N_KV_GROUPS)]).astype(BF16)
    return pl.pallas_call(
        functools.partial(_compress_kernel, pad_value=pad_value),
        grid=(bsz, N_KV_GROUPS),
        in_specs=[pl.BlockSpec((1, s, KV_WIDTH), lambda b, g: (b, 0, 0)),
                  _full(pe8.shape), _full(w1.shape),
                  pl.BlockSpec((1, D_CMP, KV_WIDTH, 2 * CMP_HIDDEN), lambda b, g: (g, 0, 0, 0)),
                  _full((1, CMP_HIDDEN)), _full(w2.shape), _full((1, HEAD_DIM))],
        out_specs=pl.BlockSpec((1, 1, nch, LANES), lambda b, g: (b, g, 0, 0)),
        out_shape=jax.ShapeDtypeStruct((bsz, N_KV_GROUPS, nch, LANES), BF16),
        compiler_params=_params(("parallel", "parallel")),
        name="compress",
    )(kv.reshape(bsz, s, KV_WIDTH), pe8, w1, wg, b1.reshape(1, CMP_HIDDEN), w2, b2.reshape(1, HEAD_DIM))


def _nsa_kernel(q_ref, kc_ref, vc_ref, ksl_ref, vsl_ref, kwn_ref, vwn_ref, ng_ref, ov_ref, o_ref, *, n_top):
    n_cmp = kc_ref.shape[2]
    n_slc = ov_ref.shape[1]
    s0 = pl.program_id(1) * Q_BLOCK
    hq = HEADS_PER_GROUP * Q_BLOCK
    t_q = s0 + _iota((Q_BLOCK, 1), 0)
    sig = jax.nn.sigmoid(ng_ref[0])
    win_start = pl.multiple_of(jnp.maximum(s0 - WINDOW, 0), LANES)
    n_tiles = s0 // KEY_TILE + 1
    groups = range(N_KV_GROUPS)
    lanes = lambda g: slice(g * LANES, (g + 1) * LANES)
    qg = [jnp.concatenate([q_ref[0, :, lanes(g * HEADS_PER_GROUP + h)] for h in range(HEADS_PER_GROUP)], axis=0)
          for g in groups]

    def add_bias(s, bias):
        return (s.reshape(HEADS_PER_GROUP, Q_BLOCK, bias.shape[1]) + bias[None]).reshape(hq, bias.shape[1])

    cmp_bias = jnp.where((_iota((1, n_cmp), 1) * D_CMP + (L_CMP - 1)) <= t_q, 0.0, NEG_INF)
    has_cmp = jnp.concatenate([(t_q >= L_CMP - 1).astype(F32)] * HEADS_PER_GROUP, axis=0)
    pos_w = win_start + _iota((1, WINDOW + Q_BLOCK), 1)
    win_bias = jnp.where((pos_w <= t_q) & (pos_w > t_q - WINDOW), 0.0, NEG_INF)

    s_c = [add_bias(_mm_nt(qg[g], kc_ref[0, g]), cmp_bias) for g in groups]
    s_w = [add_bias(_mm_nt(qg[g], kwn_ref[0, pl.ds(win_start, WINDOW + Q_BLOCK), lanes(g)]), win_bias)
           for g in groups]
    e_c = [jnp.exp2(x - jnp.max(x, axis=-1, keepdims=True)) for x in s_c]
    p_c = [x * (has_cmp / jnp.sum(x, axis=-1, keepdims=True)) for x in e_c]
    e_w = [jnp.exp2(x - jnp.max(x, axis=-1, keepdims=True)) for x in s_w]
    o_c = [_mm(p_c[g], vc_ref[0, g]) for g in groups]
    acc_w = [_mm(e_w[g], vwn_ref[0, pl.ds(win_start, WINDOW + Q_BLOCK), lanes(g)]) for g in groups]
    o_win = [x * (1.0 / x[:, HEAD_DIM:HEAD_DIM + 1]) for x in acc_w]

    imp_t = []
    for g in groups:
        p_sum = p_c[g][0:Q_BLOCK]
        for h in range(1, HEADS_PER_GROUP):
            p_sum = p_sum + p_c[g][h * Q_BLOCK:(h + 1) * Q_BLOCK]
        imp_t.append(_mm_exact_rhs(p_sum, ov_ref[...]).T)
    j = _iota((n_slc, Q_BLOCK), 0)
    cur = (s0 + _iota((n_slc, Q_BLOCK), 1)) >> 6
    forced = (j == 0) | (j == cur) | (j == cur - 1)
    score = [jnp.where(forced, -3e38, jnp.where(j <= cur, x, -1.0)) for x in imp_t]
    sel_t = [forced.astype(F32) for _ in groups]
    for _ in range(n_top - 3):
        for g in groups:
            m = jnp.max(score[g], axis=0, keepdims=True)
            first = jnp.min(jnp.where(score[g] == m, j, n_slc), axis=0, keepdims=True)
            hit = j == first
            sel_t[g] = jnp.where(hit, 1.0, sel_t[g])
            score[g] = jnp.where(hit, -3e38, score[g])
    sel = [1.0 - x.T for x in sel_t]

    lane_q = _iota((Q_BLOCK, LANES), 1)
    in_code = (lane_q >= HEAD_DIM) & (lane_q < HEAD_DIM + BLOCKS_PER_TILE)

    def sel_step(kt, carry, diagonal):
        k0 = pl.multiple_of(kt * KEY_TILE, KEY_TILE)
        shift = (HEAD_DIM - BLOCKS_PER_TILE * kt) & (LANES - 1)
        s = []
        for g in groups:
            code = jnp.where(in_code, pltpu.roll(sel[g], shift, 1), 0.0).astype(BF16)
            qa = qg[g] + jnp.concatenate([code] * HEADS_PER_GROUP, axis=0)
            s.append(_mm_nt(qa, ksl_ref[0, pl.ds(k0, KEY_TILE), lanes(g)]))
        if diagonal:
            causal = jnp.where((k0 + _iota((Q_BLOCK, KEY_TILE), 1)) <= t_q, 0.0, NEG_INF)
            s = [add_bias(x, causal) for x in s]
        m_new = [jnp.maximum(carry[2 * g], jnp.max(s[g], axis=-1, keepdims=True)) for g in groups]
        p = [jnp.exp2(s[g] - m_new[g]) for g in groups]
        pv = [_mm(p[g], vsl_ref[0, pl.ds(k0, KEY_TILE), lanes(g)]) for g in groups]
        out = []
        for g in groups:
            out += [m_new[g], jnp.exp2(carry[2 * g] - m_new[g]) * carry[2 * g + 1] + pv[g]]
        return tuple(out)

    init = (jnp.full((hq, 1), NEG_INF, F32), jnp.zeros((hq, LANES), F32)) * N_KV_GROUPS
    fin = lax.fori_loop(0, n_tiles - 1, lambda kt, c: sel_step(kt, c, False), init)
    fin = sel_step(n_tiles - 1, fin, True)

    heads_out = []
    for g in groups:
        acc_s = fin[2 * g + 1]
        o_s = acc_s * (1.0 / acc_s[:, HEAD_DIM:HEAD_DIM + 1])
        o_w = o_win[g]
        for h in range(HEADS_PER_GROUP):
            hh = g * HEADS_PER_GROUP + h
            r = slice(h * Q_BLOCK, (h + 1) * Q_BLOCK)
            mix = (sig[:, 3 * hh:3 * hh + 1] * o_c[g][r] + sig[:, 3 * hh + 1:3 * hh + 2] * o_s[r]
                   + sig[:, 3 * hh + 2:3 * hh + 3] * o_w[r])
            heads_out.append(mix[:, 0:HEAD_DIM])
    o_ref[0] = jnp.concatenate(heads_out, axis=1).astype(o_ref.dtype)


def _nsa(q, kc, vc, ksl, vsl, kwn, vwn, ng, bsz, s):
    n_cmp = s // D_CMP
    n_slc = s // L_SLC
    ii = np.arange(n_cmp)[:, None]
    jj = np.arange(n_slc)[None, :]
    assert n_slc <= LANES and s % KEY_TILE == 0 and min(N_SELECT, n_slc) >= 3
    overlap = (ii * D_CMP < (jj + 1) * L_SLC) & (ii * D_CMP + L_CMP > jj * L_SLC)
    overlap = jnp.asarray(np.pad(overlap, ((0, 0), (0, LANES - n_slc))), BF16)
    seq = lambda wd: pl.BlockSpec((1, s, wd), lambda b, i: (b, 0, 0))
    blk = lambda wd: pl.BlockSpec((1, Q_BLOCK, wd), lambda b, i: (b, i, 0))
    cmp_spec = pl.BlockSpec((1, N_KV_GROUPS, n_cmp, LANES), lambda b, i: (b, 0, 0, 0))
    r3 = lambda a: a.reshape(bsz, s, a.shape[-1])
    return pl.pallas_call(
        functools.partial(_nsa_kernel, n_top=min(N_SELECT, n_slc)),
        grid=(bsz, s // Q_BLOCK),
        in_specs=[blk(Q_PAD), cmp_spec, cmp_spec, seq(KV_PAD), seq(KV_PAD), seq(KV_PAD), seq(KV_PAD),
                  blk(LANES), _full(overlap.shape)],
        out_specs=blk(WIDTH_A),
        out_shape=jax.ShapeDtypeStruct((bsz, s, WIDTH_A), BF16),
        compiler_params=_params(("parallel", "arbitrary")),
        name="nsa",
    )(r3(q), kc, vc, r3(ksl), r3(vsl), r3(kwn), r3(vwn), r3(ng), overlap)


def _rwkv_prep_kernel(p_ref, pv_ref, mu_ref, w0_ref, w2_ref, a0_ref, a2_ref, g2_ref, kk_ref, ka_ref, bd_ref,
                      r_ref, lw_ref, k_ref, v_ref, kn_ref, kb_ref, g_ref, *, tiles_per_seq):
    p = p_ref[...]
    first = (pl.program_id(0) % tiles_per_seq) == 0
    prev_last = jnp.where(first, 0.0, pv_ref[7:8, :])
    p_prev = jnp.concatenate([prev_last, p[:-1]], axis=0)
    xs = p + (p_prev - p) * mu_ref[...]
    wb = WIDTH_B
    r = xs[:, 0:wb]
    k = xs[:, wb:2 * wb]
    v = xs[:, 2 * wb:3 * wb]
    o = 3 * wb
    xw = xs[:, o:o + W_LORA]
    xa = xs[:, o + W_LORA:o + W_LORA + A_LORA]
    xg = xs[:, o + W_LORA + A_LORA:o + W_LORA + A_LORA + G_LORA]
    z = w0_ref[...] + _mm_hi(jnp.tanh(xw), w2_ref[...])
    w_log = -(jnp.maximum(-z, 0.0) + jnp.log(1.0 + jnp.exp(-jnp.abs(z)))) - 0.5
    a = jax.nn.sigmoid(a0_ref[...] + _mm_hi(xa, a2_ref[...]))
    kk = k * kk_ref[...]
    ss = _mm_exact_rhs(kk * kk, bd_ref[...])
    kn = kk * (1.0 / jnp.maximum(jnp.sqrt(ss), 1e-12))
    r_ref[...] = r
    lw_ref[...] = -jnp.exp(w_log)
    k_ref[...] = k * (1.0 + (a - 1.0) * ka_ref[...])
    v_ref[...] = v
    kn_ref[...] = kn
    kb_ref[...] = kn * a
    g_ref[...] = _mm(jax.nn.sigmoid(xg), g2_ref[...])


def _rwkv_prep(prw, mu, w0, w2, a0, a2, g2, k_k, k_a, s, tm=256):
    n = prw.shape[0]
    hid = np.arange(WIDTH_B) // HEAD_DIM_B
    bd = jnp.asarray(hid[:, None] == hid[None, :], BF16)
    row = lambda wd: pl.BlockSpec((tm, wd), lambda i: (i, 0))
    vec = lambda a: a.reshape(1, -1)
    return pl.pallas_call(
        functools.partial(_rwkv_prep_kernel, tiles_per_seq=s // tm),
        grid=(n // tm,),
        in_specs=[row(C_RWKV), pl.BlockSpec((8, C_RWKV), lambda i: (jnp.maximum(i * (tm // 8) - 1, 0), 0)),
                  _full((1, C_RWKV)), _full((1, WIDTH_B)), _full(w2.shape), _full((1, WIDTH_B)), _full(a2.shape),
                  _full(g2.shape), _full((1, WIDTH_B)), _full((1, WIDTH_B)), _full(bd.shape)],
        out_specs=[row(WIDTH_B)] * 7,
        out_shape=[jax.ShapeDtypeStruct((n, WIDTH_B), F32)] * 7,
        compiler_params=_params(("parallel",)),
        name="rwkv_prep",
    )(prw, prw, vec(mu), vec(w0), w2, vec(a0), a2, g2, vec(k_k), vec(k_a), bd)


def _rwkv_chunk_kernel(r_ref, lw_ref, k_ref, v_ref, kn_ref, kb_ref, rk_ref,
                       bonus_ref, tm_ref, *, n_sub):
    L = CHUNK
    hd = HEAD_DIM_B
    ti = _iota((L, L), 0)
    si = _iota((L, L), 1)
    strict = si < ti
    incl = si <= ti
    same_sub = (ti // SUB) == (si // SUB)
    eye = (ti == si).astype(F32)
    tri = incl.astype(BF16)
    at, bt, kt, rt, bl, kl, vv, rkr, p_tot = [], [], [], [], [], [], [], [], []
    for j in range(n_sub):
        rows = slice(j * L, (j + 1) * L)
        lw = lw_ref[0, rows, :]
        cs = _mm_exact_lhs(tri, lw)
        p_inv = jnp.exp(-cs)
        p_end = jnp.exp(cs[L - 1:L, :] - cs)
        r = r_ref[0, rows, :]
        k = k_ref[0, rows, :]
        kb = kb_ref[0, rows, :]
        at.append(-kn_ref[0, rows, :] * jnp.exp(cs - lw))
        bt.append(kb * p_inv)
        kt.append(k * p_inv)
        rt.append(r * jnp.exp(cs))
        bl.append(kb * p_end)
        kl.append(k * p_end)
        vv.append(v_ref[0, rows, :])
        rkr.append(r * k * rk_ref[...])
        p_tot.append(jnp.exp(cs[L - 1:L, :]))
    units = [(j, h) for j in range(n_sub) for h in range(N_HEADS_B)]
    sl = lambda arr, u: arr[u[0]][:, u[1] * hd:(u[1] + 1) * hd]
    aa = [_mm_nt(jnp.concatenate([sl(at, u), sl(rt, u)], axis=0),
                 jnp.concatenate([sl(bt, u), sl(kt, u)], axis=0)) for u in units]
    a_ab = [jnp.where(strict, x[0:L, 0:L], 0.0) for x in aa]
    a_low = [jnp.concatenate([jnp.where(strict, x[0:L, L:2 * L], 0.0),
                              jnp.where(incl, x[L:2 * L, L:2 * L], 0.0)], axis=0) for x in aa]
    a_rb = [jnp.where(incl, x[L:2 * L, 0:L], 0.0) for x in aa]
    av = [_mm(x, sl(vv, u)) for x, u in zip(a_low, units)]
    kv = [_mm_tn(sl(kl, u), sl(vv, u)) for u in units]
    pw = [jnp.where(same_sub, x, 0.0) for x in a_ab]
    t = [eye + x for x in pw]
    for _ in range(3):
        pw = [_mm(x, x) for x in pw]
        t = [x + _mm(x, y) for x, y in zip(t, pw)]
    width = SUB
    while width < L:
        m = ((ti // width) == (si // width) + 1) & ((ti // (2 * width)) == (si // (2 * width)))
        ot = [_mm(jnp.where(m, x, 0.0), y) for x, y in zip(a_ab, t)]
        t = [x + _mm(x, y) for x, y in zip(t, ot)]
        width *= 2
    wu = [_mm(x, jnp.concatenate([sl(at, u), y[0:L]], axis=1)) for x, y, u in zip(t, av, units)]
    qy = [jnp.concatenate([sl(rt, u), y[L:2 * L]], axis=1) + _mm(x, w)
          for x, y, w, u in zip(a_rb, av, wu, units)]
    gh = [_mm_tn(sl(bl, u), w) + jnp.concatenate([eye * sl(p_tot, u), y], axis=1)
          for w, y, u in zip(wu, kv, units)]
    bonus = [jnp.sum(sl(rkr, u), axis=-1, keepdims=True) * sl(vv, u) for u in units]
    for j in range(n_sub):
        bonus_ref[0, j * L:(j + 1) * L, :] = jnp.concatenate(bonus[j * N_HEADS_B:(j + 1) * N_HEADS_B], axis=1)
    for x, y, (j, h) in zip(qy, gh, units):
        tm_ref[0, j, h, 0:L, :] = x
        tm_ref[0, j, h, L:2 * L, :] = y


def _rwkv_chunk(r, lw, k, v, kn, kb, r_k, bsz, s, n_sub=4):
    nch = s // CHUNK
    blk = pl.BlockSpec((1, n_sub * CHUNK, WIDTH_B), lambda b, c: (b, c, 0))
    mat = pl.BlockSpec((1, n_sub, N_HEADS_B, 2 * CHUNK, 2 * HEAD_DIM_B), lambda b, c: (b, c, 0, 0, 0))
    r3 = lambda a: a.reshape(bsz, s, WIDTH_B)
    return pl.pallas_call(
        functools.partial(_rwkv_chunk_kernel, n_sub=n_sub),
        grid=(bsz, nch // n_sub),
        in_specs=[blk] * 6 + [_full((1, WIDTH_B))],
        out_specs=[blk, mat],
        out_shape=[jax.ShapeDtypeStruct((bsz, s, WIDTH_B), F32),
                   jax.ShapeDtypeStruct((bsz, nch, N_HEADS_B, 2 * CHUNK, 2 * HEAD_DIM_B), F32)],
        compiler_params=_params(("parallel", "parallel")),
        name="rwkv_chunk",
    )(r3(r), r3(lw), r3(k), r3(v), r3(kn), r3(kb), r_k.reshape(1, WIDTH_B))


def _rwkv_scan_kernel(tm_ref, yn_ref, st_ref):
    L = CHUNK
    hd = HEAD_DIM_B

    @pl.when(pl.program_id(0) == 0)
    def _():
        eye = (_iota((hd, hd), 0) == _iota((hd, hd), 1)).astype(F32)
        for i in range(st_ref.shape[0]):
            st_ref[i] = jnp.concatenate([jnp.zeros((hd, hd), F32), eye], axis=0)

    for b in range(tm_ref.shape[0]):
        res = [_mm_x3(tm_ref[b, 0, h], st_ref[b * N_HEADS_B + h]) for h in range(N_HEADS_B)]
        ys = []
        for h in range(N_HEADS_B):
            st_ref[b * N_HEADS_B + h, 0:hd, :] = res[h][L:2 * L]
            y = res[h][0:L]
            mean = jnp.mean(y, axis=-1, keepdims=True)
            d = y - mean
            var = jnp.mean(d * d, axis=-1, keepdims=True)
            ys.append(d * lax.rsqrt(var + GN_EPS))
        yn_ref[b] = jnp.concatenate(ys, axis=1)


def _rwkv_scan(tmat, bsz, s):
    nch = s // CHUNK
    return pl.pallas_call(
        _rwkv_scan_kernel,
        grid=(nch,),
        in_specs=[pl.BlockSpec((bsz, 1, N_HEADS_B, 2 * CHUNK, 2 * HEAD_DIM_B), lambda c: (0, c, 0, 0, 0))],
        out_specs=pl.BlockSpec((bsz, CHUNK, WIDTH_B), lambda c: (0, c, 0)),
        out_shape=jax.ShapeDtypeStruct((bsz, s, WIDTH_B), F32),
        scratch_shapes=[pltpu.VMEM((bsz * N_HEADS_B, 2 * HEAD_DIM_B, HEAD_DIM_B), F32)],
        compiler_params=_params(("arbitrary",)),
        name="rwkv_scan",
    )(tmat)


SUBLANES = 8
assert D_MODEL == SUBLANES * LANES


def _row_tiles_shape(rows):
    return (rows // SUBLANES, SUBLANES, SUBLANES, LANES)


def _row_tiles_spec(rows):
    return pl.BlockSpec(_row_tiles_shape(rows), lambda i, *_: (i, 0, 0, 0))


def _store_row_tiles(ref, x):
    for c in range(SUBLANES):
        ref[:, c] = x[:, c * LANES:(c + 1) * LANES].reshape(x.shape[0] // SUBLANES, SUBLANES, LANES)


def _row_view(ref, r):
    return ref.at[pl.ds(r >> 3, 1), :, r & (SUBLANES - 1)]


def _gather_buffer(rows):
    return pltpu.VMEM((2, rows // SUBLANES, SUBLANES, SUBLANES, LANES), F32)


def _gathered_rows(buf, slot, start, rows):
    t0, nt = start // SUBLANES, rows // SUBLANES
    return jnp.concatenate([buf[slot, t0:t0 + nt, c].reshape(rows, LANES) for c in range(SUBLANES)], axis=1)


MERGE_SUB = 256


def _route(logits, run):
    tm = logits.shape[0]
    lane = _iota(logits.shape, 1)
    big = 4 * LANES
    is_grp = lane < N_GROUPS
    lg = jnp.where(is_grp, logits, NEG_INF)
    mg = jnp.max(lg, axis=-1, keepdims=True)
    gidx = jnp.min(jnp.where(lg == mg, lane, big), axis=-1, keepdims=True)
    pg = 1.0 / jnp.sum(jnp.where(is_grp, jnp.exp(lg - mg), 0.0), axis=-1, keepdims=True)
    in_grp = (lane >= N_GROUPS) & (((lane - N_GROUPS) >> 3) == gidx) & (lane < N_GROUPS + N_EXPERTS)
    le = jnp.where(in_grp, logits, NEG_INF)
    m1 = jnp.max(le, axis=-1, keepdims=True)
    i1 = jnp.min(jnp.where(le == m1, lane, big), axis=-1, keepdims=True)
    le2 = jnp.where(lane == i1, NEG_INF, le)
    m2 = jnp.max(le2, axis=-1, keepdims=True)
    i2 = jnp.min(jnp.where(le2 == m2, lane, big), axis=-1, keepdims=True)
    e2 = jnp.exp(m2 - m1)
    inv = pg / (1.0 + e2)
    hot1 = (lane == i1 - N_GROUPS).astype(F32)
    hot2 = (lane == i2 - N_GROUPS).astype(F32)
    both = hot1 + hot2
    earlier = (_iota((tm, tm), 1) < _iota((tm, tm), 0)).astype(BF16)
    before = jnp.dot(earlier, both.astype(BF16), preferred_element_type=F32) + run
    rank1 = jnp.sum(hot1 * before, axis=-1, keepdims=True)
    rank2 = jnp.sum(hot2 * before, axis=-1, keepdims=True)
    cols = ((i1 - N_GROUPS).astype(F32), (i2 - N_GROUPS).astype(F32), inv, inv * e2, rank1, rank2)
    route = jnp.zeros_like(logits)
    for c, v in enumerate(cols):
        route = jnp.where(lane == c, v, route)
    return route, run + jnp.sum(both, axis=0, keepdims=True)


def _merge_kernel(x_ref, oa_ref, yn_ref, bonus_ref, g_ref, ga_ref, gb_ref, lnw_ref, lnb_ref, wa_ref, wb_ref,
                  wo_ref, gf_ref, wr_ref, br_ref, x1_ref, h2_ref, route_ref, cnt_ref, run_ref):
    @pl.when(pl.program_id(0) == 0)
    def _():
        run_ref[...] = jnp.zeros_like(run_ref)

    subs = [slice(j * MERGE_SUB, (j + 1) * MERGE_SUB) for j in range(x_ref.shape[0] // MERGE_SUB)]
    ob = [(yn_ref[r, :] * lnw_ref[...] + lnb_ref[...] + bonus_ref[r, :]) * g_ref[r, :] for r in subs]
    pa = [jnp.dot(oa_ref[r, :], wa_ref[...], preferred_element_type=F32) for r in subs]
    pb = [_mm(x, wb_ref[...]) for x in ob]
    merged = [jax.nn.sigmoid(ga_ref[r, :]) * a + jax.nn.sigmoid(gb_ref[r, :]) * b for r, a, b in zip(subs, pa, pb)]
    x1 = [x_ref[r, :] + _mm(m, wo_ref[...]) for r, m in zip(subs, merged)]
    h2 = [x * lax.rsqrt(jnp.mean(x * x, axis=-1, keepdims=True) + RMS_EPS) * gf_ref[...] for x in x1]
    logits = []
    for x in h2:
        hi = x.astype(BF16)
        lo = (x - hi.astype(F32)).astype(BF16)
        logits.append(jnp.dot(jnp.concatenate([hi, hi, lo], axis=1), wr_ref[...], preferred_element_type=F32)
                      + br_ref[...])
    run = run_ref[...]
    for r, x, h, lg in zip(subs, x1, h2, logits):
        x1_ref[r, :] = x
        _store_row_tiles(h2_ref.at[pl.ds(r.start // SUBLANES, MERGE_SUB // SUBLANES)], h)
        route_ref[r, :], run = _route(lg, run)
    run_ref[...] = run
    cnt_ref[...] = run


def _merge(x2, oa, yn, bonus, g, ga, gb, ln_w, ln_b, w_a, w_b, w_o, g_ffn, w_grp, b_grp, w_exp, b_exp,
           tm=2 * MERGE_SUB):
    n = x2.shape[0]
    w_r = jnp.pad(jnp.concatenate([w_grp, w_exp], axis=1), ((0, 0), (0, LANES - N_GROUPS - N_EXPERTS)))
    w_hi = w_r.astype(BF16)
    w_lo = (w_r - w_hi.astype(F32)).astype(BF16)
    w_r = jnp.concatenate([w_hi, w_lo, w_hi], axis=0)
    b_r = jnp.pad(jnp.concatenate([b_grp, b_exp]), (0, LANES - N_GROUPS - N_EXPERTS)).reshape(1, LANES)
    row = lambda wd: pl.BlockSpec((tm, wd), lambda i: (i, 0))
    vec = lambda a: a.reshape(1, -1)
    wa, wb, wo = w_a.astype(BF16), w_b.astype(BF16), w_o.astype(BF16)
    return pl.pallas_call(
        _merge_kernel,
        grid=(n // tm,),
        in_specs=[row(D_MODEL), row(WIDTH_A), row(WIDTH_B), row(WIDTH_B), row(WIDTH_B), row(D_MODEL), row(D_MODEL),
                  _full((1, WIDTH_B)), _full((1, WIDTH_B)), _full(wa.shape), _full(wb.shape), _full(wo.shape),
                  _full((1, D_MODEL)), _full(w_r.shape), _full((1, LANES))],
        out_specs=[row(D_MODEL), _row_tiles_spec(tm), row(LANES), _full((1, LANES))],
        out_shape=[jax.ShapeDtypeStruct((n, D_MODEL), F32), jax.ShapeDtypeStruct(_row_tiles_shape(n), F32),
                   jax.ShapeDtypeStruct((n, LANES), F32), jax.ShapeDtypeStruct((1, LANES), F32)],
        scratch_shapes=[pltpu.VMEM((1, LANES), F32)],
        compiler_params=_params(("arbitrary",)),
        name="merge",
    )(x2, oa, yn, bonus, g, ga, gb, vec(ln_w), vec(ln_b), wa, wb, wo, vec(g_ffn), w_r, b_r)


def _gather_start(idx_ref, src_hbm, buf, sem, slot, rows):
    def body(t, carry):
        for u in range(SUBLANES):
            pltpu.make_async_copy(_row_view(src_hbm, idx_ref[0, 0, t * SUBLANES + u]),
                                  buf.at[slot, pl.ds(t, 1), :, u], sem.at[slot]).start(priority=u % 2)
        return carry
    lax.fori_loop(0, rows // SUBLANES, body, 0)


def _gather_wait(buf, sem, slot):
    pltpu.make_async_copy(buf.at[slot], buf.at[slot], sem.at[slot]).wait()


def _gather_pipeline(idx_ref, idx_next_ref, src_hbm, buf, sem, rows):
    i = pl.program_id(0)
    slot = i % 2

    @pl.when(i == 0)
    def _():
        _gather_start(idx_ref, src_hbm, buf, sem, 0, rows)

    @pl.when(i + 1 < pl.num_programs(0))
    def _():
        _gather_start(idx_next_ref, src_hbm, buf, sem, 1 - slot, rows)

    _gather_wait(buf, sem, slot)
    return slot


def _dispatch_kernel(end_ref, pos_ref, h2_ref, x_hbm, zeros, sem, zsem, *, tm):
    tile_rows = MOE_BLOCK // SUBLANES

    @pl.when(pl.program_id(0) == 0)
    def _():
        zeros[...] = jnp.zeros_like(zeros)

        def fill(block):
            cp = pltpu.make_async_copy(zeros, x_hbm.at[pl.ds(block * tile_rows, tile_rows)], zsem.at[0])
            cp.start()
            cp.wait()

        for e in range(N_EXPERTS):
            start = end_ref[e - 1] if e else 0

            @pl.when(end_ref[e] > start)
            def _():
                fill(end_ref[e] // MOE_BLOCK - 1)

        def tail(block, carry):
            fill(block)
            return carry
        lax.fori_loop(end_ref[N_EXPERTS - 1] // MOE_BLOCK, x_hbm.shape[0] // tile_rows, tail, 0)

    def body(t, carry):
        for u in range(SUBLANES):
            r = t * SUBLANES + u
            for k in range(TOP_K):
                p = pos_ref[0, 0, k * tm + r]
                pltpu.make_async_copy(h2_ref.at[pl.ds(t, 1), :, u], _row_view(x_hbm, p),
                                      sem.at[k]).start(priority=(u + k) % 2)
        return carry
    lax.fori_loop(0, tm // SUBLANES, body, 0)
    for k in range(TOP_K):
        pltpu.make_async_copy(h2_ref, h2_ref, sem.at[k]).wait()


def _dispatch(h2, pos, pad_end, n_pad, tm=256):
    nt = h2.shape[0] * SUBLANES // tm
    pos3 = pos.reshape(nt, tm, TOP_K).transpose(0, 2, 1).reshape(nt, 1, TOP_K * tm)
    tiles = _row_tiles_shape(n_pad)
    grid_spec = pltpu.PrefetchScalarGridSpec(
        num_scalar_prefetch=1,
        grid=(nt,),
        in_specs=[pl.BlockSpec((1, 1, TOP_K * tm), lambda i, pe: (i, 0, 0), memory_space=pltpu.SMEM),
                  _row_tiles_spec(tm)],
        out_specs=pl.BlockSpec(memory_space=pl.ANY),
        scratch_shapes=[pltpu.VMEM(_row_tiles_shape(MOE_BLOCK), F32),
                        pltpu.SemaphoreType.DMA((TOP_K,)), pltpu.SemaphoreType.DMA((1,))],
    )
    return pl.pallas_call(
        functools.partial(_dispatch_kernel, tm=tm),
        grid_spec=grid_spec,
        out_shape=jax.ShapeDtypeStruct(tiles, F32),
        compiler_params=_params(("arbitrary",)),
        name="dispatch",
    )(pad_end, pos3, h2)


def _moe_kernel(be_ref, x_ref, w13_ref, w2_ref, y_ref):
    xb = jnp.concatenate([x_ref[:, c].reshape(MOE_BLOCK, LANES) for c in range(SUBLANES)], axis=1)
    a = _mm(xb, w13_ref[0])
    hid = jax.nn.silu(a[:, 0:D_EXPERT]) * a[:, D_EXPERT:2 * D_EXPERT]
    _store_row_tiles(y_ref, _mm(hid, w2_ref[0]))


def _moe(x_pad, blk_expert, w13, w2):
    n_blk = blk_expert.shape[0]
    grid_spec = pltpu.PrefetchScalarGridSpec(
        num_scalar_prefetch=1,
        grid=(n_blk,),
        in_specs=[_row_tiles_spec(MOE_BLOCK),
                  pl.BlockSpec((1, D_MODEL, 2 * D_EXPERT), lambda i, be: (be[i], 0, 0)),
                  pl.BlockSpec((1, D_EXPERT, D_MODEL), lambda i, be: (be[i], 0, 0))],
        out_specs=_row_tiles_spec(MOE_BLOCK),
    )
    return pl.pallas_call(
        _moe_kernel,
        grid_spec=grid_spec,
        out_shape=jax.ShapeDtypeStruct(_row_tiles_shape(n_blk * MOE_BLOCK), F32),
        compiler_params=_params(("parallel",)),
        name="moe",
    )(blk_expert, x_pad, w13, w2)


def _final_kernel(pos_ref, pos_next_ref, x1_ref, route_ref, g_ref, y_hbm, o_ref, buf, sem, *, tm):
    slot = _gather_pipeline(pos_ref, pos_next_ref, y_hbm, buf, sem, TOP_K * tm)
    x = x1_ref[...]
    for k in range(TOP_K):
        x = x + route_ref[:, TOP_K + k:TOP_K + k + 1] * _gathered_rows(buf, slot, k * tm, tm)
    o_ref[...] = x * lax.rsqrt(jnp.mean(x * x, axis=-1, keepdims=True) + RMS_EPS) * g_ref[...]


def _final(x1, route, y_pad, pos, g_final, tm=256):
    n = x1.shape[0]
    nt = n // tm
    pos3 = pos.reshape(nt, tm, TOP_K).transpose(0, 2, 1).reshape(nt, 1, TOP_K * tm)
    smem_blk = lambda f: pl.BlockSpec((1, 1, TOP_K * tm), f, memory_space=pltpu.SMEM)
    row = lambda wd: pl.BlockSpec((tm, wd), lambda i: (i, 0))
    return pl.pallas_call(
        functools.partial(_final_kernel, tm=tm),
        grid=(nt,),
        in_specs=[smem_blk(lambda i: (i, 0, 0)), smem_blk(lambda i: (jnp.minimum(i + 1, nt - 1), 0, 0)),
                  row(D_MODEL), row(LANES), _full((1, D_MODEL)), pl.BlockSpec(memory_space=pl.ANY)],
        out_specs=row(D_MODEL),
        out_shape=jax.ShapeDtypeStruct((n, D_MODEL), F32),
        scratch_shapes=[_gather_buffer(TOP_K * tm), pltpu.SemaphoreType.DMA((2,))],
        compiler_params=_params(("arbitrary",)),
        name="final",
    )(pos3, pos3, x1, route, g_final.reshape(1, D_MODEL), y_pad)


def _dispatch_plan(route, counts, n_tok):
    expert = route[:, 0:TOP_K].astype(I32)
    rank = route[:, 2 * TOP_K:3 * TOP_K].astype(I32)
    cnt = counts[0, 0:N_EXPERTS].astype(I32)
    padded = (cnt + MOE_BLOCK - 1) // MOE_BLOCK * MOE_BLOCK
    ids = jnp.arange(N_EXPERTS, dtype=I32)
    pad_end = jnp.sum(jnp.where(ids[None, :] <= ids[:, None], padded[None, :], 0), axis=1)
    pad_start = pad_end - padded
    pos = jnp.sum(jnp.where(expert[..., None] == ids, pad_start, 0), axis=-1) + rank
    n_pad = n_tok * TOP_K + N_EXPERTS * MOE_BLOCK
    n_blk = n_pad // MOE_BLOCK
    blk_start = jnp.arange(n_blk, dtype=I32) * MOE_BLOCK
    blk_expert = jnp.minimum(jnp.sum((pad_end[None, :] <= blk_start[:, None]).astype(I32), axis=1), N_EXPERTS - 1)
    return pos, blk_expert, pad_end, n_pad


def kernel(x, positions, g_mix, w_in, cmp_pe_k, cmp_w1_k, cmp_b1_k, cmp_w2_k, cmp_b2_k, cmp_pe_v, cmp_w1_v, cmp_b1_v, cmp_w2_v, cmp_b2_v, rwkv_mu, rwkv_w0, rwkv_w2, rwkv_a0, rwkv_a2, rwkv_g2, rwkv_k_k, rwkv_k_a, rwkv_r_k, rwkv_ln_w, rwkv_ln_b, w_proj_a, w_proj_b, w_out, g_ffn, w_grp, b_grp, w_exp, b_exp, e_w1, e_w3, e_w2, g_final):
    bsz, s, _ = x.shape
    assert w_in.shape[0] == 1, "single-layer stack"
    n_tok = bsz * s
    x2 = x.reshape(n_tok, D_MODEL)
    half = HEAD_DIM // 2
    freqs = jnp.power(ROPE_THETA, -jnp.arange(half, dtype=F32) / half)
    ang = positions.astype(F32).reshape(n_tok, 1) * freqs
    cs = jnp.tile(jnp.cos(ang), (1, 4))
    sn = jnp.tile(jnp.concatenate([-jnp.sin(ang), jnp.sin(ang)], axis=1), (1, 2))
    q, kc, ksl, kwn, vc, vsl, vwn, ng, prw, ga, gb = _inproj(x2, g_mix[0], cs, sn, w_in[0])
    k_c = _compress(kc, cmp_pe_k[0], cmp_w1_k[0], cmp_b1_k[0], cmp_w2_k[0], cmp_b2_k[0], bsz, s, 0.0)
    v_c = _compress(vc, cmp_pe_v[0], cmp_w1_v[0], cmp_b1_v[0], cmp_w2_v[0], cmp_b2_v[0], bsz, s, 1.0)
    o_a = _nsa(q, k_c, v_c, ksl, vsl, kwn, vwn, ng, bsz, s).reshape(n_tok, WIDTH_A)
    r, lw, k, v, kn, kb, g = _rwkv_prep(prw, rwkv_mu[0], rwkv_w0[0], rwkv_w2[0], rwkv_a0[0], rwkv_a2[0],
                                        rwkv_g2[0], rwkv_k_k[0], rwkv_k_a[0], s)
    bonus, tmat = _rwkv_chunk(r, lw, k, v, kn, kb, rwkv_r_k[0], bsz, s)
    yn = _rwkv_scan(tmat, bsz, s)
    x1, h2, route, counts = _merge(x2, o_a, yn.reshape(n_tok, WIDTH_B), bonus.reshape(n_tok, WIDTH_B), g, ga, gb,
                           rwkv_ln_w[0], rwkv_ln_b[0], w_proj_a[0], w_proj_b[0], w_out[0], g_ffn[0],
                           w_grp[0], b_grp[0], w_exp[0], b_exp[0])
    pos, blk_expert, pad_end, n_pad = _dispatch_plan(route, counts, n_tok)
    w13 = jnp.concatenate([e_w1[0], e_w3[0]], axis=-1).astype(BF16)
    y_pad = _moe(_dispatch(h2, pos, pad_end, n_pad), blk_expert, w13, e_w2[0].astype(BF16))
    out = _final(x1, route, y_pad, pos, g_final)
    return out.reshape(bsz, s, D_MODEL)
```

```python
import functools

import numpy as np
import jax
import jax.numpy as jnp
from jax import lax
from jax.experimental import pallas as pl
from jax.experimental.pallas import tpu as pltpu

F32 = jnp.float32
BF16 = jnp.bfloat16
I32 = jnp.int32

D_MODEL = 1024
N_HEADS_A = 8
N_KV_GROUPS = 2
HEADS_PER_GROUP = N_HEADS_A // N_KV_GROUPS
HEAD_DIM = 64
L_CMP = 32
D_CMP = 16
CMP_HIDDEN = 256
L_SLC = 64
N_SELECT = 16
WINDOW = 512
Q_BLOCK = 128
ROPE_THETA = 10000.0
FORCE_SCORE = 1e4
NEG_INF = -1e30
N_HEADS_B = 8
HEAD_DIM_B = 64
W_LORA = 64
A_LORA = 64
G_LORA = 128
GN_EPS = 64e-5
N_GROUPS = 4
EXPERTS_PER_GROUP = 8
N_EXPERTS = N_GROUPS * EXPERTS_PER_GROUP
TOP_K = 2
D_EXPERT = 256
MOE_BLOCK = 256
RMS_EPS = 1e-6
WIDTH_A = N_HEADS_A * HEAD_DIM
KV_WIDTH = N_KV_GROUPS * HEAD_DIM
WIDTH_B = N_HEADS_B * HEAD_DIM_B
C_RWKV = 3 * WIDTH_B + W_LORA + A_LORA + G_LORA

LANES = 128
CHUNK = 64
SUB = 16
KEY_TILE = 1024
BLOCKS_PER_TILE = KEY_TILE // L_SLC
VMEM_LIMIT = 56 * 1024 * 1024

HI = lax.Precision.HIGHEST
LOG2_E = 1.4426950408889634


def _mm(a, b):
    return jnp.dot(a.astype(BF16), b.astype(BF16), preferred_element_type=F32)


def _mm_nt(a, b):
    return lax.dot_general(a.astype(BF16), b.astype(BF16), (((1,), (1,)), ((), ())), preferred_element_type=F32)


def _mm_tn(a, b):
    return lax.dot_general(a.astype(BF16), b.astype(BF16), (((0,), (0,)), ((), ())), preferred_element_type=F32)


def _mm_hi(a, b):
    return jnp.dot(a, b, preferred_element_type=F32, precision=HI)


def _mm_x3(a, b):
    a1 = a.astype(BF16)
    a2 = (a - a1.astype(F32)).astype(BF16)
    b1 = b.astype(BF16)
    b2 = (b - b1.astype(F32)).astype(BF16)
    dot = lambda x, y: jnp.dot(x, y, preferred_element_type=F32)
    return dot(a1, b1) + dot(a1, b2) + dot(a2, b1)


def _bf16_pieces(a):
    a1 = a.astype(BF16)
    r1 = a - a1.astype(F32)
    a2 = r1.astype(BF16)
    return a1, a2, (r1 - a2.astype(F32)).astype(BF16)


def _mm_exact_rhs(a, b):
    bb = b.astype(BF16)
    return sum(jnp.dot(x, bb, preferred_element_type=F32) for x in _bf16_pieces(a))


def _mm_exact_lhs(a, b):
    aa = a.astype(BF16)
    return sum(jnp.dot(aa, x, preferred_element_type=F32) for x in _bf16_pieces(b))


def _iota(shape, dim):
    return lax.broadcasted_iota(I32, shape, dim)


def _params(sem):
    return pltpu.CompilerParams(dimension_semantics=sem, vmem_limit_bytes=VMEM_LIMIT)


def _full(shape):
    nd = len(shape)
    return pl.BlockSpec(shape, lambda *_: (0,) * nd)


Q_PAD = N_HEADS_A * LANES
KV_PAD = N_KV_GROUPS * LANES


def _spread_heads(x, pad_value):
    low = _iota((1, LANES), 1) < HEAD_DIM
    out = []
    for c in range(x.shape[1] // LANES):
        pair = x[:, c * LANES:(c + 1) * LANES]
        out.append(jnp.where(low, pair, pad_value))
        out.append(jnp.where(low, pltpu.roll(pair, HEAD_DIM, 1), pad_value))
    return jnp.concatenate(out, axis=1)


def _inproj_kernel(x_ref, g_ref, cs_ref, sn_ref, wr_ref, wv_ref, wn_ref, ww_ref, wg_ref,
                   q_ref, kc_ref, ksl_ref, kwn_ref, vc_ref, vsl_ref, vwn_ref, ng_ref, prw_ref, ga_ref, gb_ref):
    x = x_ref[...]
    h = (x * lax.rsqrt(jnp.mean(x * x, axis=-1, keepdims=True) + RMS_EPS) * g_ref[...]).astype(BF16)
    pr = jnp.dot(h, wr_ref[...], preferred_element_type=F32)
    cs = cs_ref[...]
    sn = sn_ref[...]
    first_half = (_iota((1, LANES), 1) & (HEAD_DIM // 2)) == 0
    chunks = []
    for c in range(pr.shape[1] // LANES):
        x = pr[:, c * LANES:(c + 1) * LANES]
        swapped = jnp.where(first_half, pltpu.roll(x, LANES - HEAD_DIM // 2, 1), pltpu.roll(x, HEAD_DIM // 2, 1))
        chunks.append(x * cs + swapped * sn)
    ro = jnp.concatenate(chunks, axis=1)
    q_ref[...] = _spread_heads(ro[:, 0:WIDTH_A] * (HEAD_DIM ** -0.5 * LOG2_E), 0.0).astype(BF16)
    o = WIDTH_A
    kc_ref[...] = ro[:, o:o + KV_WIDTH]
    tm = x_ref.shape[0]
    blk = ((pl.program_id(0) * tm + _iota((tm, 1), 0)) >> 6) & (BLOCKS_PER_TILE - 1)
    code = jnp.where((_iota((1, KV_PAD), 1) & (LANES - 1)) == HEAD_DIM + blk, NEG_INF, 0.0)
    ksl_ref[...] = (_spread_heads(ro[:, o + KV_WIDTH:o + 2 * KV_WIDTH], 0.0) + code).astype(BF16)
    kwn_ref[...] = _spread_heads(ro[:, o + 2 * KV_WIDTH:o + 3 * KV_WIDTH], 0.0).astype(BF16)
    v = jnp.dot(h, wv_ref[...], preferred_element_type=F32)
    vc_ref[...] = v[:, 0:KV_WIDTH]
    vsl_ref[...] = _spread_heads(v[:, KV_WIDTH:2 * KV_WIDTH], 1.0).astype(BF16)
    vwn_ref[...] = _spread_heads(v[:, 2 * KV_WIDTH:3 * KV_WIDTH], 1.0).astype(BF16)
    ng_ref[...] = jnp.dot(h, wn_ref[...], preferred_element_type=F32)
    prw_ref[...] = jnp.dot(h, ww_ref[...], preferred_element_type=F32)
    gg = jnp.dot(h, wg_ref[...], preferred_element_type=F32)
    ga_ref[...] = gg[:, 0:D_MODEL]
    gb_ref[...] = gg[:, D_MODEL:2 * D_MODEL]


def _inproj(x2, g_mix, cs, sn, w_in, tm=256):
    n = x2.shape[0]
    o = 0
    cols = {}
    for name, wd in (("q", WIDTH_A), ("kc", KV_WIDTH), ("vc", KV_WIDTH), ("ksl", KV_WIDTH), ("vsl", KV_WIDTH),
                     ("kwn", KV_WIDTH), ("vwn", KV_WIDTH), ("ng", 3 * N_HEADS_A), ("rw", C_RWKV),
                     ("ga", D_MODEL), ("gb", D_MODEL)):
        cols[name] = w_in[:, o:o + wd]
        o += wd
    w_rope = jnp.concatenate([cols["q"], cols["kc"], cols["ksl"], cols["kwn"]], axis=1)
    w_v = jnp.concatenate([cols["vc"], cols["vsl"], cols["vwn"]], axis=1)
    w_ng = jnp.pad(cols["ng"], ((0, 0), (0, LANES - 3 * N_HEADS_A)))
    w_gate = jnp.concatenate([cols["ga"], cols["gb"]], axis=1)
    ws = [w.astype(BF16) for w in (w_rope, w_v, w_ng, cols["rw"], w_gate)]
    row = lambda wd: pl.BlockSpec((tm, wd), lambda i: (i, 0))
    outs = [(Q_PAD, BF16), (KV_WIDTH, F32), (KV_PAD, BF16), (KV_PAD, BF16), (KV_WIDTH, F32), (KV_PAD, BF16),
            (KV_PAD, BF16), (LANES, F32), (C_RWKV, F32), (D_MODEL, F32), (D_MODEL, F32)]
    return pl.pallas_call(
        _inproj_kernel,
        grid=(n // tm,),
        in_specs=[row(D_MODEL), _full((1, D_MODEL)), row(LANES), row(LANES)] + [_full(w.shape) for w in ws],
        out_specs=[row(wd) for wd, _ in outs],
        out_shape=[jax.ShapeDtypeStruct((n, wd), dt) for wd, dt in outs],
        compiler_params=_params(("parallel",)),
        name="inproj",
    )(x2, g_mix.reshape(1, D_MODEL), cs, sn, *ws)


def _compress_kernel(kv_ref, pe_ref, w1_ref, wg_ref, b1_ref, w2_ref, b2_ref, o_ref, *, pad_value):
    nch = o_ref.shape[2]
    z = None
    for l in range(D_CMP):
        part = _mm(kv_ref[0, pl.ds(l, nch, stride=D_CMP), :], wg_ref[0, l])
        z = part if z is None else z + part
    z1 = z[:, 0:CMP_HIDDEN]
    z2 = z[:, CMP_HIDDEN:2 * CMP_HIDDEN]
    z2 = jnp.concatenate([z2[1:], z2[:1]], axis=0)
    pb = _mm(pe_ref[...], w1_ref[...])[0:1] + b1_ref[...]
    hid = jax.nn.gelu(z1 + z2 + pb)
    out = _mm(hid, w2_ref[...]) + b2_ref[...]
    o_ref[0, 0] = jnp.concatenate([out, jnp.full(out.shape, pad_value, F32)], axis=1).astype(o_ref.dtype)


def _compress(kv, pe, w1, b1, w2, b2, bsz, s, pad_value):
    nch = s // D_CMP
    pe8 = jnp.broadcast_to(pe.reshape(1, L_CMP * HEAD_DIM), (8, L_CMP * HEAD_DIM))
    w1r = w1.reshape(2, D_CMP, HEAD_DIM, CMP_HIDDEN)
    wcat = jnp.concatenate([w1r[0], w1r[1]], axis=-1)
    wg = jnp.stack([jnp.pad(wcat, ((0, 0), (g * HEAD_DIM, (N_KV_GROUPS - 1 - g) * HEAD_DIM), (0, 0)))
                    for g in range(N_KV_GROUPS)]).astype(BF16)
    return pl.pallas_call(
        functools.partial(_compress_kernel, pad_value=pad_value),
        grid=(bsz, N_KV_GROUPS),
        in_specs=[pl.BlockSpec((1, s, KV_WIDTH), lambda b, g: (b, 0, 0)),
                  _full(pe8.shape), _full(w1.shape),
                  pl.BlockSpec((1, D_CMP, KV_WIDTH, 2 * CMP_HIDDEN), lambda b, g: (g, 0, 0, 0)),
                  _full((1, CMP_HIDDEN)), _full(w2.shape), _full((1, HEAD_DIM))],
        out_specs=pl.BlockSpec((1, 1, nch, LANES), lambda b, g: (b, g, 0, 0)),
        out_shape=jax.ShapeDtypeStruct((bsz, N_KV_GROUPS, nch, LANES), BF16),
        compiler_params=_params(("parallel", "parallel")),
        name="compress",
    )(kv.reshape(bsz, s, KV_WIDTH), pe8, w1, wg, b1.reshape(1, CMP_HIDDEN), w2, b2.reshape(1, HEAD_DIM))


def _nsa_kernel(q_ref, kc_ref, vc_ref, ksl_ref, vsl_ref, kwn_ref, vwn_ref, ng_ref, ov_ref, o_ref, *, n_top):
    n_cmp = kc_ref.shape[2]
    n_slc = ov_ref.shape[1]
    s0 = pl.program_id(1) * Q_BLOCK
    hq = HEADS_PER_GROUP * Q_BLOCK
    t_q = s0 + _iota((Q_BLOCK, 1), 0)
    sig = jax.nn.sigmoid(ng_ref[0])
    win_start = pl.multiple_of(jnp.maximum(s0 - WINDOW, 0), LANES)
    n_tiles = s0 // KEY_TILE + 1
    groups = range(N_KV_GROUPS)
    lanes = lambda g: slice(g * LANES, (g + 1) * LANES)
    qg = [jnp.concatenate([q_ref[0, :, lanes(g * HEADS_PER_GROUP + h)] for h in range(HEADS_PER_GROUP)], axis=0)
          for g in groups]

    def add_bias(s, bias):
        return (s.reshape(HEADS_PER_GROUP, Q_BLOCK, bias.shape[1]) + bias[None]).reshape(hq, bias.shape[1])

    cmp_bias = jnp.where((_iota((1, n_cmp), 1) * D_CMP + (L_CMP - 1)) <= t_q, 0.0, NEG_INF)
    has_cmp = jnp.concatenate([(t_q >= L_CMP - 1).astype(F32)] * HEADS_PER_GROUP, axis=0)
    pos_w = win_start + _iota((1, WINDOW + Q_BLOCK), 1)
    win_bias = jnp.where((pos_w <= t_q) & (pos_w > t_q - WINDOW), 0.0, NEG_INF)

    s_c = [add_bias(_mm_nt(qg[g], kc_ref[0, g]), cmp_bias) for g in groups]
    s_w = [add_bias(_mm_nt(qg[g], kwn_ref[0, pl.ds(win_start, WINDOW + Q_BLOCK), lanes(g)]), win_bias)
           for g in groups]
    e_c = [jnp.exp2(x - jnp.max(x, axis=-1, keepdims=True)) for x in s_c]
    p_c = [x * (has_cmp / jnp.sum(x, axis=-1, keepdims=True)) for x in e_c]
    e_w = [jnp.exp2(x - jnp.max(x, axis=-1, keepdims=True)) for x in s_w]
    o_c = [_mm(p_c[g], vc_ref[0, g]) for g in groups]
    acc_w = [_mm(e_w[g], vwn_ref[0, pl.ds(win_start, WINDOW + Q_BLOCK), lanes(g)]) for g in groups]
    o_win = [x * (1.0 / x[:, HEAD_DIM:HEAD_DIM + 1]) for x in acc_w]

    imp_t = []
    for g in groups:
        p_sum = p_c[g][0:Q_BLOCK]
        for h in range(1, HEADS_PER_GROUP):
            p_sum = p_sum + p_c[g][h * Q_BLOCK:(h + 1) * Q_BLOCK]
        imp_t.append(_mm_exact_rhs(p_sum, ov_ref[...]).T)
    j = _iota((n_slc, Q_BLOCK), 0)
    cur = (s0 + _iota((n_slc, Q_BLOCK), 1)) >> 6
    forced = (j == 0) | (j == cur) | (j == cur - 1)
    score = [jnp.where(forced, -3e38, jnp.where(j <= cur, x, -1.0)) for x in imp_t]
    sel_t = [forced.astype(F32) for _ in groups]
    for _ in range(n_top - 3):
        for g in groups:
            m = jnp.max(score[g], axis=0, keepdims=True)
            first = jnp.min(jnp.where(score[g] == m, j, n_slc), axis=0, keepdims=True)
            hit = j == first
            sel_t[g] = jnp.where(hit, 1.0, sel_t[g])
            score[g] = jnp.where(hit, -3e38, score[g])
    sel = [1.0 - x.T for x in sel_t]

    lane_q = _iota((Q_BLOCK, LANES), 1)
    in_code = (lane_q >= HEAD_DIM) & (lane_q < HEAD_DIM + BLOCKS_PER_TILE)

    def sel_step(kt, carry, diagonal):
        k0 = pl.multiple_of(kt * KEY_TILE, KEY_TILE)
        shift = (HEAD_DIM - BLOCKS_PER_TILE * kt) & (LANES - 1)
        s = []
        for g in groups:
            code = jnp.where(in_code, pltpu.roll(sel[g], shift, 1), 0.0).astype(BF16)
            qa = qg[g] + jnp.concatenate([code] * HEADS_PER_GROUP, axis=0)
            s.append(_mm_nt(qa, ksl_ref[0, pl.ds(k0, KEY_TILE), lanes(g)]))
        if diagonal:
            causal = jnp.where((k0 + _iota((Q_BLOCK, KEY_TILE), 1)) <= t_q, 0.0, NEG_INF)
            s = [add_bias(x, causal) for x in s]
        m_new = [jnp.maximum(carry[2 * g], jnp.max(s[g], axis=-1, keepdims=True)) for g in groups]
        p = [jnp.exp2(s[g] - m_new[g]) for g in groups]
        pv = [_mm(p[g], vsl_ref[0, pl.ds(k0, KEY_TILE), lanes(g)]) for g in groups]
        out = []
        for g in groups:
            out += [m_new[g], jnp.exp2(carry[2 * g] - m_new[g]) * carry[2 * g + 1] + pv[g]]
        return tuple(out)

    init = (jnp.full((hq, 1), NEG_INF, F32), jnp.zeros((hq, LANES), F32)) * N_KV_GROUPS
    fin = lax.fori_loop(0, n_tiles - 1, lambda kt, c: sel_step(kt, c, False), init)
    fin = sel_step(n_tiles - 1, fin, True)

    heads_out = []
    for g in groups:
        acc_s = fin[2 * g + 1]
        o_s = acc_s * (1.0 / acc_s[:, HEAD_DIM:HEAD_DIM + 1])
        o_w = o_win[g]
        for h in range(HEADS_PER_GROUP):
            hh = g * HEADS_PER_GROUP + h
            r = slice(h * Q_BLOCK, (h + 1) * Q_BLOCK)
            mix = (sig[:, 3 * hh:3 * hh + 1] * o_c[g][r] + sig[:, 3 * hh + 1:3 * hh + 2] * o_s[r]
                   + sig[:, 3 * hh + 2:3 * hh + 3] * o_w[r])
            heads_out.append(mix[:, 0:HEAD_DIM])
    o_ref[0] = jnp.concatenate(heads_out, axis=1).astype(o_ref.dtype)


def _nsa(q, kc, vc, ksl, vsl, kwn, vwn, ng, bsz, s):
    n_cmp = s // D_CMP
    n_slc = s // L_SLC
    ii = np.arange(n_cmp)[:, None]
    jj = np.arange(n_slc)[None, :]
    assert n_slc <= LANES and s % KEY_TILE == 0 and min(N_SELECT, n_slc) >= 3
    overlap = (ii * D_CMP < (jj + 1) * L_SLC) & (ii * D_CMP + L_CMP > jj * L_SLC)
    overlap = jnp.asarray(np.pad(overlap, ((0, 0), (0, LANES - n_slc))), BF16)
    seq = lambda wd: pl.BlockSpec((1, s, wd), lambda b, i: (b, 0, 0))
    blk = lambda wd: pl.BlockSpec((1, Q_BLOCK, wd), lambda b, i: (b, i, 0))
    cmp_spec = pl.BlockSpec((1, N_KV_GROUPS, n_cmp, LANES), lambda b, i: (b, 0, 0, 0))
    r3 = lambda a: a.reshape(bsz, s, a.shape[-1])
    return pl.pallas_call(
        functools.partial(_nsa_kernel, n_top=min(N_SELECT, n_slc)),
        grid=(bsz, s // Q_BLOCK),
        in_specs=[blk(Q_PAD), cmp_spec, cmp_spec, seq(KV_PAD), seq(KV_PAD), seq(KV_PAD), seq(KV_PAD),
                  blk(LANES), _full(overlap.shape)],
        out_specs=blk(WIDTH_A),
        out_shape=jax.ShapeDtypeStruct((bsz, s, WIDTH_A), BF16),
        compiler_params=_params(("parallel", "arbitrary")),
        name="nsa",
    )(r3(q), kc, vc, r3(ksl), r3(vsl), r3(kwn), r3(vwn), r3(ng), overlap)


PREP_SUB = 256


def _rwkv_prep_kernel(p_ref, pv_ref, mu_ref, w0_ref, w2_ref, a0_ref, a2_ref, g2_ref, kk_ref, ka_ref, bd_ref,
                      r_ref, lw_ref, k_ref, v_ref, kn_ref, kb_ref, g_ref, *, tiles_per_seq):
    p = p_ref[...]
    first = (pl.program_id(0) % tiles_per_seq) == 0
    prev_last = jnp.where(first, 0.0, pv_ref[7:8, :])
    p_prev = jnp.concatenate([prev_last, p[:-1]], axis=0)
    xs = p + (p_prev - p) * mu_ref[...]
    wb = WIDTH_B
    o = 3 * wb
    subs = [slice(j * PREP_SUB, (j + 1) * PREP_SUB) for j in range(p.shape[0] // PREP_SUB)]
    k = [xs[r, wb:2 * wb] for r in subs]
    kk = [x * kk_ref[...] for x in k]
    z = [w0_ref[...] + _mm_x3(jnp.tanh(xs[r, o:o + W_LORA]), w2_ref[...]) for r in subs]
    a = [jax.nn.sigmoid(a0_ref[...] + _mm_x3(xs[r, o + W_LORA:o + W_LORA + A_LORA], a2_ref[...])) for r in subs]
    ss = [_mm_exact_rhs(x * x, bd_ref[...]) for x in kk]
    g = [_mm(jax.nn.sigmoid(xs[r, o + W_LORA + A_LORA:o + W_LORA + A_LORA + G_LORA]), g2_ref[...]) for r in subs]
    for j, r in enumerate(subs):
        w_log = -(jnp.maximum(-z[j], 0.0) + jnp.log(1.0 + jnp.exp(-jnp.abs(z[j])))) - 0.5
        kn = kk[j] * (1.0 / jnp.maximum(jnp.sqrt(ss[j]), 1e-12))
        r_ref[r, :] = xs[r, 0:wb]
        lw_ref[r, :] = -jnp.exp(w_log)
        k_ref[r, :] = k[j] * (1.0 + (a[j] - 1.0) * ka_ref[...])
        v_ref[r, :] = xs[r, 2 * wb:3 * wb]
        kn_ref[r, :] = kn
        kb_ref[r, :] = kn * a[j]
        g_ref[r, :] = g[j]


def _rwkv_prep(prw, mu, w0, w2, a0, a2, g2, k_k, k_a, s, tm=2 * PREP_SUB):
    n = prw.shape[0]
    hid = np.arange(WIDTH_B) // HEAD_DIM_B
    bd = jnp.asarray(hid[:, None] == hid[None, :], BF16)
    row = lambda wd: pl.BlockSpec((tm, wd), lambda i: (i, 0))
    vec = lambda a: a.reshape(1, -1)
    return pl.pallas_call(
        functools.partial(_rwkv_prep_kernel, tiles_per_seq=s // tm),
        grid=(n // tm,),
        in_specs=[row(C_RWKV), pl.BlockSpec((8, C_RWKV), lambda i: (jnp.maximum(i * (tm // 8) - 1, 0), 0)),
                  _full((1, C_RWKV)), _full((1, WIDTH_B)), _full(w2.shape), _full((1, WIDTH_B)), _full(a2.shape),
                  _full(g2.shape), _full((1, WIDTH_B)), _full((1, WIDTH_B)), _full(bd.shape)],
        out_specs=[row(WIDTH_B)] * 7,
        out_shape=[jax.ShapeDtypeStruct((n, WIDTH_B), F32)] * 7,
        compiler_params=_params(("parallel",)),
        name="rwkv_prep",
    )(prw, prw, vec(mu), vec(w0), w2, vec(a0), a2, g2, vec(k_k), vec(k_a), bd)


def _rwkv_chunk_kernel(r_ref, lw_ref, k_ref, v_ref, kn_ref, kb_ref, rk_ref,
                       bonus_ref, tm_ref, *, n_sub):
    L = CHUNK
    hd = HEAD_DIM_B
    ti = _iota((L, L), 0)
    si = _iota((L, L), 1)
    strict = si < ti
    incl = si <= ti
    same_sub = (ti // SUB) == (si // SUB)
    eye = (ti == si).astype(F32)
    tri = incl.astype(BF16)
    at, bt, kt, rt, bl, kl, vv, rkr, p_tot = [], [], [], [], [], [], [], [], []
    for j in range(n_sub):
        rows = slice(j * L, (j + 1) * L)
        lw = lw_ref[0, rows, :]
        cs = _mm_exact_lhs(tri, lw)
        p_inv = jnp.exp(-cs)
        p_end = jnp.exp(cs[L - 1:L, :] - cs)
        r = r_ref[0, rows, :]
        k = k_ref[0, rows, :]
        kb = kb_ref[0, rows, :]
        at.append(-kn_ref[0, rows, :] * jnp.exp(cs - lw))
        bt.append(kb * p_inv)
        kt.append(k * p_inv)
        rt.append(r * jnp.exp(cs))
        bl.append(kb * p_end)
        kl.append(k * p_end)
        vv.append(v_ref[0, rows, :])
        rkr.append(r * k * rk_ref[...])
        p_tot.append(jnp.exp(cs[L - 1:L, :]))
    units = [(j, h) for j in range(n_sub) for h in range(N_HEADS_B)]
    sl = lambda arr, u: arr[u[0]][:, u[1] * hd:(u[1] + 1) * hd]
    aa = [_mm_nt(jnp.concatenate([sl(at, u), sl(rt, u)], axis=0),
                 jnp.concatenate([sl(bt, u), sl(kt, u)], axis=0)) for u in units]
    a_ab = [jnp.where(strict, x[0:L, 0:L], 0.0) for x in aa]
    a_low = [jnp.concatenate([jnp.where(strict, x[0:L, L:2 * L], 0.0),
                              jnp.where(incl, x[L:2 * L, L:2 * L], 0.0)], axis=0) for x in aa]
    a_rb = [jnp.where(incl, x[L:2 * L, 0:L], 0.0) for x in aa]
    av = [_mm(x, sl(vv, u)) for x, u in zip(a_low, units)]
    kv = [_mm_tn(sl(kl, u), sl(vv, u)) for u in units]
    pw = [jnp.where(same_sub, x, 0.0) for x in a_ab]
    t = [eye + x for x in pw]
    for _ in range(3):
        pw = [_mm(x, x) for x in pw]
        t = [x + _mm(x, y) for x, y in zip(t, pw)]
    width = SUB
    while width < L:
        m = ((ti // width) == (si // width) + 1) & ((ti // (2 * width)) == (si // (2 * width)))
        ot = [_mm(jnp.where(m, x, 0.0), y) for x, y in zip(a_ab, t)]
        t = [x + _mm(x, y) for x, y in zip(t, ot)]
        width *= 2
    wu = [_mm(x, jnp.concatenate([sl(at, u), y[0:L]], axis=1)) for x, y, u in zip(t, av, units)]
    qy = [jnp.concatenate([sl(rt, u), y[L:2 * L]], axis=1) + _mm(x, w)
          for x, y, w, u in zip(a_rb, av, wu, units)]
    gh = [_mm_tn(sl(bl, u), w) + jnp.concatenate([eye * sl(p_tot, u), y], axis=1)
          for w, y, u in zip(wu, kv, units)]
    bonus = [jnp.sum(sl(rkr, u), axis=-1, keepdims=True) * sl(vv, u) for u in units]
    for j in range(n_sub):
        bonus_ref[0, j * L:(j + 1) * L, :] = jnp.concatenate(bonus[j * N_HEADS_B:(j + 1) * N_HEADS_B], axis=1)
    for x, y, (j, h) in zip(qy, gh, units):
        tm_ref[0, j, h, 0:L, :] = x
        tm_ref[0, j, h, L:2 * L, :] = y


def _rwkv_chunk(r, lw, k, v, kn, kb, r_k, bsz, s, n_sub=4):
    nch = s // CHUNK
    blk = pl.BlockSpec((1, n_sub * CHUNK, WIDTH_B), lambda b, c: (b, c, 0))
    mat = pl.BlockSpec((1, n_sub, N_HEADS_B, 2 * CHUNK, 2 * HEAD_DIM_B), lambda b, c: (b, c, 0, 0, 0))
    r3 = lambda a: a.reshape(bsz, s, WIDTH_B)
    return pl.pallas_call(
        functools.partial(_rwkv_chunk_kernel, n_sub=n_sub),
        grid=(bsz, nch // n_sub),
        in_specs=[blk] * 6 + [_full((1, WIDTH_B))],
        out_specs=[blk, mat],
        out_shape=[jax.ShapeDtypeStruct((bsz, s, WIDTH_B), F32),
                   jax.ShapeDtypeStruct((bsz, nch, N_HEADS_B, 2 * CHUNK, 2 * HEAD_DIM_B), F32)],
        compiler_params=_params(("parallel", "parallel")),
        name="rwkv_chunk",
    )(r3(r), r3(lw), r3(k), r3(v), r3(kn), r3(kb), r_k.reshape(1, WIDTH_B))


def _rwkv_scan_kernel(tm_ref, yn_ref, st_ref):
    L = CHUNK
    hd = HEAD_DIM_B

    @pl.when(pl.program_id(0) == 0)
    def _():
        eye = (_iota((hd, hd), 0) == _iota((hd, hd), 1)).astype(F32)
        for i in range(st_ref.shape[0]):
            st_ref[i] = jnp.concatenate([jnp.zeros((hd, hd), F32), eye], axis=0)

    for b in range(tm_ref.shape[0]):
        res = [_mm_x3(tm_ref[b, 0, h], st_ref[b * N_HEADS_B + h]) for h in range(N_HEADS_B)]
        ys = []
        for h in range(N_HEADS_B):
            st_ref[b * N_HEADS_B + h, 0:hd, :] = res[h][L:2 * L]
            y = res[h][0:L]
            mean = jnp.mean(y, axis=-1, keepdims=True)
            d = y - mean
            var = jnp.mean(d * d, axis=-1, keepdims=True)
            ys.append(d * lax.rsqrt(var + GN_EPS))
        yn_ref[b] = jnp.concatenate(ys, axis=1)


def _rwkv_scan(tmat, bsz, s):
    nch = s // CHUNK
    return pl.pallas_call(
        _rwkv_scan_kernel,
        grid=(nch,),
        in_specs=[pl.BlockSpec((bsz, 1, N_HEADS_B, 2 * CHUNK, 2 * HEAD_DIM_B), lambda c: (0, c, 0, 0, 0))],
        out_specs=pl.BlockSpec((bsz, CHUNK, WIDTH_B), lambda c: (0, c, 0)),
        out_shape=jax.ShapeDtypeStruct((bsz, s, WIDTH_B), F32),
        scratch_shapes=[pltpu.VMEM((bsz * N_HEADS_B, 2 * HEAD_DIM_B, HEAD_DIM_B), F32)],
        compiler_params=_params(("arbitrary",)),
        name="rwkv_scan",
    )(tmat)


SUBLANES = 8
assert D_MODEL == SUBLANES * LANES


def _row_tiles_shape(rows):
    return (rows // SUBLANES, SUBLANES, SUBLANES, LANES)


def _row_tiles_spec(rows):
    return pl.BlockSpec(_row_tiles_shape(rows), lambda i, *_: (i, 0, 0, 0))


def _store_row_tiles(ref, x):
    for c in range(SUBLANES):
        ref[:, c] = x[:, c * LANES:(c + 1) * LANES].reshape(x.shape[0] // SUBLANES, SUBLANES, LANES)


def _row_view(ref, r):
    return ref.at[pl.ds(r >> 3, 1), :, r & (SUBLANES - 1)]


def _gather_buffer(rows):
    return pltpu.VMEM((2, rows // SUBLANES, SUBLANES, SUBLANES, LANES), F32)


def _gathered_rows(buf, slot, start, rows):
    t0, nt = start // SUBLANES, rows // SUBLANES
    return jnp.concatenate([buf[slot, t0:t0 + nt, c].reshape(rows, LANES) for c in range(SUBLANES)], axis=1)


MERGE_SUB = 256


def _route(logits, run):
    tm = logits.shape[0]
    lane = _iota(logits.shape, 1)
    big = 4 * LANES
    is_grp = lane < N_GROUPS
    lg = jnp.where(is_grp, logits, NEG_INF)
    mg = jnp.max(lg, axis=-1, keepdims=True)
    gidx = jnp.min(jnp.where(lg == mg, lane, big), axis=-1, keepdims=True)
    pg = 1.0 / jnp.sum(jnp.where(is_grp, jnp.exp(lg - mg), 0.0), axis=-1, keepdims=True)
    in_grp = (lane >= N_GROUPS) & (((lane - N_GROUPS) >> 3) == gidx) & (lane < N_GROUPS + N_EXPERTS)
    le = jnp.where(in_grp, logits, NEG_INF)
    m1 = jnp.max(le, axis=-1, keepdims=True)
    i1 = jnp.min(jnp.where(le == m1, lane, big), axis=-1, keepdims=True)
    le2 = jnp.where(lane == i1, NEG_INF, le)
    m2 = jnp.max(le2, axis=-1, keepdims=True)
    i2 = jnp.min(jnp.where(le2 == m2, lane, big), axis=-1, keepdims=True)
    e2 = jnp.exp(m2 - m1)
    inv = pg / (1.0 + e2)
    hot1 = (lane == i1 - N_GROUPS).astype(F32)
    hot2 = (lane == i2 - N_GROUPS).astype(F32)
    both = hot1 + hot2
    earlier = (_iota((tm, tm), 1) < _iota((tm, tm), 0)).astype(BF16)
    before = jnp.dot(earlier, both.astype(BF16), preferred_element_type=F32) + run
    rank1 = jnp.sum(hot1 * before, axis=-1, keepdims=True)
    rank2 = jnp.sum(hot2 * before, axis=-1, keepdims=True)
    cols = ((i1 - N_GROUPS).astype(F32), (i2 - N_GROUPS).astype(F32), inv, inv * e2, rank1, rank2)
    route = jnp.zeros_like(logits)
    for c, v in enumerate(cols):
        route = jnp.where(lane == c, v, route)
    return route, run + jnp.sum(both, axis=0, keepdims=True)


def _merge_kernel(x_ref, oa_ref, yn_ref, bonus_ref, g_ref, ga_ref, gb_ref, lnw_ref, lnb_ref, wa_ref, wb_ref,
                  wo_ref, gf_ref, wr_ref, br_ref, x1_ref, h2_ref, route_ref, cnt_ref, run_ref):
    @pl.when(pl.program_id(0) == 0)
    def _():
        run_ref[...] = jnp.zeros_like(run_ref)

    subs = [slice(j * MERGE_SUB, (j + 1) * MERGE_SUB) for j in range(x_ref.shape[0] // MERGE_SUB)]
    ob = [(yn_ref[r, :] * lnw_ref[...] + lnb_ref[...] + bonus_ref[r, :]) * g_ref[r, :] for r in subs]
    pa = [jnp.dot(oa_ref[r, :], wa_ref[...], preferred_element_type=F32) for r in subs]
    pb = [_mm(x, wb_ref[...]) for x in ob]
    merged = [jax.nn.sigmoid(ga_ref[r, :]) * a + jax.nn.sigmoid(gb_ref[r, :]) * b for r, a, b in zip(subs, pa, pb)]
    x1 = [x_ref[r, :] + _mm(m, wo_ref[...]) for r, m in zip(subs, merged)]
    h2 = [x * lax.rsqrt(jnp.mean(x * x, axis=-1, keepdims=True) + RMS_EPS) * gf_ref[...] for x in x1]
    logits = []
    for x in h2:
        hi = x.astype(BF16)
        lo = (x - hi.astype(F32)).astype(BF16)
        logits.append(jnp.dot(jnp.concatenate([hi, hi, lo], axis=1), wr_ref[...], preferred_element_type=F32)
                      + br_ref[...])
    run = run_ref[...]
    for r, x, h, lg in zip(subs, x1, h2, logits):
        x1_ref[r, :] = x
        _store_row_tiles(h2_ref.at[pl.ds(r.start // SUBLANES, MERGE_SUB // SUBLANES)], h)
        route_ref[r, :], run = _route(lg, run)
    run_ref[...] = run
    cnt_ref[...] = run


def _merge(x2, oa, yn, bonus, g, ga, gb, ln_w, ln_b, w_a, w_b, w_o, g_ffn, w_grp, b_grp, w_exp, b_exp,
           tm=2 * MERGE_SUB):
    n = x2.shape[0]
    w_r = jnp.pad(jnp.concatenate([w_grp, w_exp], axis=1), ((0, 0), (0, LANES - N_GROUPS - N_EXPERTS)))
    w_hi = w_r.astype(BF16)
    w_lo = (w_r - w_hi.astype(F32)).astype(BF16)
    w_r = jnp.concatenate([w_hi, w_lo, w_hi], axis=0)
    b_r = jnp.pad(jnp.concatenate([b_grp, b_exp]), (0, LANES - N_GROUPS - N_EXPERTS)).reshape(1, LANES)
    row = lambda wd: pl.BlockSpec((tm, wd), lambda i: (i, 0))
    vec = lambda a: a.reshape(1, -1)
    wa, wb, wo = w_a.astype(BF16), w_b.astype(BF16), w_o.astype(BF16)
    return pl.pallas_call(
        _merge_kernel,
        grid=(n // tm,),
        in_specs=[row(D_MODEL), row(WIDTH_A), row(WIDTH_B), row(WIDTH_B), row(WIDTH_B), row(D_MODEL), row(D_MODEL),
                  _full((1, WIDTH_B)), _full((1, WIDTH_B)), _full(wa.shape), _full(wb.shape), _full(wo.shape),
                  _full((1, D_MODEL)), _full(w_r.shape), _full((1, LANES))],
        out_specs=[row(D_MODEL), _row_tiles_spec(tm), row(LANES), _full((1, LANES))],
        out_shape=[jax.ShapeDtypeStruct((n, D_MODEL), F32), jax.ShapeDtypeStruct(_row_tiles_shape(n), F32),
                   jax.ShapeDtypeStruct((n, LANES), F32), jax.ShapeDtypeStruct((1, LANES), F32)],
        scratch_shapes=[pltpu.VMEM((1, LANES), F32)],
        compiler_params=_params(("arbitrary",)),
        name="merge",
    )(x2, oa, yn, bonus, g, ga, gb, vec(ln_w), vec(ln_b), wa, wb, wo, vec(g_ffn), w_r, b_r)


def _gather_start(idx_ref, src_hbm, buf, sem, slot, rows):
    def body(t, carry):
        for u in range(SUBLANES):
            pltpu.make_async_copy(_row_view(src_hbm, idx_ref[0, 0, t * SUBLANES + u]),
                                  buf.at[slot, pl.ds(t, 1), :, u], sem.at[slot]).start(priority=u % 2)
        return carry
    lax.fori_loop(0, rows // SUBLANES, body, 0)


def _gather_wait(buf, sem, slot):
    pltpu.make_async_copy(buf.at[slot], buf.at[slot], sem.at[slot]).wait()


def _gather_pipeline(idx_ref, idx_next_ref, src_hbm, buf, sem, rows):
    i = pl.program_id(0)
    slot = i % 2

    @pl.when(i == 0)
    def _():
        _gather_start(idx_ref, src_hbm, buf, sem, 0, rows)

    @pl.when(i + 1 < pl.num_programs(0))
    def _():
        _gather_start(idx_next_ref, src_hbm, buf, sem, 1 - slot, rows)

    _gather_wait(buf, sem, slot)
    return slot


def _dispatch_kernel(end_ref, pos_ref, h2_ref, x_hbm, zeros, sem, zsem, *, tm):
    tile_rows = MOE_BLOCK // SUBLANES

    @pl.when(pl.program_id(0) == 0)
    def _():
        zeros[...] = jnp.zeros_like(zeros)

        def fill(block):
            return pltpu.make_async_copy(zeros, x_hbm.at[pl.ds(block * tile_rows, tile_rows)], zsem.at[0])

        for act in ("start", "wait"):
            for e in range(N_EXPERTS):
                start = end_ref[e - 1] if e else 0

                @pl.when(end_ref[e] > start)
                def _():
                    getattr(fill(end_ref[e] // MOE_BLOCK - 1), act)()

            def tail(block, carry):
                getattr(fill(block), act)()
                return carry
            lax.fori_loop(end_ref[N_EXPERTS - 1] // MOE_BLOCK, x_hbm.shape[0] // tile_rows, tail, 0)

    def body(t, carry):
        for u in range(SUBLANES):
            r = t * SUBLANES + u
            for k in range(TOP_K):
                p = pos_ref[0, 0, k * tm + r]
                pltpu.make_async_copy(h2_ref.at[pl.ds(t, 1), :, u], _row_view(x_hbm, p),
                                      sem.at[k]).start(priority=(u + k) % 2)
        return carry
    lax.fori_loop(0, tm // SUBLANES, body, 0)
    for k in range(TOP_K):
        pltpu.make_async_copy(h2_ref, h2_ref, sem.at[k]).wait()


def _dispatch(h2, pos, pad_end, n_pad, tm=256):
    nt = h2.shape[0] * SUBLANES // tm
    pos3 = pos.reshape(nt, tm, TOP_K).transpose(0, 2, 1).reshape(nt, 1, TOP_K * tm)
    tiles = _row_tiles_shape(n_pad)
    grid_spec = pltpu.PrefetchScalarGridSpec(
        num_scalar_prefetch=1,
        grid=(nt,),
        in_specs=[pl.BlockSpec((1, 1, TOP_K * tm), lambda i, pe: (i, 0, 0), memory_space=pltpu.SMEM),
                  _row_tiles_spec(tm)],
        out_specs=pl.BlockSpec(memory_space=pl.ANY),
        scratch_shapes=[pltpu.VMEM(_row_tiles_shape(MOE_BLOCK), F32),
                        pltpu.SemaphoreType.DMA((TOP_K,)), pltpu.SemaphoreType.DMA((1,))],
    )
    return pl.pallas_call(
        functools.partial(_dispatch_kernel, tm=tm),
        grid_spec=grid_spec,
        out_shape=jax.ShapeDtypeStruct(tiles, F32),
        compiler_params=_params(("arbitrary",)),
        name="dispatch",
    )(pad_end, pos3, h2)


def _moe_kernel(be_ref, x_ref, w13_ref, w2_ref, y_ref):
    xb = jnp.concatenate([x_ref[:, c].reshape(MOE_BLOCK, LANES) for c in range(SUBLANES)], axis=1)
    a = _mm(xb, w13_ref[0])
    hid = jax.nn.silu(a[:, 0:D_EXPERT]) * a[:, D_EXPERT:2 * D_EXPERT]
    _store_row_tiles(y_ref, _mm(hid, w2_ref[0]))


def _moe(x_pad, blk_expert, w13, w2):
    n_blk = blk_expert.shape[0]
    grid_spec = pltpu.PrefetchScalarGridSpec(
        num_scalar_prefetch=1,
        grid=(n_blk,),
        in_specs=[_row_tiles_spec(MOE_BLOCK),
                  pl.BlockSpec((1, D_MODEL, 2 * D_EXPERT), lambda i, be: (be[i], 0, 0)),
                  pl.BlockSpec((1, D_EXPERT, D_MODEL), lambda i, be: (be[i], 0, 0))],
        out_specs=_row_tiles_spec(MOE_BLOCK),
    )
    return pl.pallas_call(
        _moe_kernel,
        grid_spec=grid_spec,
        out_shape=jax.ShapeDtypeStruct(_row_tiles_shape(n_blk * MOE_BLOCK), F32),
        compiler_params=_params(("parallel",)),
        name="moe",
    )(blk_expert, x_pad, w13, w2)


def _final_kernel(pos_ref, pos_next_ref, x1_ref, route_ref, g_ref, y_hbm, o_ref, buf, sem, *, tm):
    slot = _gather_pipeline(pos_ref, pos_next_ref, y_hbm, buf, sem, TOP_K * tm)
    x = x1_ref[...]
    for k in range(TOP_K):
        x = x + route_ref[:, TOP_K + k:TOP_K + k + 1] * _gathered_rows(buf, slot, k * tm, tm)
    o_ref[...] = x * lax.rsqrt(jnp.mean(x * x, axis=-1, keepdims=True) + RMS_EPS) * g_ref[...]


def _final(x1, route, y_pad, pos, g_final, tm=256):
    n = x1.shape[0]
    nt = n // tm
    pos3 = pos.reshape(nt, tm, TOP_K).transpose(0, 2, 1).reshape(nt, 1, TOP_K * tm)
    smem_blk = lambda f: pl.BlockSpec((1, 1, TOP_K * tm), f, memory_space=pltpu.SMEM)
    row = lambda wd: pl.BlockSpec((tm, wd), lambda i: (i, 0))
    return pl.pallas_call(
        functools.partial(_final_kernel, tm=tm),
        grid=(nt,),
        in_specs=[smem_blk(lambda i: (i, 0, 0)), smem_blk(lambda i: (jnp.minimum(i + 1, nt - 1), 0, 0)),
                  row(D_MODEL), row(LANES), _full((1, D_MODEL)), pl.BlockSpec(memory_space=pl.ANY)],
        out_specs=row(D_MODEL),
        out_shape=jax.ShapeDtypeStruct((n, D_MODEL), F32),
        scratch_shapes=[_gather_buffer(TOP_K * tm), pltpu.SemaphoreType.DMA((2,))],
        compiler_params=_params(("arbitrary",)),
        name="final",
    )(pos3, pos3, x1, route, g_final.reshape(1, D_MODEL), y_pad)


def _dispatch_plan(route, counts, n_tok):
    expert = route[:, 0:TOP_K].astype(I32)
    rank = route[:, 2 * TOP_K:3 * TOP_K].astype(I32)
    cnt = counts[0, 0:N_EXPERTS].astype(I32)
    padded = (cnt + MOE_BLOCK - 1) // MOE_BLOCK * MOE_BLOCK
    ids = jnp.arange(N_EXPERTS, dtype=I32)
    pad_end = jnp.sum(jnp.where(ids[None, :] <= ids[:, None], padded[None, :], 0), axis=1)
    pad_start = pad_end - padded
    pos = jnp.sum(jnp.where(expert[..., None] == ids, pad_start, 0), axis=-1) + rank
    n_pad = n_tok * TOP_K + N_EXPERTS * MOE_BLOCK
    n_blk = n_pad // MOE_BLOCK
    blk_start = jnp.arange(n_blk, dtype=I32) * MOE_BLOCK
    blk_expert = jnp.minimum(jnp.sum((pad_end[None, :] <= blk_start[:, None]).astype(I32), axis=1), N_EXPERTS - 1)
    return pos, blk_expert, pad_end, n_pad


def kernel(x, positions, g_mix, w_in, cmp_pe_k, cmp_w1_k, cmp_b1_k, cmp_w2_k, cmp_b2_k, cmp_pe_v, cmp_w1_v, cmp_b1_v, cmp_w2_v, cmp_b2_v, rwkv_mu, rwkv_w0, rwkv_w2, rwkv_a0, rwkv_a2, rwkv_g2, rwkv_k_k, rwkv_k_a, rwkv_r_k, rwkv_ln_w, rwkv_ln_b, w_proj_a, w_proj_b, w_out, g_ffn, w_grp, b_grp, w_exp, b_exp, e_w1, e_w3, e_w2, g_final):
    bsz, s, _ = x.shape
    assert w_in.shape[0] == 1, "single-layer stack"
    n_tok = bsz * s
    x2 = x.reshape(n_tok, D_MODEL)
    half = HEAD_DIM // 2
    freqs = jnp.power(ROPE_THETA, -jnp.arange(half, dtype=F32) / half)
    ang = positions.astype(F32).reshape(n_tok, 1) * freqs
    cs = jnp.tile(jnp.cos(ang), (1, 4))
    sn = jnp.tile(jnp.concatenate([-jnp.sin(ang), jnp.sin(ang)], axis=1), (1, 2))
    q, kc, ksl, kwn, vc, vsl, vwn, ng, prw, ga, gb = _inproj(x2, g_mix[0], cs, sn, w_in[0])
    k_c = _compress(kc, cmp_pe_k[0], cmp_w1_k[0], cmp_b1_k[0], cmp_w2_k[0], cmp_b2_k[0], bsz, s, 0.0)
    v_c = _compress(vc, cmp_pe_v[0], cmp_w1_v[0], cmp_b1_v[0], cmp_w2_v[0], cmp_b2_v[0], bsz, s, 1.0)
    o_a = _nsa(q, k_c, v_c, ksl, vsl, kwn, vwn, ng, bsz, s).reshape(n_tok, WIDTH_A)
    r, lw, k, v, kn, kb, g = _rwkv_prep(prw, rwkv_mu[0], rwkv_w0[0], rwkv_w2[0], rwkv_a0[0], rwkv_a2[0],
                                        rwkv_g2[0], rwkv_k_k[0], rwkv_k_a[0], s)
    bonus, tmat = _rwkv_chunk(r, lw, k, v, kn, kb, rwkv_r_k[0], bsz, s)
    yn = _rwkv_scan(tmat, bsz, s)
    x1, h2, route, counts = _merge(x2, o_a, yn.reshape(n_tok, WIDTH_B), bonus.reshape(n_tok, WIDTH_B), g, ga, gb,
                           rwkv_ln_w[0], rwkv_ln_b[0], w_proj_a[0], w_proj_b[0], w_out[0], g_ffn[0],
                           w_grp[0], b_grp[0], w_exp[0], b_exp[0])
    pos, blk_expert, pad_end, n_pad = _dispatch_plan(route, counts, n_tok)
    w13 = jnp.concatenate([e_w1[0], e_w3[0]], axis=-1).astype(BF16)
    y_pad = _moe(_dispatch(h2, pos, pad_end, n_pad), blk_expert, w13, e_w2[0].astype(BF16))
    out = _final(x1, route, y_pad, pos, g_final)
    return out.reshape(bsz, s, D_MODEL)
```

```python
import functools

import numpy as np
import jax
import jax.numpy as jnp
from jax import lax
from jax.experimental import pallas as pl
from jax.experimental.pallas import tpu as pltpu

F32 = jnp.float32
BF16 = jnp.bfloat16
I32 = jnp.int32

D_MODEL = 1024
N_HEADS_A = 8
N_KV_GROUPS = 2
HEADS_PER_GROUP = N_HEADS_A // N_KV_GROUPS
HEAD_DIM = 64
L_CMP = 32
D_CMP = 16
CMP_HIDDEN = 256
L_SLC = 64
N_SELECT = 16
WINDOW = 512
Q_BLOCK = 128
ROPE_THETA = 10000.0
FORCE_SCORE = 1e4
NEG_INF = -1e30
N_HEADS_B = 8
HEAD_DIM_B = 64
W_LORA = 64
A_LORA = 64
G_LORA = 128
GN_EPS = 64e-5
N_GROUPS = 4
EXPERTS_PER_GROUP = 8
N_EXPERTS = N_GROUPS * EXPERTS_PER_GROUP
TOP_K = 2
D_EXPERT = 256
MOE_BLOCK = 256
RMS_EPS = 1e-6
WIDTH_A = N_HEADS_A * HEAD_DIM
KV_WIDTH = N_KV_GROUPS * HEAD_DIM
WIDTH_B = N_HEADS_B * HEAD_DIM_B
C_RWKV = 3 * WIDTH_B + W_LORA + A_LORA + G_LORA

LANES = 128
CHUNK = 64
SUB = 16
KEY_TILE = 1024
BLOCKS_PER_TILE = KEY_TILE // L_SLC
VMEM_LIMIT = 56 * 1024 * 1024

HI = lax.Precision.HIGHEST
LOG2_E = 1.4426950408889634


def _mm(a, b):
    return jnp.dot(a.astype(BF16), b.astype(BF16), preferred_element_type=F32)


def _mm_nt(a, b):
    return lax.dot_general(a.astype(BF16), b.astype(BF16), (((1,), (1,)), ((), ())), preferred_element_type=F32)


def _mm_tn(a, b):
    return lax.dot_general(a.astype(BF16), b.astype(BF16), (((0,), (0,)), ((), ())), preferred_element_type=F32)


def _mm_hi(a, b):
    return jnp.dot(a, b, preferred_element_type=F32, precision=HI)


def _mm_x3(a, b):
    a1 = a.astype(BF16)
    a2 = (a - a1.astype(F32)).astype(BF16)
    b1 = b.astype(BF16)
    b2 = (b - b1.astype(F32)).astype(BF16)
    dot = lambda x, y: jnp.dot(x, y, preferred_element_type=F32)
    return dot(a1, b1) + dot(a1, b2) + dot(a2, b1)


def _bf16_pieces(a):
    a1 = a.astype(BF16)
    r1 = a - a1.astype(F32)
    a2 = r1.astype(BF16)
    return a1, a2, (r1 - a2.astype(F32)).astype(BF16)


def _mm_exact_rhs(a, b):
    bb = b.astype(BF16)
    return sum(jnp.dot(x, bb, preferred_element_type=F32) for x in _bf16_pieces(a))


def _mm_exact_lhs(a, b):
    aa = a.astype(BF16)
    return sum(jnp.dot(aa, x, preferred_element_type=F32) for x in _bf16_pieces(b))


def _iota(shape, dim):
    return lax.broadcasted_iota(I32, shape, dim)


def _params(sem):
    return pltpu.CompilerParams(dimension_semantics=sem, vmem_limit_bytes=VMEM_LIMIT)


def _full(shape):
    nd = len(shape)
    return pl.BlockSpec(shape, lambda *_: (0,) * nd)


Q_PAD = N_HEADS_A * LANES
KV_PAD = N_KV_GROUPS * LANES


def _spread_heads(x, pad_value):
    low = _iota((1, LANES), 1) < HEAD_DIM
    out = []
    for c in range(x.shape[1] // LANES):
        pair = x[:, c * LANES:(c + 1) * LANES]
        out.append(jnp.where(low, pair, pad_value))
        out.append(jnp.where(low, pltpu.roll(pair, HEAD_DIM, 1), pad_value))
    return jnp.concatenate(out, axis=1)


def _inproj_kernel(x_ref, g_ref, cs_ref, sn_ref, wr_ref, wv_ref, wn_ref, ww_ref, wg_ref,
                   q_ref, kc_ref, ksl_ref, kwn_ref, vc_ref, vsl_ref, vwn_ref, ng_ref, prw_ref, ga_ref, gb_ref):
    x = x_ref[...]
    h = (x * lax.rsqrt(jnp.mean(x * x, axis=-1, keepdims=True) + RMS_EPS) * g_ref[...]).astype(BF16)
    pr = jnp.dot(h, wr_ref[...], preferred_element_type=F32)
    cs = cs_ref[...]
    sn = sn_ref[...]
    first_half = (_iota((1, LANES), 1) & (HEAD_DIM // 2)) == 0
    chunks = []
    for c in range(pr.shape[1] // LANES):
        x = pr[:, c * LANES:(c + 1) * LANES]
        swapped = jnp.where(first_half, pltpu.roll(x, LANES - HEAD_DIM // 2, 1), pltpu.roll(x, HEAD_DIM // 2, 1))
        chunks.append(x * cs + swapped * sn)
    ro = jnp.concatenate(chunks, axis=1)
    q_ref[...] = _spread_heads(ro[:, 0:WIDTH_A] * (HEAD_DIM ** -0.5 * LOG2_E), 0.0).astype(BF16)
    o = WIDTH_A
    kc_ref[...] = ro[:, o:o + KV_WIDTH]
    tm = x_ref.shape[0]
    blk = ((pl.program_id(0) * tm + _iota((tm, 1), 0)) >> 6) & (BLOCKS_PER_TILE - 1)
    code = jnp.where((_iota((1, KV_PAD), 1) & (LANES - 1)) == HEAD_DIM + blk, NEG_INF, 0.0)
    ksl_ref[...] = (_spread_heads(ro[:, o + KV_WIDTH:o + 2 * KV_WIDTH], 0.0) + code).astype(BF16)
    kwn_ref[...] = _spread_heads(ro[:, o + 2 * KV_WIDTH:o + 3 * KV_WIDTH], 0.0).astype(BF16)
    v = jnp.dot(h, wv_ref[...], preferred_element_type=F32)
    vc_ref[...] = v[:, 0:KV_WIDTH]
    vsl_ref[...] = _spread_heads(v[:, KV_WIDTH:2 * KV_WIDTH], 1.0).astype(BF16)
    vwn_ref[...] = _spread_heads(v[:, 2 * KV_WIDTH:3 * KV_WIDTH], 1.0).astype(BF16)
    ng_ref[...] = jnp.dot(h, wn_ref[...], preferred_element_type=F32)
    prw_ref[...] = jnp.dot(h, ww_ref[...], preferred_element_type=F32)
    gg = jnp.dot(h, wg_ref[...], preferred_element_type=F32)
    ga_ref[...] = gg[:, 0:D_MODEL]
    gb_ref[...] = gg[:, D_MODEL:2 * D_MODEL]


def _inproj(x2, g_mix, cs, sn, w_in, tm=512):
    n = x2.shape[0]
    o = 0
    cols = {}
    for name, wd in (("q", WIDTH_A), ("kc", KV_WIDTH), ("vc", KV_WIDTH), ("ksl", KV_WIDTH), ("vsl", KV_WIDTH),
                     ("kwn", KV_WIDTH), ("vwn", KV_WIDTH), ("ng", 3 * N_HEADS_A), ("rw", C_RWKV),
                     ("ga", D_MODEL), ("gb", D_MODEL)):
        cols[name] = w_in[:, o:o + wd]
        o += wd
    w_rope = jnp.concatenate([cols["q"], cols["kc"], cols["ksl"], cols["kwn"]], axis=1)
    w_v = jnp.concatenate([cols["vc"], cols["vsl"], cols["vwn"]], axis=1)
    w_ng = jnp.pad(cols["ng"], ((0, 0), (0, LANES - 3 * N_HEADS_A)))
    w_gate = jnp.concatenate([cols["ga"], cols["gb"]], axis=1)
    ws = [w.astype(BF16) for w in (w_rope, w_v, w_ng, cols["rw"], w_gate)]
    row = lambda wd: pl.BlockSpec((tm, wd), lambda i: (i, 0))
    outs = [(Q_PAD, BF16), (KV_WIDTH, F32), (KV_PAD, BF16), (KV_PAD, BF16), (KV_WIDTH, F32), (KV_PAD, BF16),
            (KV_PAD, BF16), (LANES, F32), (C_RWKV, F32), (D_MODEL, F32), (D_MODEL, F32)]
    return pl.pallas_call(
        _inproj_kernel,
        grid=(n // tm,),
        in_specs=[row(D_MODEL), _full((1, D_MODEL)), row(LANES), row(LANES)] + [_full(w.shape) for w in ws],
        out_specs=[row(wd) for wd, _ in outs],
        out_shape=[jax.ShapeDtypeStruct((n, wd), dt) for wd, dt in outs],
        compiler_params=_params(("parallel",)),
        name="inproj",
    )(x2, g_mix.reshape(1, D_MODEL), cs, sn, *ws)


def _compress_kernel(kv_ref, pe_ref, w1_ref, wg_ref, b1_ref, w2_ref, b2_ref, o_ref, *, pad_value):
    nch = o_ref.shape[2]
    z = None
    for l in range(D_CMP):
        part = _mm(kv_ref[0, pl.ds(l, nch, stride=D_CMP), :], wg_ref[0, l])
        z = part if z is None else z + part
    z1 = z[:, 0:CMP_HIDDEN]
    z2 = z[:, CMP_HIDDEN:2 * CMP_HIDDEN]
    z2 = jnp.concatenate([z2[1:], z2[:1]], axis=0)
    pb = _mm(pe_ref[...], w1_ref[...])[0:1] + b1_ref[...]
    hid = jax.nn.gelu(z1 + z2 + pb)
    out = _mm(hid, w2_ref[...]) + b2_ref[...]
    o_ref[0, 0] = jnp.concatenate([out, jnp.full(out.shape, pad_value, F32)], axis=1).astype(o_ref.dtype)


def _compress(kv, pe, w1, b1, w2, b2, bsz, s, pad_value):
    nch = s // D_CMP
    pe8 = jnp.broadcast_to(pe.reshape(1, L_CMP * HEAD_DIM), (8, L_CMP * HEAD_DIM))
    w1r = w1.reshape(2, D_CMP, HEAD_DIM, CMP_HIDDEN)
    wcat = jnp.concatenate([w1r[0], w1r[1]], axis=-1)
    wg = jnp.stack([jnp.pad(wcat, ((0, 0), (g * HEAD_DIM, (N_KV_GROUPS - 1 - g) * HEAD_DIM), (0, 0)))
                    for g in range(N_KV_GROUPS)]).astype(BF16)
    return pl.pallas_call(
        functools.partial(_compress_kernel, pad_value=pad_value),
        grid=(bsz, N_KV_GROUPS),
        in_specs=[pl.BlockSpec((1, s, KV_WIDTH), lambda b, g: (b, 0, 0)),
                  _full(pe8.shape), _full(w1.shape),
                  pl.BlockSpec((1, D_CMP, KV_WIDTH, 2 * CMP_HIDDEN), lambda b, g: (g, 0, 0, 0)),
                  _full((1, CMP_HIDDEN)), _full(w2.shape), _full((1, HEAD_DIM))],
        out_specs=pl.BlockSpec((1, 1, nch, LANES), lambda b, g: (b, g, 0, 0)),
        out_shape=jax.ShapeDtypeStruct((bsz, N_KV_GROUPS, nch, LANES), BF16),
        compiler_params=_params(("parallel", "parallel")),
        name="compress",
    )(kv.reshape(bsz, s, KV_WIDTH), pe8, w1, wg, b1.reshape(1, CMP_HIDDEN), w2, b2.reshape(1, HEAD_DIM))


def _nsa_kernel(q_ref, kc_ref, vc_ref, ksl_ref, vsl_ref, kwn_ref, vwn_ref, ng_ref, ov_ref, o_ref, *, n_top):
    n_cmp = kc_ref.shape[2]
    n_slc = ov_ref.shape[1]
    s0 = pl.program_id(1) * Q_BLOCK
    hq = HEADS_PER_GROUP * Q_BLOCK
    t_q = s0 + _iota((Q_BLOCK, 1), 0)
    sig = jax.nn.sigmoid(ng_ref[0])
    win_start = pl.multiple_of(jnp.maximum(s0 - WINDOW, 0), LANES)
    n_tiles = s0 // KEY_TILE + 1
    groups = range(N_KV_GROUPS)
    lanes = lambda g: slice(g * LANES, (g + 1) * LANES)
    qg = [jnp.concatenate([q_ref[0, :, lanes(g * HEADS_PER_GROUP + h)] for h in range(HEADS_PER_GROUP)], axis=0)
          for g in groups]

    def add_bias(s, bias):
        return (s.reshape(HEADS_PER_GROUP, Q_BLOCK, bias.shape[1]) + bias[None]).reshape(hq, bias.shape[1])

    cmp_bias = jnp.where((_iota((1, n_cmp), 1) * D_CMP + (L_CMP - 1)) <= t_q, 0.0, NEG_INF)
    has_cmp = jnp.concatenate([(t_q >= L_CMP - 1).astype(F32)] * HEADS_PER_GROUP, axis=0)
    pos_w = win_start + _iota((1, WINDOW + Q_BLOCK), 1)
    win_bias = jnp.where((pos_w <= t_q) & (pos_w > t_q - WINDOW), 0.0, NEG_INF)

    s_c = [add_bias(_mm_nt(qg[g], kc_ref[0, g]), cmp_bias) for g in groups]
    s_w = [add_bias(_mm_nt(qg[g], kwn_ref[0, pl.ds(win_start, WINDOW + Q_BLOCK), lanes(g)]), win_bias)
           for g in groups]
    e_c = [jnp.exp2(x - jnp.max(x, axis=-1, keepdims=True)) for x in s_c]
    p_c = [x * (has_cmp / jnp.sum(x, axis=-1, keepdims=True)) for x in e_c]
    e_w = [jnp.exp2(x - jnp.max(x, axis=-1, keepdims=True)) for x in s_w]
    o_c = [_mm(p_c[g], vc_ref[0, g]) for g in groups]
    acc_w = [_mm(e_w[g], vwn_ref[0, pl.ds(win_start, WINDOW + Q_BLOCK), lanes(g)]) for g in groups]
    o_win = [x * (1.0 / x[:, HEAD_DIM:HEAD_DIM + 1]) for x in acc_w]

    imp_t = []
    for g in groups:
        p_sum = p_c[g][0:Q_BLOCK]
        for h in range(1, HEADS_PER_GROUP):
            p_sum = p_sum + p_c[g][h * Q_BLOCK:(h + 1) * Q_BLOCK]
        imp_t.append(_mm_exact_rhs(p_sum, ov_ref[...]).T)
    j = _iota((n_slc, Q_BLOCK), 0)
    cur = (s0 + _iota((n_slc, Q_BLOCK), 1)) >> 6
    forced = (j == 0) | (j == cur) | (j == cur - 1)
    score = [jnp.where(forced, -3e38, jnp.where(j <= cur, x, -1.0)) for x in imp_t]
    sel_t = [forced.astype(F32) for _ in groups]
    for _ in range(n_top - 3):
        for g in groups:
            m = jnp.max(score[g], axis=0, keepdims=True)
            first = jnp.min(jnp.where(score[g] == m, j, n_slc), axis=0, keepdims=True)
            hit = j == first
            sel_t[g] = jnp.where(hit, 1.0, sel_t[g])
            score[g] = jnp.where(hit, -3e38, score[g])
    sel = [1.0 - x.T for x in sel_t]

    lane_q = _iota((Q_BLOCK, LANES), 1)
    in_code = (lane_q >= HEAD_DIM) & (lane_q < HEAD_DIM + BLOCKS_PER_TILE)

    def sel_step(kt, carry, diagonal):
        k0 = pl.multiple_of(kt * KEY_TILE, KEY_TILE)
        shift = (HEAD_DIM - BLOCKS_PER_TILE * kt) & (LANES - 1)
        s = []
        for g in groups:
            code = jnp.where(in_code, pltpu.roll(sel[g], shift, 1), 0.0).astype(BF16)
            qa = qg[g] + jnp.concatenate([code] * HEADS_PER_GROUP, axis=0)
            s.append(_mm_nt(qa, ksl_ref[0, pl.ds(k0, KEY_TILE), lanes(g)]))
        if diagonal:
            causal = jnp.where((k0 + _iota((Q_BLOCK, KEY_TILE), 1)) <= t_q, 0.0, NEG_INF)
            s = [add_bias(x, causal) for x in s]
        m_new = [jnp.maximum(carry[2 * g], jnp.max(s[g], axis=-1, keepdims=True)) for g in groups]
        p = [jnp.exp2(s[g] - m_new[g]) for g in groups]
        pv = [_mm(p[g], vsl_ref[0, pl.ds(k0, KEY_TILE), lanes(g)]) for g in groups]
        out = []
        for g in groups:
            out += [m_new[g], jnp.exp2(carry[2 * g] - m_new[g]) * carry[2 * g + 1] + pv[g]]
        return tuple(out)

    init = (jnp.full((hq, 1), NEG_INF, F32), jnp.zeros((hq, LANES), F32)) * N_KV_GROUPS
    fin = lax.fori_loop(0, n_tiles - 1, lambda kt, c: sel_step(kt, c, False), init)
    fin = sel_step(n_tiles - 1, fin, True)

    heads_out = []
    for g in groups:
        acc_s = fin[2 * g + 1]
        o_s = acc_s * (1.0 / acc_s[:, HEAD_DIM:HEAD_DIM + 1])
        o_w = o_win[g]
        for h in range(HEADS_PER_GROUP):
            hh = g * HEADS_PER_GROUP + h
            r = slice(h * Q_BLOCK, (h + 1) * Q_BLOCK)
            mix = (sig[:, 3 * hh:3 * hh + 1] * o_c[g][r] + sig[:, 3 * hh + 1:3 * hh + 2] * o_s[r]
                   + sig[:, 3 * hh + 2:3 * hh + 3] * o_w[r])
            heads_out.append(mix[:, 0:HEAD_DIM])
    o_ref[0] = jnp.concatenate(heads_out, axis=1).astype(o_ref.dtype)


def _nsa(q, kc, vc, ksl, vsl, kwn, vwn, ng, bsz, s):
    n_cmp = s // D_CMP
    n_slc = s // L_SLC
    ii = np.arange(n_cmp)[:, None]
    jj = np.arange(n_slc)[None, :]
    assert n_slc <= LANES and s % KEY_TILE == 0 and min(N_SELECT, n_slc) >= 3
    overlap = (ii * D_CMP < (jj + 1) * L_SLC) & (ii * D_CMP + L_CMP > jj * L_SLC)
    overlap = jnp.asarray(np.pad(overlap, ((0, 0), (0, LANES - n_slc))), BF16)
    seq = lambda wd: pl.BlockSpec((1, s, wd), lambda b, i: (b, 0, 0))
    blk = lambda wd: pl.BlockSpec((1, Q_BLOCK, wd), lambda b, i: (b, i, 0))
    cmp_spec = pl.BlockSpec((1, N_KV_GROUPS, n_cmp, LANES), lambda b, i: (b, 0, 0, 0))
    r3 = lambda a: a.reshape(bsz, s, a.shape[-1])
    return pl.pallas_call(
        functools.partial(_nsa_kernel, n_top=min(N_SELECT, n_slc)),
        grid=(bsz, s // Q_BLOCK),
        in_specs=[blk(Q_PAD), cmp_spec, cmp_spec, seq(KV_PAD), seq(KV_PAD), seq(KV_PAD), seq(KV_PAD),
                  blk(LANES), _full(overlap.shape)],
        out_specs=blk(WIDTH_A),
        out_shape=jax.ShapeDtypeStruct((bsz, s, WIDTH_A), BF16),
        compiler_params=_params(("parallel", "arbitrary")),
        name="nsa",
    )(r3(q), kc, vc, r3(ksl), r3(vsl), r3(kwn), r3(vwn), r3(ng), overlap)


PREP_SUB = 256


def _rwkv_prep_kernel(p_ref, pv_ref, mu_ref, w0_ref, w2_ref, a0_ref, a2_ref, g2_ref, kk_ref, ka_ref, rk_ref, bd_ref,
                      r_ref, lw_ref, k_ref, v_ref, kn_ref, kb_ref, g_ref, bonus_ref, *, tiles_per_seq):
    p = p_ref[...]
    first = (pl.program_id(0) % tiles_per_seq) == 0
    prev_last = jnp.where(first, 0.0, pv_ref[7:8, :])
    p_prev = jnp.concatenate([prev_last, p[:-1]], axis=0)
    xs = p + (p_prev - p) * mu_ref[...]
    wb = WIDTH_B
    o = 3 * wb
    subs = [slice(j * PREP_SUB, (j + 1) * PREP_SUB) for j in range(p.shape[0] // PREP_SUB)]
    k = [xs[r, wb:2 * wb] for r in subs]
    kk = [x * kk_ref[...] for x in k]
    z = [w0_ref[...] + _mm_x3(jnp.tanh(xs[r, o:o + W_LORA]), w2_ref[...]) for r in subs]
    a = [jax.nn.sigmoid(a0_ref[...] + _mm_x3(xs[r, o + W_LORA:o + W_LORA + A_LORA], a2_ref[...])) for r in subs]
    ss = [_mm_exact_rhs(x * x, bd_ref[...]) for x in kk]
    g = [_mm(jax.nn.sigmoid(xs[r, o + W_LORA + A_LORA:o + W_LORA + A_LORA + G_LORA]), g2_ref[...]) for r in subs]
    k_out = [x * (1.0 + (y - 1.0) * ka_ref[...]) for x, y in zip(k, a)]
    rk_sum = [_mm_exact_rhs(xs[r, 0:wb] * x * rk_ref[...], bd_ref[...]) for r, x in zip(subs, k_out)]
    for j, r in enumerate(subs):
        w_log = -(jnp.maximum(-z[j], 0.0) + jnp.log(1.0 + jnp.exp(-jnp.abs(z[j])))) - 0.5
        kn = kk[j] * (1.0 / jnp.maximum(jnp.sqrt(ss[j]), 1e-12))
        r_ref[r, :] = xs[r, 0:wb]
        lw_ref[r, :] = -jnp.exp(w_log)
        k_ref[r, :] = k_out[j]
        v_ref[r, :] = xs[r, 2 * wb:3 * wb]
        bonus_ref[r, :] = rk_sum[j] * xs[r, 2 * wb:3 * wb]
        kn_ref[r, :] = kn
        kb_ref[r, :] = kn * a[j]
        g_ref[r, :] = g[j]


def _rwkv_prep(prw, mu, w0, w2, a0, a2, g2, k_k, k_a, r_k, s, tm=2 * PREP_SUB):
    n = prw.shape[0]
    hid = np.arange(WIDTH_B) // HEAD_DIM_B
    bd = jnp.asarray(hid[:, None] == hid[None, :], BF16)
    row = lambda wd: pl.BlockSpec((tm, wd), lambda i: (i, 0))
    vec = lambda a: a.reshape(1, -1)
    return pl.pallas_call(
        functools.partial(_rwkv_prep_kernel, tiles_per_seq=s // tm),
        grid=(n // tm,),
        in_specs=[row(C_RWKV), pl.BlockSpec((8, C_RWKV), lambda i: (jnp.maximum(i * (tm // 8) - 1, 0), 0)),
                  _full((1, C_RWKV)), _full((1, WIDTH_B)), _full(w2.shape), _full((1, WIDTH_B)), _full(a2.shape),
                  _full(g2.shape), _full((1, WIDTH_B)), _full((1, WIDTH_B)), _full((1, WIDTH_B)), _full(bd.shape)],
        out_specs=[row(WIDTH_B)] * 8,
        out_shape=[jax.ShapeDtypeStruct((n, WIDTH_B), F32)] * 8,
        compiler_params=_params(("parallel",)),
        name="rwkv_prep",
    )(prw, prw, vec(mu), vec(w0), w2, vec(a0), a2, g2, vec(k_k), vec(k_a), vec(r_k), bd)


def _rwkv_chunk_kernel(r_ref, lw_ref, k_ref, v_ref, kn_ref, kb_ref, tm_ref, *, n_sub):
    L = CHUNK
    hd = HEAD_DIM_B
    ti = _iota((L, L), 0)
    si = _iota((L, L), 1)
    strict = si < ti
    incl = si <= ti
    same_sub = (ti // SUB) == (si // SUB)
    eye = (ti == si).astype(F32)
    tri = incl.astype(BF16)
    at, bt, kt, rt, bl, kl, vv, p_tot = [], [], [], [], [], [], [], []
    for j in range(n_sub):
        rows = slice(j * L, (j + 1) * L)
        lw = lw_ref[0, rows, :]
        cs = _mm_exact_lhs(tri, lw)
        p_inv = jnp.exp(-cs)
        p_end = jnp.exp(cs[L - 1:L, :] - cs)
        r = r_ref[0, rows, :]
        k = k_ref[0, rows, :]
        kb = kb_ref[0, rows, :]
        at.append(-kn_ref[0, rows, :] * jnp.exp(cs - lw))
        bt.append(kb * p_inv)
        kt.append(k * p_inv)
        rt.append(r * jnp.exp(cs))
        bl.append(kb * p_end)
        kl.append(k * p_end)
        vv.append(v_ref[0, rows, :])
        p_tot.append(jnp.exp(cs[L - 1:L, :]))
    units = [(j, h) for j in range(n_sub) for h in range(N_HEADS_B)]
    sl = lambda arr, u: arr[u[0]][:, u[1] * hd:(u[1] + 1) * hd]
    aa = [_mm_nt(jnp.concatenate([sl(at, u), sl(rt, u)], axis=0),
                 jnp.concatenate([sl(bt, u), sl(kt, u)], axis=0)) for u in units]
    a_ab = [jnp.where(strict, x[0:L, 0:L], 0.0) for x in aa]
    a_low = [jnp.concatenate([jnp.where(strict, x[0:L, L:2 * L], 0.0),
                              jnp.where(incl, x[L:2 * L, L:2 * L], 0.0)], axis=0) for x in aa]
    a_rb = [jnp.where(incl, x[L:2 * L, 0:L], 0.0) for x in aa]
    av = [_mm(x, sl(vv, u)) for x, u in zip(a_low, units)]
    kv = [_mm_tn(sl(kl, u), sl(vv, u)) for u in units]
    pw = [jnp.where(same_sub, x, 0.0) for x in a_ab]
    t = [eye + x for x in pw]
    for _ in range(3):
        pw = [_mm(x, x) for x in pw]
        t = [x + _mm(x, y) for x, y in zip(t, pw)]
    width = SUB
    while width < L:
        m = ((ti // width) == (si // width) + 1) & ((ti // (2 * width)) == (si // (2 * width)))
        ot = [_mm(jnp.where(m, x, 0.0), y) for x, y in zip(a_ab, t)]
        t = [x + _mm(x, y) for x, y in zip(t, ot)]
        width *= 2
    wu = [_mm(x, jnp.concatenate([sl(at, u), y[0:L]], axis=1)) for x, y, u in zip(t, av, units)]
    qy = [jnp.concatenate([sl(rt, u), y[L:2 * L]], axis=1) + _mm(x, w)
          for x, y, w, u in zip(a_rb, av, wu, units)]
    gh = [_mm_tn(sl(bl, u), w) + jnp.concatenate([eye * sl(p_tot, u), y], axis=1)
          for w, y, u in zip(wu, kv, units)]
    for x, y, (j, h) in zip(qy, gh, units):
        tm_ref[0, j, h, 0:L, :] = x
        tm_ref[0, j, h, L:2 * L, :] = y


def _rwkv_chunk(r, lw, k, v, kn, kb, bsz, s, n_sub=4):
    nch = s // CHUNK
    blk = pl.BlockSpec((1, n_sub * CHUNK, WIDTH_B), lambda b, c: (b, c, 0))
    mat = pl.BlockSpec((1, n_sub, N_HEADS_B, 2 * CHUNK, 2 * HEAD_DIM_B), lambda b, c: (b, c, 0, 0, 0))
    r3 = lambda a: a.reshape(bsz, s, WIDTH_B)
    return pl.pallas_call(
        functools.partial(_rwkv_chunk_kernel, n_sub=n_sub),
        grid=(bsz, nch // n_sub),
        in_specs=[blk] * 6,
        out_specs=mat,
        out_shape=jax.ShapeDtypeStruct((bsz, nch, N_HEADS_B, 2 * CHUNK, 2 * HEAD_DIM_B), F32),
        compiler_params=_params(("parallel", "parallel")),
        name="rwkv_chunk",
    )(r3(r), r3(lw), r3(k), r3(v), r3(kn), r3(kb))


def _rwkv_scan_kernel(tm_ref, yn_ref, st_ref):
    L = CHUNK
    hd = HEAD_DIM_B

    @pl.when(pl.program_id(0) == 0)
    def _():
        eye = (_iota((hd, hd), 0) == _iota((hd, hd), 1)).astype(F32)
        for i in range(st_ref.shape[0]):
            st_ref[i] = jnp.concatenate([jnp.zeros((hd, hd), F32), eye], axis=0)

    for b in range(tm_ref.shape[0]):
        res = [_mm_x3(tm_ref[b, 0, h], st_ref[b * N_HEADS_B + h]) for h in range(N_HEADS_B)]
        ys = []
        for h in range(N_HEADS_B):
            st_ref[b * N_HEADS_B + h, 0:hd, :] = res[h][L:2 * L]
            y = res[h][0:L]
            mean = jnp.mean(y, axis=-1, keepdims=True)
            d = y - mean
            var = jnp.mean(d * d, axis=-1, keepdims=True)
            ys.append(d * lax.rsqrt(var + GN_EPS))
        yn_ref[b] = jnp.concatenate(ys, axis=1)


def _rwkv_scan(tmat, bsz, s):
    nch = s // CHUNK
    return pl.pallas_call(
        _rwkv_scan_kernel,
        grid=(nch,),
        in_specs=[pl.BlockSpec((bsz, 1, N_HEADS_B, 2 * CHUNK, 2 * HEAD_DIM_B), lambda c: (0, c, 0, 0, 0))],
        out_specs=pl.BlockSpec((bsz, CHUNK, WIDTH_B), lambda c: (0, c, 0)),
        out_shape=jax.ShapeDtypeStruct((bsz, s, WIDTH_B), F32),
        scratch_shapes=[pltpu.VMEM((bsz * N_HEADS_B, 2 * HEAD_DIM_B, HEAD_DIM_B), F32)],
        compiler_params=_params(("arbitrary",)),
        name="rwkv_scan",
    )(tmat)


SUBLANES = 8
assert D_MODEL == SUBLANES * LANES


def _row_tiles_shape(rows):
    return (rows // SUBLANES, SUBLANES, SUBLANES, LANES)


def _row_tiles_spec(rows):
    return pl.BlockSpec(_row_tiles_shape(rows), lambda i, *_: (i, 0, 0, 0))


def _store_row_tiles(ref, x):
    for c in range(SUBLANES):
        ref[:, c] = x[:, c * LANES:(c + 1) * LANES].reshape(x.shape[0] // SUBLANES, SUBLANES, LANES)


def _row_view(ref, r):
    return ref.at[pl.ds(r >> 3, 1), :, r & (SUBLANES - 1)]


def _gather_buffer(rows):
    return pltpu.VMEM((2, rows // SUBLANES, SUBLANES, SUBLANES, LANES), F32)


def _gathered_rows(buf, slot, start, rows):
    t0, nt = start // SUBLANES, rows // SUBLANES
    return jnp.concatenate([buf[slot, t0:t0 + nt, c].reshape(rows, LANES) for c in range(SUBLANES)], axis=1)


MERGE_SUB = 256


def _route(logits, run):
    tm = logits.shape[0]
    lane = _iota(logits.shape, 1)
    big = 4 * LANES
    is_grp = lane < N_GROUPS
    lg = jnp.where(is_grp, logits, NEG_INF)
    mg = jnp.max(lg, axis=-1, keepdims=True)
    gidx = jnp.min(jnp.where(lg == mg, lane, big), axis=-1, keepdims=True)
    pg = 1.0 / jnp.sum(jnp.where(is_grp, jnp.exp(lg - mg), 0.0), axis=-1, keepdims=True)
    in_grp = (lane >= N_GROUPS) & (((lane - N_GROUPS) >> 3) == gidx) & (lane < N_GROUPS + N_EXPERTS)
    le = jnp.where(in_grp, logits, NEG_INF)
    m1 = jnp.max(le, axis=-1, keepdims=True)
    i1 = jnp.min(jnp.where(le == m1, lane, big), axis=-1, keepdims=True)
    le2 = jnp.where(lane == i1, NEG_INF, le)
    m2 = jnp.max(le2, axis=-1, keepdims=True)
    i2 = jnp.min(jnp.where(le2 == m2, lane, big), axis=-1, keepdims=True)
    e2 = jnp.exp(m2 - m1)
    inv = pg / (1.0 + e2)
    hot1 = (lane == i1 - N_GROUPS).astype(F32)
    hot2 = (lane == i2 - N_GROUPS).astype(F32)
    both = hot1 + hot2
    earlier = (_iota((tm, tm), 1) < _iota((tm, tm), 0)).astype(BF16)
    before = jnp.dot(earlier, both.astype(BF16), preferred_element_type=F32) + run
    rank1 = jnp.sum(hot1 * before, axis=-1, keepdims=True)
    rank2 = jnp.sum(hot2 * before, axis=-1, keepdims=True)
    cols = ((i1 - N_GROUPS).astype(F32), (i2 - N_GROUPS).astype(F32), inv, inv * e2, rank1, rank2)
    route = jnp.zeros_like(logits)
    for c, v in enumerate(cols):
        route = jnp.where(lane == c, v, route)
    return route, run + jnp.sum(both, axis=0, keepdims=True)


def _merge_kernel(x_ref, oa_ref, yn_ref, bonus_ref, g_ref, ga_ref, gb_ref, lnw_ref, lnb_ref, wa_ref, wb_ref,
                  wo_ref, gf_ref, wr_ref, br_ref, x1_ref, h2_ref, route_ref, cnt_ref, run_ref):
    @pl.when(pl.program_id(0) == 0)
    def _():
        run_ref[...] = jnp.zeros_like(run_ref)

    subs = [slice(j * MERGE_SUB, (j + 1) * MERGE_SUB) for j in range(x_ref.shape[0] // MERGE_SUB)]
    ob = [(yn_ref[r, :] * lnw_ref[...] + lnb_ref[...] + bonus_ref[r, :]) * g_ref[r, :] for r in subs]
    pa = [jnp.dot(oa_ref[r, :], wa_ref[...], preferred_element_type=F32) for r in subs]
    pb = [_mm(x, wb_ref[...]) for x in ob]
    merged = [jax.nn.sigmoid(ga_ref[r, :]) * a + jax.nn.sigmoid(gb_ref[r, :]) * b for r, a, b in zip(subs, pa, pb)]
    x1 = [x_ref[r, :] + _mm(m, wo_ref[...]) for r, m in zip(subs, merged)]
    h2 = [x * lax.rsqrt(jnp.mean(x * x, axis=-1, keepdims=True) + RMS_EPS) * gf_ref[...] for x in x1]
    logits = []
    for x in h2:
        hi = x.astype(BF16)
        lo = (x - hi.astype(F32)).astype(BF16)
        logits.append(jnp.dot(jnp.concatenate([hi, hi, lo], axis=1), wr_ref[...], preferred_element_type=F32)
                      + br_ref[...])
    run = run_ref[...]
    for r, x, h, lg in zip(subs, x1, h2, logits):
        x1_ref[r, :] = x
        _store_row_tiles(h2_ref.at[pl.ds(r.start // SUBLANES, MERGE_SUB // SUBLANES)], h)
        route_ref[r, :], run = _route(lg, run)
    run_ref[...] = run
    cnt_ref[...] = run


def _merge(x2, oa, yn, bonus, g, ga, gb, ln_w, ln_b, w_a, w_b, w_o, g_ffn, w_grp, b_grp, w_exp, b_exp,
           tm=2 * MERGE_SUB):
    n = x2.shape[0]
    w_r = jnp.pad(jnp.concatenate([w_grp, w_exp], axis=1), ((0, 0), (0, LANES - N_GROUPS - N_EXPERTS)))
    w_hi = w_r.astype(BF16)
    w_lo = (w_r - w_hi.astype(F32)).astype(BF16)
    w_r = jnp.concatenate([w_hi, w_lo, w_hi], axis=0)
    b_r = jnp.pad(jnp.concatenate([b_grp, b_exp]), (0, LANES - N_GROUPS - N_EXPERTS)).reshape(1, LANES)
    row = lambda wd: pl.BlockSpec((tm, wd), lambda i: (i, 0))
    vec = lambda a: a.reshape(1, -1)
    wa, wb, wo = w_a.astype(BF16), w_b.astype(BF16), w_o.astype(BF16)
    return pl.pallas_call(
        _merge_kernel,
        grid=(n // tm,),
        in_specs=[row(D_MODEL), row(WIDTH_A), row(WIDTH_B), row(WIDTH_B), row(WIDTH_B), row(D_MODEL), row(D_MODEL),
                  _full((1, WIDTH_B)), _full((1, WIDTH_B)), _full(wa.shape), _full(wb.shape), _full(wo.shape),
                  _full((1, D_MODEL)), _full(w_r.shape), _full((1, LANES))],
        out_specs=[row(D_MODEL), _row_tiles_spec(tm), row(LANES), _full((1, LANES))],
        out_shape=[jax.ShapeDtypeStruct((n, D_MODEL), F32), jax.ShapeDtypeStruct(_row_tiles_shape(n), F32),
                   jax.ShapeDtypeStruct((n, LANES), F32), jax.ShapeDtypeStruct((1, LANES), F32)],
        scratch_shapes=[pltpu.VMEM((1, LANES), F32)],
        compiler_params=_params(("arbitrary",)),
        name="merge",
    )(x2, oa, yn, bonus, g, ga, gb, vec(ln_w), vec(ln_b), wa, wb, wo, vec(g_ffn), w_r, b_r)


def _gather_start(idx_ref, src_hbm, buf, sem, slot, rows):
    def body(t, carry):
        for u in range(SUBLANES):
            pltpu.make_async_copy(_row_view(src_hbm, idx_ref[0, 0, t * SUBLANES + u]),
                                  buf.at[slot, pl.ds(t, 1), :, u], sem.at[slot]).start(priority=u % 2)
        return carry
    lax.fori_loop(0, rows // SUBLANES, body, 0)


def _gather_wait(buf, sem, slot):
    pltpu.make_async_copy(buf.at[slot], buf.at[slot], sem.at[slot]).wait()


def _gather_pipeline(idx_ref, idx_next_ref, src_hbm, buf, sem, rows):
    i = pl.program_id(0)
    slot = i % 2

    @pl.when(i == 0)
    def _():
        _gather_start(idx_ref, src_hbm, buf, sem, 0, rows)

    @pl.when(i + 1 < pl.num_programs(0))
    def _():
        _gather_start(idx_next_ref, src_hbm, buf, sem, 1 - slot, rows)

    _gather_wait(buf, sem, slot)
    return slot


def _dispatch_kernel(end_ref, pos_ref, h2_ref, x_hbm, zeros, sem, zsem, *, tm):
    tile_rows = MOE_BLOCK // SUBLANES

    @pl.when(pl.program_id(0) == 0)
    def _():
        zeros[...] = jnp.zeros_like(zeros)

        def fill(block):
            return pltpu.make_async_copy(zeros, x_hbm.at[pl.ds(block * tile_rows, tile_rows)], zsem.at[0])

        for act in ("start", "wait"):
            for e in range(N_EXPERTS):
                start = end_ref[e - 1] if e else 0

                @pl.when(end_ref[e] > start)
                def _():
                    getattr(fill(end_ref[e] // MOE_BLOCK - 1), act)()

            def tail(block, carry):
                getattr(fill(block), act)()
                return carry
            lax.fori_loop(end_ref[N_EXPERTS - 1] // MOE_BLOCK, x_hbm.shape[0] // tile_rows, tail, 0)

    def body(t, carry):
        for u in range(SUBLANES):
            r = t * SUBLANES + u
            for k in range(TOP_K):
                p = pos_ref[0, 0, k * tm + r]
                pltpu.make_async_copy(h2_ref.at[pl.ds(t, 1), :, u], _row_view(x_hbm, p),
                                      sem.at[k]).start(priority=(u + k) % 2)
        return carry
    lax.fori_loop(0, tm // SUBLANES, body, 0)
    for k in range(TOP_K):
        pltpu.make_async_copy(h2_ref, h2_ref, sem.at[k]).wait()


def _dispatch(h2, pos, pad_end, n_pad, tm=256):
    nt = h2.shape[0] * SUBLANES // tm
    pos3 = pos.reshape(nt, tm, TOP_K).transpose(0, 2, 1).reshape(nt, 1, TOP_K * tm)
    tiles = _row_tiles_shape(n_pad)
    grid_spec = pltpu.PrefetchScalarGridSpec(
        num_scalar_prefetch=1,
        grid=(nt,),
        in_specs=[pl.BlockSpec((1, 1, TOP_K * tm), lambda i, pe: (i, 0, 0), memory_space=pltpu.SMEM),
                  _row_tiles_spec(tm)],
        out_specs=pl.BlockSpec(memory_space=pl.ANY),
        scratch_shapes=[pltpu.VMEM(_row_tiles_shape(MOE_BLOCK), F32),
                        pltpu.SemaphoreType.DMA((TOP_K,)), pltpu.SemaphoreType.DMA((1,))],
    )
    return pl.pallas_call(
        functools.partial(_dispatch_kernel, tm=tm),
        grid_spec=grid_spec,
        out_shape=jax.ShapeDtypeStruct(tiles, F32),
        compiler_params=_params(("arbitrary",)),
        name="dispatch",
    )(pad_end, pos3, h2)


def _moe_kernel(be_ref, x_ref, w13_ref, w2_ref, y_ref):
    xb = jnp.concatenate([x_ref[:, c].reshape(MOE_BLOCK, LANES) for c in range(SUBLANES)], axis=1)
    a = _mm(xb, w13_ref[0])
    hid = jax.nn.silu(a[:, 0:D_EXPERT]) * a[:, D_EXPERT:2 * D_EXPERT]
    _store_row_tiles(y_ref, _mm(hid, w2_ref[0]))


def _moe(x_pad, blk_expert, w13, w2):
    n_blk = blk_expert.shape[0]
    grid_spec = pltpu.PrefetchScalarGridSpec(
        num_scalar_prefetch=1,
        grid=(n_blk,),
        in_specs=[_row_tiles_spec(MOE_BLOCK),
                  pl.BlockSpec((1, D_MODEL, 2 * D_EXPERT), lambda i, be: (be[i], 0, 0)),
                  pl.BlockSpec((1, D_EXPERT, D_MODEL), lambda i, be: (be[i], 0, 0))],
        out_specs=_row_tiles_spec(MOE_BLOCK),
    )
    return pl.pallas_call(
        _moe_kernel,
        grid_spec=grid_spec,
        out_shape=jax.ShapeDtypeStruct(_row_tiles_shape(n_blk * MOE_BLOCK), F32),
        compiler_params=_params(("parallel",)),
        name="moe",
    )(blk_expert, x_pad, w13, w2)


def _final_kernel(pos_ref, pos_next_ref, x1_ref, route_ref, g_ref, y_hbm, o_ref, buf, sem, *, tm):
    slot = _gather_pipeline(pos_ref, pos_next_ref, y_hbm, buf, sem, TOP_K * tm)
    x = x1_ref[...]
    for k in range(TOP_K):
        x = x + route_ref[:, TOP_K + k:TOP_K + k + 1] * _gathered_rows(buf, slot, k * tm, tm)
    o_ref[...] = x * lax.rsqrt(jnp.mean(x * x, axis=-1, keepdims=True) + RMS_EPS) * g_ref[...]


def _final(x1, route, y_pad, pos, g_final, tm=256):
    n = x1.shape[0]
    nt = n // tm
    pos3 = pos.reshape(nt, tm, TOP_K).transpose(0, 2, 1).reshape(nt, 1, TOP_K * tm)
    smem_blk = lambda f: pl.BlockSpec((1, 1, TOP_K * tm), f, memory_space=pltpu.SMEM)
    row = lambda wd: pl.BlockSpec((tm, wd), lambda i: (i, 0))
    return pl.pallas_call(
        functools.partial(_final_kernel, tm=tm),
        grid=(nt,),
        in_specs=[smem_blk(lambda i: (i, 0, 0)), smem_blk(lambda i: (jnp.minimum(i + 1, nt - 1), 0, 0)),
                  row(D_MODEL), row(LANES), _full((1, D_MODEL)), pl.BlockSpec(memory_space=pl.ANY)],
        out_specs=row(D_MODEL),
        out_shape=jax.ShapeDtypeStruct((n, D_MODEL), F32),
        scratch_shapes=[_gather_buffer(TOP_K * tm), pltpu.SemaphoreType.DMA((2,))],
        compiler_params=_params(("arbitrary",)),
        name="final",
    )(pos3, pos3, x1, route, g_final.reshape(1, D_MODEL), y_pad)


def _dispatch_plan(route, counts, n_tok):
    expert = route[:, 0:TOP_K].astype(I32)
    rank = route[:, 2 * TOP_K:3 * TOP_K].astype(I32)
    cnt = counts[0, 0:N_EXPERTS].astype(I32)
    padded = (cnt + MOE_BLOCK - 1) // MOE_BLOCK * MOE_BLOCK
    ids = jnp.arange(N_EXPERTS, dtype=I32)
    pad_end = jnp.sum(jnp.where(ids[None, :] <= ids[:, None], padded[None, :], 0), axis=1)
    pad_start = pad_end - padded
    pos = jnp.sum(jnp.where(expert[..., None] == ids, pad_start, 0), axis=-1) + rank
    n_pad = n_tok * TOP_K + N_EXPERTS * MOE_BLOCK
    n_blk = n_pad // MOE_BLOCK
    blk_start = jnp.arange(n_blk, dtype=I32) * MOE_BLOCK
    blk_expert = jnp.minimum(jnp.sum((pad_end[None, :] <= blk_start[:, None]).astype(I32), axis=1), N_EXPERTS - 1)
    return pos, blk_expert, pad_end, n_pad


def kernel(x, positions, g_mix, w_in, cmp_pe_k, cmp_w1_k, cmp_b1_k, cmp_w2_k, cmp_b2_k, cmp_pe_v, cmp_w1_v, cmp_b1_v, cmp_w2_v, cmp_b2_v, rwkv_mu, rwkv_w0, rwkv_w2, rwkv_a0, rwkv_a2, rwkv_g2, rwkv_k_k, rwkv_k_a, rwkv_r_k, rwkv_ln_w, rwkv_ln_b, w_proj_a, w_proj_b, w_out, g_ffn, w_grp, b_grp, w_exp, b_exp, e_w1, e_w3, e_w2, g_final):
    bsz, s, _ = x.shape
    assert w_in.shape[0] == 1, "single-layer stack"
    n_tok = bsz * s
    x2 = x.reshape(n_tok, D_MODEL)
    half = HEAD_DIM // 2
    freqs = jnp.power(ROPE_THETA, -jnp.arange(half, dtype=F32) / half)
    ang = positions.astype(F32).reshape(n_tok, 1) * freqs
    cs = jnp.tile(jnp.cos(ang), (1, 4))
    sn = jnp.tile(jnp.concatenate([-jnp.sin(ang), jnp.sin(ang)], axis=1), (1, 2))
    q, kc, ksl, kwn, vc, vsl, vwn, ng, prw, ga, gb = _inproj(x2, g_mix[0], cs, sn, w_in[0])
    k_c = _compress(kc, cmp_pe_k[0], cmp_w1_k[0], cmp_b1_k[0], cmp_w2_k[0], cmp_b2_k[0], bsz, s, 0.0)
    v_c = _compress(vc, cmp_pe_v[0], cmp_w1_v[0], cmp_b1_v[0], cmp_w2_v[0], cmp_b2_v[0], bsz, s, 1.0)
    o_a = _nsa(q, k_c, v_c, ksl, vsl, kwn, vwn, ng, bsz, s).reshape(n_tok, WIDTH_A)
    r, lw, k, v, kn, kb, g, bonus = _rwkv_prep(prw, rwkv_mu[0], rwkv_w0[0], rwkv_w2[0], rwkv_a0[0], rwkv_a2[0],
                                               rwkv_g2[0], rwkv_k_k[0], rwkv_k_a[0], rwkv_r_k[0].reshape(-1), s)
    tmat = _rwkv_chunk(r, lw, k, v, kn, kb, bsz, s)
    yn = _rwkv_scan(tmat, bsz, s)
    x1, h2, route, counts = _merge(x2, o_a, yn.reshape(n_tok, WIDTH_B), bonus, g, ga, gb,
                           rwkv_ln_w[0], rwkv_ln_b[0], w_proj_a[0], w_proj_b[0], w_out[0], g_ffn[0],
                           w_grp[0], b_grp[0], w_exp[0], b_exp[0])
    pos, blk_expert, pad_end, n_pad = _dispatch_plan(route, counts, n_tok)
    w13 = jnp.concatenate([e_w1[0], e_w3[0]], axis=-1).astype(BF16)
    y_pad = _moe(_dispatch(h2, pos, pad_end, n_pad), blk_expert, w13, e_w2[0].astype(BF16))
    out = _final(x1, route, y_pad, pos, g_final)
    return out.reshape(bsz, s, D_MODEL)
```

```python
import functools

import numpy as np
import jax
import jax.numpy as jnp
from jax import lax
from jax.experimental import pallas as pl
from jax.experimental.pallas import tpu as pltpu

F32 = jnp.float32
BF16 = jnp.bfloat16
I32 = jnp.int32

D_MODEL = 1024
N_HEADS_A = 8
N_KV_GROUPS = 2
HEADS_PER_GROUP = N_HEADS_A // N_KV_GROUPS
HEAD_DIM = 64
L_CMP = 32
D_CMP = 16
CMP_HIDDEN = 256
L_SLC = 64
N_SELECT = 16
WINDOW = 512
Q_BLOCK = 128
ROPE_THETA = 10000.0
NEG_INF = -1e30
N_HEADS_B = 8
HEAD_DIM_B = 64
W_LORA = 64
A_LORA = 64
G_LORA = 128
GN_EPS = 64e-5
N_GROUPS = 4
EXPERTS_PER_GROUP = 8
N_EXPERTS = N_GROUPS * EXPERTS_PER_GROUP
TOP_K = 2
D_EXPERT = 256
MOE_BLOCK = 256
RMS_EPS = 1e-6
WIDTH_A = N_HEADS_A * HEAD_DIM
KV_WIDTH = N_KV_GROUPS * HEAD_DIM
WIDTH_B = N_HEADS_B * HEAD_DIM_B
C_RWKV = 3 * WIDTH_B + W_LORA + A_LORA + G_LORA

LANES = 128
CHUNK = 64
SUB = 16
KEY_TILE = 1024
BLOCKS_PER_TILE = KEY_TILE // L_SLC
VMEM_LIMIT = 56 * 1024 * 1024

LOG2_E = 1.4426950408889634


def _mm(a, b):
    return jnp.dot(a.astype(BF16), b.astype(BF16), preferred_element_type=F32)


def _mm_nt(a, b):
    return lax.dot_general(a.astype(BF16), b.astype(BF16), (((1,), (1,)), ((), ())), preferred_element_type=F32)


def _mm_tn(a, b):
    return lax.dot_general(a.astype(BF16), b.astype(BF16), (((0,), (0,)), ((), ())), preferred_element_type=F32)


def _mm_x3(a, b):
    a1 = a.astype(BF16)
    a2 = (a - a1.astype(F32)).astype(BF16)
    b1 = b.astype(BF16)
    b2 = (b - b1.astype(F32)).astype(BF16)
    dot = lambda x, y: jnp.dot(x, y, preferred_element_type=F32)
    return dot(a1, b1) + dot(a1, b2) + dot(a2, b1)


def _bf16_pieces(a):
    a1 = a.astype(BF16)
    r1 = a - a1.astype(F32)
    a2 = r1.astype(BF16)
    return a1, a2, (r1 - a2.astype(F32)).astype(BF16)


def _mm_exact_rhs(a, b):
    bb = b.astype(BF16)
    return sum(jnp.dot(x, bb, preferred_element_type=F32) for x in _bf16_pieces(a))


def _mm_exact_lhs(a, b):
    aa = a.astype(BF16)
    return sum(jnp.dot(aa, x, preferred_element_type=F32) for x in _bf16_pieces(b))


def _iota(shape, dim):
    return lax.broadcasted_iota(I32, shape, dim)


def _params(sem):
    return pltpu.CompilerParams(dimension_semantics=sem, vmem_limit_bytes=VMEM_LIMIT)


def _full(shape):
    nd = len(shape)
    return pl.BlockSpec(shape, lambda *_: (0,) * nd)


Q_PAD = N_HEADS_A * LANES
KV_PAD = N_KV_GROUPS * LANES


def _spread_heads(x, pad_value):
    low = _iota((1, LANES), 1) < HEAD_DIM
    out = []
    for c in range(x.shape[1] // LANES):
        pair = x[:, c * LANES:(c + 1) * LANES]
        out.append(jnp.where(low, pair, pad_value))
        out.append(jnp.where(low, pltpu.roll(pair, HEAD_DIM, 1), pad_value))
    return jnp.concatenate(out, axis=1)


def _inproj_kernel(x_ref, g_ref, cs_ref, sn_ref, wr_ref, wv_ref, wn_ref, ww_ref, wg_ref,
                   q_ref, kc_ref, ksl_ref, kwn_ref, vc_ref, vsl_ref, vwn_ref, ng_ref, prw_ref, ga_ref, gb_ref):
    x = x_ref[...]
    h = (x * lax.rsqrt(jnp.mean(x * x, axis=-1, keepdims=True) + RMS_EPS) * g_ref[...]).astype(BF16)
    pr = jnp.dot(h, wr_ref[...], preferred_element_type=F32)
    cs = cs_ref[...]
    sn = sn_ref[...]
    first_half = (_iota((1, LANES), 1) & (HEAD_DIM // 2)) == 0
    chunks = []
    for c in range(pr.shape[1] // LANES):
        x = pr[:, c * LANES:(c + 1) * LANES]
        swapped = jnp.where(first_half, pltpu.roll(x, LANES - HEAD_DIM // 2, 1), pltpu.roll(x, HEAD_DIM // 2, 1))
        chunks.append(x * cs + swapped * sn)
    ro = jnp.concatenate(chunks, axis=1)
    q_ref[...] = _spread_heads(ro[:, 0:WIDTH_A] * (HEAD_DIM ** -0.5 * LOG2_E), 0.0).astype(BF16)
    o = WIDTH_A
    kc_ref[...] = ro[:, o:o + KV_WIDTH]
    tm = x_ref.shape[0]
    blk = ((pl.program_id(0) * tm + _iota((tm, 1), 0)) >> 6) & (BLOCKS_PER_TILE - 1)
    code = jnp.where((_iota((1, KV_PAD), 1) & (LANES - 1)) == HEAD_DIM + blk, NEG_INF, 0.0)
    ksl_ref[...] = (_spread_heads(ro[:, o + KV_WIDTH:o + 2 * KV_WIDTH], 0.0) + code).astype(BF16)
    kwn_ref[...] = _spread_heads(ro[:, o + 2 * KV_WIDTH:o + 3 * KV_WIDTH], 0.0).astype(BF16)
    v = jnp.dot(h, wv_ref[...], preferred_element_type=F32)
    vc_ref[...] = v[:, 0:KV_WIDTH]
    vsl_ref[...] = _spread_heads(v[:, KV_WIDTH:2 * KV_WIDTH], 1.0).astype(BF16)
    vwn_ref[...] = _spread_heads(v[:, 2 * KV_WIDTH:3 * KV_WIDTH], 1.0).astype(BF16)
    ng_ref[...] = jnp.dot(h, wn_ref[...], preferred_element_type=F32)
    prw_ref[...] = jnp.dot(h, ww_ref[...], preferred_element_type=F32)
    gg = jnp.dot(h, wg_ref[...], preferred_element_type=F32)
    ga_ref[...] = gg[:, 0:D_MODEL]
    gb_ref[...] = gg[:, D_MODEL:2 * D_MODEL]


def _inproj(x2, g_mix, cs, sn, w_in, tm=512):
    n = x2.shape[0]
    o = 0
    cols = {}
    for name, wd in (("q", WIDTH_A), ("kc", KV_WIDTH), ("vc", KV_WIDTH), ("ksl", KV_WIDTH), ("vsl", KV_WIDTH),
                     ("kwn", KV_WIDTH), ("vwn", KV_WIDTH), ("ng", 3 * N_HEADS_A), ("rw", C_RWKV),
                     ("ga", D_MODEL), ("gb", D_MODEL)):
        cols[name] = w_in[:, o:o + wd]
        o += wd
    w_rope = jnp.concatenate([cols["q"], cols["kc"], cols["ksl"], cols["kwn"]], axis=1)
    w_v = jnp.concatenate([cols["vc"], cols["vsl"], cols["vwn"]], axis=1)
    w_ng = jnp.pad(cols["ng"], ((0, 0), (0, LANES - 3 * N_HEADS_A)))
    w_gate = jnp.concatenate([cols["ga"], cols["gb"]], axis=1)
    ws = [w.astype(BF16) for w in (w_rope, w_v, w_ng, cols["rw"], w_gate)]
    row = lambda wd: pl.BlockSpec((tm, wd), lambda i: (i, 0))
    outs = [(Q_PAD, BF16), (KV_WIDTH, F32), (KV_PAD, BF16), (KV_PAD, BF16), (KV_WIDTH, F32), (KV_PAD, BF16),
            (KV_PAD, BF16), (LANES, F32), (C_RWKV, F32), (D_MODEL, F32), (D_MODEL, F32)]
    return pl.pallas_call(
        _inproj_kernel,
        grid=(n // tm,),
        in_specs=[row(D_MODEL), _full((1, D_MODEL)), row(LANES), row(LANES)] + [_full(w.shape) for w in ws],
        out_specs=[row(wd) for wd, _ in outs],
        out_shape=[jax.ShapeDtypeStruct((n, wd), dt) for wd, dt in outs],
        compiler_params=_params(("parallel",)),
        name="inproj",
    )(x2, g_mix.reshape(1, D_MODEL), cs, sn, *ws)


def _compress_kernel(kv_ref, pe_ref, w1_ref, wg_ref, b1_ref, w2_ref, b2_ref, o_ref, *, pad_value):
    nch = o_ref.shape[2]
    z = None
    for l in range(D_CMP):
        part = _mm(kv_ref[0, pl.ds(l, nch, stride=D_CMP), :], wg_ref[0, l])
        z = part if z is None else z + part
    z1 = z[:, 0:CMP_HIDDEN]
    z2 = z[:, CMP_HIDDEN:2 * CMP_HIDDEN]
    z2 = jnp.concatenate([z2[1:], z2[:1]], axis=0)
    pb = _mm(pe_ref[...], w1_ref[...])[0:1] + b1_ref[...]
    hid = jax.nn.gelu(z1 + z2 + pb)
    out = _mm(hid, w2_ref[...]) + b2_ref[...]
    o_ref[0, 0] = jnp.concatenate([out, jnp.full(out.shape, pad_value, F32)], axis=1).astype(o_ref.dtype)


def _compress(kv, pe, w1, b1, w2, b2, bsz, s, pad_value):
    nch = s // D_CMP
    pe8 = jnp.broadcast_to(pe.reshape(1, L_CMP * HEAD_DIM), (8, L_CMP * HEAD_DIM))
    w1r = w1.reshape(2, D_CMP, HEAD_DIM, CMP_HIDDEN)
    wcat = jnp.concatenate([w1r[0], w1r[1]], axis=-1)
    wg = jnp.stack([jnp.pad(wcat, ((0, 0), (g * HEAD_DIM, (N_KV_GROUPS - 1 - g) * HEAD_DIM), (0, 0)))
                    for g in range(N_KV_GROUPS)]).astype(BF16)
    return pl.pallas_call(
        functools.partial(_compress_kernel, pad_value=pad_value),
        grid=(bsz, N_KV_GROUPS),
        in_specs=[pl.BlockSpec((1, s, KV_WIDTH), lambda b, g: (b, 0, 0)),
                  _full(pe8.shape), _full(w1.shape),
                  pl.BlockSpec((1, D_CMP, KV_WIDTH, 2 * CMP_HIDDEN), lambda b, g: (g, 0, 0, 0)),
                  _full((1, CMP_HIDDEN)), _full(w2.shape), _full((1, HEAD_DIM))],
        out_specs=pl.BlockSpec((1, 1, nch, LANES), lambda b, g: (b, g, 0, 0)),
        out_shape=jax.ShapeDtypeStruct((bsz, N_KV_GROUPS, nch, LANES), BF16),
        compiler_params=_params(("parallel", "parallel")),
        name="compress",
    )(kv.reshape(bsz, s, KV_WIDTH), pe8, w1, wg, b1.reshape(1, CMP_HIDDEN), w2, b2.reshape(1, HEAD_DIM))


def _nsa_kernel(q_ref, kc_ref, vc_ref, ksl_ref, vsl_ref, kwn_ref, vwn_ref, ng_ref, ov_ref, o_ref, *, n_top):
    n_cmp = kc_ref.shape[2]
    n_slc = ov_ref.shape[1]
    s0 = pl.program_id(1) * Q_BLOCK
    hq = HEADS_PER_GROUP * Q_BLOCK
    t_q = s0 + _iota((Q_BLOCK, 1), 0)
    sig = jax.nn.sigmoid(ng_ref[0])
    win_start = pl.multiple_of(jnp.maximum(s0 - WINDOW, 0), LANES)
    n_tiles = s0 // KEY_TILE + 1
    groups = range(N_KV_GROUPS)
    lanes = lambda g: slice(g * LANES, (g + 1) * LANES)
    qg = [jnp.concatenate([q_ref[0, :, lanes(g * HEADS_PER_GROUP + h)] for h in range(HEADS_PER_GROUP)], axis=0)
          for g in groups]

    def add_bias(s, bias):
        return (s.reshape(HEADS_PER_GROUP, Q_BLOCK, bias.shape[1]) + bias[None]).reshape(hq, bias.shape[1])

    cmp_bias = jnp.where((_iota((1, n_cmp), 1) * D_CMP + (L_CMP - 1)) <= t_q, 0.0, NEG_INF)
    has_cmp = jnp.concatenate([(t_q >= L_CMP - 1).astype(F32)] * HEADS_PER_GROUP, axis=0)
    pos_w = win_start + _iota((1, WINDOW + Q_BLOCK), 1)
    win_bias = jnp.where((pos_w <= t_q) & (pos_w > t_q - WINDOW), 0.0, NEG_INF)

    s_c = [add_bias(_mm_nt(qg[g], kc_ref[0, g]), cmp_bias) for g in groups]
    s_w = [add_bias(_mm_nt(qg[g], kwn_ref[0, pl.ds(win_start, WINDOW + Q_BLOCK), lanes(g)]), win_bias)
           for g in groups]
    e_c = [jnp.exp2(x - jnp.max(x, axis=-1, keepdims=True)) for x in s_c]
    p_c = [x * (has_cmp / jnp.sum(x, axis=-1, keepdims=True)) for x in e_c]
    e_w = [jnp.exp2(x - jnp.max(x, axis=-1, keepdims=True)) for x in s_w]
    o_c = [_mm(p_c[g], vc_ref[0, g]) for g in groups]
    acc_w = [_mm(e_w[g], vwn_ref[0, pl.ds(win_start, WINDOW + Q_BLOCK), lanes(g)]) for g in groups]
    o_win = [x * (1.0 / x[:, HEAD_DIM:HEAD_DIM + 1]) for x in acc_w]

    imp_t = []
    for g in groups:
        p_sum = p_c[g][0:Q_BLOCK]
        for h in range(1, HEADS_PER_GROUP):
            p_sum = p_sum + p_c[g][h * Q_BLOCK:(h + 1) * Q_BLOCK]
        imp_t.append(_mm_exact_rhs(p_sum, ov_ref[...]).T)
    j = _iota((n_slc, Q_BLOCK), 0)
    cur = (s0 + _iota((n_slc, Q_BLOCK), 1)) >> 6
    forced = (j == 0) | (j == cur) | (j == cur - 1)
    score = [jnp.where(forced, -3e38, jnp.where(j <= cur, x, -1.0)) for x in imp_t]
    sel_t = [forced.astype(F32) for _ in groups]
    for _ in range(n_top - 3):
        for g in groups:
            m = jnp.max(score[g], axis=0, keepdims=True)
            first = jnp.min(jnp.where(score[g] == m, j, n_slc), axis=0, keepdims=True)
            hit = j == first
            sel_t[g] = jnp.where(hit, 1.0, sel_t[g])
            score[g] = jnp.where(hit, -3e38, score[g])
    sel = [1.0 - x.T for x in sel_t]

    lane_q = _iota((Q_BLOCK, LANES), 1)
    in_code = (lane_q >= HEAD_DIM) & (lane_q < HEAD_DIM + BLOCKS_PER_TILE)

    def sel_step(kt, carry, diagonal):
        k0 = pl.multiple_of(kt * KEY_TILE, KEY_TILE)
        shift = (HEAD_DIM - BLOCKS_PER_TILE * kt) & (LANES - 1)
        s = []
        for g in groups:
            code = jnp.where(in_code, pltpu.roll(sel[g], shift, 1), 0.0).astype(BF16)
            qa = qg[g] + jnp.concatenate([code] * HEADS_PER_GROUP, axis=0)
            s.append(_mm_nt(qa, ksl_ref[0, pl.ds(k0, KEY_TILE), lanes(g)]))
        if diagonal:
            causal = jnp.where((k0 + _iota((Q_BLOCK, KEY_TILE), 1)) <= t_q, 0.0, NEG_INF)
            s = [add_bias(x, causal) for x in s]
        m_new = [jnp.maximum(carry[2 * g], jnp.max(s[g], axis=-1, keepdims=True)) for g in groups]
        p = [jnp.exp2(s[g] - m_new[g]) for g in groups]
        pv = [_mm(p[g], vsl_ref[0, pl.ds(k0, KEY_TILE), lanes(g)]) for g in groups]
        out = []
        for g in groups:
            out += [m_new[g], jnp.exp2(carry[2 * g] - m_new[g]) * carry[2 * g + 1] + pv[g]]
        return tuple(out)

    init = (jnp.full((hq, 1), NEG_INF, F32), jnp.zeros((hq, LANES), F32)) * N_KV_GROUPS
    fin = lax.fori_loop(0, n_tiles - 1, lambda kt, c: sel_step(kt, c, False), init)
    fin = sel_step(n_tiles - 1, fin, True)

    heads_out = []
    for g in groups:
        acc_s = fin[2 * g + 1]
        o_s = acc_s * (1.0 / acc_s[:, HEAD_DIM:HEAD_DIM + 1])
        o_w = o_win[g]
        for h in range(HEADS_PER_GROUP):
            hh = g * HEADS_PER_GROUP + h
            r = slice(h * Q_BLOCK, (h + 1) * Q_BLOCK)
            mix = (sig[:, 3 * hh:3 * hh + 1] * o_c[g][r] + sig[:, 3 * hh + 1:3 * hh + 2] * o_s[r]
                   + sig[:, 3 * hh + 2:3 * hh + 3] * o_w[r])
            heads_out.append(mix[:, 0:HEAD_DIM])
    o_ref[0] = jnp.concatenate(heads_out, axis=1).astype(o_ref.dtype)


def _nsa(q, kc, vc, ksl, vsl, kwn, vwn, ng, bsz, s):
    n_cmp = s // D_CMP
    n_slc = s // L_SLC
    ii = np.arange(n_cmp)[:, None]
    jj = np.arange(n_slc)[None, :]
    assert n_slc <= LANES and s % KEY_TILE == 0 and min(N_SELECT, n_slc) >= 3
    overlap = (ii * D_CMP < (jj + 1) * L_SLC) & (ii * D_CMP + L_CMP > jj * L_SLC)
    overlap = jnp.asarray(np.pad(overlap, ((0, 0), (0, LANES - n_slc))), BF16)
    seq = lambda wd: pl.BlockSpec((1, s, wd), lambda b, i: (b, 0, 0))
    blk = lambda wd: pl.BlockSpec((1, Q_BLOCK, wd), lambda b, i: (b, i, 0))
    cmp_spec = pl.BlockSpec((1, N_KV_GROUPS, n_cmp, LANES), lambda b, i: (b, 0, 0, 0))
    r3 = lambda a: a.reshape(bsz, s, a.shape[-1])
    return pl.pallas_call(
        functools.partial(_nsa_kernel, n_top=min(N_SELECT, n_slc)),
        grid=(bsz, s // Q_BLOCK),
        in_specs=[blk(Q_PAD), cmp_spec, cmp_spec, seq(KV_PAD), seq(KV_PAD), seq(KV_PAD), seq(KV_PAD),
                  blk(LANES), _full(overlap.shape)],
        out_specs=blk(WIDTH_A),
        out_shape=jax.ShapeDtypeStruct((bsz, s, WIDTH_A), BF16),
        compiler_params=_params(("parallel", "arbitrary")),
        name="nsa",
    )(r3(q), kc, vc, r3(ksl), r3(vsl), r3(kwn), r3(vwn), r3(ng), overlap)


PREP_SUB = 256


def _rwkv_prep_kernel(p_ref, pv_ref, mu_ref, w0_ref, w2_ref, a0_ref, a2_ref, g2_ref, kk_ref, ka_ref, rk_ref, bd_ref,
                      r_ref, lw_ref, k_ref, v_ref, kn_ref, kb_ref, g_ref, bonus_ref, *, tiles_per_seq):
    p = p_ref[...]
    first = (pl.program_id(0) % tiles_per_seq) == 0
    prev_last = jnp.where(first, 0.0, pv_ref[7:8, :])
    p_prev = jnp.concatenate([prev_last, p[:-1]], axis=0)
    xs = p + (p_prev - p) * mu_ref[...]
    wb = WIDTH_B
    o = 3 * wb
    subs = [slice(j * PREP_SUB, (j + 1) * PREP_SUB) for j in range(p.shape[0] // PREP_SUB)]
    k = [xs[r, wb:2 * wb] for r in subs]
    kk = [x * kk_ref[...] for x in k]
    z = [w0_ref[...] + _mm_x3(jnp.tanh(xs[r, o:o + W_LORA]), w2_ref[...]) for r in subs]
    a = [jax.nn.sigmoid(a0_ref[...] + _mm_x3(xs[r, o + W_LORA:o + W_LORA + A_LORA], a2_ref[...])) for r in subs]
    ss = [_mm_exact_rhs(x * x, bd_ref[...]) for x in kk]
    g = [_mm(jax.nn.sigmoid(xs[r, o + W_LORA + A_LORA:o + W_LORA + A_LORA + G_LORA]), g2_ref[...]) for r in subs]
    k_out = [x * (1.0 + (y - 1.0) * ka_ref[...]) for x, y in zip(k, a)]
    rk_sum = [_mm_exact_rhs(xs[r, 0:wb] * x * rk_ref[...], bd_ref[...]) for r, x in zip(subs, k_out)]
    for j, r in enumerate(subs):
        w_log = -(jnp.maximum(-z[j], 0.0) + jnp.log(1.0 + jnp.exp(-jnp.abs(z[j])))) - 0.5
        kn = kk[j] * (1.0 / jnp.maximum(jnp.sqrt(ss[j]), 1e-12))
        r_ref[r, :] = xs[r, 0:wb]
        lw_ref[r, :] = -jnp.exp(w_log)
        k_ref[r, :] = k_out[j]
        v_ref[r, :] = xs[r, 2 * wb:3 * wb]
        bonus_ref[r, :] = rk_sum[j] * xs[r, 2 * wb:3 * wb]
        kn_ref[r, :] = kn
        kb_ref[r, :] = kn * a[j]
        g_ref[r, :] = g[j]


def _rwkv_prep(prw, mu, w0, w2, a0, a2, g2, k_k, k_a, r_k, s, tm=2 * PREP_SUB):
    n = prw.shape[0]
    hid = np.arange(WIDTH_B) // HEAD_DIM_B
    bd = jnp.asarray(hid[:, None] == hid[None, :], BF16)
    row = lambda wd: pl.BlockSpec((tm, wd), lambda i: (i, 0))
    vec = lambda a: a.reshape(1, -1)
    return pl.pallas_call(
        functools.partial(_rwkv_prep_kernel, tiles_per_seq=s // tm),
        grid=(n // tm,),
        in_specs=[row(C_RWKV), pl.BlockSpec((8, C_RWKV), lambda i: (jnp.maximum(i * (tm // 8) - 1, 0), 0)),
                  _full((1, C_RWKV)), _full((1, WIDTH_B)), _full(w2.shape), _full((1, WIDTH_B)), _full(a2.shape),
                  _full(g2.shape), _full((1, WIDTH_B)), _full((1, WIDTH_B)), _full((1, WIDTH_B)), _full(bd.shape)],
        out_specs=[row(WIDTH_B)] * 8,
        out_shape=[jax.ShapeDtypeStruct((n, WIDTH_B), F32)] * 8,
        compiler_params=_params(("parallel",)),
        name="rwkv_prep",
    )(prw, prw, vec(mu), vec(w0), w2, vec(a0), a2, g2, vec(k_k), vec(k_a), vec(r_k), bd)


def _rwkv_chunk_kernel(r_ref, lw_ref, k_ref, v_ref, kn_ref, kb_ref, tm_ref, *, n_sub):
    L = CHUNK
    hd = HEAD_DIM_B
    ti = _iota((L, L), 0)
    si = _iota((L, L), 1)
    strict = si < ti
    incl = si <= ti
    same_sub = (ti // SUB) == (si // SUB)
    eye = (ti == si).astype(F32)
    tri = incl.astype(BF16)
    at, bt, kt, rt, bl, kl, vv, p_tot = [], [], [], [], [], [], [], []
    for j in range(n_sub):
        rows = slice(j * L, (j + 1) * L)
        lw = lw_ref[0, rows, :]
        cs = _mm_exact_lhs(tri, lw)
        p_inv = jnp.exp(-cs)
        p_end = jnp.exp(cs[L - 1:L, :] - cs)
        r = r_ref[0, rows, :]
        k = k_ref[0, rows, :]
        kb = kb_ref[0, rows, :]
        at.append(-kn_ref[0, rows, :] * jnp.exp(cs - lw))
        bt.append(kb * p_inv)
        kt.append(k * p_inv)
        rt.append(r * jnp.exp(cs))
        bl.append(kb * p_end)
        kl.append(k * p_end)
        vv.append(v_ref[0, rows, :])
        p_tot.append(jnp.exp(cs[L - 1:L, :]))
    units = [(j, h) for j in range(n_sub) for h in range(N_HEADS_B)]
    sl = lambda arr, u: arr[u[0]][:, u[1] * hd:(u[1] + 1) * hd]
    aa = [_mm_nt(jnp.concatenate([sl(at, u), sl(rt, u)], axis=0),
                 jnp.concatenate([sl(bt, u), sl(kt, u)], axis=0)) for u in units]
    a_ab = [jnp.where(strict, x[0:L, 0:L], 0.0) for x in aa]
    a_low = [jnp.concatenate([jnp.where(strict, x[0:L, L:2 * L], 0.0),
                              jnp.where(incl, x[L:2 * L, L:2 * L], 0.0)], axis=0) for x in aa]
    a_rb = [jnp.where(incl, x[L:2 * L, 0:L], 0.0) for x in aa]
    av = [_mm(x, sl(vv, u)) for x, u in zip(a_low, units)]
    kv = [_mm_tn(sl(kl, u), sl(vv, u)) for u in units]
    pw = [jnp.where(same_sub, x, 0.0) for x in a_ab]
    t = [eye + x for x in pw]
    for _ in range(3):
        pw = [_mm(x, x) for x in pw]
        t = [x + _mm(x, y) for x, y in zip(t, pw)]
    width = SUB
    while width < L:
        m = ((ti // width) == (si // width) + 1) & ((ti // (2 * width)) == (si // (2 * width)))
        ot = [_mm(jnp.where(m, x, 0.0), y) for x, y in zip(a_ab, t)]
        t = [x + _mm(x, y) for x, y in zip(t, ot)]
        width *= 2
    wu = [_mm(x, jnp.concatenate([sl(at, u), y[0:L]], axis=1)) for x, y, u in zip(t, av, units)]
    qy = [jnp.concatenate([sl(rt, u), y[L:2 * L]], axis=1) + _mm(x, w)
          for x, y, w, u in zip(a_rb, av, wu, units)]
    gh = [_mm_tn(sl(bl, u), w) + jnp.concatenate([eye * sl(p_tot, u), y], axis=1)
          for w, y, u in zip(wu, kv, units)]
    for x, y, (j, h) in zip(qy, gh, units):
        tm_ref[0, j, h, 0:L, :] = x
        tm_ref[0, j, h, L:2 * L, :] = y


def _rwkv_chunk(r, lw, k, v, kn, kb, bsz, s, n_sub=4):
    nch = s // CHUNK
    blk = pl.BlockSpec((1, n_sub * CHUNK, WIDTH_B), lambda b, c: (b, c, 0))
    mat = pl.BlockSpec((1, n_sub, N_HEADS_B, 2 * CHUNK, 2 * HEAD_DIM_B), lambda b, c: (b, c, 0, 0, 0))
    r3 = lambda a: a.reshape(bsz, s, WIDTH_B)
    return pl.pallas_call(
        functools.partial(_rwkv_chunk_kernel, n_sub=n_sub),
        grid=(bsz, nch // n_sub),
        in_specs=[blk] * 6,
        out_specs=mat,
        out_shape=jax.ShapeDtypeStruct((bsz, nch, N_HEADS_B, 2 * CHUNK, 2 * HEAD_DIM_B), F32),
        compiler_params=_params(("parallel", "parallel")),
        name="rwkv_chunk",
    )(r3(r), r3(lw), r3(k), r3(v), r3(kn), r3(kb))


def _rwkv_scan_kernel(tm_ref, yn_ref, st_ref):
    L = CHUNK
    hd = HEAD_DIM_B

    @pl.when(pl.program_id(0) == 0)
    def _():
        eye = (_iota((hd, hd), 0) == _iota((hd, hd), 1)).astype(F32)
        for i in range(st_ref.shape[0]):
            st_ref[i] = jnp.concatenate([jnp.zeros((hd, hd), F32), eye], axis=0)

    for b in range(tm_ref.shape[0]):
        res = [_mm_x3(tm_ref[b, 0, h], st_ref[b * N_HEADS_B + h]) for h in range(N_HEADS_B)]
        ys = []
        for h in range(N_HEADS_B):
            st_ref[b * N_HEADS_B + h, 0:hd, :] = res[h][L:2 * L]
            y = res[h][0:L]
            mean = jnp.mean(y, axis=-1, keepdims=True)
            d = y - mean
            var = jnp.mean(d * d, axis=-1, keepdims=True)
            ys.append(d * lax.rsqrt(var + GN_EPS))
        yn_ref[b] = jnp.concatenate(ys, axis=1)


def _rwkv_scan(tmat, bsz, s):
    nch = s // CHUNK
    return pl.pallas_call(
        _rwkv_scan_kernel,
        grid=(nch,),
        in_specs=[pl.BlockSpec((bsz, 1, N_HEADS_B, 2 * CHUNK, 2 * HEAD_DIM_B), lambda c: (0, c, 0, 0, 0))],
        out_specs=pl.BlockSpec((bsz, CHUNK, WIDTH_B), lambda c: (0, c, 0)),
        out_shape=jax.ShapeDtypeStruct((bsz, s, WIDTH_B), F32),
        scratch_shapes=[pltpu.VMEM((bsz * N_HEADS_B, 2 * HEAD_DIM_B, HEAD_DIM_B), F32)],
        compiler_params=_params(("arbitrary",)),
        name="rwkv_scan",
    )(tmat)


SUBLANES = 8
assert D_MODEL == SUBLANES * LANES


def _row_tiles_shape(rows):
    return (rows // SUBLANES, SUBLANES, SUBLANES, LANES)


def _row_tiles_spec(rows):
    return pl.BlockSpec(_row_tiles_shape(rows), lambda i, *_: (i, 0, 0, 0))


def _store_row_tiles(ref, x):
    for c in range(SUBLANES):
        ref[:, c] = x[:, c * LANES:(c + 1) * LANES].reshape(x.shape[0] // SUBLANES, SUBLANES, LANES)


def _row_view(ref, r):
    return ref.at[pl.ds(r >> 3, 1), :, r & (SUBLANES - 1)]


def _gather_buffer(rows):
    return pltpu.VMEM((2, rows // SUBLANES, SUBLANES, SUBLANES, LANES), F32)


def _gathered_rows(buf, slot, start, rows):
    t0, nt = start // SUBLANES, rows // SUBLANES
    return jnp.concatenate([buf[slot, t0:t0 + nt, c].reshape(rows, LANES) for c in range(SUBLANES)], axis=1)


MERGE_SUB = 256


def _route(logits, run):
    tm = logits.shape[0]
    lane = _iota(logits.shape, 1)
    big = 4 * LANES
    is_grp = lane < N_GROUPS
    lg = jnp.where(is_grp, logits, NEG_INF)
    mg = jnp.max(lg, axis=-1, keepdims=True)
    gidx = jnp.min(jnp.where(lg == mg, lane, big), axis=-1, keepdims=True)
    pg = 1.0 / jnp.sum(jnp.where(is_grp, jnp.exp(lg - mg), 0.0), axis=-1, keepdims=True)
    in_grp = (lane >= N_GROUPS) & (((lane - N_GROUPS) >> 3) == gidx) & (lane < N_GROUPS + N_EXPERTS)
    le = jnp.where(in_grp, logits, NEG_INF)
    m1 = jnp.max(le, axis=-1, keepdims=True)
    i1 = jnp.min(jnp.where(le == m1, lane, big), axis=-1, keepdims=True)
    le2 = jnp.where(lane == i1, NEG_INF, le)
    m2 = jnp.max(le2, axis=-1, keepdims=True)
    i2 = jnp.min(jnp.where(le2 == m2, lane, big), axis=-1, keepdims=True)
    e2 = jnp.exp(m2 - m1)
    inv = pg / (1.0 + e2)
    hot1 = (lane == i1 - N_GROUPS).astype(F32)
    hot2 = (lane == i2 - N_GROUPS).astype(F32)
    both = hot1 + hot2
    earlier = (_iota((tm, tm), 1) < _iota((tm, tm), 0)).astype(BF16)
    before = jnp.dot(earlier, both.astype(BF16), preferred_element_type=F32) + run
    rank1 = jnp.sum(hot1 * before, axis=-1, keepdims=True)
    rank2 = jnp.sum(hot2 * before, axis=-1, keepdims=True)
    cols = ((i1 - N_GROUPS).astype(F32), (i2 - N_GROUPS).astype(F32), inv, inv * e2, rank1, rank2)
    route = jnp.zeros_like(logits)
    for c, v in enumerate(cols):
        route = jnp.where(lane == c, v, route)
    return route, run + jnp.sum(both, axis=0, keepdims=True)


def _merge_kernel(x_ref, oa_ref, yn_ref, bonus_ref, g_ref, ga_ref, gb_ref, lnw_ref, lnb_ref, wa_ref, wb_ref,
                  wo_ref, gf_ref, wr_ref, br_ref, x1_ref, h2_ref, route_ref, cnt_ref, run_ref):
    @pl.when(pl.program_id(0) == 0)
    def _():
        run_ref[...] = jnp.zeros_like(run_ref)

    subs = [slice(j * MERGE_SUB, (j + 1) * MERGE_SUB) for j in range(x_ref.shape[0] // MERGE_SUB)]
    ob = [(yn_ref[r, :] * lnw_ref[...] + lnb_ref[...] + bonus_ref[r, :]) * g_ref[r, :] for r in subs]
    pa = [jnp.dot(oa_ref[r, :], wa_ref[...], preferred_element_type=F32) for r in subs]
    pb = [_mm(x, wb_ref[...]) for x in ob]
    merged = [jax.nn.sigmoid(ga_ref[r, :]) * a + jax.nn.sigmoid(gb_ref[r, :]) * b for r, a, b in zip(subs, pa, pb)]
    x1 = [x_ref[r, :] + _mm(m, wo_ref[...]) for r, m in zip(subs, merged)]
    h2 = [x * lax.rsqrt(jnp.mean(x * x, axis=-1, keepdims=True) + RMS_EPS) * gf_ref[...] for x in x1]
    logits = []
    for x in h2:
        hi = x.astype(BF16)
        lo = (x - hi.astype(F32)).astype(BF16)
        logits.append(jnp.dot(jnp.concatenate([hi, hi, lo], axis=1), wr_ref[...], preferred_element_type=F32)
                      + br_ref[...])
    run = run_ref[...]
    for r, x, h, lg in zip(subs, x1, h2, logits):
        x1_ref[r, :] = x
        _store_row_tiles(h2_ref.at[pl.ds(r.start // SUBLANES, MERGE_SUB // SUBLANES)], h)
        route_ref[r, :], run = _route(lg, run)
    run_ref[...] = run
    cnt_ref[...] = run


def _merge(x2, oa, yn, bonus, g, ga, gb, ln_w, ln_b, w_a, w_b, w_o, g_ffn, w_grp, b_grp, w_exp, b_exp,
           tm=2 * MERGE_SUB):
    n = x2.shape[0]
    w_r = jnp.pad(jnp.concatenate([w_grp, w_exp], axis=1), ((0, 0), (0, LANES - N_GROUPS - N_EXPERTS)))
    w_hi = w_r.astype(BF16)
    w_lo = (w_r - w_hi.astype(F32)).astype(BF16)
    w_r = jnp.concatenate([w_hi, w_lo, w_hi], axis=0)
    b_r = jnp.pad(jnp.concatenate([b_grp, b_exp]), (0, LANES - N_GROUPS - N_EXPERTS)).reshape(1, LANES)
    row = lambda wd: pl.BlockSpec((tm, wd), lambda i: (i, 0))
    vec = lambda a: a.reshape(1, -1)
    wa, wb, wo = w_a.astype(BF16), w_b.astype(BF16), w_o.astype(BF16)
    return pl.pallas_call(
        _merge_kernel,
        grid=(n // tm,),
        in_specs=[row(D_MODEL), row(WIDTH_A), row(WIDTH_B), row(WIDTH_B), row(WIDTH_B), row(D_MODEL), row(D_MODEL),
                  _full((1, WIDTH_B)), _full((1, WIDTH_B)), _full(wa.shape), _full(wb.shape), _full(wo.shape),
                  _full((1, D_MODEL)), _full(w_r.shape), _full((1, LANES))],
        out_specs=[row(D_MODEL), _row_tiles_spec(tm), row(LANES), _full((1, LANES))],
        out_shape=[jax.ShapeDtypeStruct((n, D_MODEL), F32), jax.ShapeDtypeStruct(_row_tiles_shape(n), F32),
                   jax.ShapeDtypeStruct((n, LANES), F32), jax.ShapeDtypeStruct((1, LANES), F32)],
        scratch_shapes=[pltpu.VMEM((1, LANES), F32)],
        compiler_params=_params(("arbitrary",)),
        name="merge",
    )(x2, oa, yn, bonus, g, ga, gb, vec(ln_w), vec(ln_b), wa, wb, wo, vec(g_ffn), w_r, b_r)


def _gather_start(idx_ref, src_hbm, buf, sem, slot, rows):
    def body(t, carry):
        for u in range(SUBLANES):
            pltpu.make_async_copy(_row_view(src_hbm, idx_ref[0, 0, t * SUBLANES + u]),
                                  buf.at[slot, pl.ds(t, 1), :, u], sem.at[slot]).start(priority=u % 2)
        return carry
    lax.fori_loop(0, rows // SUBLANES, body, 0)


def _gather_wait(buf, sem, slot):
    pltpu.make_async_copy(buf.at[slot], buf.at[slot], sem.at[slot]).wait()


def _gather_pipeline(idx_ref, idx_next_ref, src_hbm, buf, sem, rows):
    i = pl.program_id(0)
    slot = i % 2

    @pl.when(i == 0)
    def _():
        _gather_start(idx_ref, src_hbm, buf, sem, 0, rows)

    @pl.when(i + 1 < pl.num_programs(0))
    def _():
        _gather_start(idx_next_ref, src_hbm, buf, sem, 1 - slot, rows)

    _gather_wait(buf, sem, slot)
    return slot


def _dispatch_kernel(end_ref, pos_ref, h2_ref, x_hbm, zeros, sem, zsem, *, tm):
    tile_rows = MOE_BLOCK // SUBLANES

    @pl.when(pl.program_id(0) == 0)
    def _():
        zeros[...] = jnp.zeros_like(zeros)

        def fill(block):
            return pltpu.make_async_copy(zeros, x_hbm.at[pl.ds(block * tile_rows, tile_rows)], zsem.at[0])

        for act in ("start", "wait"):
            for e in range(N_EXPERTS):
                start = end_ref[e - 1] if e else 0

                @pl.when(end_ref[e] > start)
                def _():
                    getattr(fill(end_ref[e] // MOE_BLOCK - 1), act)()

            def tail(block, carry):
                getattr(fill(block), act)()
                return carry
            lax.fori_loop(end_ref[N_EXPERTS - 1] // MOE_BLOCK, x_hbm.shape[0] // tile_rows, tail, 0)

    def body(t, carry):
        for u in range(SUBLANES):
            r = t * SUBLANES + u
            for k in range(TOP_K):
                p = pos_ref[0, 0, k * tm + r]
                pltpu.make_async_copy(h2_ref.at[pl.ds(t, 1), :, u], _row_view(x_hbm, p),
                                      sem.at[k]).start(priority=(u + k) % 2)
        return carry
    lax.fori_loop(0, tm // SUBLANES, body, 0)
    for k in range(TOP_K):
        pltpu.make_async_copy(h2_ref, h2_ref, sem.at[k]).wait()


def _dispatch(h2, pos, pad_end, n_pad, tm=1024):
    nt = h2.shape[0] * SUBLANES // tm
    pos3 = pos.reshape(nt, tm, TOP_K).transpose(0, 2, 1).reshape(nt, 1, TOP_K * tm)
    tiles = _row_tiles_shape(n_pad)
    grid_spec = pltpu.PrefetchScalarGridSpec(
        num_scalar_prefetch=1,
        grid=(nt,),
        in_specs=[pl.BlockSpec((1, 1, TOP_K * tm), lambda i, pe: (i, 0, 0), memory_space=pltpu.SMEM),
                  _row_tiles_spec(tm)],
        out_specs=pl.BlockSpec(memory_space=pl.ANY),
        scratch_shapes=[pltpu.VMEM(_row_tiles_shape(MOE_BLOCK), F32),
                        pltpu.SemaphoreType.DMA((TOP_K,)), pltpu.SemaphoreType.DMA((1,))],
    )
    return pl.pallas_call(
        functools.partial(_dispatch_kernel, tm=tm),
        grid_spec=grid_spec,
        out_shape=jax.ShapeDtypeStruct(tiles, F32),
        compiler_params=_params(("arbitrary",)),
        name="dispatch",
    )(pad_end, pos3, h2)


def _moe_kernel(be_ref, x_ref, w13_ref, w2_ref, y_ref):
    xb = jnp.concatenate([x_ref[:, c].reshape(MOE_BLOCK, LANES) for c in range(SUBLANES)], axis=1)
    a = _mm(xb, w13_ref[0])
    hid = jax.nn.silu(a[:, 0:D_EXPERT]) * a[:, D_EXPERT:2 * D_EXPERT]
    _store_row_tiles(y_ref, _mm(hid, w2_ref[0]))


def _moe(x_pad, blk_expert, w13, w2):
    n_blk = blk_expert.shape[0]
    grid_spec = pltpu.PrefetchScalarGridSpec(
        num_scalar_prefetch=1,
        grid=(n_blk,),
        in_specs=[_row_tiles_spec(MOE_BLOCK),
                  pl.BlockSpec((1, D_MODEL, 2 * D_EXPERT), lambda i, be: (be[i], 0, 0)),
                  pl.BlockSpec((1, D_EXPERT, D_MODEL), lambda i, be: (be[i], 0, 0))],
        out_specs=_row_tiles_spec(MOE_BLOCK),
    )
    return pl.pallas_call(
        _moe_kernel,
        grid_spec=grid_spec,
        out_shape=jax.ShapeDtypeStruct(_row_tiles_shape(n_blk * MOE_BLOCK), F32),
        compiler_params=_params(("parallel",)),
        name="moe",
    )(blk_expert, x_pad, w13, w2)


def _final_kernel(pos_ref, pos_next_ref, x1_ref, route_ref, g_ref, y_hbm, o_ref, buf, sem, *, tm):
    slot = _gather_pipeline(pos_ref, pos_next_ref, y_hbm, buf, sem, TOP_K * tm)
    x = x1_ref[...]
    for k in range(TOP_K):
        x = x + route_ref[:, TOP_K + k:TOP_K + k + 1] * _gathered_rows(buf, slot, k * tm, tm)
    o_ref[...] = x * lax.rsqrt(jnp.mean(x * x, axis=-1, keepdims=True) + RMS_EPS) * g_ref[...]


def _final(x1, route, y_pad, pos, g_final, tm=512):
    n = x1.shape[0]
    nt = n // tm
    pos3 = pos.reshape(nt, tm, TOP_K).transpose(0, 2, 1).reshape(nt, 1, TOP_K * tm)
    smem_blk = lambda f: pl.BlockSpec((1, 1, TOP_K * tm), f, memory_space=pltpu.SMEM)
    row = lambda wd: pl.BlockSpec((tm, wd), lambda i: (i, 0))
    return pl.pallas_call(
        functools.partial(_final_kernel, tm=tm),
        grid=(nt,),
        in_specs=[smem_blk(lambda i: (i, 0, 0)), smem_blk(lambda i: (jnp.minimum(i + 1, nt - 1), 0, 0)),
                  row(D_MODEL), row(LANES), _full((1, D_MODEL)), pl.BlockSpec(memory_space=pl.ANY)],
        out_specs=row(D_MODEL),
        out_shape=jax.ShapeDtypeStruct((n, D_MODEL), F32),
        scratch_shapes=[_gather_buffer(TOP_K * tm), pltpu.SemaphoreType.DMA((2,))],
        compiler_params=_params(("arbitrary",)),
        name="final",
    )(pos3, pos3, x1, route, g_final.reshape(1, D_MODEL), y_pad)


def _dispatch_plan(route, counts, n_tok):
    expert = route[:, 0:TOP_K].astype(I32)
    rank = route[:, 2 * TOP_K:3 * TOP_K].astype(I32)
    cnt = counts[0, 0:N_EXPERTS].astype(I32)
    padded = (cnt + MOE_BLOCK - 1) // MOE_BLOCK * MOE_BLOCK
    ids = jnp.arange(N_EXPERTS, dtype=I32)
    pad_end = jnp.sum(jnp.where(ids[None, :] <= ids[:, None], padded[None, :], 0), axis=1)
    pad_start = pad_end - padded
    pos = jnp.sum(jnp.where(expert[..., None] == ids, pad_start, 0), axis=-1) + rank
    n_pad = n_tok * TOP_K + N_EXPERTS * MOE_BLOCK
    n_blk = n_pad // MOE_BLOCK
    blk_start = jnp.arange(n_blk, dtype=I32) * MOE_BLOCK
    blk_expert = jnp.minimum(jnp.sum((pad_end[None, :] <= blk_start[:, None]).astype(I32), axis=1), N_EXPERTS - 1)
    return pos, blk_expert, pad_end, n_pad


def kernel(x, positions, g_mix, w_in, cmp_pe_k, cmp_w1_k, cmp_b1_k, cmp_w2_k, cmp_b2_k, cmp_pe_v, cmp_w1_v, cmp_b1_v, cmp_w2_v, cmp_b2_v, rwkv_mu, rwkv_w0, rwkv_w2, rwkv_a0, rwkv_a2, rwkv_g2, rwkv_k_k, rwkv_k_a, rwkv_r_k, rwkv_ln_w, rwkv_ln_b, w_proj_a, w_proj_b, w_out, g_ffn, w_grp, b_grp, w_exp, b_exp, e_w1, e_w3, e_w2, g_final):
    bsz, s, _ = x.shape
    assert w_in.shape[0] == 1, "single-layer stack"
    n_tok = bsz * s
    x2 = x.reshape(n_tok, D_MODEL)
    half = HEAD_DIM // 2
    freqs = jnp.power(ROPE_THETA, -jnp.arange(half, dtype=F32) / half)
    ang = positions.astype(F32).reshape(n_tok, 1) * freqs
    cs = jnp.tile(jnp.cos(ang), (1, 4))
    sn = jnp.tile(jnp.concatenate([-jnp.sin(ang), jnp.sin(ang)], axis=1), (1, 2))
    q, kc, ksl, kwn, vc, vsl, vwn, ng, prw, ga, gb = _inproj(x2, g_mix[0], cs, sn, w_in[0])
    k_c = _compress(kc, cmp_pe_k[0], cmp_w1_k[0], cmp_b1_k[0], cmp_w2_k[0], cmp_b2_k[0], bsz, s, 0.0)
    v_c = _compress(vc, cmp_pe_v[0], cmp_w1_v[0], cmp_b1_v[0], cmp_w2_v[0], cmp_b2_v[0], bsz, s, 1.0)
    o_a = _nsa(q, k_c, v_c, ksl, vsl, kwn, vwn, ng, bsz, s).reshape(n_tok, WIDTH_A)
    r, lw, k, v, kn, kb, g, bonus = _rwkv_prep(prw, rwkv_mu[0], rwkv_w0[0], rwkv_w2[0], rwkv_a0[0], rwkv_a2[0],
                                               rwkv_g2[0], rwkv_k_k[0], rwkv_k_a[0], rwkv_r_k[0].reshape(-1), s)
    tmat = _rwkv_chunk(r, lw, k, v, kn, kb, bsz, s)
    yn = _rwkv_scan(tmat, bsz, s)
    x1, h2, route, counts = _merge(x2, o_a, yn.reshape(n_tok, WIDTH_B), bonus, g, ga, gb,
                           rwkv_ln_w[0], rwkv_ln_b[0], w_proj_a[0], w_proj_b[0], w_out[0], g_ffn[0],
                           w_grp[0], b_grp[0], w_exp[0], b_exp[0])
    pos, blk_expert, pad_end, n_pad = _dispatch_plan(route, counts, n_tok)
    w13 = jnp.concatenate([e_w1[0], e_w3[0]], axis=-1).astype(BF16)
    y_pad = _moe(_dispatch(h2, pos, pad_end, n_pad), blk_expert, w13, e_w2[0].astype(BF16))
    out = _final(x1, route, y_pad, pos, g_final)
    return out.reshape(bsz, s, D_MODEL)
```
